```python
import math
import jax, jax.numpy as jnp
from jax import lax
import numpy as np

D_MODEL = 1024
BATCH = 4
SEQ = 8192
DEPTH = 2
DEC_BATCH = 32
DEC_SEQ = 32
PAST_LEN = 1024

CHUNK = 64
RMS_EPS = 1e-6
SSM_GROUPS = 16
SSM_GC = 16
SSM_STATE = 64
SSM_W = SSM_GROUPS * SSM_GC
GLA_HEADS = 6
GLA_DK = 32
GLA_DV = 64
GLA_GATE_RANK = 16
GLA_GATE_NORM = 16.0
GLA_W = GLA_HEADS * GLA_DV
SWA_HEADS = 6
SWA_KV_HEADS = 2
SWA_HEAD_DIM = 64
SWA_WINDOW = 128
SWA_W = SWA_HEADS * SWA_HEAD_DIM
MIX_W = SSM_W + GLA_W + SWA_W
IN_SPLITS = (SSM_W, GLA_HEADS * GLA_DK, GLA_HEADS * GLA_DK, GLA_HEADS * GLA_DV, GLA_GATE_RANK, GLA_HEADS * GLA_DV, SWA_HEADS * SWA_HEAD_DIM, SWA_KV_HEADS * SWA_HEAD_DIM, SWA_KV_HEADS * SWA_HEAD_DIM)
IN_W = sum(IN_SPLITS)
N_BUCKETS = 32
T5_MAX_DIST = 128
N_EXPERT_GROUPS = 4
EXPERTS_PER_GROUP = 4
N_EXPERTS = N_EXPERT_GROUPS * EXPERTS_PER_GROUP
TOP_K = 2
EXPERT_FF = 512

kernel_name = 'hybrid_s5_gla_swa_hmoe_stream_step'


def rms_norm(x, g):
    x32 = x.astype(jnp.float32)
    y = x32 * lax.rsqrt(jnp.mean(x32 * x32, axis=-1, keepdims=True) + RMS_EPS)
    return (y * g.astype(jnp.float32)).astype(x.dtype)


def _complex_affine_combine(left, right):
    a1r, a1i, b1r, b1i = left
    a2r, a2i, b2r, b2i = right
    return (a2r * a1r - a2i * a1i, a2r * a1i + a2i * a1r,
            a2r * b1r - a2i * b1i + b2r, a2r * b1i + a2i * b1r + b2i)


def s5_mixer(u, a_re, a_im, log_dt, b_re, b_im, c_re, c_im, d_skip, w_glu, h0_re, h0_im):
    f32 = jnp.float32
    B, T, _ = u.shape
    a_re = a_re.astype(f32)
    a_im = a_im.astype(f32)
    dt = jnp.exp(log_dt.astype(f32))[:, None]
    mag = jnp.exp(a_re * dt)
    ab_re = mag * jnp.cos(a_im * dt)
    ab_im = mag * jnp.sin(a_im * dt)
    den = a_re * a_re + a_im * a_im
    nr = ab_re - 1.0
    f_re = (nr * a_re + ab_im * a_im) / den
    f_im = (ab_im * a_re - nr * a_im) / den
    b_re = b_re.astype(f32)
    b_im = b_im.astype(f32)
    bb_re = f_re[..., None] * b_re - f_im[..., None] * b_im
    bb_im = f_re[..., None] * b_im + f_im[..., None] * b_re
    u32 = u.astype(f32).reshape(B, T, SSM_GROUPS, SSM_GC)
    bu_re = jnp.einsum('btgc,gpc->btgp', u32, bb_re)
    bu_im = jnp.einsum('btgc,gpc->btgp', u32, bb_im)
    h0_re = h0_re.astype(f32)
    h0_im = h0_im.astype(f32)
    bu_re = bu_re.at[:, 0].add(ab_re * h0_re - ab_im * h0_im)
    bu_im = bu_im.at[:, 0].add(ab_re * h0_im + ab_im * h0_re)
    shp = bu_re.shape
    _, _, h_re, h_im = lax.associative_scan(
        _complex_affine_combine,
        (jnp.broadcast_to(ab_re, shp), jnp.broadcast_to(ab_im, shp), bu_re, bu_im), axis=1)
    y = (jnp.einsum('btgp,gcp->btgc', h_re, c_re.astype(f32))
         - jnp.einsum('btgp,gcp->btgc', h_im, c_im.astype(f32))
         + d_skip.astype(f32).reshape(SSM_GROUPS, SSM_GC) * u32)
    g = jax.nn.gelu(y.reshape(B, T, SSM_W))
    out = g * jax.nn.sigmoid(g @ w_glu.astype(f32))
    return out, h_re[:, -1], h_im[:, -1]


def gla_mixer(q, k, v, glog, h0):
    f32 = jnp.float32
    B, T, H, DK = q.shape
    L = min(CHUNK, T)
    N = T // L
    blk = lambda a: a.astype(f32).reshape(B, N, L, H, a.shape[-1])
    q, k, v, glog = blk(q) * (DK ** -0.5), blk(k), blk(v), blk(glog)
    G = jnp.cumsum(glog, axis=2)
    GL = G[:, :, -1]
    qe = q * jnp.exp(G)
    ke = k * jnp.exp(-G)
    attn = jnp.einsum('bnthk,bnshk->bnhts', qe, ke)
    attn = jnp.where(jnp.tril(jnp.ones((L, L), bool)), attn, 0.0)
    o_intra = jnp.einsum('bnhts,bnshv->bnthv', attn, v)
    kv = jnp.einsum('bnshk,bnshv->bnhkv', k * jnp.exp(GL[:, :, None] - G), v)
    decay = jnp.exp(GL)

    def step(S, inp):
        dc, kvc = inp
        return dc[..., None] * S + kvc, S

    S_fin, S_in = lax.scan(step, h0.astype(f32), (jnp.moveaxis(decay, 1, 0), jnp.moveaxis(kv, 1, 0)))
    S_in = jnp.moveaxis(S_in, 0, 1)
    o_inter = jnp.einsum('bnthk,bnhkv->bnthv', qe, S_in)
    return (o_intra + o_inter).reshape(B, T, H, v.shape[-1]), S_fin


def t5_bias(t5_table, q_pos, k_pos):
    rel = k_pos[..., None, :] - q_pos[..., :, None]
    half = N_BUCKETS // 2
    max_exact = half // 2
    n = jnp.abs(rel)
    far = max_exact + (jnp.log(jnp.maximum(n, 1).astype(jnp.float32) / max_exact)
                       / math.log(T5_MAX_DIST / max_exact) * (half - max_exact)).astype(jnp.int32)
    bucket = jnp.where(rel > 0, half, 0) + jnp.where(n < max_exact, n, jnp.minimum(far, half - 1))
    return jnp.moveaxis(t5_table.astype(jnp.float32)[bucket], -1, -3)


def chunk_band_mask(q_pos, k_pos):
    qc = q_pos[..., :, None] // CHUNK
    kc = k_pos[..., None, :] // CHUNK
    return (k_pos[..., None, :] >= 0) & (kc <= qc) & (kc >= qc - SWA_WINDOW // CHUNK)


def sink_attention(q, k, v, bias, mask, sinks):
    f32 = jnp.float32
    B, N, Q, H, D = q.shape
    G = k.shape[3]
    R = H // G
    K = k.shape[2]
    s = jnp.einsum('bnqgrd,bnkgd->bngrqk', q.reshape(B, N, Q, G, R, D), k).astype(f32) * (D ** -0.5)
    s = s + bias.reshape(N, G, R, Q, K)[None]
    s = jnp.where(mask[None, :, None, None], s, -1e30)
    sink = jnp.broadcast_to(sinks.astype(f32).reshape(G, R)[None, None, :, :, None, None], (B, N, G, R, Q, 1))
    p = jax.nn.softmax(jnp.concatenate([s, sink], axis=-1), axis=-1)[..., :K]
    o = jnp.einsum('bngrqk,bnkgd->bnqgrd', p, v.astype(f32))
    return o.reshape(B, N, Q, H, D).astype(q.dtype)


def swa_prompt(q, k, v, sinks, t5_table):
    B, T, H, D = q.shape
    G = k.shape[2]
    n = T // CHUNK
    wc = SWA_WINDOW // CHUNK
    pad = ((0, 0), (wc * CHUNK, 0), (0, 0), (0, 0))
    kp = jnp.pad(k, pad).reshape(B, n + wc, CHUNK, G, D)
    vp = jnp.pad(v, pad).reshape(B, n + wc, CHUNK, G, D)
    kb = jnp.concatenate([kp[:, j:j + n] for j in range(wc + 1)], axis=2)
    vb = jnp.concatenate([vp[:, j:j + n] for j in range(wc + 1)], axis=2)
    q_pos = jnp.arange(T, dtype=jnp.int32).reshape(n, CHUNK)
    kpos = jnp.arange(-wc * CHUNK, T, dtype=jnp.int32).reshape(n + wc, CHUNK)
    k_pos = jnp.concatenate([kpos[j:j + n] for j in range(wc + 1)], axis=1)
    o = sink_attention(q.reshape(B, n, CHUNK, H, D), kb, vb, t5_bias(t5_table, q_pos, k_pos),
                       chunk_band_mask(q_pos, k_pos), sinks)
    return o.reshape(B, T, H * D)


def swa_sample(q, k, v, past_k, past_v, sinks, t5_table):
    B, T, H, D = q.shape
    W = past_k.shape[1]
    kb = jnp.concatenate([past_k.astype(k.dtype), k], axis=1)[:, None]
    vb = jnp.concatenate([past_v.astype(v.dtype), v], axis=1)[:, None]
    q_pos = (PAST_LEN + jnp.arange(T, dtype=jnp.int32))[None]
    k_pos = (PAST_LEN - W + jnp.arange(W + T, dtype=jnp.int32))[None]
    o = sink_attention(q[:, None], kb, vb, t5_bias(t5_table, q_pos, k_pos),
                       chunk_band_mask(q_pos, k_pos), sinks)
    return o.reshape(B, T, H * D)


def mixing_sublayer(x, lw, t5_table, ssm_h0_re, ssm_h0_im, gla_h0, past_k, past_v):
    f32 = jnp.float32
    B, T, _ = x.shape
    xn = rms_norm(x, lw['norm_mix'])
    proj = xn @ lw['w_in']
    offsets = np.cumsum(IN_SPLITS)[:-1].tolist()
    u_ssm, q_g, k_g, v_g, z_g, og_g, q_s, k_s, v_s = jnp.split(proj, offsets, axis=-1)
    y_ssm, h_re, h_im = s5_mixer(u_ssm, lw['ssm_a_re'], lw['ssm_a_im'], lw['ssm_log_dt'], lw['ssm_b_re'],
                                 lw['ssm_b_im'], lw['ssm_c_re'], lw['ssm_c_im'], lw['ssm_d'], lw['ssm_w_glu'],
                                 ssm_h0_re, ssm_h0_im)
    glog = jax.nn.log_sigmoid((z_g @ lw['gla_w_gate']).astype(f32) + lw['gla_b_gate'].astype(f32)) / GLA_GATE_NORM
    o_g, s_g = gla_mixer(q_g.reshape(B, T, GLA_HEADS, GLA_DK), k_g.reshape(B, T, GLA_HEADS, GLA_DK),
                         v_g.reshape(B, T, GLA_HEADS, GLA_DV), glog.reshape(B, T, GLA_HEADS, GLA_DK), gla_h0)
    o_g = (rms_norm(o_g, lw['gla_norm']).reshape(B, T, GLA_W) * jax.nn.silu(og_g.astype(f32)))
    q = rms_norm(q_s.reshape(B, T, SWA_HEADS, SWA_HEAD_DIM), lw['swa_q_norm'])
    k = rms_norm(k_s.reshape(B, T, SWA_KV_HEADS, SWA_HEAD_DIM), lw['swa_k_norm'])
    v = v_s.reshape(B, T, SWA_KV_HEADS, SWA_HEAD_DIM)
    if past_k is None:
        o_s = swa_prompt(q, k, v, lw['swa_sinks'], t5_table)
        keep = min(SWA_WINDOW, T)
        new_k, new_v = k[:, T - keep:], v[:, T - keep:]
    else:
        o_s = swa_sample(q, k, v, past_k, past_v, lw['swa_sinks'], t5_table)
        new_k, new_v = k, v
    mix = jnp.concatenate([y_ssm.astype(x.dtype), o_g.astype(x.dtype), o_s.astype(x.dtype)], axis=-1)
    out = x + (mix @ lw['w_out']).astype(x.dtype)
    return out, new_k, new_v, s_g.astype(x.dtype), h_re.astype(x.dtype), h_im.astype(x.dtype)


def hier_moe(xn, w_group, b_group, w_expert, b_expert, w_gate, w_up, w_down):
    f32 = jnp.float32
    n_tok = xn.shape[0]
    lg = (xn @ w_group).astype(f32) + b_group.astype(f32)
    p_group = jax.nn.softmax(lg, axis=-1)
    g_idx = jnp.argmax(lg, axis=-1)
    p_sel = jnp.take_along_axis(p_group, g_idx[:, None], axis=-1)
    le = ((xn @ w_expert).astype(f32) + b_expert.astype(f32)).reshape(n_tok, N_EXPERT_GROUPS, EXPERTS_PER_GROUP)
    le_sel = jnp.take_along_axis(le, g_idx[:, None, None], axis=1)[:, 0]
    top_v, top_i = lax.top_k(jax.nn.softmax(le_sel, axis=-1), TOP_K)
    top_v = top_v / jnp.sum(top_v, axis=-1, keepdims=True)
    eid = g_idx[:, None] * EXPERTS_PER_GROUP + top_i
    combine = jnp.sum(jax.nn.one_hot(eid, N_EXPERTS, dtype=f32) * (p_sel * top_v)[..., None], axis=1)
    acc = jnp.zeros((n_tok, xn.shape[1]), f32)
    for e in range(N_EXPERTS):
        h = jax.nn.silu(xn @ w_gate[e]) * (xn @ w_up[e])
        acc = acc + combine[:, e:e + 1] * (h @ w_down[e]).astype(f32)
    return acc


def moe_sublayer(x, lw):
    B, T, D = x.shape
    xn = rms_norm(x, lw['norm_ffn']).reshape(B * T, D)
    y = hier_moe(xn, lw['moe_w_group'], lw['moe_b_group'], lw['moe_w_expert'], lw['moe_b_expert'],
                 lw['moe_w_gate'], lw['moe_w_up'], lw['moe_w_down'])
    return x + y.reshape(B, T, D).astype(x.dtype)


def setup_inputs(seed: int = 0) -> dict:
    key = jax.random.key(seed)
    ks = jax.random.split(key, 40)
    f32 = jnp.float32
    nrm = lambda k, shape, scale: scale * jax.random.normal(k, shape, f32)
    n_win = min(SWA_WINDOW, PAST_LEN)
    a_im0 = jnp.pi * jnp.arange(SSM_STATE, dtype=f32)
    return {
        'x_prompt': nrm(ks[0], (BATCH, SEQ, D_MODEL), 1.0),
        'x_sample': nrm(ks[1], (DEC_BATCH, DEC_SEQ, D_MODEL), 1.0),
        'cache_swa_k': nrm(ks[2], (DEPTH, DEC_BATCH, n_win, SWA_KV_HEADS, SWA_HEAD_DIM), 1.0),
        'cache_swa_v': nrm(ks[3], (DEPTH, DEC_BATCH, n_win, SWA_KV_HEADS, SWA_HEAD_DIM), 1.0),
        'state_gla': nrm(ks[4], (DEPTH, DEC_BATCH, GLA_HEADS, GLA_DK, GLA_DV), 0.5),
        'state_ssm_re': nrm(ks[5], (DEPTH, DEC_BATCH, SSM_GROUPS, SSM_STATE), 0.5),
        'state_ssm_im': nrm(ks[6], (DEPTH, DEC_BATCH, SSM_GROUPS, SSM_STATE), 0.5),
        'norm_mix': 1.0 + nrm(ks[7], (DEPTH, D_MODEL), 0.05),
        'w_in': nrm(ks[8], (DEPTH, D_MODEL, IN_W), D_MODEL ** -0.5),
        'ssm_a_re': -0.5 + nrm(ks[9], (DEPTH, SSM_GROUPS, SSM_STATE), 0.01),
        'ssm_a_im': a_im0 + nrm(ks[10], (DEPTH, SSM_GROUPS, SSM_STATE), 0.01),
        'ssm_log_dt': jax.random.uniform(ks[11], (DEPTH, SSM_GROUPS), f32, math.log(1e-3), math.log(1e-1)),
        'ssm_b_re': nrm(ks[12], (DEPTH, SSM_GROUPS, SSM_STATE, SSM_GC), (2 * SSM_GC) ** -0.5),
        'ssm_b_im': nrm(ks[13], (DEPTH, SSM_GROUPS, SSM_STATE, SSM_GC), (2 * SSM_GC) ** -0.5),
        'ssm_c_re': nrm(ks[14], (DEPTH, SSM_GROUPS, SSM_GC, SSM_STATE), SSM_STATE ** -0.5),
        'ssm_c_im': nrm(ks[15], (DEPTH, SSM_GROUPS, SSM_GC, SSM_STATE), SSM_STATE ** -0.5),
        'ssm_d': nrm(ks[16], (DEPTH, SSM_W), 1.0),
        'ssm_w_glu': nrm(ks[17], (DEPTH, SSM_W, SSM_W), SSM_W ** -0.5),
        'gla_w_gate': nrm(ks[18], (DEPTH, GLA_GATE_RANK, GLA_HEADS * GLA_DK), GLA_GATE_RANK ** -0.5),
        'gla_b_gate': nrm(ks[19], (DEPTH, GLA_HEADS * GLA_DK), 0.1),
        'gla_norm': 1.0 + nrm(ks[20], (DEPTH, GLA_DV), 0.05),
        'swa_q_norm': 1.0 + nrm(ks[21], (DEPTH, SWA_HEAD_DIM), 0.05),
        'swa_k_norm': 1.0 + nrm(ks[22], (DEPTH, SWA_HEAD_DIM), 0.05),
        'swa_sinks': nrm(ks[23], (DEPTH, SWA_HEADS), 0.5),
        't5_table': nrm(ks[24], (N_BUCKETS, SWA_HEADS), 0.5),
        'w_out': nrm(ks[25], (DEPTH, MIX_W, D_MODEL), MIX_W ** -0.5),
        'norm_ffn': 1.0 + nrm(ks[26], (DEPTH, D_MODEL), 0.05),
        'moe_w_group': nrm(ks[27], (DEPTH, D_MODEL, N_EXPERT_GROUPS), D_MODEL ** -0.5),
        'moe_b_group': nrm(ks[28], (DEPTH, N_EXPERT_GROUPS), 0.01),
        'moe_w_expert': nrm(ks[29], (DEPTH, D_MODEL, N_EXPERTS), D_MODEL ** -0.5),
        'moe_b_expert': nrm(ks[30], (DEPTH, N_EXPERTS), 0.01),
        'moe_w_gate': nrm(ks[31], (DEPTH, N_EXPERTS, D_MODEL, EXPERT_FF), D_MODEL ** -0.5),
        'moe_w_up': nrm(ks[32], (DEPTH, N_EXPERTS, D_MODEL, EXPERT_FF), D_MODEL ** -0.5),
        'moe_w_down': nrm(ks[33], (DEPTH, N_EXPERTS, EXPERT_FF, D_MODEL), EXPERT_FF ** -0.5),
    }


def reference(x_prompt, x_sample, cache_swa_k, cache_swa_v, state_gla, state_ssm_re, state_ssm_im,
              norm_mix, w_in, ssm_a_re, ssm_a_im, ssm_log_dt, ssm_b_re, ssm_b_im, ssm_c_re, ssm_c_im,
              ssm_d, ssm_w_glu, gla_w_gate, gla_b_gate, gla_norm, swa_q_norm, swa_k_norm, swa_sinks,
              t5_table, w_out, norm_ffn, moe_w_group, moe_b_group, moe_w_expert, moe_b_expert,
              moe_w_gate, moe_w_up, moe_w_down):
    f32 = jnp.float32
    bp = x_prompt.shape[0]
    hp, hs = x_prompt, x_sample
    kp_l, vp_l, ks_l, vs_l, gp_l, gs_l, srp_l, sip_l, srs_l, sis_l = [], [], [], [], [], [], [], [], [], []
    for l in range(DEPTH):
        lw = {
            'norm_mix': norm_mix[l], 'w_in': w_in[l], 'ssm_a_re': ssm_a_re[l], 'ssm_a_im': ssm_a_im[l],
            'ssm_log_dt': ssm_log_dt[l], 'ssm_b_re': ssm_b_re[l], 'ssm_b_im': ssm_b_im[l],
            'ssm_c_re': ssm_c_re[l], 'ssm_c_im': ssm_c_im[l], 'ssm_d': ssm_d[l], 'ssm_w_glu': ssm_w_glu[l],
            'gla_w_gate': gla_w_gate[l], 'gla_b_gate': gla_b_gate[l], 'gla_norm': gla_norm[l],
            'swa_q_norm': swa_q_norm[l], 'swa_k_norm': swa_k_norm[l], 'swa_sinks': swa_sinks[l],
            'w_out': w_out[l], 'norm_ffn': norm_ffn[l], 'moe_w_group': moe_w_group[l],
            'moe_b_group': moe_b_group[l], 'moe_w_expert': moe_w_expert[l], 'moe_b_expert': moe_b_expert[l],
            'moe_w_gate': moe_w_gate[l], 'moe_w_up': moe_w_up[l], 'moe_w_down': moe_w_down[l],
        }
        zs = jnp.zeros((bp, SSM_GROUPS, SSM_STATE), f32)
        zg = jnp.zeros((bp, GLA_HEADS, GLA_DK, GLA_DV), f32)
        hp, nk, nv, ng, nr, ni = mixing_sublayer(hp, lw, t5_table, zs, zs, zg, None, None)
        hp = moe_sublayer(hp, lw)
        kp_l.append(nk); vp_l.append(nv); gp_l.append(ng); srp_l.append(nr); sip_l.append(ni)
        hs, nk, nv, ng, nr, ni = mixing_sublayer(hs, lw, t5_table, state_ssm_re[l], state_ssm_im[l],
                                                 state_gla[l], cache_swa_k[l], cache_swa_v[l])
        hs = moe_sublayer(hs, lw)
        ks_l.append(nk); vs_l.append(nv); gs_l.append(ng); srs_l.append(nr); sis_l.append(ni)
    return (hp, hs, jnp.stack(kp_l), jnp.stack(vp_l), jnp.stack(ks_l), jnp.stack(vs_l),
            jnp.stack(gp_l), jnp.stack(gs_l), jnp.stack(srp_l), jnp.stack(sip_l),
            jnp.stack(srs_l), jnp.stack(sis_l))
```

```python
import functools
import math

import jax
import jax.numpy as jnp
from jax import lax
from jax.experimental import pallas as pl
from jax.experimental.pallas import tpu as pltpu

F32 = jnp.float32
BF16 = jnp.bfloat16

D_MODEL = 1024
CHUNK = 64
RMS_EPS = 1e-6
SSM_GROUPS = 16
SSM_GC = 16
SSM_STATE = 64
SSM_W = SSM_GROUPS * SSM_GC
SSM_FLAT = SSM_GROUPS * SSM_STATE
GLA_HEADS = 6
GLA_DK = 32
GLA_DV = 64
GLA_GATE_RANK = 16
GLA_GATE_NORM = 16.0
GLA_K = GLA_HEADS * GLA_DK
GLA_KP = 256
GLA_W = GLA_HEADS * GLA_DV
SWA_HEADS = 6
SWA_KV_HEADS = 2
SWA_REP = SWA_HEADS // SWA_KV_HEADS
SWA_HEAD_DIM = 64
SWA_WINDOW = 128
SWA_W = SWA_HEADS * SWA_HEAD_DIM
SWA_KV_W = SWA_KV_HEADS * SWA_HEAD_DIM
MIX_W = SSM_W + GLA_W + SWA_W
N_BUCKETS = 32
T5_MAX_DIST = 128
N_EXPERT_GROUPS = 4
EXPERTS_PER_GROUP = 4
N_EXPERTS = 16
EXPERT_FF = 512
LANES = 128
ROUTER_EXPERT_LANE0 = N_EXPERT_GROUPS

GLA_Q0, GLA_K0, GLA_V0, GLA_OG0, GLA_Z0, GLA_PW = 0, 256, 512, 896, 1280, 1408
SWA_PW = SWA_W + 2 * SWA_KV_W
P_GLA0, P_SWA0, P_SSM0, P_TOTAL = 0, GLA_PW, GLA_PW + SWA_PW, GLA_PW + SWA_PW + SSM_W

VMEM_LIMIT = 48 * 1024 * 1024


def _cparams(*sem):
    return pltpu.CompilerParams(dimension_semantics=sem, vmem_limit_bytes=VMEM_LIMIT)


def _dot(a, b):
    return jnp.dot(a, b, preferred_element_type=F32)


def _dot_nt(a, b):
    return lax.dot_general(a, b, (((1,), (1,)), ((), ())), preferred_element_type=F32)


def _dot_tn(a, b):
    return lax.dot_general(a, b, (((0,), (0,)), ((), ())), preferred_element_type=F32)


def _hi_lo(x):
    hi = x.astype(BF16)
    return hi, (x - hi.astype(F32)).astype(BF16)


def _dot_f32_rhs(a_bf16, x):
    hi, lo = _hi_lo(x)
    return _dot(a_bf16, hi) + _dot(a_bf16, lo)


def _dot_f32_lhs(x, b_bf16):
    hi, lo = _hi_lo(x)
    return _dot(hi, b_bf16) + _dot(lo, b_bf16)


def _log2(n):
    assert n & (n - 1) == 0
    return n.bit_length() - 1


def _sigmoid(x):
    return 1.0 / (1.0 + jnp.exp(-x))


def _const_spec(shape):
    nd = len(shape)
    return pl.BlockSpec(shape, lambda *_: (0,) * nd)


def _in_proj_body(x_ref, g_ref, w_ref, mavg_ref, qkg_ref, gla_ref, swa_ref, ssm_ref):
    x = x_ref[...]
    xn = x * lax.rsqrt(jnp.mean(x * x, axis=-1, keepdims=True) + RMS_EPS) * g_ref[...]
    xb = xn.astype(BF16)
    gla_ref[...] = _dot(xb, w_ref[:, P_GLA0:P_SWA0])
    ssm_ref[...] = _dot(xb, w_ref[:, P_SSM0:P_TOTAL])
    s = _dot(xb, w_ref[:, P_SWA0:P_SSM0])
    qk = s[:, :SWA_W + SWA_KV_W]
    ms = _dot_f32_lhs(qk * qk, mavg_ref[...])
    swa_ref[:, :SWA_W + SWA_KV_W] = qk * lax.rsqrt(ms + RMS_EPS) * qkg_ref[...]
    swa_ref[:, SWA_W + SWA_KV_W:] = s[:, SWA_W + SWA_KV_W:]


def _in_proj(x, g, w, mavg, qkg, tm):
    n = x.shape[0]
    return pl.pallas_call(
        _in_proj_body,
        grid=(n // tm,),
        in_specs=[pl.BlockSpec((tm, D_MODEL), lambda i: (i, 0)), _const_spec(g.shape), _const_spec(w.shape),
                  _const_spec(mavg.shape), _const_spec(qkg.shape)],
        out_specs=[pl.BlockSpec((tm, GLA_PW), lambda i: (i, 0)), pl.BlockSpec((tm, SWA_PW), lambda i: (i, 0)),
                   pl.BlockSpec((tm, SSM_W), lambda i: (i, 0))],
        out_shape=[jax.ShapeDtypeStruct((n, GLA_PW), F32), jax.ShapeDtypeStruct((n, SWA_PW), F32),
                   jax.ShapeDtypeStruct((n, SSM_W), F32)],
        compiler_params=_cparams("parallel"),
        name="in_proj",
    )(x, g, w, mavg, qkg)


def _ssm_body(u_ref, h0r_ref, h0i_ref, ab_ref, bd_ref, cd_ref, pwp_ref, pwn_ref, tril_ref, d_ref, wglu_ref,
              y_ref, hr_ref, hi_ref, carry_ref, *, n_sub, lc):
    @pl.when(pl.program_id(1) == 0)
    def _():
        carry_ref[0:1, :] = h0r_ref[0]
        carry_ref[1:2, :] = h0i_ref[0]

    hr = carry_ref[0:1, :]
    hi = carry_ref[1:2, :]
    ab_re = ab_ref[0:1, :]
    ab_im = ab_ref[1:2, :]
    tril = tril_ref[...]
    for c in range(n_sub):
        rows = slice(c * lc, (c + 1) * lc)
        u = u_ref[0, rows, :]
        bu = _dot(u.astype(BF16), bd_ref[...])
        bur, bui = bu[:, :SSM_FLAT], bu[:, SSM_FLAT:]
        nr, ni = pwn_ref[0], pwn_ref[1]
        sr = nr * bur - ni * bui
        si = nr * bui + ni * bur
        cr = _dot_f32_rhs(tril, sr) + (ab_re * hr - ab_im * hi)
        ci = _dot_f32_rhs(tril, si) + (ab_re * hi + ab_im * hr)
        pr, pi = pwp_ref[0], pwp_ref[1]
        h_r = pr * cr - pi * ci
        h_i = pr * ci + pi * cr
        hr = h_r[lc - 1:lc, :]
        hi = h_i[lc - 1:lc, :]
        hcat = jnp.concatenate([h_r, h_i], axis=1).astype(BF16)
        y = _dot(hcat, cd_ref[...]) + d_ref[...] * u
        g = 0.5 * y * (1.0 + jnp.tanh(math.sqrt(2.0 / math.pi) * (y + 0.044715 * (y * y * y))))
        y_ref[0, rows, :] = g * _sigmoid(_dot(g.astype(BF16), wglu_ref[...]))
    carry_ref[0:1, :] = hr
    carry_ref[1:2, :] = hi
    hr_ref[0] = hr
    hi_ref[0] = hi


def _ssm(u, h0r, h0i, sw, lblk, lc):
    b, t, _ = u.shape
    consts = [sw["ab"], sw["bd"], sw["cd"], sw["pwp"], sw["pwn"], sw["tril"], sw["d"], sw["wglu"]]
    state_spec = pl.BlockSpec((1, 1, SSM_FLAT), lambda i, j: (i, 0, 0))
    return pl.pallas_call(
        functools.partial(_ssm_body, n_sub=lblk // lc, lc=lc),
        grid=(b, t // lblk),
        in_specs=[pl.BlockSpec((1, lblk, SSM_W), lambda i, j: (i, j, 0)), state_spec, state_spec]
        + [_const_spec(c.shape) for c in consts],
        out_specs=[pl.BlockSpec((1, lblk, SSM_W), lambda i, j: (i, j, 0)), state_spec, state_spec],
        out_shape=[jax.ShapeDtypeStruct((b, t, SSM_W), F32), jax.ShapeDtypeStruct((b, 1, SSM_FLAT), F32),
                   jax.ShapeDtypeStruct((b, 1, SSM_FLAT), F32)],
        scratch_shapes=[pltpu.VMEM((8, SSM_FLAT), F32)],
        compiler_params=_cparams("parallel", "arbitrary"),
        name="ssm",
    )(u, h0r, h0i, *consts)


def _gla_body(p_ref, s0_ref, wg_ref, bg_ref, tril_ref, mavg_ref, gn_ref, o_ref, sfin_ref, s_ref, *, n_sub, lc):
    @pl.when(pl.program_id(1) == 0)
    def _():
        s_ref[...] = s0_ref[0]

    lane_k = lax.broadcasted_iota(jnp.int32, (1, GLA_KP), 1)
    lane_v = lax.broadcasted_iota(jnp.int32, (1, GLA_W), 1)
    head_k = [((lane_k >= h * GLA_DK) & (lane_k < (h + 1) * GLA_DK)).astype(F32) for h in range(GLA_HEADS)]
    head_v = [((lane_v >= h * GLA_DV) & (lane_v < (h + 1) * GLA_DV)).astype(F32) for h in range(GLA_HEADS)]
    row_v = lax.broadcasted_iota(jnp.int32, (GLA_W, GLA_KP), 0)
    col_k = lax.broadcasted_iota(jnp.int32, (GLA_W, GLA_KP), 1)
    same_head = ((row_v >> _log2(GLA_DV)) == (col_k >> _log2(GLA_DK))).astype(F32)
    row_t = lax.broadcasted_iota(jnp.int32, (GLA_HEADS * lc, lc), 0)
    col_s = lax.broadcasted_iota(jnp.int32, (GLA_HEADS * lc, lc), 1)
    causal = (row_t & (lc - 1)) >= col_s
    tril = tril_ref[...]
    st = s_ref[...]
    for c in range(n_sub):
        rows = slice(c * lc, (c + 1) * lc)
        q = p_ref[0, rows, GLA_Q0:GLA_Q0 + GLA_KP] * (GLA_DK ** -0.5)
        k = p_ref[0, rows, GLA_K0:GLA_K0 + GLA_KP]
        v = p_ref[0, rows, GLA_V0:GLA_V0 + GLA_W]
        og = p_ref[0, rows, GLA_OG0:GLA_OG0 + GLA_W]
        z = p_ref[0, rows, GLA_Z0:GLA_Z0 + LANES]
        gin = _dot(z.astype(BF16), wg_ref[...]) + bg_ref[...]
        glog = (jnp.minimum(gin, 0.0) - jnp.log(1.0 + jnp.exp(-jnp.abs(gin)))) / GLA_GATE_NORM
        gc = _dot_f32_rhs(tril, glog)
        gl = gc[lc - 1:lc, :]
        qe = q * jnp.exp(gc)
        ke = (k * jnp.exp(-gc)).astype(BF16)
        kd = (k * jnp.exp(gl - gc)).astype(BF16)
        vb = v.astype(BF16)
        qs = jnp.concatenate([qe * m for m in head_k], axis=0).astype(BF16)
        attn = jnp.where(causal, _dot_nt(qs, ke), 0.0)
        o2 = _dot(attn.astype(BF16), vb)
        o = _dot_nt(qe.astype(BF16), st.astype(BF16))
        for h in range(GLA_HEADS):
            o = o + head_v[h] * o2[h * lc:(h + 1) * lc, :]
        st = st * jnp.exp(gl) + _dot_tn(vb, kd) * same_head
        ms = _dot_f32_lhs(o * o, mavg_ref[...])
        on = o * lax.rsqrt(ms + RMS_EPS) * gn_ref[...]
        o_ref[0, rows, :] = on * (og * _sigmoid(og))
    s_ref[...] = st
    sfin_ref[0] = st


def _gla(p, s0, gw, lblk, lc):
    b, t, _ = p.shape
    consts = [gw["wg"], gw["bg"], gw["tril"], gw["mavg"], gw["gn"]]
    st_spec = pl.BlockSpec((1, GLA_W, GLA_KP), lambda i, j: (i, 0, 0))
    return pl.pallas_call(
        functools.partial(_gla_body, n_sub=lblk // lc, lc=lc),
        grid=(b, t // lblk),
        in_specs=[pl.BlockSpec((1, lblk, GLA_PW), lambda i, j: (i, j, 0)), st_spec]
        + [_const_spec(c.shape) for c in consts],
        out_specs=[pl.BlockSpec((1, lblk, GLA_W), lambda i, j: (i, j, 0)), st_spec],
        out_shape=[jax.ShapeDtypeStruct((b, t, GLA_W), F32), jax.ShapeDtypeStruct((b, GLA_W, GLA_KP), F32)],
        scratch_shapes=[pltpu.VMEM((GLA_W, GLA_KP), F32)],
        compiler_params=_cparams("parallel", "arbitrary"),
        name="gla",
    )(p, s0, *consts)


def _bias_body(tab_ref, bucket_ref, o_ref):
    bucket = bucket_ref[...]
    for h in range(SWA_HEADS):
        acc = jnp.zeros(bucket.shape, F32)
        for bkt in range(N_BUCKETS):
            acc = jnp.where(bucket == bkt, tab_ref[bkt, h], acc)
        o_ref[h] = acc


def _t5_bias(t5_table, q_pos, k_pos):
    rel = k_pos[None, :] - q_pos[:, None]
    half = N_BUCKETS // 2
    max_exact = half // 2
    n = jnp.abs(rel)
    far = max_exact + (jnp.log(jnp.maximum(n, 1).astype(F32) / max_exact)
                       / math.log(T5_MAX_DIST / max_exact) * (half - max_exact)).astype(jnp.int32)
    bucket = jnp.where(rel > 0, half, 0) + jnp.where(n < max_exact, n, jnp.minimum(far, half - 1))
    nq, nk = bucket.shape
    return pl.pallas_call(
        _bias_body,
        in_specs=[pl.BlockSpec(memory_space=pltpu.SMEM), pl.BlockSpec(memory_space=pltpu.VMEM)],
        out_specs=pl.BlockSpec(memory_space=pltpu.VMEM),
        out_shape=jax.ShapeDtypeStruct((SWA_HEADS, nq, nk), F32),
        name="t5_bias",
    )(t5_table.astype(F32), bucket.astype(jnp.int32))


def _swa_body(sink_ref, q_ref, kp_ref, vp_ref, kc_ref, vc_ref, bias_ref, o_ref, *, qb, cq, win, pos0):
    blk = pl.program_id(1)
    kwin = jnp.concatenate([kp_ref[0], kc_ref[0]], axis=0)
    vwin = jnp.concatenate([vp_ref[0], vc_ref[0]], axis=0)
    col = lax.broadcasted_iota(jnp.int32, (1, win), 1)
    row = lax.broadcasted_iota(jnp.int32, (SWA_REP * cq, 1), 0)
    for j in range(qb // cq):
        valid = (pos0 + blk * qb + j * cq + col) >= 0
        for g in range(SWA_KV_HEADS):
            heads = [SWA_REP * g + r for r in range(SWA_REP)]
            qs = jnp.concatenate(
                [q_ref[0, j * cq:(j + 1) * cq, h * SWA_HEAD_DIM:(h + 1) * SWA_HEAD_DIM] for h in heads], axis=0)
            kk = kwin[j * cq:j * cq + win, g * SWA_HEAD_DIM:(g + 1) * SWA_HEAD_DIM]
            vv = vwin[j * cq:j * cq + win, g * SWA_HEAD_DIM:(g + 1) * SWA_HEAD_DIM]
            s = _dot_nt(qs.astype(BF16), kk.astype(BF16)) * (SWA_HEAD_DIM ** -0.5) + bias_ref[g]
            s = jnp.where(valid, s, -1e30)
            sink = jnp.where(row < cq, sink_ref[heads[0]], jnp.where(row < 2 * cq, sink_ref[heads[1]], sink_ref[heads[2]]))
            m = jnp.maximum(jnp.max(s, axis=-1, keepdims=True), sink)
            e = jnp.exp(s - m)
            den = jnp.sum(e, axis=-1, keepdims=True) + jnp.exp(sink - m)
            o = _dot((e / den).astype(BF16), vv.astype(BF16))
            for r, h in enumerate(heads):
                o_ref[0, j * cq:(j + 1) * cq, h * SWA_HEAD_DIM:(h + 1) * SWA_HEAD_DIM] = o[r * cq:(r + 1) * cq, :]


def _swa(sinks, p_swa, k_prev, v_prev, prev_k_map, prev_v_map, bias, qb, cq, win, pos0):
    b, t, _ = p_swa.shape
    kcol, vcol = SWA_W // SWA_KV_W, SWA_W // SWA_KV_W + 1
    return pl.pallas_call(
        functools.partial(_swa_body, qb=qb, cq=cq, win=win, pos0=pos0),
        grid=(b, t // qb),
        in_specs=[pl.BlockSpec(memory_space=pltpu.SMEM),
                  pl.BlockSpec((1, qb, SWA_W), lambda i, j: (i, j, 0)),
                  pl.BlockSpec((1, SWA_WINDOW, SWA_KV_W), prev_k_map),
                  pl.BlockSpec((1, SWA_WINDOW, SWA_KV_W), prev_v_map),
                  pl.BlockSpec((1, qb, SWA_KV_W), lambda i, j: (i, j, kcol)),
                  pl.BlockSpec((1, qb, SWA_KV_W), lambda i, j: (i, j, vcol)),
                  _const_spec(bias.shape)],
        out_specs=pl.BlockSpec((1, qb, SWA_W), lambda i, j: (i, j, 0)),
        out_shape=jax.ShapeDtypeStruct((b, t, SWA_W), F32),
        compiler_params=_cparams("parallel", "parallel"),
        name="swa",
    )(sinks, p_swa, k_prev, v_prev, p_swa, p_swa, bias)


def _out_proj_body(x_ref, ys_ref, og_ref, os_ref, wo_ref, gf_ref, wrh_ref, wrl_ref, br_ref, x1_ref, xn_ref, comb_ref):
    x1 = (x_ref[...] + _dot(ys_ref[...].astype(BF16), wo_ref[0:SSM_W, :])
          + _dot(og_ref[...].astype(BF16), wo_ref[SSM_W:SSM_W + GLA_W, :])
          + _dot(os_ref[...].astype(BF16), wo_ref[SSM_W + GLA_W:MIX_W, :]))
    x1_ref[...] = x1
    xn = x1 * lax.rsqrt(jnp.mean(x1 * x1, axis=-1, keepdims=True) + RMS_EPS) * gf_ref[...]
    xn_ref[...] = xn.astype(BF16)
    xh, xl = _hi_lo(xn)
    logit = _dot(xh, wrh_ref[...]) + _dot(xh, wrl_ref[...]) + _dot(xl, wrh_ref[...]) + br_ref[...]
    lane_i = lax.broadcasted_iota(jnp.int32, logit.shape, 1)
    lane = lane_i.astype(F32)
    neg = -jnp.inf
    big = float(LANES)
    lg = jnp.where(lane_i < N_EXPERT_GROUPS, logit, neg)
    gmax = jnp.max(lg, axis=-1, keepdims=True)
    p_sel = 1.0 / jnp.sum(jnp.exp(lg - gmax), axis=-1, keepdims=True)
    g_idx = jnp.min(jnp.where(lg == gmax, lane, big), axis=-1, keepdims=True)
    e_lane = lane_i - ROUTER_EXPERT_LANE0
    lane_group = (e_lane >> _log2(EXPERTS_PER_GROUP)).astype(F32)
    in_group = (e_lane >= 0) & (e_lane < N_EXPERTS) & (lane_group == g_idx)
    le = jnp.where(in_group, logit, neg)
    m1 = jnp.max(le, axis=-1, keepdims=True)
    i1 = jnp.min(jnp.where(le == m1, lane, big), axis=-1, keepdims=True)
    le2 = jnp.where(lane == i1, neg, le)
    m2 = jnp.max(le2, axis=-1, keepdims=True)
    i2 = jnp.min(jnp.where(le2 == m2, lane, big), axis=-1, keepdims=True)
    r = jnp.exp(m2 - m1)
    w1 = 1.0 / (1.0 + r)
    comb_ref[...] = p_sel * (jnp.where(lane == i1, w1, 0.0) + jnp.where(lane == i2, r * w1, 0.0))


def _out_proj(x, ys, og, osw, wo, gf, wrh, wrl, br, tm):
    n = x.shape[0]
    row = lambda w: pl.BlockSpec((tm, w), lambda i: (i, 0))
    return pl.pallas_call(
        _out_proj_body,
        grid=(n // tm,),
        in_specs=[row(D_MODEL), row(SSM_W), row(GLA_W), row(SWA_W), _const_spec(wo.shape), _const_spec(gf.shape),
                  _const_spec(wrh.shape), _const_spec(wrl.shape), _const_spec(br.shape)],
        out_specs=[row(D_MODEL), row(D_MODEL), row(LANES)],
        out_shape=[jax.ShapeDtypeStruct((n, D_MODEL), F32), jax.ShapeDtypeStruct((n, D_MODEL), BF16),
                   jax.ShapeDtypeStruct((n, LANES), F32)],
        compiler_params=_cparams("parallel"),
        name="out_proj",
    )(x, ys, og, osw, wo, gf, wrh, wrl, br)


def _moe_body(xn_ref, comb_ref, x1_ref, wg_ref, wu_ref, wd_ref, o_ref, acc_ref):
    e = pl.program_id(1)

    @pl.when(e == 0)
    def _():
        acc_ref[...] = jnp.zeros_like(acc_ref)

    xb = xn_ref[...]
    a = _dot(xb, wg_ref[0])
    h = (a * _sigmoid(a)) * _dot(xb, wu_ref[0])
    y = _dot(h.astype(BF16), wd_ref[0])
    lane = lax.broadcasted_iota(jnp.int32, comb_ref.shape, 1)
    c = jnp.sum(jnp.where(lane == e + ROUTER_EXPERT_LANE0, comb_ref[...], 0.0), axis=-1, keepdims=True)
    acc_ref[...] += c * y

    @pl.when(e == N_EXPERTS - 1)
    def _():
        o_ref[...] = x1_ref[...] + acc_ref[...]


def _moe(xn, comb, x1, wg, wu, wd, tm):
    n = xn.shape[0]
    return pl.pallas_call(
        _moe_body,
        grid=(n // tm, N_EXPERTS),
        in_specs=[pl.BlockSpec((tm, D_MODEL), lambda i, e: (i, 0)), pl.BlockSpec((tm, LANES), lambda i, e: (i, 0)),
                  pl.BlockSpec((tm, D_MODEL), lambda i, e: (i, 0)),
                  pl.BlockSpec((1, D_MODEL, EXPERT_FF), lambda i, e: (e, 0, 0)),
                  pl.BlockSpec((1, D_MODEL, EXPERT_FF), lambda i, e: (e, 0, 0)),
                  pl.BlockSpec((1, EXPERT_FF, D_MODEL), lambda i, e: (e, 0, 0))],
        out_specs=pl.BlockSpec((tm, D_MODEL), lambda i, e: (i, 0)),
        out_shape=jax.ShapeDtypeStruct((n, D_MODEL), F32),
        scratch_shapes=[pltpu.VMEM((tm, D_MODEL), F32)],
        compiler_params=_cparams("parallel", "arbitrary"),
        name="moe",
    )(xn, comb, x1, wg, wu, wd)


def _seg_mean_matrix(width, seg):
    i = jnp.arange(width)
    return jnp.where((i[:, None] // seg) == (i[None, :] // seg), 1.0 / seg, 0.0).astype(BF16)


def _tril_ones(n):
    i = jnp.arange(n)
    return (i[:, None] >= i[None, :]).astype(BF16)


def _prep_layer(lw, lc_list):
    w_in = lw["w_in"].astype(F32)
    cols = {}
    off = 0
    for name, wdt in (("u", SSM_W), ("qg", GLA_K), ("kg", GLA_K), ("vg", GLA_W), ("z", GLA_GATE_RANK), ("og", GLA_W),
                      ("qs", SWA_W), ("ks", SWA_KV_W), ("vs", SWA_KV_W)):
        cols[name] = w_in[:, off:off + wdt]
        off += wdt
    zpad = lambda wdt: jnp.zeros((D_MODEL, wdt), F32)
    w_all = jnp.concatenate(
        [cols["qg"], zpad(GLA_KP - GLA_K), cols["kg"], zpad(GLA_KP - GLA_K), cols["vg"], cols["og"], cols["z"],
         zpad(LANES - GLA_GATE_RANK), cols["qs"], cols["ks"], cols["vs"], cols["u"]], axis=1).astype(BF16)
    out = {"norm_mix": lw["norm_mix"].astype(F32)[None, :], "w_all": w_all,
           "mavg_qk": _seg_mean_matrix(SWA_W + SWA_KV_W, SWA_HEAD_DIM),
           "qk_gain": jnp.concatenate([jnp.tile(lw["swa_q_norm"].astype(F32), SWA_HEADS),
                                       jnp.tile(lw["swa_k_norm"].astype(F32), SWA_KV_HEADS)])[None, :]}
    a_re = lw["ssm_a_re"].astype(F32)
    a_im = lw["ssm_a_im"].astype(F32)
    dt = jnp.exp(lw["ssm_log_dt"].astype(F32))[:, None]
    mag = jnp.exp(a_re * dt)
    ab_re = mag * jnp.cos(a_im * dt)
    ab_im = mag * jnp.sin(a_im * dt)
    den = a_re * a_re + a_im * a_im
    nr = ab_re - 1.0
    f_re = (nr * a_re + ab_im * a_im) / den
    f_im = (ab_im * a_re - nr * a_im) / den
    b_re = lw["ssm_b_re"].astype(F32)
    b_im = lw["ssm_b_im"].astype(F32)
    bb_re = f_re[..., None] * b_re - f_im[..., None] * b_im
    bb_im = f_re[..., None] * b_im + f_im[..., None] * b_re
    eye_g = jnp.eye(SSM_GROUPS, dtype=F32)
    blockdiag_in = lambda bb: jnp.einsum("gpc,gh->gchp", bb, eye_g).reshape(SSM_W, SSM_FLAT)
    blockdiag_out = lambda cc: jnp.einsum("gcp,gh->gphc", cc, eye_g).reshape(SSM_FLAT, SSM_W)
    bd = jnp.concatenate([blockdiag_in(bb_re), blockdiag_in(bb_im)], axis=1).astype(BF16)
    cd = jnp.concatenate([blockdiag_out(lw["ssm_c_re"].astype(F32)),
                          -blockdiag_out(lw["ssm_c_im"].astype(F32))], axis=0).astype(BF16)
    ssm = {"ab": jnp.concatenate([ab_re.reshape(1, SSM_FLAT), ab_im.reshape(1, SSM_FLAT),
                                  jnp.zeros((6, SSM_FLAT), F32)], axis=0),
           "bd": bd, "cd": cd, "d": lw["ssm_d"].astype(F32)[None, :], "wglu": lw["ssm_w_glu"].astype(BF16)}
    ssm_by_lc = {}
    for lc in lc_list:
        tt = jnp.arange(lc, dtype=F32)[:, None, None]
        def powers(sign):
            m = jnp.exp(sign * tt * (a_re * dt)[None])
            ang = sign * tt * (a_im * dt)[None]
            return jnp.stack([(m * jnp.cos(ang)).reshape(lc, SSM_FLAT), (m * jnp.sin(ang)).reshape(lc, SSM_FLAT)])
        ssm_by_lc[lc] = dict(ssm, pwp=powers(1.0), pwn=powers(-1.0), tril=_tril_ones(lc))
    out["ssm"] = ssm_by_lc
    wg = jnp.zeros((LANES, GLA_KP), F32).at[:GLA_GATE_RANK, :GLA_K].set(lw["gla_w_gate"].astype(F32)).astype(BF16)
    bg = jnp.zeros((1, GLA_KP), F32).at[0, :GLA_K].set(lw["gla_b_gate"].astype(F32))
    gla = {"wg": wg, "bg": bg, "mavg": _seg_mean_matrix(GLA_W, GLA_DV),
           "gn": jnp.tile(lw["gla_norm"].astype(F32), GLA_HEADS)[None, :]}
    out["gla"] = {lc: dict(gla, tril=_tril_ones(lc)) for lc in lc_list}
    out["sinks"] = lw["swa_sinks"].astype(F32)
    out["w_out"] = lw["w_out"].astype(BF16)
    out["norm_ffn"] = lw["norm_ffn"].astype(F32)[None, :]
    wr = jnp.zeros((D_MODEL, LANES), F32)
    wr = wr.at[:, :N_EXPERT_GROUPS].set(lw["moe_w_group"].astype(F32))
    wr = wr.at[:, ROUTER_EXPERT_LANE0:ROUTER_EXPERT_LANE0 + N_EXPERTS].set(lw["moe_w_expert"].astype(F32))
    out["wr_hi"], out["wr_lo"] = _hi_lo(wr)
    br = jnp.zeros((1, LANES), F32)
    br = br.at[0, :N_EXPERT_GROUPS].set(lw["moe_b_group"].astype(F32))
    br = br.at[0, ROUTER_EXPERT_LANE0:ROUTER_EXPERT_LANE0 + N_EXPERTS].set(lw["moe_b_expert"].astype(F32))
    out["br"] = br
    out["moe_wg"] = lw["moe_w_gate"].astype(BF16)
    out["moe_wu"] = lw["moe_w_up"].astype(BF16)
    out["moe_wd"] = lw["moe_w_down"].astype(BF16)
    return out


def _gla_state_in(h0):
    b = h0.shape[0]
    eye_h = jnp.eye(GLA_HEADS, dtype=F32)
    s = jnp.einsum("bhkv,hg->bhvgk", h0.astype(F32), eye_h).reshape(b, GLA_W, GLA_K)
    return jnp.pad(s, ((0, 0), (0, 0), (0, GLA_KP - GLA_K)))


def _gla_state_out(st):
    b = st.shape[0]
    s = st[:, :, :GLA_K].reshape(b, GLA_HEADS, GLA_DV, GLA_HEADS, GLA_DK)
    idx = jnp.arange(GLA_HEADS)
    return jnp.transpose(s[:, idx, :, idx, :], (1, 0, 3, 2))


def _layer(x, pw, bias, ssm_h0r, ssm_h0i, gla_h0, past_k, past_v, cfg):
    b, t, _ = x.shape
    n = b * t
    lc = cfg["lc"]
    p_gla, p_swa, p_ssm = _in_proj(x.reshape(n, D_MODEL), pw["norm_mix"], pw["w_all"], pw["mavg_qk"], pw["qk_gain"],
                                   cfg["tm"])
    p_gla = p_gla.reshape(b, t, GLA_PW)
    p_swa = p_swa.reshape(b, t, SWA_PW)
    y_ssm, h_re, h_im = _ssm(p_ssm.reshape(b, t, SSM_W), ssm_h0r.reshape(b, 1, SSM_FLAT).astype(F32),
                             ssm_h0i.reshape(b, 1, SSM_FLAT).astype(F32), pw["ssm"][lc], cfg["lblk"], lc)
    o_g, s_fin = _gla(p_gla, _gla_state_in(gla_h0), pw["gla"][lc], cfg["lblk"], lc)
    kcol, vcol = SWA_W // SWA_KV_W, SWA_W // SWA_KV_W + 1
    if past_k is None:
        per_blk = cfg["qb"] // SWA_WINDOW
        prev_k_map = lambda i, j: (i, jnp.maximum(j * per_blk - 1, 0), kcol)
        prev_v_map = lambda i, j: (i, jnp.maximum(j * per_blk - 1, 0), vcol)
        k_prev, v_prev = p_swa, p_swa
        keep = min(SWA_WINDOW, t)
        new_k = p_swa[:, t - keep:, SWA_W:SWA_W + SWA_KV_W]
        new_v = p_swa[:, t - keep:, SWA_W + SWA_KV_W:]
    else:
        prev_k_map = prev_v_map = lambda i, j: (i, 0, 0)
        k_prev = past_k.reshape(b, SWA_WINDOW, SWA_KV_W).astype(F32)
        v_prev = past_v.reshape(b, SWA_WINDOW, SWA_KV_W).astype(F32)
        new_k = p_swa[:, :, SWA_W:SWA_W + SWA_KV_W]
        new_v = p_swa[:, :, SWA_W + SWA_KV_W:]
    o_s = _swa(pw["sinks"], p_swa, k_prev, v_prev, prev_k_map, prev_v_map, bias, cfg["qb"], cfg["cq"], cfg["win"],
               cfg["pos0"])
    x1, xn, comb = _out_proj(x.reshape(n, D_MODEL), y_ssm.reshape(n, SSM_W), o_g.reshape(n, GLA_W),
                             o_s.reshape(n, SWA_W), pw["w_out"], pw["norm_ffn"], pw["wr_hi"], pw["wr_lo"], pw["br"],
                             cfg["tm"])
    x2 = _moe(xn, comb, x1, pw["moe_wg"], pw["moe_wu"], pw["moe_wd"], cfg["tm"])
    kv_shape = (b, new_k.shape[1], SWA_KV_HEADS, SWA_HEAD_DIM)
    return (x2.reshape(b, t, D_MODEL), new_k.reshape(kv_shape), new_v.reshape(kv_shape), _gla_state_out(s_fin),
            h_re.reshape(b, SSM_GROUPS, SSM_STATE), h_im.reshape(b, SSM_GROUPS, SSM_STATE))


def _group_cfg(t, past_len, n_past):
    if n_past is None:
        assert t % (4 * CHUNK) == 0
        cfg = dict(lc=CHUNK, lblk=4 * CHUNK, qb=4 * CHUNK, cq=CHUNK, win=SWA_WINDOW + CHUNK, pos0=-SWA_WINDOW, tm=512)
        q_pos = jnp.arange(CHUNK, dtype=jnp.int32) + SWA_WINDOW
        k_pos = jnp.arange(SWA_WINDOW + CHUNK, dtype=jnp.int32)
    else:
        first_key, last_q = past_len - n_past, past_len + t - 1
        assert n_past == SWA_WINDOW and t <= CHUNK and t % 8 == 0
        assert past_len // CHUNK == last_q // CHUNK and first_key // CHUNK >= past_len // CHUNK - SWA_WINDOW // CHUNK
        cfg = dict(lc=t, lblk=t, qb=t, cq=t, win=n_past + t, pos0=first_key, tm=512)
        q_pos = past_len + jnp.arange(t, dtype=jnp.int32)
        k_pos = first_key + jnp.arange(n_past + t, dtype=jnp.int32)
    return cfg, q_pos, k_pos


def _stacked_bias(t5_table, q_pos, k_pos):
    bias = _t5_bias(t5_table, q_pos, k_pos)
    nq, nk = bias.shape[1:]
    return bias.reshape(SWA_KV_HEADS, SWA_REP * nq, nk)


PAST_LEN = 1024


def kernel(x_prompt, x_sample, cache_swa_k, cache_swa_v, state_gla, state_ssm_re, state_ssm_im, norm_mix, w_in, ssm_a_re, ssm_a_im, ssm_log_dt, ssm_b_re, ssm_b_im, ssm_c_re, ssm_c_im, ssm_d, ssm_w_glu, gla_w_gate, gla_b_gate, gla_norm, swa_q_norm, swa_k_norm, swa_sinks, t5_table, w_out, norm_ffn, moe_w_group, moe_b_group, moe_w_expert, moe_b_expert, moe_w_gate, moe_w_up, moe_w_down):
    depth = w_in.shape[0]
    bp, tp, _ = x_prompt.shape
    bs, ts, _ = x_sample.shape
    cfg_p, qpos_p, kpos_p = _group_cfg(tp, 0, None)
    cfg_s, qpos_s, kpos_s = _group_cfg(ts, PAST_LEN, cache_swa_k.shape[2])
    bias_p = _stacked_bias(t5_table, qpos_p, kpos_p)
    bias_s = _stacked_bias(t5_table, qpos_s, kpos_s)
    hp, hs = x_prompt.astype(F32), x_sample.astype(F32)
    outs = [[] for _ in range(10)]
    for l in range(depth):
        lw = {
            "norm_mix": norm_mix[l], "w_in": w_in[l], "ssm_a_re": ssm_a_re[l], "ssm_a_im": ssm_a_im[l],
            "ssm_log_dt": ssm_log_dt[l], "ssm_b_re": ssm_b_re[l], "ssm_b_im": ssm_b_im[l], "ssm_c_re": ssm_c_re[l],
            "ssm_c_im": ssm_c_im[l], "ssm_d": ssm_d[l], "ssm_w_glu": ssm_w_glu[l], "gla_w_gate": gla_w_gate[l],
            "gla_b_gate": gla_b_gate[l], "gla_norm": gla_norm[l], "swa_q_norm": swa_q_norm[l],
            "swa_k_norm": swa_k_norm[l], "swa_sinks": swa_sinks[l], "w_out": w_out[l], "norm_ffn": norm_ffn[l],
            "moe_w_group": moe_w_group[l], "moe_b_group": moe_b_group[l], "moe_w_expert": moe_w_expert[l],
            "moe_b_expert": moe_b_expert[l], "moe_w_gate": moe_w_gate[l], "moe_w_up": moe_w_up[l],
            "moe_w_down": moe_w_down[l],
        }
        pw = _prep_layer(lw, sorted({cfg_p["lc"], cfg_s["lc"]}))
        zs = jnp.zeros((bp, SSM_GROUPS, SSM_STATE), F32)
        zg = jnp.zeros((bp, GLA_HEADS, GLA_DK, GLA_DV), F32)
        hp, nk, nv, ng, nr, ni = _layer(hp, pw, bias_p, zs, zs, zg, None, None, cfg_p)
        for slot, val in zip((0, 1, 4, 6, 7), (nk, nv, ng, nr, ni)):
            outs[slot].append(val)
        hs, nk, nv, ng, nr, ni = _layer(hs, pw, bias_s, state_ssm_re[l], state_ssm_im[l], state_gla[l],
                                        cache_swa_k[l], cache_swa_v[l], cfg_s)
        for slot, val in zip((2, 3, 5, 8, 9), (nk, nv, ng, nr, ni)):
            outs[slot].append(val)
    return (hp, hs) + tuple(jnp.stack(o) for o in outs)
```

```python
import functools
import math

import jax
import jax.numpy as jnp
from jax import lax
from jax.experimental import pallas as pl
from jax.experimental.pallas import tpu as pltpu

F32 = jnp.float32
BF16 = jnp.bfloat16

D_MODEL = 1024
CHUNK = 64
RMS_EPS = 1e-6
SSM_GROUPS = 16
SSM_GC = 16
SSM_STATE = 64
SSM_W = SSM_GROUPS * SSM_GC
SSM_FLAT = SSM_GROUPS * SSM_STATE
GLA_HEADS = 6
GLA_DK = 32
GLA_DV = 64
GLA_GATE_RANK = 16
GLA_GATE_NORM = 16.0
GLA_K = GLA_HEADS * GLA_DK
GLA_KP = 256
GLA_W = GLA_HEADS * GLA_DV
SWA_HEADS = 6
SWA_KV_HEADS = 2
SWA_REP = SWA_HEADS // SWA_KV_HEADS
SWA_HEAD_DIM = 64
SWA_WINDOW = 128
SWA_W = SWA_HEADS * SWA_HEAD_DIM
SWA_KV_W = SWA_KV_HEADS * SWA_HEAD_DIM
MIX_W = SSM_W + GLA_W + SWA_W
N_BUCKETS = 32
T5_MAX_DIST = 128
N_EXPERT_GROUPS = 4
EXPERTS_PER_GROUP = 4
N_EXPERTS = 16
EXPERT_FF = 512
LANES = 128
ROUTER_GROUP_LANE0 = N_EXPERTS
AUX_W1, AUX_W2, AUX_D1, AUX_D2 = 0, 1, 2, 3
MOE_TB = 512
MOE_PIECE = 16
MOE_RLOC = 2 * MOE_TB + N_EXPERTS * MOE_PIECE
MOE_PPB = MOE_RLOC // MOE_PIECE

GLA_Q0, GLA_K0, GLA_V0, GLA_OG0, GLA_Z0, GLA_PW = 0, 256, 512, 896, 1280, 1408
SWA_PW = SWA_W + 2 * SWA_KV_W
P_GLA0, P_SWA0, P_SSM0, P_TOTAL = 0, GLA_PW, GLA_PW + SWA_PW, GLA_PW + SWA_PW + SSM_W

VMEM_LIMIT = 48 * 1024 * 1024


def _cparams(*sem):
    return pltpu.CompilerParams(dimension_semantics=sem, vmem_limit_bytes=VMEM_LIMIT)


def _dot(a, b):
    return jnp.dot(a, b, preferred_element_type=F32)


def _dot_nt(a, b):
    return lax.dot_general(a, b, (((1,), (1,)), ((), ())), preferred_element_type=F32)


def _dot_tn(a, b):
    return lax.dot_general(a, b, (((0,), (0,)), ((), ())), preferred_element_type=F32)


def _hi_lo(x):
    hi = x.astype(BF16)
    return hi, (x - hi.astype(F32)).astype(BF16)


def _dot_f32_rhs(a_bf16, x):
    hi, lo = _hi_lo(x)
    return _dot(a_bf16, hi) + _dot(a_bf16, lo)


def _dot_f32_lhs(x, b_bf16):
    hi, lo = _hi_lo(x)
    return _dot(hi, b_bf16) + _dot(lo, b_bf16)


def _log2(n):
    assert n & (n - 1) == 0
    return n.bit_length() - 1


def _sigmoid(x):
    return 1.0 / (1.0 + jnp.exp(-x))


def _const_spec(shape):
    nd = len(shape)
    return pl.BlockSpec(shape, lambda *_: (0,) * nd)


def _in_proj_body(x_ref, g_ref, w_ref, mavg_ref, qkg_ref, gla_ref, swa_ref, ssm_ref):
    x = x_ref[...]
    xn = x * lax.rsqrt(jnp.mean(x * x, axis=-1, keepdims=True) + RMS_EPS) * g_ref[...]
    xb = xn.astype(BF16)
    gla_ref[...] = _dot(xb, w_ref[:, P_GLA0:P_SWA0])
    ssm_ref[...] = _dot(xb, w_ref[:, P_SSM0:P_TOTAL])
    s = _dot(xb, w_ref[:, P_SWA0:P_SSM0])
    qk = s[:, :SWA_W + SWA_KV_W]
    ms = _dot_f32_lhs(qk * qk, mavg_ref[...])
    swa_ref[:, :SWA_W + SWA_KV_W] = qk * lax.rsqrt(ms + RMS_EPS) * qkg_ref[...]
    swa_ref[:, SWA_W + SWA_KV_W:] = s[:, SWA_W + SWA_KV_W:]


def _in_proj(x, g, w, mavg, qkg, tm):
    n = x.shape[0]
    return pl.pallas_call(
        _in_proj_body,
        grid=(n // tm,),
        in_specs=[pl.BlockSpec((tm, D_MODEL), lambda i: (i, 0)), _const_spec(g.shape), _const_spec(w.shape),
                  _const_spec(mavg.shape), _const_spec(qkg.shape)],
        out_specs=[pl.BlockSpec((tm, GLA_PW), lambda i: (i, 0)), pl.BlockSpec((tm, SWA_PW), lambda i: (i, 0)),
                   pl.BlockSpec((tm, SSM_W), lambda i: (i, 0))],
        out_shape=[jax.ShapeDtypeStruct((n, GLA_PW), F32), jax.ShapeDtypeStruct((n, SWA_PW), F32),
                   jax.ShapeDtypeStruct((n, SSM_W), F32)],
        compiler_params=_cparams("parallel"),
        name="in_proj",
    )(x, g, w, mavg, qkg)


def _ssm_body(u_ref, h0r_ref, h0i_ref, ab_ref, bd_ref, cd_ref, pwp_ref, pwn_ref, tril_ref, d_ref, wglu_ref,
              y_ref, hr_ref, hi_ref, carry_ref, *, n_sub, lc):
    @pl.when(pl.program_id(1) == 0)
    def _():
        carry_ref[0:1, :] = h0r_ref[0]
        carry_ref[1:2, :] = h0i_ref[0]

    hr = carry_ref[0:1, :]
    hi = carry_ref[1:2, :]
    ab_re = ab_ref[0:1, :]
    ab_im = ab_ref[1:2, :]
    tril = tril_ref[...]
    for c in range(n_sub):
        rows = slice(c * lc, (c + 1) * lc)
        u = u_ref[0, rows, :]
        bu = _dot(u.astype(BF16), bd_ref[...])
        bur, bui = bu[:, :SSM_FLAT], bu[:, SSM_FLAT:]
        nr, ni = pwn_ref[0], pwn_ref[1]
        sr = nr * bur - ni * bui
        si = nr * bui + ni * bur
        cr = _dot_f32_rhs(tril, sr) + (ab_re * hr - ab_im * hi)
        ci = _dot_f32_rhs(tril, si) + (ab_re * hi + ab_im * hr)
        pr, pi = pwp_ref[0], pwp_ref[1]
        h_r = pr * cr - pi * ci
        h_i = pr * ci + pi * cr
        hr = h_r[lc - 1:lc, :]
        hi = h_i[lc - 1:lc, :]
        hcat = jnp.concatenate([h_r, h_i], axis=1).astype(BF16)
        y = _dot(hcat, cd_ref[...]) + d_ref[...] * u
        g = 0.5 * y * (1.0 + jnp.tanh(math.sqrt(2.0 / math.pi) * (y + 0.044715 * (y * y * y))))
        y_ref[0, rows, :] = g * _sigmoid(_dot(g.astype(BF16), wglu_ref[...]))
    carry_ref[0:1, :] = hr
    carry_ref[1:2, :] = hi
    hr_ref[0] = hr
    hi_ref[0] = hi


def _ssm(u, h0r, h0i, sw, lblk, lc):
    b, t, _ = u.shape
    consts = [sw["ab"], sw["bd"], sw["cd"], sw["pwp"], sw["pwn"], sw["tril"], sw["d"], sw["wglu"]]
    state_spec = pl.BlockSpec((1, 1, SSM_FLAT), lambda i, j: (i, 0, 0))
    return pl.pallas_call(
        functools.partial(_ssm_body, n_sub=lblk // lc, lc=lc),
        grid=(b, t // lblk),
        in_specs=[pl.BlockSpec((1, lblk, SSM_W), lambda i, j: (i, j, 0)), state_spec, state_spec]
        + [_const_spec(c.shape) for c in consts],
        out_specs=[pl.BlockSpec((1, lblk, SSM_W), lambda i, j: (i, j, 0)), state_spec, state_spec],
        out_shape=[jax.ShapeDtypeStruct((b, t, SSM_W), F32), jax.ShapeDtypeStruct((b, 1, SSM_FLAT), F32),
                   jax.ShapeDtypeStruct((b, 1, SSM_FLAT), F32)],
        scratch_shapes=[pltpu.VMEM((8, SSM_FLAT), F32)],
        compiler_params=_cparams("parallel", "arbitrary"),
        name="ssm",
    )(u, h0r, h0i, *consts)


def _gla_body(p_ref, s0_ref, wg_ref, bg_ref, tril_ref, mavg_ref, gn_ref, o_ref, sfin_ref, s_ref, *, n_sub, lc):
    @pl.when(pl.program_id(1) == 0)
    def _():
        s_ref[...] = s0_ref[0]

    lane_k = lax.broadcasted_iota(jnp.int32, (1, GLA_KP), 1)
    lane_v = lax.broadcasted_iota(jnp.int32, (1, GLA_W), 1)
    head_k = [((lane_k >= h * GLA_DK) & (lane_k < (h + 1) * GLA_DK)).astype(F32) for h in range(GLA_HEADS)]
    head_v = [((lane_v >= h * GLA_DV) & (lane_v < (h + 1) * GLA_DV)).astype(F32) for h in range(GLA_HEADS)]
    row_v = lax.broadcasted_iota(jnp.int32, (GLA_W, GLA_KP), 0)
    col_k = lax.broadcasted_iota(jnp.int32, (GLA_W, GLA_KP), 1)
    same_head = ((row_v >> _log2(GLA_DV)) == (col_k >> _log2(GLA_DK))).astype(F32)
    row_t = lax.broadcasted_iota(jnp.int32, (GLA_HEADS * lc, lc), 0)
    col_s = lax.broadcasted_iota(jnp.int32, (GLA_HEADS * lc, lc), 1)
    causal = (row_t & (lc - 1)) >= col_s
    tril = tril_ref[...]
    st = s_ref[...]
    for c in range(n_sub):
        rows = slice(c * lc, (c + 1) * lc)
        q = p_ref[0, rows, GLA_Q0:GLA_Q0 + GLA_KP] * (GLA_DK ** -0.5)
        k = p_ref[0, rows, GLA_K0:GLA_K0 + GLA_KP]
        v = p_ref[0, rows, GLA_V0:GLA_V0 + GLA_W]
        og = p_ref[0, rows, GLA_OG0:GLA_OG0 + GLA_W]
        z = p_ref[0, rows, GLA_Z0:GLA_Z0 + LANES]
        gin = _dot(z.astype(BF16), wg_ref[...]) + bg_ref[...]
        glog = (jnp.minimum(gin, 0.0) - jnp.log(1.0 + jnp.exp(-jnp.abs(gin)))) / GLA_GATE_NORM
        gc = _dot_f32_rhs(tril, glog)
        gl = gc[lc - 1:lc, :]
        qe = q * jnp.exp(gc)
        ke = (k * jnp.exp(-gc)).astype(BF16)
        kd = (k * jnp.exp(gl - gc)).astype(BF16)
        vb = v.astype(BF16)
        qs = jnp.concatenate([qe * m for m in head_k], axis=0).astype(BF16)
        attn = jnp.where(causal, _dot_nt(qs, ke), 0.0)
        o2 = _dot(attn.astype(BF16), vb)
        o = _dot_nt(qe.astype(BF16), st.astype(BF16))
        for h in range(GLA_HEADS):
            o = o + head_v[h] * o2[h * lc:(h + 1) * lc, :]
        st = st * jnp.exp(gl) + _dot_tn(vb, kd) * same_head
        ms = _dot_f32_lhs(o * o, mavg_ref[...])
        on = o * lax.rsqrt(ms + RMS_EPS) * gn_ref[...]
        o_ref[0, rows, :] = on * (og * _sigmoid(og))
    s_ref[...] = st
    sfin_ref[0] = st


def _gla(p, s0, gw, lblk, lc):
    b, t, _ = p.shape
    consts = [gw["wg"], gw["bg"], gw["tril"], gw["mavg"], gw["gn"]]
    st_spec = pl.BlockSpec((1, GLA_W, GLA_KP), lambda i, j: (i, 0, 0))
    return pl.pallas_call(
        functools.partial(_gla_body, n_sub=lblk // lc, lc=lc),
        grid=(b, t // lblk),
        in_specs=[pl.BlockSpec((1, lblk, GLA_PW), lambda i, j: (i, j, 0)), st_spec]
        + [_const_spec(c.shape) for c in consts],
        out_specs=[pl.BlockSpec((1, lblk, GLA_W), lambda i, j: (i, j, 0)), st_spec],
        out_shape=[jax.ShapeDtypeStruct((b, t, GLA_W), F32), jax.ShapeDtypeStruct((b, GLA_W, GLA_KP), F32)],
        scratch_shapes=[pltpu.VMEM((GLA_W, GLA_KP), F32)],
        compiler_params=_cparams("parallel", "arbitrary"),
        name="gla",
    )(p, s0, *consts)


def _bias_body(tab_ref, bucket_ref, o_ref):
    bucket = bucket_ref[...]
    for h in range(SWA_HEADS):
        acc = jnp.zeros(bucket.shape, F32)
        for bkt in range(N_BUCKETS):
            acc = jnp.where(bucket == bkt, tab_ref[bkt, h], acc)
        o_ref[h] = acc


def _t5_bias(t5_table, q_pos, k_pos):
    rel = k_pos[None, :] - q_pos[:, None]
    half = N_BUCKETS // 2
    max_exact = half // 2
    n = jnp.abs(rel)
    far = max_exact + (jnp.log(jnp.maximum(n, 1).astype(F32) / max_exact)
                       / math.log(T5_MAX_DIST / max_exact) * (half - max_exact)).astype(jnp.int32)
    bucket = jnp.where(rel > 0, half, 0) + jnp.where(n < max_exact, n, jnp.minimum(far, half - 1))
    nq, nk = bucket.shape
    return pl.pallas_call(
        _bias_body,
        in_specs=[pl.BlockSpec(memory_space=pltpu.SMEM), pl.BlockSpec(memory_space=pltpu.VMEM)],
        out_specs=pl.BlockSpec(memory_space=pltpu.VMEM),
        out_shape=jax.ShapeDtypeStruct((SWA_HEADS, nq, nk), F32),
        name="t5_bias",
    )(t5_table.astype(F32), bucket.astype(jnp.int32))


def _swa_body(sink_ref, q_ref, kp_ref, vp_ref, kc_ref, vc_ref, bias_ref, o_ref, *, qb, cq, win, pos0):
    blk = pl.program_id(1)
    kwin = jnp.concatenate([kp_ref[0], kc_ref[0]], axis=0)
    vwin = jnp.concatenate([vp_ref[0], vc_ref[0]], axis=0)
    col = lax.broadcasted_iota(jnp.int32, (1, win), 1)
    row = lax.broadcasted_iota(jnp.int32, (SWA_REP * cq, 1), 0)
    for j in range(qb // cq):
        valid = (pos0 + blk * qb + j * cq + col) >= 0
        for g in range(SWA_KV_HEADS):
            heads = [SWA_REP * g + r for r in range(SWA_REP)]
            qs = jnp.concatenate(
                [q_ref[0, j * cq:(j + 1) * cq, h * SWA_HEAD_DIM:(h + 1) * SWA_HEAD_DIM] for h in heads], axis=0)
            kk = kwin[j * cq:j * cq + win, g * SWA_HEAD_DIM:(g + 1) * SWA_HEAD_DIM]
            vv = vwin[j * cq:j * cq + win, g * SWA_HEAD_DIM:(g + 1) * SWA_HEAD_DIM]
            s = _dot_nt(qs.astype(BF16), kk.astype(BF16)) * (SWA_HEAD_DIM ** -0.5) + bias_ref[g]
            s = jnp.where(valid, s, -1e30)
            sink = jnp.where(row < cq, sink_ref[heads[0]], jnp.where(row < 2 * cq, sink_ref[heads[1]], sink_ref[heads[2]]))
            m = jnp.maximum(jnp.max(s, axis=-1, keepdims=True), sink)
            e = jnp.exp(s - m)
            den = jnp.sum(e, axis=-1, keepdims=True) + jnp.exp(sink - m)
            o = _dot((e / den).astype(BF16), vv.astype(BF16))
            for r, h in enumerate(heads):
                o_ref[0, j * cq:(j + 1) * cq, h * SWA_HEAD_DIM:(h + 1) * SWA_HEAD_DIM] = o[r * cq:(r + 1) * cq, :]


def _swa(sinks, p_swa, k_prev, v_prev, prev_k_map, prev_v_map, bias, qb, cq, win, pos0):
    b, t, _ = p_swa.shape
    kcol, vcol = SWA_W // SWA_KV_W, SWA_W // SWA_KV_W + 1
    return pl.pallas_call(
        functools.partial(_swa_body, qb=qb, cq=cq, win=win, pos0=pos0),
        grid=(b, t // qb),
        in_specs=[pl.BlockSpec(memory_space=pltpu.SMEM),
                  pl.BlockSpec((1, qb, SWA_W), lambda i, j: (i, j, 0)),
                  pl.BlockSpec((1, SWA_WINDOW, SWA_KV_W), prev_k_map),
                  pl.BlockSpec((1, SWA_WINDOW, SWA_KV_W), prev_v_map),
                  pl.BlockSpec((1, qb, SWA_KV_W), lambda i, j: (i, j, kcol)),
                  pl.BlockSpec((1, qb, SWA_KV_W), lambda i, j: (i, j, vcol)),
                  _const_spec(bias.shape)],
        out_specs=pl.BlockSpec((1, qb, SWA_W), lambda i, j: (i, j, 0)),
        out_shape=jax.ShapeDtypeStruct((b, t, SWA_W), F32),
        compiler_params=_cparams("parallel", "parallel"),
        name="swa",
    )(sinks, p_swa, k_prev, v_prev, p_swa, p_swa, bias)


def _out_proj_body(x_ref, ys_ref, og_ref, os_ref, wo_ref, gf_ref, wrh_ref, wrl_ref, br_ref, x1_ref, xn_ref, sel_ref,
                   aux_ref):
    x1 = (x_ref[...] + _dot(ys_ref[...].astype(BF16), wo_ref[0:SSM_W, :])
          + _dot(og_ref[...].astype(BF16), wo_ref[SSM_W:SSM_W + GLA_W, :])
          + _dot(os_ref[...].astype(BF16), wo_ref[SSM_W + GLA_W:MIX_W, :]))
    x1_ref[...] = x1
    xn = x1 * lax.rsqrt(jnp.mean(x1 * x1, axis=-1, keepdims=True) + RMS_EPS) * gf_ref[...]
    xn_ref[...] = xn.astype(BF16)
    xh, xl = _hi_lo(xn)
    logit = _dot(xh, wrh_ref[...]) + _dot(xh, wrl_ref[...]) + _dot(xl, wrh_ref[...]) + br_ref[...]
    lane_i = lax.broadcasted_iota(jnp.int32, logit.shape, 1)
    lane = lane_i.astype(F32)
    neg = -jnp.inf
    big = float(LANES)
    g_lane = lane_i - ROUTER_GROUP_LANE0
    lg = jnp.where((g_lane >= 0) & (g_lane < N_EXPERT_GROUPS), logit, neg)
    gmax = jnp.max(lg, axis=-1, keepdims=True)
    p_sel = 1.0 / jnp.sum(jnp.exp(lg - gmax), axis=-1, keepdims=True)
    g_idx = jnp.min(jnp.where(lg == gmax, g_lane.astype(F32), big), axis=-1, keepdims=True)
    lane_group = (lane_i >> _log2(EXPERTS_PER_GROUP)).astype(F32)
    in_group = (lane_i < N_EXPERTS) & (lane_group == g_idx)
    le = jnp.where(in_group, logit, neg)
    m1 = jnp.max(le, axis=-1, keepdims=True)
    i1 = jnp.min(jnp.where(le == m1, lane, big), axis=-1, keepdims=True)
    le2 = jnp.where(lane == i1, neg, le)
    m2 = jnp.max(le2, axis=-1, keepdims=True)
    i2 = jnp.min(jnp.where(le2 == m2, lane, big), axis=-1, keepdims=True)
    r = jnp.exp(m2 - m1)
    w1 = 1.0 / (1.0 + r)
    aux_ref[...] = p_sel * (jnp.where(lane_i == AUX_W1, w1, 0.0) + jnp.where(lane_i == AUX_W2, r * w1, 0.0))
    sel = jnp.where(lane == i1, 1.0, 0.0) + jnp.where((lane - float(N_EXPERTS)) == i2, 1.0, 0.0)
    sel_ref[...] = jnp.transpose(sel)[:2 * N_EXPERTS, :]


def _out_proj(x, ys, og, osw, wo, gf, wrh, wrl, br, tm):
    n = x.shape[0]
    row = lambda w: pl.BlockSpec((tm, w), lambda i: (i, 0))
    return pl.pallas_call(
        _out_proj_body,
        grid=(n // tm,),
        in_specs=[row(D_MODEL), row(SSM_W), row(GLA_W), row(SWA_W), _const_spec(wo.shape), _const_spec(gf.shape),
                  _const_spec(wrh.shape), _const_spec(wrl.shape), _const_spec(br.shape)],
        out_specs=[row(D_MODEL), row(D_MODEL), pl.BlockSpec((2 * N_EXPERTS, tm), lambda i: (0, i)), row(LANES)],
        out_shape=[jax.ShapeDtypeStruct((n, D_MODEL), F32), jax.ShapeDtypeStruct((n, D_MODEL), BF16),
                   jax.ShapeDtypeStruct((2 * N_EXPERTS, n), F32), jax.ShapeDtypeStruct((n, LANES), F32)],
        compiler_params=_cparams("parallel"),
        name="out_proj",
    )(x, ys, og, osw, wo, gf, wrh, wrl, br)


def _moe_sort_body(xn_ref, sel_ref, lo_ref, triu_ref, aux_ref, xs_ref, auxo_ref):
    s1 = sel_ref[0:N_EXPERTS, :]
    s2 = sel_ref[N_EXPERTS:2 * N_EXPERTS, :]
    rank = _dot((s1 + s2).astype(BF16), triu_ref[...])
    pos = lo_ref[0][:, 0:1] + rank
    d1 = jnp.sum(s1 * pos, axis=0, keepdims=True)
    d2 = jnp.sum(s2 * pos, axis=0, keepdims=True)
    r = lax.broadcasted_iota(jnp.int32, (MOE_RLOC, MOE_TB), 0).astype(F32)
    perm = jnp.where((r == d1) | (r == d2), 1.0, 0.0).astype(BF16)
    xs_ref[...] = _dot(perm, xn_ref[...]).astype(BF16)
    row = lax.broadcasted_iota(jnp.int32, (LANES, MOE_TB), 0)
    dt = jnp.where(row == AUX_D1, d1, jnp.where(row == AUX_D2, d2, 0.0))
    auxo_ref[...] = aux_ref[...] + jnp.transpose(dt)


def _moe_sort(xn, sel, lo_rows, triu, aux):
    n = xn.shape[0]
    nblk = n // MOE_TB
    return pl.pallas_call(
        _moe_sort_body,
        grid=(nblk,),
        in_specs=[pl.BlockSpec((MOE_TB, D_MODEL), lambda i: (i, 0)), pl.BlockSpec((2 * N_EXPERTS, MOE_TB), lambda i: (0, i)),
                  pl.BlockSpec((1, N_EXPERTS, LANES), lambda i: (i, 0, 0)), _const_spec(triu.shape),
                  pl.BlockSpec((MOE_TB, LANES), lambda i: (i, 0))],
        out_specs=[pl.BlockSpec((MOE_RLOC, D_MODEL), lambda i: (i, 0)), pl.BlockSpec((MOE_TB, LANES), lambda i: (i, 0))],
        out_shape=[jax.ShapeDtypeStruct((nblk * MOE_RLOC, D_MODEL), BF16), jax.ShapeDtypeStruct((n, LANES), F32)],
        compiler_params=_cparams("parallel"),
        name="moe_sort",
    )(xn, sel, lo_rows, triu, aux)


def _piece_copy(hbm_ref, piece, buf_ref, slot, p, sem, to_hbm):
    rows = pl.ds(pl.multiple_of(piece * MOE_PIECE, MOE_PIECE), MOE_PIECE)
    vm = buf_ref.at[slot, pl.ds(pl.multiple_of(p * MOE_PIECE, MOE_PIECE), MOE_PIECE)]
    if to_hbm:
        return pltpu.make_async_copy(vm, hbm_ref.at[rows], sem)
    return pltpu.make_async_copy(hbm_ref.at[rows], vm, sem)


def _moe_expert_body(texp_ref, piece_ref, nv_ref, xs_hbm, wg_ref, wu_ref, wd_ref, yinit_hbm, ys_hbm,
                     xbuf, ybuf, sem_in, sem_out, *, pt, n_steps):
    del texp_ref, yinit_hbm
    i = pl.program_id(0)
    slot = i % 2

    def for_pieces(tile, s, hbm_ref, buf_ref, sem, to_hbm, wait):
        def body(p, carry):
            c = _piece_copy(hbm_ref, piece_ref[tile * pt + p], buf_ref, s, p, sem.at[s], to_hbm)
            c.wait() if wait else c.start()
            return carry
        lax.fori_loop(0, nv_ref[tile], body, 0)

    gather = lambda tile, s, wait: for_pieces(tile, s, xs_hbm, xbuf, sem_in, False, wait)
    scatter = lambda tile, s, wait: for_pieces(tile, s, ys_hbm, ybuf, sem_out, True, wait)

    @pl.when(i == 0)
    def _():
        xbuf[...] = jnp.zeros_like(xbuf)
        gather(0, 0, False)

    @pl.when(i + 1 < n_steps)
    def _():
        gather(i + 1, 1 - slot, False)

    gather(i, slot, True)

    @pl.when(nv_ref[i] > 0)
    def _():
        xb = xbuf[slot]
        a = _dot(xb, wg_ref[0])
        h = (a * _sigmoid(a)) * _dot(xb, wu_ref[0])
        ybuf[slot] = _dot(h.astype(BF16), wd_ref[0]).astype(BF16)

    scatter(i, slot, False)

    @pl.when(i >= 1)
    def _():
        scatter(i - 1, 1 - slot, True)

    @pl.when(i == n_steps - 1)
    def _():
        scatter(i, slot, True)


def _moe_experts(tile_expert, piece, nvalid, xs, wg, wu, wd, y_init, pt):
    n_steps = tile_expert.shape[0]
    tm = pt * MOE_PIECE
    wspec = lambda shape: pl.BlockSpec(shape, lambda i, te, pc, nv: (te[i], 0, 0))
    grid_spec = pltpu.PrefetchScalarGridSpec(
        num_scalar_prefetch=3,
        grid=(n_steps,),
        in_specs=[pl.BlockSpec(memory_space=pl.ANY), wspec((1, D_MODEL, EXPERT_FF)), wspec((1, D_MODEL, EXPERT_FF)),
                  wspec((1, EXPERT_FF, D_MODEL)), pl.BlockSpec(memory_space=pl.ANY)],
        out_specs=pl.BlockSpec(memory_space=pl.ANY),
        scratch_shapes=[pltpu.VMEM((2, tm, D_MODEL), BF16), pltpu.VMEM((2, tm, D_MODEL), BF16),
                        pltpu.SemaphoreType.DMA((2,)), pltpu.SemaphoreType.DMA((2,))],
    )
    return pl.pallas_call(
        functools.partial(_moe_expert_body, pt=pt, n_steps=n_steps),
        grid_spec=grid_spec,
        out_shape=jax.ShapeDtypeStruct(y_init.shape, BF16),
        input_output_aliases={7: 0},
        compiler_params=_cparams("arbitrary"),
        name="moe_experts",
    )(tile_expert, piece, nvalid, xs, wg, wu, wd, y_init)


def _moe_combine_body(ys_ref, aux_ref, x1_ref, o_ref):
    col = lax.broadcasted_iota(jnp.int32, (MOE_TB, MOE_RLOC), 1).astype(F32)
    y = ys_ref[...]
    pick = lambda lane: _dot(jnp.where(col == aux_ref[:, lane:lane + 1], 1.0, 0.0).astype(BF16), y)
    o_ref[...] = (x1_ref[...] + aux_ref[:, AUX_W1:AUX_W1 + 1] * pick(AUX_D1)
                  + aux_ref[:, AUX_W2:AUX_W2 + 1] * pick(AUX_D2))


def _moe_combine(ys, aux, x1):
    n = x1.shape[0]
    return pl.pallas_call(
        _moe_combine_body,
        grid=(n // MOE_TB,),
        in_specs=[pl.BlockSpec((MOE_RLOC, D_MODEL), lambda i: (i, 0)), pl.BlockSpec((MOE_TB, LANES), lambda i: (i, 0)),
                  pl.BlockSpec((MOE_TB, D_MODEL), lambda i: (i, 0))],
        out_specs=pl.BlockSpec((MOE_TB, D_MODEL), lambda i: (i, 0)),
        out_shape=jax.ShapeDtypeStruct((n, D_MODEL), F32),
        compiler_params=_cparams("parallel"),
        name="moe_combine",
    )(ys, aux, x1)


def _route_tables(sel, pt):
    n = sel.shape[1]
    nblk = n // MOE_TB
    cnt = (sel[:N_EXPERTS] + sel[N_EXPERTS:]).reshape(N_EXPERTS, nblk, MOE_TB).sum(-1).astype(jnp.int32)
    pc = (cnt + MOE_PIECE - 1) // MOE_PIECE
    lo_p = jnp.cumsum(pc, axis=0) - pc
    lo_rows = jnp.broadcast_to((lo_p.T * MOE_PIECE).astype(F32)[:, :, None], (nblk, N_EXPERTS, LANES))
    pe = pc.sum(1)
    tiles_e = (pe + pt - 1) // pt
    tile_start = jnp.cumsum(tiles_e) - tiles_e
    seg_start = (tile_start[:, None] * pt + jnp.cumsum(pc, axis=1) - pc).reshape(-1)
    pcs = pc.reshape(-1)
    seg_src = (jnp.arange(nblk, dtype=jnp.int32)[None, :] * MOE_PPB + lo_p).reshape(-1)
    n_steps = -(-(2 * n // MOE_PIECE + nblk * N_EXPERTS + N_EXPERTS * (pt - 1)) // pt)
    slot = jnp.arange(n_steps * pt, dtype=jnp.int32)
    seg = jnp.searchsorted(seg_start, slot, side="right").astype(jnp.int32) - 1
    j = slot - seg_start[seg]
    piece = jnp.where(j < pcs[seg], seg_src[seg] + j, 0).astype(jnp.int32)
    tile = jnp.arange(n_steps, dtype=jnp.int32)
    tile_expert = jnp.clip(jnp.searchsorted(tile_start, tile, side="right") - 1, 0, N_EXPERTS - 1).astype(jnp.int32)
    nvalid = jnp.where(tile < tiles_e.sum(), jnp.clip(pe[tile_expert] - (tile - tile_start[tile_expert]) * pt, 0, pt),
                       0).astype(jnp.int32)
    return lo_rows, tile_expert, piece, nvalid


def _moe(xn, sel, aux, x1, wg, wu, wd, triu, pt):
    n = xn.shape[0]
    lo_rows, tile_expert, piece, nvalid = _route_tables(sel, pt)
    xs, aux2 = _moe_sort(xn, sel, lo_rows, triu, aux)
    y_init = jnp.zeros(((n // MOE_TB) * MOE_RLOC, D_MODEL), BF16)
    ys = _moe_experts(tile_expert, piece, nvalid, xs, wg, wu, wd, y_init, pt)
    return _moe_combine(ys, aux2, x1)


def _seg_mean_matrix(width, seg):
    i = jnp.arange(width)
    return jnp.where((i[:, None] // seg) == (i[None, :] // seg), 1.0 / seg, 0.0).astype(BF16)


def _tril_ones(n):
    i = jnp.arange(n)
    return (i[:, None] >= i[None, :]).astype(BF16)


def _prep_layer(lw, lc_list):
    w_in = lw["w_in"].astype(F32)
    cols = {}
    off = 0
    for name, wdt in (("u", SSM_W), ("qg", GLA_K), ("kg", GLA_K), ("vg", GLA_W), ("z", GLA_GATE_RANK), ("og", GLA_W),
                      ("qs", SWA_W), ("ks", SWA_KV_W), ("vs", SWA_KV_W)):
        cols[name] = w_in[:, off:off + wdt]
        off += wdt
    zpad = lambda wdt: jnp.zeros((D_MODEL, wdt), F32)
    w_all = jnp.concatenate(
        [cols["qg"], zpad(GLA_KP - GLA_K), cols["kg"], zpad(GLA_KP - GLA_K), cols["vg"], cols["og"], cols["z"],
         zpad(LANES - GLA_GATE_RANK), cols["qs"], cols["ks"], cols["vs"], cols["u"]], axis=1).astype(BF16)
    out = {"norm_mix": lw["norm_mix"].astype(F32)[None, :], "w_all": w_all,
           "mavg_qk": _seg_mean_matrix(SWA_W + SWA_KV_W, SWA_HEAD_DIM),
           "qk_gain": jnp.concatenate([jnp.tile(lw["swa_q_norm"].astype(F32), SWA_HEADS),
                                       jnp.tile(lw["swa_k_norm"].astype(F32), SWA_KV_HEADS)])[None, :]}
    a_re = lw["ssm_a_re"].astype(F32)
    a_im = lw["ssm_a_im"].astype(F32)
    dt = jnp.exp(lw["ssm_log_dt"].astype(F32))[:, None]
    mag = jnp.exp(a_re * dt)
    ab_re = mag * jnp.cos(a_im * dt)
    ab_im = mag * jnp.sin(a_im * dt)
    den = a_re * a_re + a_im * a_im
    nr = ab_re - 1.0
    f_re = (nr * a_re + ab_im * a_im) / den
    f_im = (ab_im * a_re - nr * a_im) / den
    b_re = lw["ssm_b_re"].astype(F32)
    b_im = lw["ssm_b_im"].astype(F32)
    bb_re = f_re[..., None] * b_re - f_im[..., None] * b_im
    bb_im = f_re[..., None] * b_im + f_im[..., None] * b_re
    eye_g = jnp.eye(SSM_GROUPS, dtype=F32)
    blockdiag_in = lambda bb: jnp.einsum("gpc,gh->gchp", bb, eye_g).reshape(SSM_W, SSM_FLAT)
    blockdiag_out = lambda cc: jnp.einsum("gcp,gh->gphc", cc, eye_g).reshape(SSM_FLAT, SSM_W)
    bd = jnp.concatenate([blockdiag_in(bb_re), blockdiag_in(bb_im)], axis=1).astype(BF16)
    cd = jnp.concatenate([blockdiag_out(lw["ssm_c_re"].astype(F32)),
                          -blockdiag_out(lw["ssm_c_im"].astype(F32))], axis=0).astype(BF16)
    ssm = {"ab": jnp.concatenate([ab_re.reshape(1, SSM_FLAT), ab_im.reshape(1, SSM_FLAT),
                                  jnp.zeros((6, SSM_FLAT), F32)], axis=0),
           "bd": bd, "cd": cd, "d": lw["ssm_d"].astype(F32)[None, :], "wglu": lw["ssm_w_glu"].astype(BF16)}
    ssm_by_lc = {}
    for lc in lc_list:
        tt = jnp.arange(lc, dtype=F32)[:, None, None]
        def powers(sign):
            m = jnp.exp(sign * tt * (a_re * dt)[None])
            ang = sign * tt * (a_im * dt)[None]
            return jnp.stack([(m * jnp.cos(ang)).reshape(lc, SSM_FLAT), (m * jnp.sin(ang)).reshape(lc, SSM_FLAT)])
        ssm_by_lc[lc] = dict(ssm, pwp=powers(1.0), pwn=powers(-1.0), tril=_tril_ones(lc))
    out["ssm"] = ssm_by_lc
    wg = jnp.zeros((LANES, GLA_KP), F32).at[:GLA_GATE_RANK, :GLA_K].set(lw["gla_w_gate"].astype(F32)).astype(BF16)
    bg = jnp.zeros((1, GLA_KP), F32).at[0, :GLA_K].set(lw["gla_b_gate"].astype(F32))
    gla = {"wg": wg, "bg": bg, "mavg": _seg_mean_matrix(GLA_W, GLA_DV),
           "gn": jnp.tile(lw["gla_norm"].astype(F32), GLA_HEADS)[None, :]}
    out["gla"] = {lc: dict(gla, tril=_tril_ones(lc)) for lc in lc_list}
    out["sinks"] = lw["swa_sinks"].astype(F32)
    out["w_out"] = lw["w_out"].astype(BF16)
    out["norm_ffn"] = lw["norm_ffn"].astype(F32)[None, :]
    wr = jnp.zeros((D_MODEL, LANES), F32)
    wr = wr.at[:, :N_EXPERTS].set(lw["moe_w_expert"].astype(F32))
    wr = wr.at[:, ROUTER_GROUP_LANE0:ROUTER_GROUP_LANE0 + N_EXPERT_GROUPS].set(lw["moe_w_group"].astype(F32))
    out["wr_hi"], out["wr_lo"] = _hi_lo(wr)
    br = jnp.zeros((1, LANES), F32)
    br = br.at[0, :N_EXPERTS].set(lw["moe_b_expert"].astype(F32))
    br = br.at[0, ROUTER_GROUP_LANE0:ROUTER_GROUP_LANE0 + N_EXPERT_GROUPS].set(lw["moe_b_group"].astype(F32))
    out["br"] = br
    ti = jnp.arange(MOE_TB)
    out["moe_triu"] = (ti[:, None] < ti[None, :]).astype(BF16)
    out["moe_wg"] = lw["moe_w_gate"].astype(BF16)
    out["moe_wu"] = lw["moe_w_up"].astype(BF16)
    out["moe_wd"] = lw["moe_w_down"].astype(BF16)
    return out


def _gla_state_in(h0):
    b = h0.shape[0]
    eye_h = jnp.eye(GLA_HEADS, dtype=F32)
    s = jnp.einsum("bhkv,hg->bhvgk", h0.astype(F32), eye_h).reshape(b, GLA_W, GLA_K)
    return jnp.pad(s, ((0, 0), (0, 0), (0, GLA_KP - GLA_K)))


def _gla_state_out(st):
    b = st.shape[0]
    s = st[:, :, :GLA_K].reshape(b, GLA_HEADS, GLA_DV, GLA_HEADS, GLA_DK)
    idx = jnp.arange(GLA_HEADS)
    return jnp.transpose(s[:, idx, :, idx, :], (1, 0, 3, 2))


def _layer(x, pw, bias, ssm_h0r, ssm_h0i, gla_h0, past_k, past_v, cfg):
    b, t, _ = x.shape
    n = b * t
    lc = cfg["lc"]
    p_gla, p_swa, p_ssm = _in_proj(x.reshape(n, D_MODEL), pw["norm_mix"], pw["w_all"], pw["mavg_qk"], pw["qk_gain"],
                                   cfg["tm"])
    p_gla = p_gla.reshape(b, t, GLA_PW)
    p_swa = p_swa.reshape(b, t, SWA_PW)
    y_ssm, h_re, h_im = _ssm(p_ssm.reshape(b, t, SSM_W), ssm_h0r.reshape(b, 1, SSM_FLAT).astype(F32),
                             ssm_h0i.reshape(b, 1, SSM_FLAT).astype(F32), pw["ssm"][lc], cfg["lblk"], lc)
    o_g, s_fin = _gla(p_gla, _gla_state_in(gla_h0), pw["gla"][lc], cfg["lblk"], lc)
    kcol, vcol = SWA_W // SWA_KV_W, SWA_W // SWA_KV_W + 1
    if past_k is None:
        per_blk = cfg["qb"] // SWA_WINDOW
        prev_k_map = lambda i, j: (i, jnp.maximum(j * per_blk - 1, 0), kcol)
        prev_v_map = lambda i, j: (i, jnp.maximum(j * per_blk - 1, 0), vcol)
        k_prev, v_prev = p_swa, p_swa
        keep = min(SWA_WINDOW, t)
        new_k = p_swa[:, t - keep:, SWA_W:SWA_W + SWA_KV_W]
        new_v = p_swa[:, t - keep:, SWA_W + SWA_KV_W:]
    else:
        prev_k_map = prev_v_map = lambda i, j: (i, 0, 0)
        k_prev = past_k.reshape(b, SWA_WINDOW, SWA_KV_W).astype(F32)
        v_prev = past_v.reshape(b, SWA_WINDOW, SWA_KV_W).astype(F32)
        new_k = p_swa[:, :, SWA_W:SWA_W + SWA_KV_W]
        new_v = p_swa[:, :, SWA_W + SWA_KV_W:]
    o_s = _swa(pw["sinks"], p_swa, k_prev, v_prev, prev_k_map, prev_v_map, bias, cfg["qb"], cfg["cq"], cfg["win"],
               cfg["pos0"])
    x1, xn, sel, aux = _out_proj(x.reshape(n, D_MODEL), y_ssm.reshape(n, SSM_W), o_g.reshape(n, GLA_W),
                                 o_s.reshape(n, SWA_W), pw["w_out"], pw["norm_ffn"], pw["wr_hi"], pw["wr_lo"],
                                 pw["br"], cfg["tm"])
    x2 = _moe(xn, sel, aux, x1, pw["moe_wg"], pw["moe_wu"], pw["moe_wd"], pw["moe_triu"], cfg["pt"])
    kv_shape = (b, new_k.shape[1], SWA_KV_HEADS, SWA_HEAD_DIM)
    return (x2.reshape(b, t, D_MODEL), new_k.reshape(kv_shape), new_v.reshape(kv_shape), _gla_state_out(s_fin),
            h_re.reshape(b, SSM_GROUPS, SSM_STATE), h_im.reshape(b, SSM_GROUPS, SSM_STATE))


def _group_cfg(t, past_len, n_past):
    if n_past is None:
        assert t % (4 * CHUNK) == 0
        cfg = dict(lc=CHUNK, lblk=4 * CHUNK, qb=4 * CHUNK, cq=CHUNK, win=SWA_WINDOW + CHUNK, pos0=-SWA_WINDOW, tm=512,
                   pt=32)
        q_pos = jnp.arange(CHUNK, dtype=jnp.int32) + SWA_WINDOW
        k_pos = jnp.arange(SWA_WINDOW + CHUNK, dtype=jnp.int32)
    else:
        first_key, last_q = past_len - n_past, past_len + t - 1
        assert n_past == SWA_WINDOW and t <= CHUNK and t % 8 == 0
        assert past_len // CHUNK == last_q // CHUNK and first_key // CHUNK >= past_len // CHUNK - SWA_WINDOW // CHUNK
        cfg = dict(lc=t, lblk=t, qb=t, cq=t, win=n_past + t, pos0=first_key, tm=512, pt=16)
        q_pos = past_len + jnp.arange(t, dtype=jnp.int32)
        k_pos = first_key + jnp.arange(n_past + t, dtype=jnp.int32)
    return cfg, q_pos, k_pos


def _stacked_bias(t5_table, q_pos, k_pos):
    bias = _t5_bias(t5_table, q_pos, k_pos)
    nq, nk = bias.shape[1:]
    return bias.reshape(SWA_KV_HEADS, SWA_REP * nq, nk)


PAST_LEN = 1024


def kernel(x_prompt, x_sample, cache_swa_k, cache_swa_v, state_gla, state_ssm_re, state_ssm_im, norm_mix, w_in, ssm_a_re, ssm_a_im, ssm_log_dt, ssm_b_re, ssm_b_im, ssm_c_re, ssm_c_im, ssm_d, ssm_w_glu, gla_w_gate, gla_b_gate, gla_norm, swa_q_norm, swa_k_norm, swa_sinks, t5_table, w_out, norm_ffn, moe_w_group, moe_b_group, moe_w_expert, moe_b_expert, moe_w_gate, moe_w_up, moe_w_down):
    depth = w_in.shape[0]
    bp, tp, _ = x_prompt.shape
    bs, ts, _ = x_sample.shape
    cfg_p, qpos_p, kpos_p = _group_cfg(tp, 0, None)
    cfg_s, qpos_s, kpos_s = _group_cfg(ts, PAST_LEN, cache_swa_k.shape[2])
    bias_p = _stacked_bias(t5_table, qpos_p, kpos_p)
    bias_s = _stacked_bias(t5_table, qpos_s, kpos_s)
    hp, hs = x_prompt.astype(F32), x_sample.astype(F32)
    outs = [[] for _ in range(10)]
    for l in range(depth):
        lw = {
            "norm_mix": norm_mix[l], "w_in": w_in[l], "ssm_a_re": ssm_a_re[l], "ssm_a_im": ssm_a_im[l],
            "ssm_log_dt": ssm_log_dt[l], "ssm_b_re": ssm_b_re[l], "ssm_b_im": ssm_b_im[l], "ssm_c_re": ssm_c_re[l],
            "ssm_c_im": ssm_c_im[l], "ssm_d": ssm_d[l], "ssm_w_glu": ssm_w_glu[l], "gla_w_gate": gla_w_gate[l],
            "gla_b_gate": gla_b_gate[l], "gla_norm": gla_norm[l], "swa_q_norm": swa_q_norm[l],
            "swa_k_norm": swa_k_norm[l], "swa_sinks": swa_sinks[l], "w_out": w_out[l], "norm_ffn": norm_ffn[l],
            "moe_w_group": moe_w_group[l], "moe_b_group": moe_b_group[l], "moe_w_expert": moe_w_expert[l],
            "moe_b_expert": moe_b_expert[l], "moe_w_gate": moe_w_gate[l], "moe_w_up": moe_w_up[l],
            "moe_w_down": moe_w_down[l],
        }
        pw = _prep_layer(lw, sorted({cfg_p["lc"], cfg_s["lc"]}))
        zs = jnp.zeros((bp, SSM_GROUPS, SSM_STATE), F32)
        zg = jnp.zeros((bp, GLA_HEADS, GLA_DK, GLA_DV), F32)
        hp, nk, nv, ng, nr, ni = _layer(hp, pw, bias_p, zs, zs, zg, None, None, cfg_p)
        for slot, val in zip((0, 1, 4, 6, 7), (nk, nv, ng, nr, ni)):
            outs[slot].append(val)
        hs, nk, nv, ng, nr, ni = _layer(hs, pw, bias_s, state_ssm_re[l], state_ssm_im[l], state_gla[l],
                                        cache_swa_k[l], cache_swa_v[l], cfg_s)
        for slot, val in zip((2, 3, 5, 8, 9), (nk, nv, ng, nr, ni)):
            outs[slot].append(val)
    return (hp, hs) + tuple(jnp.stack(o) for o in outs)
```

```python
import functools
import math

import jax
import jax.numpy as jnp
from jax import lax
from jax.experimental import pallas as pl
from jax.experimental.pallas import tpu as pltpu

F32 = jnp.float32
BF16 = jnp.bfloat16

D_MODEL = 1024
CHUNK = 64
RMS_EPS = 1e-6
SSM_GROUPS = 16
SSM_GC = 16
SSM_STATE = 64
SSM_W = SSM_GROUPS * SSM_GC
SSM_FLAT = SSM_GROUPS * SSM_STATE
GLA_HEADS = 6
GLA_DK = 32
GLA_DV = 64
GLA_GATE_RANK = 16
GLA_GATE_NORM = 16.0
GLA_K = GLA_HEADS * GLA_DK
GLA_KP = 256
GLA_W = GLA_HEADS * GLA_DV
SWA_HEADS = 6
SWA_KV_HEADS = 2
SWA_REP = SWA_HEADS // SWA_KV_HEADS
SWA_HEAD_DIM = 64
SWA_WINDOW = 128
SWA_W = SWA_HEADS * SWA_HEAD_DIM
SWA_KV_W = SWA_KV_HEADS * SWA_HEAD_DIM
MIX_W = SSM_W + GLA_W + SWA_W
N_BUCKETS = 32
T5_MAX_DIST = 128
N_EXPERT_GROUPS = 4
EXPERTS_PER_GROUP = 4
N_EXPERTS = 16
EXPERT_FF = 512
LANES = 128
ROUTER_GROUP_LANE0 = N_EXPERTS
AUX_W1, AUX_W2, AUX_D1, AUX_D2 = 0, 1, 2, 3
MOE_TB = 512
MOE_PIECE = 16
MOE_RLOC = 2 * MOE_TB + N_EXPERTS * MOE_PIECE
MOE_PPB = MOE_RLOC // MOE_PIECE

GLA_Q0, GLA_K0, GLA_V0, GLA_OG0, GLA_Z0, GLA_PW = 0, 256, 512, 896, 1280, 1408
SWA_PW = SWA_W + 2 * SWA_KV_W
P_GLA0, P_SWA0, P_SSM0, P_TOTAL = 0, GLA_PW, GLA_PW + SWA_PW, GLA_PW + SWA_PW + SSM_W

VMEM_LIMIT = 48 * 1024 * 1024


def _cparams(*sem):
    return pltpu.CompilerParams(dimension_semantics=sem, vmem_limit_bytes=VMEM_LIMIT)


def _dot(a, b):
    return jnp.dot(a, b, preferred_element_type=F32)


def _dot_nt(a, b):
    return lax.dot_general(a, b, (((1,), (1,)), ((), ())), preferred_element_type=F32)


def _dot_tn(a, b):
    return lax.dot_general(a, b, (((0,), (0,)), ((), ())), preferred_element_type=F32)


def _hi_lo(x):
    hi = x.astype(BF16)
    return hi, (x - hi.astype(F32)).astype(BF16)


def _dot_f32_rhs(a_bf16, x):
    hi, lo = _hi_lo(x)
    return _dot(a_bf16, hi) + _dot(a_bf16, lo)


def _dot_f32_lhs(x, b_bf16):
    hi, lo = _hi_lo(x)
    return _dot(hi, b_bf16) + _dot(lo, b_bf16)


def _log2(n):
    assert n & (n - 1) == 0
    return n.bit_length() - 1


def _sigmoid(x):
    return 1.0 / (1.0 + jnp.exp(-x))


def _const_spec(shape):
    nd = len(shape)
    return pl.BlockSpec(shape, lambda *_: (0,) * nd)


def _in_proj_body(x_ref, g_ref, w_ref, mavg_ref, qkg_ref, gla_ref, swa_ref, ssm_ref):
    x = x_ref[...]
    xn = x * lax.rsqrt(jnp.mean(x * x, axis=-1, keepdims=True) + RMS_EPS) * g_ref[...]
    xb = xn.astype(BF16)
    gla_ref[...] = _dot(xb, w_ref[:, P_GLA0:P_SWA0])
    ssm_ref[...] = _dot(xb, w_ref[:, P_SSM0:P_TOTAL])
    s = _dot(xb, w_ref[:, P_SWA0:P_SSM0])
    qk = s[:, :SWA_W + SWA_KV_W]
    ms = _dot_f32_lhs(qk * qk, mavg_ref[...])
    swa_ref[:, :SWA_W + SWA_KV_W] = qk * lax.rsqrt(ms + RMS_EPS) * qkg_ref[...]
    swa_ref[:, SWA_W + SWA_KV_W:] = s[:, SWA_W + SWA_KV_W:]


def _in_proj(x, g, w, mavg, qkg, tm):
    n = x.shape[0]
    return pl.pallas_call(
        _in_proj_body,
        grid=(n // tm,),
        in_specs=[pl.BlockSpec((tm, D_MODEL), lambda i: (i, 0)), _const_spec(g.shape), _const_spec(w.shape),
                  _const_spec(mavg.shape), _const_spec(qkg.shape)],
        out_specs=[pl.BlockSpec((tm, GLA_PW), lambda i: (i, 0)), pl.BlockSpec((tm, SWA_PW), lambda i: (i, 0)),
                   pl.BlockSpec((tm, SSM_W), lambda i: (i, 0))],
        out_shape=[jax.ShapeDtypeStruct((n, GLA_PW), F32), jax.ShapeDtypeStruct((n, SWA_PW), F32),
                   jax.ShapeDtypeStruct((n, SSM_W), F32)],
        compiler_params=_cparams("parallel"),
        name="in_proj",
    )(x, g, w, mavg, qkg)


def _ssm_body(u_ref, h0r_ref, h0i_ref, ab_ref, bd_ref, cd_ref, pwp_ref, pwn_ref, tril_ref, d_ref, wglu_ref,
              y_ref, hr_ref, hi_ref, carry_ref, *, n_sub, lc):
    @pl.when(pl.program_id(1) == 0)
    def _():
        carry_ref[0:1, :] = h0r_ref[0]
        carry_ref[1:2, :] = h0i_ref[0]

    hr = carry_ref[0:1, :]
    hi = carry_ref[1:2, :]
    ab_re = ab_ref[0:1, :]
    ab_im = ab_ref[1:2, :]
    tril = tril_ref[...]
    for c in range(n_sub):
        rows = slice(c * lc, (c + 1) * lc)
        u = u_ref[0, rows, :]
        bu = _dot(u.astype(BF16), bd_ref[...])
        bur, bui = bu[:, :SSM_FLAT], bu[:, SSM_FLAT:]
        nr, ni = pwn_ref[0], pwn_ref[1]
        sr = nr * bur - ni * bui
        si = nr * bui + ni * bur
        cr = _dot_f32_rhs(tril, sr) + (ab_re * hr - ab_im * hi)
        ci = _dot_f32_rhs(tril, si) + (ab_re * hi + ab_im * hr)
        pr, pi = pwp_ref[0], pwp_ref[1]
        h_r = pr * cr - pi * ci
        h_i = pr * ci + pi * cr
        hr = h_r[lc - 1:lc, :]
        hi = h_i[lc - 1:lc, :]
        hcat = jnp.concatenate([h_r, h_i], axis=1).astype(BF16)
        y = _dot(hcat, cd_ref[...]) + d_ref[...] * u
        g = 0.5 * y * (1.0 + jnp.tanh(math.sqrt(2.0 / math.pi) * (y + 0.044715 * (y * y * y))))
        y_ref[0, rows, :] = g * _sigmoid(_dot(g.astype(BF16), wglu_ref[...]))
    carry_ref[0:1, :] = hr
    carry_ref[1:2, :] = hi
    hr_ref[0] = hr
    hi_ref[0] = hi


def _ssm(u, h0r, h0i, sw, lblk, lc):
    b, t, _ = u.shape
    consts = [sw["ab"], sw["bd"], sw["cd"], sw["pwp"], sw["pwn"], sw["tril"], sw["d"], sw["wglu"]]
    state_spec = pl.BlockSpec((1, 1, SSM_FLAT), lambda i, j: (i, 0, 0))
    return pl.pallas_call(
        functools.partial(_ssm_body, n_sub=lblk // lc, lc=lc),
        grid=(b, t // lblk),
        in_specs=[pl.BlockSpec((1, lblk, SSM_W), lambda i, j: (i, j, 0)), state_spec, state_spec]
        + [_const_spec(c.shape) for c in consts],
        out_specs=[pl.BlockSpec((1, lblk, SSM_W), lambda i, j: (i, j, 0)), state_spec, state_spec],
        out_shape=[jax.ShapeDtypeStruct((b, t, SSM_W), F32), jax.ShapeDtypeStruct((b, 1, SSM_FLAT), F32),
                   jax.ShapeDtypeStruct((b, 1, SSM_FLAT), F32)],
        scratch_shapes=[pltpu.VMEM((8, SSM_FLAT), F32)],
        compiler_params=_cparams("parallel", "arbitrary"),
        name="ssm",
    )(u, h0r, h0i, *consts)


def _gla_body(p_ref, s0_ref, wg_ref, bg_ref, tril_ref, mavg_ref, gn_ref, o_ref, sfin_ref, s_ref, *, n_sub, lc):
    @pl.when(pl.program_id(1) == 0)
    def _():
        s_ref[...] = s0_ref[0]

    lane_k = lax.broadcasted_iota(jnp.int32, (1, GLA_KP), 1)
    lane_v = lax.broadcasted_iota(jnp.int32, (1, GLA_W), 1)
    head_k = [((lane_k >= h * GLA_DK) & (lane_k < (h + 1) * GLA_DK)).astype(F32) for h in range(GLA_HEADS)]
    head_v = [((lane_v >= h * GLA_DV) & (lane_v < (h + 1) * GLA_DV)).astype(F32) for h in range(GLA_HEADS)]
    row_v = lax.broadcasted_iota(jnp.int32, (GLA_W, GLA_KP), 0)
    col_k = lax.broadcasted_iota(jnp.int32, (GLA_W, GLA_KP), 1)
    same_head = ((row_v >> _log2(GLA_DV)) == (col_k >> _log2(GLA_DK))).astype(F32)
    row_t = lax.broadcasted_iota(jnp.int32, (GLA_HEADS * lc, lc), 0)
    col_s = lax.broadcasted_iota(jnp.int32, (GLA_HEADS * lc, lc), 1)
    causal = (row_t & (lc - 1)) >= col_s
    tril = tril_ref[...]
    st = s_ref[...]
    for c in range(n_sub):
        rows = slice(c * lc, (c + 1) * lc)
        q = p_ref[0, rows, GLA_Q0:GLA_Q0 + GLA_KP] * (GLA_DK ** -0.5)
        k = p_ref[0, rows, GLA_K0:GLA_K0 + GLA_KP]
        v = p_ref[0, rows, GLA_V0:GLA_V0 + GLA_W]
        og = p_ref[0, rows, GLA_OG0:GLA_OG0 + GLA_W]
        z = p_ref[0, rows, GLA_Z0:GLA_Z0 + LANES]
        gin = _dot(z.astype(BF16), wg_ref[...]) + bg_ref[...]
        glog = (jnp.minimum(gin, 0.0) - jnp.log(1.0 + jnp.exp(-jnp.abs(gin)))) / GLA_GATE_NORM
        gc = _dot_f32_rhs(tril, glog)
        gl = gc[lc - 1:lc, :]
        qe = q * jnp.exp(gc)
        ke = (k * jnp.exp(-gc)).astype(BF16)
        kd = (k * jnp.exp(gl - gc)).astype(BF16)
        vb = v.astype(BF16)
        qs = jnp.concatenate([qe * m for m in head_k], axis=0).astype(BF16)
        attn = jnp.where(causal, _dot_nt(qs, ke), 0.0)
        o2 = _dot(attn.astype(BF16), vb)
        o = _dot_nt(qe.astype(BF16), st.astype(BF16))
        for h in range(GLA_HEADS):
            o = o + head_v[h] * o2[h * lc:(h + 1) * lc, :]
        st = st * jnp.exp(gl) + _dot_tn(vb, kd) * same_head
        ms = _dot_f32_lhs(o * o, mavg_ref[...])
        on = o * lax.rsqrt(ms + RMS_EPS) * gn_ref[...]
        o_ref[0, rows, :] = on * (og * _sigmoid(og))
    s_ref[...] = st
    sfin_ref[0] = st


def _gla(p, s0, gw, lblk, lc):
    b, t, _ = p.shape
    consts = [gw["wg"], gw["bg"], gw["tril"], gw["mavg"], gw["gn"]]
    st_spec = pl.BlockSpec((1, GLA_W, GLA_KP), lambda i, j: (i, 0, 0))
    return pl.pallas_call(
        functools.partial(_gla_body, n_sub=lblk // lc, lc=lc),
        grid=(b, t // lblk),
        in_specs=[pl.BlockSpec((1, lblk, GLA_PW), lambda i, j: (i, j, 0)), st_spec]
        + [_const_spec(c.shape) for c in consts],
        out_specs=[pl.BlockSpec((1, lblk, GLA_W), lambda i, j: (i, j, 0)), st_spec],
        out_shape=[jax.ShapeDtypeStruct((b, t, GLA_W), F32), jax.ShapeDtypeStruct((b, GLA_W, GLA_KP), F32)],
        scratch_shapes=[pltpu.VMEM((GLA_W, GLA_KP), F32)],
        compiler_params=_cparams("parallel", "arbitrary"),
        name="gla",
    )(p, s0, *consts)


def _bias_body(tab_ref, bucket_ref, o_ref):
    bucket = bucket_ref[...]
    for h in range(SWA_HEADS):
        acc = jnp.zeros(bucket.shape, F32)
        for bkt in range(N_BUCKETS):
            acc = jnp.where(bucket == bkt, tab_ref[bkt, h], acc)
        o_ref[h] = acc


def _t5_bias(t5_table, q_pos, k_pos):
    rel = k_pos[None, :] - q_pos[:, None]
    half = N_BUCKETS // 2
    max_exact = half // 2
    n = jnp.abs(rel)
    far = max_exact + (jnp.log(jnp.maximum(n, 1).astype(F32) / max_exact)
                       / math.log(T5_MAX_DIST / max_exact) * (half - max_exact)).astype(jnp.int32)
    bucket = jnp.where(rel > 0, half, 0) + jnp.where(n < max_exact, n, jnp.minimum(far, half - 1))
    nq, nk = bucket.shape
    return pl.pallas_call(
        _bias_body,
        in_specs=[pl.BlockSpec(memory_space=pltpu.SMEM), pl.BlockSpec(memory_space=pltpu.VMEM)],
        out_specs=pl.BlockSpec(memory_space=pltpu.VMEM),
        out_shape=jax.ShapeDtypeStruct((SWA_HEADS, nq, nk), F32),
        name="t5_bias",
    )(t5_table.astype(F32), bucket.astype(jnp.int32))


def _swa_body(sink_ref, q_ref, kp_ref, vp_ref, kc_ref, vc_ref, bias_ref, o_ref, *, qb, cq, win, pos0):
    blk = pl.program_id(1)
    kwin = jnp.concatenate([kp_ref[0], kc_ref[0]], axis=0)
    vwin = jnp.concatenate([vp_ref[0], vc_ref[0]], axis=0)
    col = lax.broadcasted_iota(jnp.int32, (1, win), 1)
    row = lax.broadcasted_iota(jnp.int32, (SWA_REP * cq, 1), 0)
    for j in range(qb // cq):
        valid = (pos0 + blk * qb + j * cq + col) >= 0
        for g in range(SWA_KV_HEADS):
            heads = [SWA_REP * g + r for r in range(SWA_REP)]
            qs = jnp.concatenate(
                [q_ref[0, j * cq:(j + 1) * cq, h * SWA_HEAD_DIM:(h + 1) * SWA_HEAD_DIM] for h in heads], axis=0)
            kk = kwin[j * cq:j * cq + win, g * SWA_HEAD_DIM:(g + 1) * SWA_HEAD_DIM]
            vv = vwin[j * cq:j * cq + win, g * SWA_HEAD_DIM:(g + 1) * SWA_HEAD_DIM]
            s = _dot_nt(qs.astype(BF16), kk.astype(BF16)) * (SWA_HEAD_DIM ** -0.5) + bias_ref[g]
            s = jnp.where(valid, s, -1e30)
            sink = jnp.where(row < cq, sink_ref[heads[0]], jnp.where(row < 2 * cq, sink_ref[heads[1]], sink_ref[heads[2]]))
            m = jnp.maximum(jnp.max(s, axis=-1, keepdims=True), sink)
            e = jnp.exp(s - m)
            den = jnp.sum(e, axis=-1, keepdims=True) + jnp.exp(sink - m)
            o = _dot((e / den).astype(BF16), vv.astype(BF16))
            for r, h in enumerate(heads):
                o_ref[0, j * cq:(j + 1) * cq, h * SWA_HEAD_DIM:(h + 1) * SWA_HEAD_DIM] = o[r * cq:(r + 1) * cq, :]


def _swa(sinks, p_swa, k_prev, v_prev, prev_k_map, prev_v_map, bias, qb, cq, win, pos0):
    b, t, _ = p_swa.shape
    kcol, vcol = SWA_W // SWA_KV_W, SWA_W // SWA_KV_W + 1
    return pl.pallas_call(
        functools.partial(_swa_body, qb=qb, cq=cq, win=win, pos0=pos0),
        grid=(b, t // qb),
        in_specs=[pl.BlockSpec(memory_space=pltpu.SMEM),
                  pl.BlockSpec((1, qb, SWA_W), lambda i, j: (i, j, 0)),
                  pl.BlockSpec((1, SWA_WINDOW, SWA_KV_W), prev_k_map),
                  pl.BlockSpec((1, SWA_WINDOW, SWA_KV_W), prev_v_map),
                  pl.BlockSpec((1, qb, SWA_KV_W), lambda i, j: (i, j, kcol)),
                  pl.BlockSpec((1, qb, SWA_KV_W), lambda i, j: (i, j, vcol)),
                  _const_spec(bias.shape)],
        out_specs=pl.BlockSpec((1, qb, SWA_W), lambda i, j: (i, j, 0)),
        out_shape=jax.ShapeDtypeStruct((b, t, SWA_W), F32),
        compiler_params=_cparams("parallel", "parallel"),
        name="swa",
    )(sinks, p_swa, k_prev, v_prev, p_swa, p_swa, bias)


def _out_proj_body(x_ref, ys_ref, og_ref, os_ref, wo_ref, gf_ref, wr_ref, br_ref, x1_ref, xn_ref, sel_ref, aux_ref):
    x1 = (x_ref[...] + _dot(ys_ref[...].astype(BF16), wo_ref[0:SSM_W, :])
          + _dot(og_ref[...].astype(BF16), wo_ref[SSM_W:SSM_W + GLA_W, :])
          + _dot(os_ref[...].astype(BF16), wo_ref[SSM_W + GLA_W:MIX_W, :]))
    x1_ref[...] = x1
    xn = x1 * lax.rsqrt(jnp.mean(x1 * x1, axis=-1, keepdims=True) + RMS_EPS) * gf_ref[...]
    xb = xn.astype(BF16)
    xn_ref[...] = xb
    logit = _dot(xb, wr_ref[...]) + br_ref[...]
    lane_i = lax.broadcasted_iota(jnp.int32, logit.shape, 1)
    lane = lane_i.astype(F32)
    neg = -jnp.inf
    big = float(LANES)
    g_lane = lane_i - ROUTER_GROUP_LANE0
    lg = jnp.where((g_lane >= 0) & (g_lane < N_EXPERT_GROUPS), logit, neg)
    gmax = jnp.max(lg, axis=-1, keepdims=True)
    p_sel = 1.0 / jnp.sum(jnp.exp(lg - gmax), axis=-1, keepdims=True)
    g_idx = jnp.min(jnp.where(lg == gmax, g_lane.astype(F32), big), axis=-1, keepdims=True)
    lane_group = (lane_i >> _log2(EXPERTS_PER_GROUP)).astype(F32)
    in_group = (lane_i < N_EXPERTS) & (lane_group == g_idx)
    le = jnp.where(in_group, logit, neg)
    m1 = jnp.max(le, axis=-1, keepdims=True)
    i1 = jnp.min(jnp.where(le == m1, lane, big), axis=-1, keepdims=True)
    le2 = jnp.where(lane == i1, neg, le)
    m2 = jnp.max(le2, axis=-1, keepdims=True)
    i2 = jnp.min(jnp.where(le2 == m2, lane, big), axis=-1, keepdims=True)
    r = jnp.exp(m2 - m1)
    w1 = 1.0 / (1.0 + r)
    aux_ref[...] = p_sel * (jnp.where(lane_i == AUX_W1, w1, 0.0) + jnp.where(lane_i == AUX_W2, r * w1, 0.0))
    sel = jnp.where(lane == i1, 1.0, 0.0) + jnp.where((lane - float(N_EXPERTS)) == i2, 1.0, 0.0)
    sel_ref[...] = jnp.transpose(sel)[:2 * N_EXPERTS, :]


def _out_proj(x, ys, og, osw, wo, gf, wr, br, tm):
    n = x.shape[0]
    row = lambda w: pl.BlockSpec((tm, w), lambda i: (i, 0))
    return pl.pallas_call(
        _out_proj_body,
        grid=(n // tm,),
        in_specs=[row(D_MODEL), row(SSM_W), row(GLA_W), row(SWA_W), _const_spec(wo.shape), _const_spec(gf.shape),
                  _const_spec(wr.shape), _const_spec(br.shape)],
        out_specs=[row(D_MODEL), row(D_MODEL), pl.BlockSpec((2 * N_EXPERTS, tm), lambda i: (0, i)), row(LANES)],
        out_shape=[jax.ShapeDtypeStruct((n, D_MODEL), F32), jax.ShapeDtypeStruct((n, D_MODEL), BF16),
                   jax.ShapeDtypeStruct((2 * N_EXPERTS, n), F32), jax.ShapeDtypeStruct((n, LANES), F32)],
        compiler_params=_cparams("parallel"),
        name="out_proj",
    )(x, ys, og, osw, wo, gf, wr, br)


def _moe_sort_body(xn_ref, sel_ref, lo_ref, triu_ref, aux_ref, xs_ref, auxo_ref):
    s1 = sel_ref[0:N_EXPERTS, :]
    s2 = sel_ref[N_EXPERTS:2 * N_EXPERTS, :]
    rank = _dot((s1 + s2).astype(BF16), triu_ref[...])
    pos = lo_ref[0][:, 0:1] + rank
    d1 = jnp.sum(s1 * pos, axis=0, keepdims=True)
    d2 = jnp.sum(s2 * pos, axis=0, keepdims=True)
    r = lax.broadcasted_iota(jnp.int32, (MOE_RLOC, MOE_TB), 0).astype(F32)
    perm = jnp.where((r == d1) | (r == d2), 1.0, 0.0).astype(BF16)
    xs_ref[...] = _dot(perm, xn_ref[...]).astype(BF16)
    row = lax.broadcasted_iota(jnp.int32, (LANES, MOE_TB), 0)
    dt = jnp.where(row == AUX_D1, d1, jnp.where(row == AUX_D2, d2, 0.0))
    auxo_ref[...] = aux_ref[...] + jnp.transpose(dt)


def _moe_sort(xn, sel, lo_rows, triu, aux):
    n = xn.shape[0]
    nblk = n // MOE_TB
    return pl.pallas_call(
        _moe_sort_body,
        grid=(nblk,),
        in_specs=[pl.BlockSpec((MOE_TB, D_MODEL), lambda i: (i, 0)), pl.BlockSpec((2 * N_EXPERTS, MOE_TB), lambda i: (0, i)),
                  pl.BlockSpec((1, N_EXPERTS, LANES), lambda i: (i, 0, 0)), _const_spec(triu.shape),
                  pl.BlockSpec((MOE_TB, LANES), lambda i: (i, 0))],
        out_specs=[pl.BlockSpec((MOE_RLOC, D_MODEL), lambda i: (i, 0)), pl.BlockSpec((MOE_TB, LANES), lambda i: (i, 0))],
        out_shape=[jax.ShapeDtypeStruct((nblk * MOE_RLOC, D_MODEL), BF16), jax.ShapeDtypeStruct((n, LANES), F32)],
        compiler_params=_cparams("parallel"),
        name="moe_sort",
    )(xn, sel, lo_rows, triu, aux)


def _piece_copy(hbm_ref, piece, buf_ref, slot, p, sem, to_hbm):
    rows = pl.ds(pl.multiple_of(piece * MOE_PIECE, MOE_PIECE), MOE_PIECE)
    vm = buf_ref.at[slot, pl.ds(pl.multiple_of(p * MOE_PIECE, MOE_PIECE), MOE_PIECE)]
    if to_hbm:
        return pltpu.make_async_copy(vm, hbm_ref.at[rows], sem)
    return pltpu.make_async_copy(hbm_ref.at[rows], vm, sem)


def _moe_expert_body(texp_ref, piece_ref, nv_ref, xs_hbm, wg_ref, wu_ref, wd_ref, ys_hbm,
                     xbuf, ybuf, wgb, wub, wdb, sem_in, sem_out, *, pt, n_steps):
    i = pl.program_id(0)
    slot = i % 2

    def for_pieces(tile, s, hbm_ref, buf_ref, sem, to_hbm, wait):
        def body(p, carry):
            c = _piece_copy(hbm_ref, piece_ref[tile * pt + p], buf_ref, s, p, sem.at[s], to_hbm)
            c.wait() if wait else c.start()
            return carry
        lax.fori_loop(0, nv_ref[tile], body, 0)

    gather = lambda tile, s, wait: for_pieces(tile, s, xs_hbm, xbuf, sem_in, False, wait)
    scatter = lambda tile, s, wait: for_pieces(tile, s, ys_hbm, ybuf, sem_out, True, wait)

    @pl.when(i == 0)
    def _():
        xbuf[...] = jnp.zeros_like(xbuf)
        gather(0, 0, False)

    @pl.when(i + 1 < n_steps)
    def _():
        gather(i + 1, 1 - slot, False)

    gather(i, slot, True)

    @pl.when((i == 0) | (texp_ref[i] != texp_ref[jnp.maximum(i - 1, 0)]))
    def _():
        wgb[...] = wg_ref[0].astype(BF16)
        wub[...] = wu_ref[0].astype(BF16)
        wdb[...] = wd_ref[0].astype(BF16)

    @pl.when(nv_ref[i] > 0)
    def _():
        xb = xbuf[slot]
        a = _dot(xb, wgb[...])
        h = (a * _sigmoid(a)) * _dot(xb, wub[...])
        ybuf[slot] = _dot(h.astype(BF16), wdb[...]).astype(BF16)

    scatter(i, slot, False)

    @pl.when(i >= 1)
    def _():
        scatter(i - 1, 1 - slot, True)

    @pl.when(i == n_steps - 1)
    def _():
        scatter(i, slot, True)


def _moe_experts(tile_expert, piece, nvalid, xs, wg, wu, wd, pt):
    n_steps = tile_expert.shape[0]
    tm = pt * MOE_PIECE
    wspec = lambda shape: pl.BlockSpec(shape, lambda i, te, pc, nv: (te[i], 0, 0))
    grid_spec = pltpu.PrefetchScalarGridSpec(
        num_scalar_prefetch=3,
        grid=(n_steps,),
        in_specs=[pl.BlockSpec(memory_space=pl.ANY), wspec((1, D_MODEL, EXPERT_FF)), wspec((1, D_MODEL, EXPERT_FF)),
                  wspec((1, EXPERT_FF, D_MODEL))],
        out_specs=pl.BlockSpec(memory_space=pl.ANY),
        scratch_shapes=[pltpu.VMEM((2, tm, D_MODEL), BF16), pltpu.VMEM((2, tm, D_MODEL), BF16),
                        pltpu.VMEM((D_MODEL, EXPERT_FF), BF16), pltpu.VMEM((D_MODEL, EXPERT_FF), BF16),
                        pltpu.VMEM((EXPERT_FF, D_MODEL), BF16),
                        pltpu.SemaphoreType.DMA((2,)), pltpu.SemaphoreType.DMA((2,))],
    )
    return pl.pallas_call(
        functools.partial(_moe_expert_body, pt=pt, n_steps=n_steps),
        grid_spec=grid_spec,
        out_shape=jax.ShapeDtypeStruct(xs.shape, BF16),
        input_output_aliases={3: 0},
        compiler_params=_cparams("arbitrary"),
        name="moe_experts",
    )(tile_expert, piece, nvalid, xs, wg, wu, wd)


def _moe_combine_body(ys_ref, aux_ref, x1_ref, o_ref):
    col = lax.broadcasted_iota(jnp.int32, (MOE_TB, MOE_RLOC), 1).astype(F32)
    y = ys_ref[...]
    pick = lambda lane: _dot(jnp.where(col == aux_ref[:, lane:lane + 1], 1.0, 0.0).astype(BF16), y)
    o_ref[...] = (x1_ref[...] + aux_ref[:, AUX_W1:AUX_W1 + 1] * pick(AUX_D1)
                  + aux_ref[:, AUX_W2:AUX_W2 + 1] * pick(AUX_D2))


def _moe_combine(ys, aux, x1):
    n = x1.shape[0]
    return pl.pallas_call(
        _moe_combine_body,
        grid=(n // MOE_TB,),
        in_specs=[pl.BlockSpec((MOE_RLOC, D_MODEL), lambda i: (i, 0)), pl.BlockSpec((MOE_TB, LANES), lambda i: (i, 0)),
                  pl.BlockSpec((MOE_TB, D_MODEL), lambda i: (i, 0))],
        out_specs=pl.BlockSpec((MOE_TB, D_MODEL), lambda i: (i, 0)),
        out_shape=jax.ShapeDtypeStruct((n, D_MODEL), F32),
        compiler_params=_cparams("parallel"),
        name="moe_combine",
    )(ys, aux, x1)


def _route_tables(sel, pt):
    n = sel.shape[1]
    nblk = n // MOE_TB
    cnt = (sel[:N_EXPERTS] + sel[N_EXPERTS:]).reshape(N_EXPERTS, nblk, MOE_TB).sum(-1).astype(jnp.int32)
    pc = (cnt + MOE_PIECE - 1) // MOE_PIECE
    lo_p = jnp.cumsum(pc, axis=0) - pc
    lo_rows = jnp.broadcast_to((lo_p.T * MOE_PIECE).astype(F32)[:, :, None], (nblk, N_EXPERTS, LANES))
    pe = pc.sum(1)
    tiles_e = (pe + pt - 1) // pt
    tile_start = jnp.cumsum(tiles_e) - tiles_e
    seg_start = (tile_start[:, None] * pt + jnp.cumsum(pc, axis=1) - pc).reshape(-1)
    pcs = pc.reshape(-1)
    seg_src = (jnp.arange(nblk, dtype=jnp.int32)[None, :] * MOE_PPB + lo_p).reshape(-1)
    n_steps = -(-(2 * n // MOE_PIECE + nblk * N_EXPERTS + N_EXPERTS * (pt - 1)) // pt)
    slot = jnp.arange(n_steps * pt, dtype=jnp.int32)[:, None]
    in_seg = (slot >= seg_start[None, :]) & (slot < (seg_start + pcs)[None, :])
    piece = jnp.sum(jnp.where(in_seg, seg_src[None, :] + slot - seg_start[None, :], 0), axis=1).astype(jnp.int32)
    tile = jnp.arange(n_steps, dtype=jnp.int32)[:, None]
    in_exp = (tile >= tile_start[None, :]) & (tile < (tile_start + tiles_e)[None, :])
    tile_expert = jnp.sum(jnp.where(in_exp, jnp.arange(N_EXPERTS, dtype=jnp.int32)[None, :], 0), axis=1)
    tile_expert = jnp.where(tile[:, 0] < tiles_e.sum(), tile_expert, N_EXPERTS - 1).astype(jnp.int32)
    nvalid = jnp.sum(jnp.where(in_exp, jnp.clip(pe[None, :] - (tile - tile_start[None, :]) * pt, 0, pt), 0),
                     axis=1).astype(jnp.int32)
    return lo_rows, tile_expert, piece, nvalid


def _moe(xn, sel, aux, x1, wg, wu, wd, triu, pt):
    lo_rows, tile_expert, piece, nvalid = _route_tables(sel, pt)
    xs, aux2 = _moe_sort(xn, sel, lo_rows, triu, aux)
    ys = _moe_experts(tile_expert, piece, nvalid, xs, wg, wu, wd, pt)
    return _moe_combine(ys, aux2, x1)


def _seg_mean_matrix(width, seg):
    i = jnp.arange(width)
    return jnp.where((i[:, None] // seg) == (i[None, :] // seg), 1.0 / seg, 0.0).astype(BF16)


def _tril_ones(n):
    i = jnp.arange(n)
    return (i[:, None] >= i[None, :]).astype(BF16)


def _prep_layer(lw, lc_list):
    w_in = lw["w_in"].astype(F32)
    cols = {}
    off = 0
    for name, wdt in (("u", SSM_W), ("qg", GLA_K), ("kg", GLA_K), ("vg", GLA_W), ("z", GLA_GATE_RANK), ("og", GLA_W),
                      ("qs", SWA_W), ("ks", SWA_KV_W), ("vs", SWA_KV_W)):
        cols[name] = w_in[:, off:off + wdt]
        off += wdt
    zpad = lambda wdt: jnp.zeros((D_MODEL, wdt), F32)
    w_all = jnp.concatenate(
        [cols["qg"], zpad(GLA_KP - GLA_K), cols["kg"], zpad(GLA_KP - GLA_K), cols["vg"], cols["og"], cols["z"],
         zpad(LANES - GLA_GATE_RANK), cols["qs"], cols["ks"], cols["vs"], cols["u"]], axis=1).astype(BF16)
    out = {"norm_mix": lw["norm_mix"].astype(F32)[None, :], "w_all": w_all,
           "mavg_qk": _seg_mean_matrix(SWA_W + SWA_KV_W, SWA_HEAD_DIM),
           "qk_gain": jnp.concatenate([jnp.tile(lw["swa_q_norm"].astype(F32), SWA_HEADS),
                                       jnp.tile(lw["swa_k_norm"].astype(F32), SWA_KV_HEADS)])[None, :]}
    a_re = lw["ssm_a_re"].astype(F32)
    a_im = lw["ssm_a_im"].astype(F32)
    dt = jnp.exp(lw["ssm_log_dt"].astype(F32))[:, None]
    mag = jnp.exp(a_re * dt)
    ab_re = mag * jnp.cos(a_im * dt)
    ab_im = mag * jnp.sin(a_im * dt)
    den = a_re * a_re + a_im * a_im
    nr = ab_re - 1.0
    f_re = (nr * a_re + ab_im * a_im) / den
    f_im = (ab_im * a_re - nr * a_im) / den
    b_re = lw["ssm_b_re"].astype(F32)
    b_im = lw["ssm_b_im"].astype(F32)
    bb_re = f_re[..., None] * b_re - f_im[..., None] * b_im
    bb_im = f_re[..., None] * b_im + f_im[..., None] * b_re
    eye_g = jnp.eye(SSM_GROUPS, dtype=F32)
    blockdiag_in = lambda bb: jnp.einsum("gpc,gh->gchp", bb, eye_g).reshape(SSM_W, SSM_FLAT)
    blockdiag_out = lambda cc: jnp.einsum("gcp,gh->gphc", cc, eye_g).reshape(SSM_FLAT, SSM_W)
    bd = jnp.concatenate([blockdiag_in(bb_re), blockdiag_in(bb_im)], axis=1).astype(BF16)
    cd = jnp.concatenate([blockdiag_out(lw["ssm_c_re"].astype(F32)),
                          -blockdiag_out(lw["ssm_c_im"].astype(F32))], axis=0).astype(BF16)
    ssm = {"ab": jnp.concatenate([ab_re.reshape(1, SSM_FLAT), ab_im.reshape(1, SSM_FLAT),
                                  jnp.zeros((6, SSM_FLAT), F32)], axis=0),
           "bd": bd, "cd": cd, "d": lw["ssm_d"].astype(F32)[None, :], "wglu": lw["ssm_w_glu"].astype(BF16)}
    ssm_by_lc = {}
    for lc in lc_list:
        tt = jnp.arange(lc, dtype=F32)[:, None, None]
        def powers(sign):
            m = jnp.exp(sign * tt * (a_re * dt)[None])
            ang = sign * tt * (a_im * dt)[None]
            return jnp.stack([(m * jnp.cos(ang)).reshape(lc, SSM_FLAT), (m * jnp.sin(ang)).reshape(lc, SSM_FLAT)])
        ssm_by_lc[lc] = dict(ssm, pwp=powers(1.0), pwn=powers(-1.0), tril=_tril_ones(lc))
    out["ssm"] = ssm_by_lc
    wg = jnp.zeros((LANES, GLA_KP), F32).at[:GLA_GATE_RANK, :GLA_K].set(lw["gla_w_gate"].astype(F32)).astype(BF16)
    bg = jnp.zeros((1, GLA_KP), F32).at[0, :GLA_K].set(lw["gla_b_gate"].astype(F32))
    gla = {"wg": wg, "bg": bg, "mavg": _seg_mean_matrix(GLA_W, GLA_DV),
           "gn": jnp.tile(lw["gla_norm"].astype(F32), GLA_HEADS)[None, :]}
    out["gla"] = {lc: dict(gla, tril=_tril_ones(lc)) for lc in lc_list}
    out["sinks"] = lw["swa_sinks"].astype(F32)
    out["w_out"] = lw["w_out"].astype(BF16)
    out["norm_ffn"] = lw["norm_ffn"].astype(F32)[None, :]
    wr = jnp.zeros((D_MODEL, LANES), F32)
    wr = wr.at[:, :N_EXPERTS].set(lw["moe_w_expert"].astype(F32))
    wr = wr.at[:, ROUTER_GROUP_LANE0:ROUTER_GROUP_LANE0 + N_EXPERT_GROUPS].set(lw["moe_w_group"].astype(F32))
    out["wr"] = wr.astype(BF16)
    br = jnp.zeros((1, LANES), F32)
    br = br.at[0, :N_EXPERTS].set(lw["moe_b_expert"].astype(F32))
    br = br.at[0, ROUTER_GROUP_LANE0:ROUTER_GROUP_LANE0 + N_EXPERT_GROUPS].set(lw["moe_b_group"].astype(F32))
    out["br"] = br
    ti = jnp.arange(MOE_TB)
    out["moe_triu"] = (ti[:, None] < ti[None, :]).astype(BF16)
    out["moe_wg"] = lw["moe_w_gate"].astype(F32)
    out["moe_wu"] = lw["moe_w_up"].astype(F32)
    out["moe_wd"] = lw["moe_w_down"].astype(F32)
    return out


def _gla_state_in(h0):
    b = h0.shape[0]
    eye_h = jnp.eye(GLA_HEADS, dtype=F32)
    s = jnp.einsum("bhkv,hg->bhvgk", h0.astype(F32), eye_h).reshape(b, GLA_W, GLA_K)
    return jnp.pad(s, ((0, 0), (0, 0), (0, GLA_KP - GLA_K)))


def _gla_state_out(st):
    b = st.shape[0]
    s = st[:, :, :GLA_K].reshape(b, GLA_HEADS, GLA_DV, GLA_HEADS, GLA_DK)
    diag = jnp.stack([s[:, h, :, h, :] for h in range(GLA_HEADS)], axis=1)
    return jnp.transpose(diag, (0, 1, 3, 2))


def _layer(x, pw, bias, ssm_h0r, ssm_h0i, gla_h0, past_k, past_v, cfg):
    b, t, _ = x.shape
    n = b * t
    lc = cfg["lc"]
    p_gla, p_swa, p_ssm = _in_proj(x.reshape(n, D_MODEL), pw["norm_mix"], pw["w_all"], pw["mavg_qk"], pw["qk_gain"],
                                   cfg["tm"])
    p_gla = p_gla.reshape(b, t, GLA_PW)
    p_swa = p_swa.reshape(b, t, SWA_PW)
    y_ssm, h_re, h_im = _ssm(p_ssm.reshape(b, t, SSM_W), ssm_h0r.reshape(b, 1, SSM_FLAT).astype(F32),
                             ssm_h0i.reshape(b, 1, SSM_FLAT).astype(F32), pw["ssm"][lc], cfg["lblk"], lc)
    o_g, s_fin = _gla(p_gla, _gla_state_in(gla_h0), pw["gla"][lc], cfg["lblk"], lc)
    kcol, vcol = SWA_W // SWA_KV_W, SWA_W // SWA_KV_W + 1
    if past_k is None:
        per_blk = cfg["qb"] // SWA_WINDOW
        prev_k_map = lambda i, j: (i, jnp.maximum(j * per_blk - 1, 0), kcol)
        prev_v_map = lambda i, j: (i, jnp.maximum(j * per_blk - 1, 0), vcol)
        k_prev, v_prev = p_swa, p_swa
        keep = min(SWA_WINDOW, t)
        new_k = p_swa[:, t - keep:, SWA_W:SWA_W + SWA_KV_W]
        new_v = p_swa[:, t - keep:, SWA_W + SWA_KV_W:]
    else:
        prev_k_map = prev_v_map = lambda i, j: (i, 0, 0)
        k_prev = past_k.reshape(b, SWA_WINDOW, SWA_KV_W).astype(F32)
        v_prev = past_v.reshape(b, SWA_WINDOW, SWA_KV_W).astype(F32)
        new_k = p_swa[:, :, SWA_W:SWA_W + SWA_KV_W]
        new_v = p_swa[:, :, SWA_W + SWA_KV_W:]
    o_s = _swa(pw["sinks"], p_swa, k_prev, v_prev, prev_k_map, prev_v_map, bias, cfg["qb"], cfg["cq"], cfg["win"],
               cfg["pos0"])
    x1, xn, sel, aux = _out_proj(x.reshape(n, D_MODEL), y_ssm.reshape(n, SSM_W), o_g.reshape(n, GLA_W),
                                 o_s.reshape(n, SWA_W), pw["w_out"], pw["norm_ffn"], pw["wr"], pw["br"],
                                 cfg["tm"])
    x2 = _moe(xn, sel, aux, x1, pw["moe_wg"], pw["moe_wu"], pw["moe_wd"], pw["moe_triu"], cfg["pt"])
    kv_shape = (b, new_k.shape[1], SWA_KV_HEADS, SWA_HEAD_DIM)
    return (x2.reshape(b, t, D_MODEL), new_k.reshape(kv_shape), new_v.reshape(kv_shape), _gla_state_out(s_fin),
            h_re.reshape(b, SSM_GROUPS, SSM_STATE), h_im.reshape(b, SSM_GROUPS, SSM_STATE))


def _group_cfg(t, past_len, n_past):
    if n_past is None:
        assert t % (4 * CHUNK) == 0
        cfg = dict(lc=CHUNK, lblk=4 * CHUNK, qb=4 * CHUNK, cq=CHUNK, win=SWA_WINDOW + CHUNK, pos0=-SWA_WINDOW, tm=512,
                   pt=32)
        q_pos = jnp.arange(CHUNK, dtype=jnp.int32) + SWA_WINDOW
        k_pos = jnp.arange(SWA_WINDOW + CHUNK, dtype=jnp.int32)
    else:
        first_key, last_q = past_len - n_past, past_len + t - 1
        assert n_past == SWA_WINDOW and t <= CHUNK and t % 8 == 0
        assert past_len // CHUNK == last_q // CHUNK and first_key // CHUNK >= past_len // CHUNK - SWA_WINDOW // CHUNK
        cfg = dict(lc=t, lblk=t, qb=t, cq=t, win=n_past + t, pos0=first_key, tm=512, pt=16)
        q_pos = past_len + jnp.arange(t, dtype=jnp.int32)
        k_pos = first_key + jnp.arange(n_past + t, dtype=jnp.int32)
    return cfg, q_pos, k_pos


def _stacked_bias(t5_table, q_pos, k_pos):
    bias = _t5_bias(t5_table, q_pos, k_pos)
    nq, nk = bias.shape[1:]
    return bias.reshape(SWA_KV_HEADS, SWA_REP * nq, nk)


PAST_LEN = 1024


def kernel(x_prompt, x_sample, cache_swa_k, cache_swa_v, state_gla, state_ssm_re, state_ssm_im, norm_mix, w_in, ssm_a_re, ssm_a_im, ssm_log_dt, ssm_b_re, ssm_b_im, ssm_c_re, ssm_c_im, ssm_d, ssm_w_glu, gla_w_gate, gla_b_gate, gla_norm, swa_q_norm, swa_k_norm, swa_sinks, t5_table, w_out, norm_ffn, moe_w_group, moe_b_group, moe_w_expert, moe_b_expert, moe_w_gate, moe_w_up, moe_w_down):
    depth = w_in.shape[0]
    bp, tp, _ = x_prompt.shape
    bs, ts, _ = x_sample.shape
    cfg_p, qpos_p, kpos_p = _group_cfg(tp, 0, None)
    cfg_s, qpos_s, kpos_s = _group_cfg(ts, PAST_LEN, cache_swa_k.shape[2])
    bias_p = _stacked_bias(t5_table, qpos_p, kpos_p)
    bias_s = _stacked_bias(t5_table, qpos_s, kpos_s)
    hp, hs = x_prompt.astype(F32), x_sample.astype(F32)
    outs = [[] for _ in range(10)]
    for l in range(depth):
        lw = {
            "norm_mix": norm_mix[l], "w_in": w_in[l], "ssm_a_re": ssm_a_re[l], "ssm_a_im": ssm_a_im[l],
            "ssm_log_dt": ssm_log_dt[l], "ssm_b_re": ssm_b_re[l], "ssm_b_im": ssm_b_im[l], "ssm_c_re": ssm_c_re[l],
            "ssm_c_im": ssm_c_im[l], "ssm_d": ssm_d[l], "ssm_w_glu": ssm_w_glu[l], "gla_w_gate": gla_w_gate[l],
            "gla_b_gate": gla_b_gate[l], "gla_norm": gla_norm[l], "swa_q_norm": swa_q_norm[l],
            "swa_k_norm": swa_k_norm[l], "swa_sinks": swa_sinks[l], "w_out": w_out[l], "norm_ffn": norm_ffn[l],
            "moe_w_group": moe_w_group[l], "moe_b_group": moe_b_group[l], "moe_w_expert": moe_w_expert[l],
            "moe_b_expert": moe_b_expert[l], "moe_w_gate": moe_w_gate[l], "moe_w_up": moe_w_up[l],
            "moe_w_down": moe_w_down[l],
        }
        pw = _prep_layer(lw, sorted({cfg_p["lc"], cfg_s["lc"]}))
        zs = jnp.zeros((bp, SSM_GROUPS, SSM_STATE), F32)
        zg = jnp.zeros((bp, GLA_HEADS, GLA_DK, GLA_DV), F32)
        hp, nk, nv, ng, nr, ni = _layer(hp, pw, bias_p, zs, zs, zg, None, None, cfg_p)
        for slot, val in zip((0, 1, 4, 6, 7), (nk, nv, ng, nr, ni)):
            outs[slot].append(val)
        hs, nk, nv, ng, nr, ni = _layer(hs, pw, bias_s, state_ssm_re[l], state_ssm_im[l], state_gla[l],
                                        cache_swa_k[l], cache_swa_v[l], cfg_s)
        for slot, val in zip((2, 3, 5, 8, 9), (nk, nv, ng, nr, ni)):
            outs[slot].append(val)
    return (hp, hs) + tuple(jnp.stack(o) for o in outs)
```

```python
import functools
import math

import jax
import jax.numpy as jnp
from jax import lax
from jax.experimental import pallas as pl
from jax.experimental.pallas import tpu as pltpu

F32 = jnp.float32
BF16 = jnp.bfloat16

D_MODEL = 1024
CHUNK = 64
RMS_EPS = 1e-6
SSM_GROUPS = 16
SSM_GC = 16
SSM_STATE = 64
SSM_W = SSM_GROUPS * SSM_GC
SSM_FLAT = SSM_GROUPS * SSM_STATE
GLA_HEADS = 6
GLA_DK = 32
GLA_DV = 64
GLA_GATE_RANK = 16
GLA_GATE_NORM = 16.0
GLA_K = GLA_HEADS * GLA_DK
GLA_KP = 256
GLA_W = GLA_HEADS * GLA_DV
SWA_HEADS = 6
SWA_KV_HEADS = 2
SWA_REP = SWA_HEADS // SWA_KV_HEADS
SWA_HEAD_DIM = 64
SWA_WINDOW = 128
SWA_W = SWA_HEADS * SWA_HEAD_DIM
SWA_KV_W = SWA_KV_HEADS * SWA_HEAD_DIM
MIX_W = SSM_W + GLA_W + SWA_W
N_BUCKETS = 32
T5_MAX_DIST = 128
N_EXPERT_GROUPS = 4
EXPERTS_PER_GROUP = 4
N_EXPERTS = 16
EXPERT_FF = 512
LANES = 128
ROUTER_GROUP_LANE0 = N_EXPERTS
AUX_W1, AUX_W2, AUX_D1, AUX_D2 = 0, 1, 2, 3
MOE_TB = 512
MOE_PIECE = 16
MOE_RLOC = 2 * MOE_TB + N_EXPERTS * MOE_PIECE
MOE_PPB = MOE_RLOC // MOE_PIECE

GLA_Q0, GLA_K0, GLA_V0, GLA_OG0, GLA_Z0, GLA_PW = 0, 256, 512, 896, 1280, 1408
SWA_PW = SWA_W + 2 * SWA_KV_W
P_GLA0, P_SWA0, P_SSM0, P_TOTAL = 0, GLA_PW, GLA_PW + SWA_PW, GLA_PW + SWA_PW + SSM_W

VMEM_LIMIT = 48 * 1024 * 1024


def _cparams(*sem):
    return pltpu.CompilerParams(dimension_semantics=sem, vmem_limit_bytes=VMEM_LIMIT)


def _dot(a, b):
    return jnp.dot(a, b, preferred_element_type=F32)


def _dot_nt(a, b):
    return lax.dot_general(a, b, (((1,), (1,)), ((), ())), preferred_element_type=F32)


def _dot_tn(a, b):
    return lax.dot_general(a, b, (((0,), (0,)), ((), ())), preferred_element_type=F32)


def _hi_lo(x):
    hi = x.astype(BF16)
    return hi, (x - hi.astype(F32)).astype(BF16)


def _dot_f32_rhs(a_bf16, x):
    hi, lo = _hi_lo(x)
    return _dot(a_bf16, hi) + _dot(a_bf16, lo)


def _dot_f32_lhs(x, b_bf16):
    hi, lo = _hi_lo(x)
    return _dot(hi, b_bf16) + _dot(lo, b_bf16)


def _log2(n):
    assert n & (n - 1) == 0
    return n.bit_length() - 1


def _sigmoid(x):
    return 1.0 / (1.0 + jnp.exp(-x))


def _const_spec(shape):
    nd = len(shape)
    return pl.BlockSpec(shape, lambda *_: (0,) * nd)


def _in_proj_body(x_ref, g_ref, w_ref, mavg_ref, qkg_ref, gla_ref, swa_ref, ssm_ref):
    x = x_ref[...]
    xn = x * lax.rsqrt(jnp.mean(x * x, axis=-1, keepdims=True) + RMS_EPS) * g_ref[...]
    xb = xn.astype(BF16)
    gla_ref[...] = _dot(xb, w_ref[:, P_GLA0:P_SWA0])
    ssm_ref[...] = _dot(xb, w_ref[:, P_SSM0:P_TOTAL])
    s = _dot(xb, w_ref[:, P_SWA0:P_SSM0])
    qk = s[:, :SWA_W + SWA_KV_W]
    ms = _dot_f32_lhs(qk * qk, mavg_ref[...])
    swa_ref[:, :SWA_W + SWA_KV_W] = qk * lax.rsqrt(ms + RMS_EPS) * qkg_ref[...]
    swa_ref[:, SWA_W + SWA_KV_W:] = s[:, SWA_W + SWA_KV_W:]


def _in_proj(x, g, w, mavg, qkg, tm):
    n = x.shape[0]
    return pl.pallas_call(
        _in_proj_body,
        grid=(n // tm,),
        in_specs=[pl.BlockSpec((tm, D_MODEL), lambda i: (i, 0)), _const_spec(g.shape), _const_spec(w.shape),
                  _const_spec(mavg.shape), _const_spec(qkg.shape)],
        out_specs=[pl.BlockSpec((tm, GLA_PW), lambda i: (i, 0)), pl.BlockSpec((tm, SWA_PW), lambda i: (i, 0)),
                   pl.BlockSpec((tm, SSM_W), lambda i: (i, 0))],
        out_shape=[jax.ShapeDtypeStruct((n, GLA_PW), F32), jax.ShapeDtypeStruct((n, SWA_PW), F32),
                   jax.ShapeDtypeStruct((n, SSM_W), F32)],
        compiler_params=_cparams("parallel"),
        name="in_proj",
    )(x, g, w, mavg, qkg)


def _ssm_body(u_ref, h0r_ref, h0i_ref, ab_ref, bd_ref, cd_ref, pwp_ref, pwn_ref, tril_ref, d_ref, wglu_ref,
              y_ref, hr_ref, hi_ref, carry_ref, *, n_part, n_sub, lc):
    @pl.when(pl.program_id(1) == 0)
    def _():
        carry_ref[0:1, :] = h0r_ref[0]
        carry_ref[1:2, :] = h0i_ref[0]

    hr = carry_ref[0:1, :]
    hi = carry_ref[1:2, :]
    ab_re = ab_ref[0:1, :]
    ab_im = ab_ref[1:2, :]
    nr, ni = pwn_ref[0], pwn_ref[1]
    pr, pi = pwp_ref[0], pwp_ref[1]
    span = n_sub * lc
    for part in range(n_part):
        rows = slice(part * span, (part + 1) * span)
        u = u_ref[0, rows, :]
        bu = _dot(u.astype(BF16), bd_ref[...])
        sr, si = [], []
        for c in range(n_sub):
            bur = bu[c * lc:(c + 1) * lc, :SSM_FLAT]
            bui = bu[c * lc:(c + 1) * lc, SSM_FLAT:]
            sr.append(nr * bur - ni * bui)
            si.append(nr * bui + ni * bur)
        scaled = jnp.concatenate([jnp.concatenate(sr, axis=0), jnp.concatenate(si, axis=0)], axis=1)
        cs = _dot_f32_rhs(tril_ref[...], scaled)
        h_r, h_i = [], []
        for c in range(n_sub):
            cr = cs[c * lc:(c + 1) * lc, :SSM_FLAT] + (ab_re * hr - ab_im * hi)
            ci = cs[c * lc:(c + 1) * lc, SSM_FLAT:] + (ab_re * hi + ab_im * hr)
            h_r.append(pr * cr - pi * ci)
            h_i.append(pr * ci + pi * cr)
            hr = h_r[-1][lc - 1:lc, :]
            hi = h_i[-1][lc - 1:lc, :]
        hcat = jnp.concatenate([jnp.concatenate(h_r, axis=0), jnp.concatenate(h_i, axis=0)], axis=1).astype(BF16)
        y = _dot(hcat, cd_ref[...]) + d_ref[...] * u
        g = 0.5 * y * (1.0 + jnp.tanh(math.sqrt(2.0 / math.pi) * (y + 0.044715 * (y * y * y))))
        y_ref[0, rows, :] = g * _sigmoid(_dot(g.astype(BF16), wglu_ref[...]))
    carry_ref[0:1, :] = hr
    carry_ref[1:2, :] = hi
    hr_ref[0] = hr
    hi_ref[0] = hi


def _ssm(u, h0r, h0i, sw, lblk, span, lc):
    b, t, _ = u.shape
    consts = [sw["ab"], sw["bd"], sw["cd"], sw["pwp"], sw["pwn"], sw["tril"], sw["d"], sw["wglu"]]
    state_spec = pl.BlockSpec((1, 1, SSM_FLAT), lambda i, j: (i, 0, 0))
    return pl.pallas_call(
        functools.partial(_ssm_body, n_part=lblk // span, n_sub=span // lc, lc=lc),
        grid=(b, t // lblk),
        in_specs=[pl.BlockSpec((1, lblk, SSM_W), lambda i, j: (i, j, 0)), state_spec, state_spec]
        + [_const_spec(c.shape) for c in consts],
        out_specs=[pl.BlockSpec((1, lblk, SSM_W), lambda i, j: (i, j, 0)), state_spec, state_spec],
        out_shape=[jax.ShapeDtypeStruct((b, t, SSM_W), F32), jax.ShapeDtypeStruct((b, 1, SSM_FLAT), F32),
                   jax.ShapeDtypeStruct((b, 1, SSM_FLAT), F32)],
        scratch_shapes=[pltpu.VMEM((8, SSM_FLAT), F32)],
        compiler_params=_cparams("parallel", "arbitrary"),
        name="ssm",
    )(u, h0r, h0i, *consts)


def _gla_body(p_ref, s0_ref, wg_ref, bg_ref, tril_ref, mavg_ref, gn_ref, spread_ref, o_ref, sfin_ref, s_ref, *,
              n_part, n_sub, lc):
    row_v = lax.broadcasted_iota(jnp.int32, (GLA_W, GLA_KP), 0)
    col_k = lax.broadcasted_iota(jnp.int32, (GLA_W, GLA_KP), 1)
    same_head = ((row_v >> _log2(GLA_DV)) == (col_k >> _log2(GLA_DK))).astype(F32)

    @pl.when(pl.program_id(1) == 0)
    def _():
        s_ref[...] = _dot_f32_lhs(s0_ref[0], spread_ref[...]) * same_head

    lane_k = lax.broadcasted_iota(jnp.int32, (1, GLA_KP), 1)
    lane_v = lax.broadcasted_iota(jnp.int32, (1, GLA_W), 1)
    head_k = [((lane_k >= h * GLA_DK) & (lane_k < (h + 1) * GLA_DK)).astype(F32) for h in range(GLA_HEADS)]
    head_v = [((lane_v >= h * GLA_DV) & (lane_v < (h + 1) * GLA_DV)).astype(F32) for h in range(GLA_HEADS)]
    row_t = lax.broadcasted_iota(jnp.int32, (GLA_HEADS * lc, lc), 0)
    col_s = lax.broadcasted_iota(jnp.int32, (GLA_HEADS * lc, lc), 1)
    causal = (row_t & (lc - 1)) >= col_s
    mid = lc // 2 - 1
    span = n_sub * lc
    st = s_ref[...]
    for part in range(n_part):
        rows = slice(part * span, (part + 1) * span)
        z = p_ref[0, rows, GLA_Z0:GLA_Z0 + LANES]
        gin = _dot(z.astype(BF16), wg_ref[...]) + bg_ref[...]
        glog = (jnp.minimum(gin, 0.0) - jnp.log(1.0 + jnp.exp(-jnp.abs(gin)))) / GLA_GATE_NORM
        g_all = _dot_f32_rhs(tril_ref[...], glog)
        outs = []
        for c in range(n_sub):
            crow = slice(part * span + c * lc, part * span + (c + 1) * lc)
            q = p_ref[0, crow, GLA_Q0:GLA_Q0 + GLA_KP] * (GLA_DK ** -0.5)
            k = p_ref[0, crow, GLA_K0:GLA_K0 + GLA_KP]
            vb = p_ref[0, crow, GLA_V0:GLA_V0 + GLA_W].astype(BF16)
            gc = g_all[c * lc:(c + 1) * lc, :]
            gl = gc[lc - 1:lc, :]
            gm = gc[mid:mid + 1, :]
            qc = (q * jnp.exp(gc - gm))
            ke = (k * jnp.exp(gm - gc)).astype(BF16)
            kd = (k * jnp.exp(gl - gc)).astype(BF16)
            qs = jnp.concatenate([qc * m for m in head_k], axis=0).astype(BF16)
            attn = jnp.where(causal, _dot_nt(qs, ke), 0.0)
            o2 = _dot(attn.astype(BF16), vb)
            o = _dot_nt((q * jnp.exp(gc)).astype(BF16), st.astype(BF16))
            for h in range(GLA_HEADS):
                o = o + head_v[h] * o2[h * lc:(h + 1) * lc, :]
            st = st * jnp.exp(gl) + _dot_tn(vb, kd) * same_head
            outs.append(o)
        o = jnp.concatenate(outs, axis=0) if n_sub > 1 else outs[0]
        og = p_ref[0, rows, GLA_OG0:GLA_OG0 + GLA_W]
        ms = _dot_f32_lhs(o * o, mavg_ref[...])
        on = o * lax.rsqrt(ms + RMS_EPS) * gn_ref[...]
        o_ref[0, rows, :] = on * (og * _sigmoid(og))
    s_ref[...] = st

    @pl.when(pl.program_id(1) == pl.num_programs(1) - 1)
    def _():
        hi, lo = _hi_lo(st)
        sfin_ref[0] = _dot_nt(hi, spread_ref[...]) + _dot_nt(lo, spread_ref[...])


def _gla(p, s0, gw, lblk, span, lc):
    b, t, _ = p.shape
    consts = [gw["wg"], gw["bg"], gw["tril"], gw["mavg"], gw["gn"], gw["spread"]]
    st_spec = pl.BlockSpec((1, GLA_W, GLA_DK), lambda i, j: (i, 0, 0))
    return pl.pallas_call(
        functools.partial(_gla_body, n_part=lblk // span, n_sub=span // lc, lc=lc),
        grid=(b, t // lblk),
        in_specs=[pl.BlockSpec((1, lblk, GLA_PW), lambda i, j: (i, j, 0)), st_spec]
        + [_const_spec(c.shape) for c in consts],
        out_specs=[pl.BlockSpec((1, lblk, GLA_W), lambda i, j: (i, j, 0)), st_spec],
        out_shape=[jax.ShapeDtypeStruct((b, t, GLA_W), F32), jax.ShapeDtypeStruct((b, GLA_W, GLA_DK), F32)],
        scratch_shapes=[pltpu.VMEM((GLA_W, GLA_KP), F32)],
        compiler_params=_cparams("parallel", "arbitrary"),
        name="gla",
    )(p, s0, *consts)


def _t5_bias(t5_table, q_pos, k_pos):
    rel = k_pos[None, :] - q_pos[:, None]
    half = N_BUCKETS // 2
    max_exact = half // 2
    n = jnp.abs(rel)
    far = max_exact + (jnp.log(jnp.maximum(n, 1).astype(jnp.float32) / max_exact)
                       / math.log(T5_MAX_DIST / max_exact) * (half - max_exact)).astype(jnp.int32)
    bucket = jnp.where(rel > 0, half, 0) + jnp.where(n < max_exact, n, jnp.minimum(far, half - 1))
    return jnp.moveaxis(t5_table.astype(F32)[bucket], -1, -3)


def _swa_body(sink_ref, q_ref, kp_ref, vp_ref, kc_ref, vc_ref, bias_ref, o_ref, *, qb, cq, win, pos0):
    blk = pl.program_id(1)
    kwin = jnp.concatenate([kp_ref[0], kc_ref[0]], axis=0)
    vwin = jnp.concatenate([vp_ref[0], vc_ref[0]], axis=0)
    col = lax.broadcasted_iota(jnp.int32, (1, win), 1)
    row = lax.broadcasted_iota(jnp.int32, (SWA_REP * cq, 1), 0)
    for j in range(qb // cq):
        valid = (pos0 + blk * qb + j * cq + col) >= 0
        for g in range(SWA_KV_HEADS):
            heads = [SWA_REP * g + r for r in range(SWA_REP)]
            qs = jnp.concatenate(
                [q_ref[0, j * cq:(j + 1) * cq, h * SWA_HEAD_DIM:(h + 1) * SWA_HEAD_DIM] for h in heads], axis=0)
            kk = kwin[j * cq:j * cq + win, g * SWA_HEAD_DIM:(g + 1) * SWA_HEAD_DIM]
            vv = vwin[j * cq:j * cq + win, g * SWA_HEAD_DIM:(g + 1) * SWA_HEAD_DIM]
            s = _dot_nt(qs.astype(BF16), kk.astype(BF16)) * (SWA_HEAD_DIM ** -0.5) + bias_ref[g]
            s = jnp.where(valid, s, -1e30)
            sink = jnp.where(row < cq, sink_ref[heads[0]], jnp.where(row < 2 * cq, sink_ref[heads[1]], sink_ref[heads[2]]))
            m = jnp.maximum(jnp.max(s, axis=-1, keepdims=True), sink)
            e = jnp.exp(s - m)
            den = jnp.sum(e, axis=-1, keepdims=True) + jnp.exp(sink - m)
            o = _dot((e / den).astype(BF16), vv.astype(BF16))
            for r, h in enumerate(heads):
                o_ref[0, j * cq:(j + 1) * cq, h * SWA_HEAD_DIM:(h + 1) * SWA_HEAD_DIM] = o[r * cq:(r + 1) * cq, :]


def _swa(sinks, p_swa, k_prev, v_prev, prev_k_map, prev_v_map, bias, qb, cq, win, pos0):
    b, t, _ = p_swa.shape
    kcol, vcol = SWA_W // SWA_KV_W, SWA_W // SWA_KV_W + 1
    return pl.pallas_call(
        functools.partial(_swa_body, qb=qb, cq=cq, win=win, pos0=pos0),
        grid=(b, t // qb),
        in_specs=[pl.BlockSpec(memory_space=pltpu.SMEM),
                  pl.BlockSpec((1, qb, SWA_W), lambda i, j: (i, j, 0)),
                  pl.BlockSpec((1, SWA_WINDOW, SWA_KV_W), prev_k_map),
                  pl.BlockSpec((1, SWA_WINDOW, SWA_KV_W), prev_v_map),
                  pl.BlockSpec((1, qb, SWA_KV_W), lambda i, j: (i, j, kcol)),
                  pl.BlockSpec((1, qb, SWA_KV_W), lambda i, j: (i, j, vcol)),
                  _const_spec(bias.shape)],
        out_specs=pl.BlockSpec((1, qb, SWA_W), lambda i, j: (i, j, 0)),
        out_shape=jax.ShapeDtypeStruct((b, t, SWA_W), F32),
        compiler_params=_cparams("parallel", "parallel"),
        name="swa",
    )(sinks, p_swa, k_prev, v_prev, p_swa, p_swa, bias)


def _out_proj_body(x_ref, ys_ref, og_ref, os_ref, wo_ref, gf_ref, wr_ref, br_ref, x1_ref, xn_ref, sel_ref, aux_ref):
    x1 = (x_ref[...] + _dot(ys_ref[...].astype(BF16), wo_ref[0:SSM_W, :])
          + _dot(og_ref[...].astype(BF16), wo_ref[SSM_W:SSM_W + GLA_W, :])
          + _dot(os_ref[...].astype(BF16), wo_ref[SSM_W + GLA_W:MIX_W, :]))
    x1_ref[...] = x1
    xn = x1 * lax.rsqrt(jnp.mean(x1 * x1, axis=-1, keepdims=True) + RMS_EPS) * gf_ref[...]
    xb = xn.astype(BF16)
    xn_ref[...] = xb
    logit = _dot(xb, wr_ref[...]) + br_ref[...]
    lane_i = lax.broadcasted_iota(jnp.int32, logit.shape, 1)
    lane = lane_i.astype(F32)
    neg = -jnp.inf
    big = float(LANES)
    g_lane = lane_i - ROUTER_GROUP_LANE0
    lg = jnp.where((g_lane >= 0) & (g_lane < N_EXPERT_GROUPS), logit, neg)
    gmax = jnp.max(lg, axis=-1, keepdims=True)
    p_sel = 1.0 / jnp.sum(jnp.exp(lg - gmax), axis=-1, keepdims=True)
    g_idx = jnp.min(jnp.where(lg == gmax, g_lane.astype(F32), big), axis=-1, keepdims=True)
    lane_group = (lane_i >> _log2(EXPERTS_PER_GROUP)).astype(F32)
    in_group = (lane_i < N_EXPERTS) & (lane_group == g_idx)
    le = jnp.where(in_group, logit, neg)
    m1 = jnp.max(le, axis=-1, keepdims=True)
    i1 = jnp.min(jnp.where(le == m1, lane, big), axis=-1, keepdims=True)
    le2 = jnp.where(lane == i1, neg, le)
    m2 = jnp.max(le2, axis=-1, keepdims=True)
    i2 = jnp.min(jnp.where(le2 == m2, lane, big), axis=-1, keepdims=True)
    r = jnp.exp(m2 - m1)
    w1 = 1.0 / (1.0 + r)
    aux_ref[...] = p_sel * (jnp.where(lane_i == AUX_W1, w1, 0.0) + jnp.where(lane_i == AUX_W2, r * w1, 0.0))
    sel = jnp.where(lane == i1, 1.0, 0.0) + jnp.where((lane - float(N_EXPERTS)) == i2, 1.0, 0.0)
    sel_ref[...] = jnp.transpose(sel)[:2 * N_EXPERTS, :]


def _out_proj(x, ys, og, osw, wo, gf, wr, br, tm):
    n = x.shape[0]
    row = lambda w: pl.BlockSpec((tm, w), lambda i: (i, 0))
    return pl.pallas_call(
        _out_proj_body,
        grid=(n // tm,),
        in_specs=[row(D_MODEL), row(SSM_W), row(GLA_W), row(SWA_W), _const_spec(wo.shape), _const_spec(gf.shape),
                  _const_spec(wr.shape), _const_spec(br.shape)],
        out_specs=[row(D_MODEL), row(D_MODEL), pl.BlockSpec((2 * N_EXPERTS, tm), lambda i: (0, i)), row(LANES)],
        out_shape=[jax.ShapeDtypeStruct((n, D_MODEL), F32), jax.ShapeDtypeStruct((n, D_MODEL), BF16),
                   jax.ShapeDtypeStruct((2 * N_EXPERTS, n), F32), jax.ShapeDtypeStruct((n, LANES), F32)],
        compiler_params=_cparams("parallel"),
        name="out_proj",
    )(x, ys, og, osw, wo, gf, wr, br)


def _moe_sort_body(xn_ref, sel_ref, lo_ref, triu_ref, aux_ref, xs_ref, auxo_ref):
    s1 = sel_ref[0:N_EXPERTS, :]
    s2 = sel_ref[N_EXPERTS:2 * N_EXPERTS, :]
    rank = _dot((s1 + s2).astype(BF16), triu_ref[...])
    pos = lo_ref[0][:, 0:1] + rank
    d1 = jnp.sum(s1 * pos, axis=0, keepdims=True)
    d2 = jnp.sum(s2 * pos, axis=0, keepdims=True)
    r = lax.broadcasted_iota(jnp.int32, (MOE_RLOC, MOE_TB), 0).astype(F32)
    perm = jnp.where((r == d1) | (r == d2), 1.0, 0.0).astype(BF16)
    xs_ref[...] = _dot(perm, xn_ref[...]).astype(BF16)
    row = lax.broadcasted_iota(jnp.int32, (LANES, MOE_TB), 0)
    dt = jnp.where(row == AUX_D1, d1, jnp.where(row == AUX_D2, d2, 0.0))
    auxo_ref[...] = aux_ref[...] + jnp.transpose(dt)


def _moe_sort(xn, sel, lo_rows, triu, aux):
    n = xn.shape[0]
    nblk = n // MOE_TB
    return pl.pallas_call(
        _moe_sort_body,
        grid=(nblk,),
        in_specs=[pl.BlockSpec((MOE_TB, D_MODEL), lambda i: (i, 0)), pl.BlockSpec((2 * N_EXPERTS, MOE_TB), lambda i: (0, i)),
                  pl.BlockSpec((1, N_EXPERTS, LANES), lambda i: (i, 0, 0)), _const_spec(triu.shape),
                  pl.BlockSpec((MOE_TB, LANES), lambda i: (i, 0))],
        out_specs=[pl.BlockSpec((MOE_RLOC, D_MODEL), lambda i: (i, 0)), pl.BlockSpec((MOE_TB, LANES), lambda i: (i, 0))],
        out_shape=[jax.ShapeDtypeStruct((nblk * MOE_RLOC, D_MODEL), BF16), jax.ShapeDtypeStruct((n, LANES), F32)],
        compiler_params=_cparams("parallel"),
        name="moe_sort",
    )(xn, sel, lo_rows, triu, aux)


def _piece_copy(hbm_ref, piece, buf_ref, slot, p, sem, to_hbm):
    rows = pl.ds(pl.multiple_of(piece * MOE_PIECE, MOE_PIECE), MOE_PIECE)
    vm = buf_ref.at[slot, pl.ds(pl.multiple_of(p * MOE_PIECE, MOE_PIECE), MOE_PIECE)]
    if to_hbm:
        return pltpu.make_async_copy(vm, hbm_ref.at[rows], sem)
    return pltpu.make_async_copy(hbm_ref.at[rows], vm, sem)


def _moe_expert_body(texp_ref, piece_ref, nv_ref, xs_hbm, wg_ref, wu_ref, wd_ref, ys_hbm,
                     xbuf, ybuf, wgb, wub, wdb, sem_in, sem_out, *, pt, n_steps):
    i = pl.program_id(0)
    slot = i % 2

    def for_pieces(tile, s, hbm_ref, buf_ref, sem, to_hbm, wait):
        def body(p, carry):
            c = _piece_copy(hbm_ref, piece_ref[tile * pt + p], buf_ref, s, p, sem.at[s], to_hbm)
            c.wait() if wait else c.start()
            return carry
        lax.fori_loop(0, nv_ref[tile], body, 0)

    gather = lambda tile, s, wait: for_pieces(tile, s, xs_hbm, xbuf, sem_in, False, wait)
    scatter = lambda tile, s, wait: for_pieces(tile, s, ys_hbm, ybuf, sem_out, True, wait)

    @pl.when(i == 0)
    def _():
        xbuf[...] = jnp.zeros_like(xbuf)
        gather(0, 0, False)

    @pl.when(i + 1 < n_steps)
    def _():
        gather(i + 1, 1 - slot, False)

    gather(i, slot, True)

    @pl.when((i == 0) | (texp_ref[i] != texp_ref[jnp.maximum(i - 1, 0)]))
    def _():
        wgb[...] = wg_ref[0].astype(BF16)
        wub[...] = wu_ref[0].astype(BF16)
        wdb[...] = wd_ref[0].astype(BF16)

    @pl.when(nv_ref[i] > 0)
    def _():
        xb = xbuf[slot]
        a = _dot(xb, wgb[...])
        h = (a * _sigmoid(a)) * _dot(xb, wub[...])
        ybuf[slot] = _dot(h.astype(BF16), wdb[...]).astype(BF16)

    scatter(i, slot, False)

    @pl.when(i >= 1)
    def _():
        scatter(i - 1, 1 - slot, True)

    @pl.when(i == n_steps - 1)
    def _():
        scatter(i, slot, True)


def _moe_experts(tile_expert, piece, nvalid, xs, wg, wu, wd, layer, pt):
    n_steps = tile_expert.shape[0]
    tm = pt * MOE_PIECE
    wspec = lambda shape: pl.BlockSpec(shape, lambda i, te, pc, nv: (te[i] + layer * N_EXPERTS, 0, 0))
    grid_spec = pltpu.PrefetchScalarGridSpec(
        num_scalar_prefetch=3,
        grid=(n_steps,),
        in_specs=[pl.BlockSpec(memory_space=pl.ANY), wspec((1, D_MODEL, EXPERT_FF)), wspec((1, D_MODEL, EXPERT_FF)),
                  wspec((1, EXPERT_FF, D_MODEL))],
        out_specs=pl.BlockSpec(memory_space=pl.ANY),
        scratch_shapes=[pltpu.VMEM((2, tm, D_MODEL), BF16), pltpu.VMEM((2, tm, D_MODEL), BF16),
                        pltpu.VMEM((D_MODEL, EXPERT_FF), BF16), pltpu.VMEM((D_MODEL, EXPERT_FF), BF16),
                        pltpu.VMEM((EXPERT_FF, D_MODEL), BF16),
                        pltpu.SemaphoreType.DMA((2,)), pltpu.SemaphoreType.DMA((2,))],
    )
    return pl.pallas_call(
        functools.partial(_moe_expert_body, pt=pt, n_steps=n_steps),
        grid_spec=grid_spec,
        out_shape=jax.ShapeDtypeStruct(xs.shape, BF16),
        input_output_aliases={3: 0},
        compiler_params=_cparams("arbitrary"),
        name="moe_experts",
    )(tile_expert, piece, nvalid, xs, wg, wu, wd)


def _moe_combine_body(ys_ref, aux_ref, x1_ref, o_ref):
    col = lax.broadcasted_iota(jnp.int32, (MOE_TB, MOE_RLOC), 1).astype(F32)
    y = ys_ref[...]
    pick = lambda lane: _dot(jnp.where(col == aux_ref[:, lane:lane + 1], 1.0, 0.0).astype(BF16), y)
    o_ref[...] = (x1_ref[...] + aux_ref[:, AUX_W1:AUX_W1 + 1] * pick(AUX_D1)
                  + aux_ref[:, AUX_W2:AUX_W2 + 1] * pick(AUX_D2))


def _moe_combine(ys, aux, x1):
    n = x1.shape[0]
    return pl.pallas_call(
        _moe_combine_body,
        grid=(n // MOE_TB,),
        in_specs=[pl.BlockSpec((MOE_RLOC, D_MODEL), lambda i: (i, 0)), pl.BlockSpec((MOE_TB, LANES), lambda i: (i, 0)),
                  pl.BlockSpec((MOE_TB, D_MODEL), lambda i: (i, 0))],
        out_specs=pl.BlockSpec((MOE_TB, D_MODEL), lambda i: (i, 0)),
        out_shape=jax.ShapeDtypeStruct((n, D_MODEL), F32),
        compiler_params=_cparams("parallel"),
        name="moe_combine",
    )(ys, aux, x1)


def _route_tables(sel, pt):
    n = sel.shape[1]
    nblk = n // MOE_TB
    cnt = (sel[:N_EXPERTS] + sel[N_EXPERTS:]).reshape(N_EXPERTS, nblk, MOE_TB).sum(-1).astype(jnp.int32)
    pc = (cnt + MOE_PIECE - 1) // MOE_PIECE
    lo_p = jnp.cumsum(pc, axis=0) - pc
    lo_rows = jnp.broadcast_to((lo_p.T * MOE_PIECE).astype(F32)[:, :, None], (nblk, N_EXPERTS, LANES))
    pe = pc.sum(1)
    tiles_e = (pe + pt - 1) // pt
    tile_start = jnp.cumsum(tiles_e) - tiles_e
    seg_start = (tile_start[:, None] * pt + jnp.cumsum(pc, axis=1) - pc).reshape(-1)
    pcs = pc.reshape(-1)
    seg_src = (jnp.arange(nblk, dtype=jnp.int32)[None, :] * MOE_PPB + lo_p).reshape(-1)
    n_steps = -(-(2 * n // MOE_PIECE + nblk * N_EXPERTS + N_EXPERTS * (pt - 1)) // pt)
    slot = jnp.arange(n_steps * pt, dtype=jnp.int32)[:, None]
    in_seg = (slot >= seg_start[None, :]) & (slot < (seg_start + pcs)[None, :])
    piece = jnp.sum(jnp.where(in_seg, seg_src[None, :] + slot - seg_start[None, :], 0), axis=1).astype(jnp.int32)
    tile = jnp.arange(n_steps, dtype=jnp.int32)[:, None]
    in_exp = (tile >= tile_start[None, :]) & (tile < (tile_start + tiles_e)[None, :])
    tile_expert = jnp.sum(jnp.where(in_exp, jnp.arange(N_EXPERTS, dtype=jnp.int32)[None, :], 0), axis=1)
    tile_expert = jnp.where(tile[:, 0] < tiles_e.sum(), tile_expert, N_EXPERTS - 1).astype(jnp.int32)
    nvalid = jnp.sum(jnp.where(in_exp, jnp.clip(pe[None, :] - (tile - tile_start[None, :]) * pt, 0, pt), 0),
                     axis=1).astype(jnp.int32)
    return lo_rows, tile_expert, piece, nvalid


def _moe(xn, sel, aux, x1, wg, wu, wd, layer, triu, pt):
    lo_rows, tile_expert, piece, nvalid = _route_tables(sel, pt)
    xs, aux2 = _moe_sort(xn, sel, lo_rows, triu, aux)
    ys = _moe_experts(tile_expert, piece, nvalid, xs, wg, wu, wd, layer, pt)
    return _moe_combine(ys, aux2, x1)


def _seg_mean_matrix(width, seg):
    i = jnp.arange(width)
    return jnp.where((i[:, None] // seg) == (i[None, :] // seg), 1.0 / seg, 0.0).astype(BF16)


def _chunk_tril(span, lc):
    i = jnp.arange(span)
    return ((i[:, None] >= i[None, :]) & ((i[:, None] // lc) == (i[None, :] // lc))).astype(BF16)


def _prep_layer(lw, ssm_shapes, gla_shapes):
    w_in = lw["w_in"].astype(F32)
    cols = {}
    off = 0
    for name, wdt in (("u", SSM_W), ("qg", GLA_K), ("kg", GLA_K), ("vg", GLA_W), ("z", GLA_GATE_RANK), ("og", GLA_W),
                      ("qs", SWA_W), ("ks", SWA_KV_W), ("vs", SWA_KV_W)):
        cols[name] = w_in[:, off:off + wdt]
        off += wdt
    zpad = lambda wdt: jnp.zeros((D_MODEL, wdt), F32)
    w_all = jnp.concatenate(
        [cols["qg"], zpad(GLA_KP - GLA_K), cols["kg"], zpad(GLA_KP - GLA_K), cols["vg"], cols["og"], cols["z"],
         zpad(LANES - GLA_GATE_RANK), cols["qs"], cols["ks"], cols["vs"], cols["u"]], axis=1).astype(BF16)
    out = {"norm_mix": lw["norm_mix"].astype(F32)[None, :], "w_all": w_all,
           "mavg_qk": _seg_mean_matrix(SWA_W + SWA_KV_W, SWA_HEAD_DIM),
           "qk_gain": jnp.concatenate([jnp.tile(lw["swa_q_norm"].astype(F32), SWA_HEADS),
                                       jnp.tile(lw["swa_k_norm"].astype(F32), SWA_KV_HEADS)])[None, :]}
    a_re = lw["ssm_a_re"].astype(F32)
    a_im = lw["ssm_a_im"].astype(F32)
    dt = jnp.exp(lw["ssm_log_dt"].astype(F32))[:, None]
    mag = jnp.exp(a_re * dt)
    ab_re = mag * jnp.cos(a_im * dt)
    ab_im = mag * jnp.sin(a_im * dt)
    den = a_re * a_re + a_im * a_im
    nr = ab_re - 1.0
    f_re = (nr * a_re + ab_im * a_im) / den
    f_im = (ab_im * a_re - nr * a_im) / den
    b_re = lw["ssm_b_re"].astype(F32)
    b_im = lw["ssm_b_im"].astype(F32)
    bb_re = f_re[..., None] * b_re - f_im[..., None] * b_im
    bb_im = f_re[..., None] * b_im + f_im[..., None] * b_re
    eye_g = jnp.eye(SSM_GROUPS, dtype=F32)
    blockdiag_in = lambda bb: jnp.einsum("gpc,gh->gchp", bb, eye_g).reshape(SSM_W, SSM_FLAT)
    blockdiag_out = lambda cc: jnp.einsum("gcp,gh->gphc", cc, eye_g).reshape(SSM_FLAT, SSM_W)
    bd = jnp.concatenate([blockdiag_in(bb_re), blockdiag_in(bb_im)], axis=1).astype(BF16)
    cd = jnp.concatenate([blockdiag_out(lw["ssm_c_re"].astype(F32)),
                          -blockdiag_out(lw["ssm_c_im"].astype(F32))], axis=0).astype(BF16)
    ssm = {"ab": jnp.concatenate([ab_re.reshape(1, SSM_FLAT), ab_im.reshape(1, SSM_FLAT),
                                  jnp.zeros((6, SSM_FLAT), F32)], axis=0),
           "bd": bd, "cd": cd, "d": lw["ssm_d"].astype(F32)[None, :], "wglu": lw["ssm_w_glu"].astype(BF16)}
    def powers(lc, sign):
        tt = jnp.arange(lc, dtype=F32)[:, None, None]
        m = jnp.exp(sign * tt * (a_re * dt)[None])
        ang = sign * tt * (a_im * dt)[None]
        return jnp.stack([(m * jnp.cos(ang)).reshape(lc, SSM_FLAT), (m * jnp.sin(ang)).reshape(lc, SSM_FLAT)])

    out["ssm"] = {(span, lc): dict(ssm, pwp=powers(lc, 1.0), pwn=powers(lc, -1.0), tril=_chunk_tril(span, lc))
                  for span, lc in ssm_shapes}
    wg = jnp.zeros((LANES, GLA_KP), F32).at[:GLA_GATE_RANK, :GLA_K].set(lw["gla_w_gate"].astype(F32)).astype(BF16)
    bg = jnp.zeros((1, GLA_KP), F32).at[0, :GLA_K].set(lw["gla_b_gate"].astype(F32))
    kk = jnp.arange(GLA_KP)
    spread = ((kk[None, :] % GLA_DK) == jnp.arange(GLA_DK)[:, None]) & (kk[None, :] < GLA_K)
    gla = {"wg": wg, "bg": bg, "mavg": _seg_mean_matrix(GLA_W, GLA_DV),
           "gn": jnp.tile(lw["gla_norm"].astype(F32), GLA_HEADS)[None, :], "spread": spread.astype(BF16)}
    out["gla"] = {(span, lc): dict(gla, tril=_chunk_tril(span, lc)) for span, lc in gla_shapes}
    out["sinks"] = lw["swa_sinks"].astype(F32)
    out["w_out"] = lw["w_out"].astype(BF16)
    out["norm_ffn"] = lw["norm_ffn"].astype(F32)[None, :]
    wr = jnp.zeros((D_MODEL, LANES), F32)
    wr = wr.at[:, :N_EXPERTS].set(lw["moe_w_expert"].astype(F32))
    wr = wr.at[:, ROUTER_GROUP_LANE0:ROUTER_GROUP_LANE0 + N_EXPERT_GROUPS].set(lw["moe_w_group"].astype(F32))
    out["wr"] = wr.astype(BF16)
    br = jnp.zeros((1, LANES), F32)
    br = br.at[0, :N_EXPERTS].set(lw["moe_b_expert"].astype(F32))
    br = br.at[0, ROUTER_GROUP_LANE0:ROUTER_GROUP_LANE0 + N_EXPERT_GROUPS].set(lw["moe_b_group"].astype(F32))
    out["br"] = br
    ti = jnp.arange(MOE_TB)
    out["moe_triu"] = (ti[:, None] < ti[None, :]).astype(BF16)
    return out


def _gla_state_in(h0):
    return jnp.transpose(h0.astype(F32), (0, 1, 3, 2)).reshape(h0.shape[0], GLA_W, GLA_DK)


def _gla_state_out(st):
    return jnp.transpose(st.reshape(st.shape[0], GLA_HEADS, GLA_DV, GLA_DK), (0, 1, 3, 2))


def _layer(x, pw, bias, ssm_h0r, ssm_h0i, gla_h0, past_k, past_v, cfg):
    b, t, _ = x.shape
    n = b * t
    p_gla, p_swa, p_ssm = _in_proj(x.reshape(n, D_MODEL), pw["norm_mix"], pw["w_all"], pw["mavg_qk"], pw["qk_gain"],
                                   cfg["tm"])
    p_gla = p_gla.reshape(b, t, GLA_PW)
    p_swa = p_swa.reshape(b, t, SWA_PW)
    y_ssm, h_re, h_im = _ssm(p_ssm.reshape(b, t, SSM_W), ssm_h0r.reshape(b, 1, SSM_FLAT).astype(F32),
                             ssm_h0i.reshape(b, 1, SSM_FLAT).astype(F32), pw["ssm"][cfg["ssm"]], cfg["lblk"],
                             *cfg["ssm"])
    o_g, s_fin = _gla(p_gla, _gla_state_in(gla_h0), pw["gla"][cfg["gla"]], cfg["lblk"], *cfg["gla"])
    kcol, vcol = SWA_W // SWA_KV_W, SWA_W // SWA_KV_W + 1
    if past_k is None:
        per_blk = cfg["qb"] // SWA_WINDOW
        prev_k_map = lambda i, j: (i, jnp.maximum(j * per_blk - 1, 0), kcol)
        prev_v_map = lambda i, j: (i, jnp.maximum(j * per_blk - 1, 0), vcol)
        k_prev, v_prev = p_swa, p_swa
        keep = min(SWA_WINDOW, t)
        new_k = p_swa[:, t - keep:, SWA_W:SWA_W + SWA_KV_W]
        new_v = p_swa[:, t - keep:, SWA_W + SWA_KV_W:]
    else:
        prev_k_map = prev_v_map = lambda i, j: (i, 0, 0)
        k_prev = past_k.reshape(b, SWA_WINDOW, SWA_KV_W).astype(F32)
        v_prev = past_v.reshape(b, SWA_WINDOW, SWA_KV_W).astype(F32)
        new_k = p_swa[:, :, SWA_W:SWA_W + SWA_KV_W]
        new_v = p_swa[:, :, SWA_W + SWA_KV_W:]
    o_s = _swa(pw["sinks"], p_swa, k_prev, v_prev, prev_k_map, prev_v_map, bias, cfg["qb"], cfg["cq"], cfg["win"],
               cfg["pos0"])
    x1, xn, sel, aux = _out_proj(x.reshape(n, D_MODEL), y_ssm.reshape(n, SSM_W), o_g.reshape(n, GLA_W),
                                 o_s.reshape(n, SWA_W), pw["w_out"], pw["norm_ffn"], pw["wr"], pw["br"],
                                 cfg["tm"])
    x2 = _moe(xn, sel, aux, x1, pw["moe_wg"], pw["moe_wu"], pw["moe_wd"], pw["layer"], pw["moe_triu"], cfg["pt"])
    kv_shape = (b, new_k.shape[1], SWA_KV_HEADS, SWA_HEAD_DIM)
    return (x2.reshape(b, t, D_MODEL), new_k.reshape(kv_shape), new_v.reshape(kv_shape), _gla_state_out(s_fin),
            h_re.reshape(b, SSM_GROUPS, SSM_STATE), h_im.reshape(b, SSM_GROUPS, SSM_STATE))


def _group_cfg(t, past_len, n_past):
    if n_past is None:
        assert t % (8 * CHUNK) == 0
        cfg = dict(ssm=(4 * CHUNK, CHUNK), gla=(4 * CHUNK, 2 * CHUNK), lblk=8 * CHUNK, qb=4 * CHUNK, cq=CHUNK,
                   win=SWA_WINDOW + CHUNK, pos0=-SWA_WINDOW, tm=512, pt=32)
        q_pos = jnp.arange(CHUNK, dtype=jnp.int32) + SWA_WINDOW
        k_pos = jnp.arange(SWA_WINDOW + CHUNK, dtype=jnp.int32)
    else:
        first_key, last_q = past_len - n_past, past_len + t - 1
        assert n_past == SWA_WINDOW and t <= CHUNK and t % 8 == 0
        assert past_len // CHUNK == last_q // CHUNK and first_key // CHUNK >= past_len // CHUNK - SWA_WINDOW // CHUNK
        cfg = dict(ssm=(t, t), gla=(t, t), lblk=t, qb=t, cq=t, win=n_past + t, pos0=first_key, tm=512, pt=16)
        q_pos = past_len + jnp.arange(t, dtype=jnp.int32)
        k_pos = first_key + jnp.arange(n_past + t, dtype=jnp.int32)
    return cfg, q_pos, k_pos


def _stacked_bias(t5_table, q_pos, k_pos):
    bias = _t5_bias(t5_table, q_pos, k_pos)
    nq, nk = bias.shape[1:]
    return bias.reshape(SWA_KV_HEADS, SWA_REP * nq, nk)


PAST_LEN = 1024


def kernel(x_prompt, x_sample, cache_swa_k, cache_swa_v, state_gla, state_ssm_re, state_ssm_im, norm_mix, w_in, ssm_a_re, ssm_a_im, ssm_log_dt, ssm_b_re, ssm_b_im, ssm_c_re, ssm_c_im, ssm_d, ssm_w_glu, gla_w_gate, gla_b_gate, gla_norm, swa_q_norm, swa_k_norm, swa_sinks, t5_table, w_out, norm_ffn, moe_w_group, moe_b_group, moe_w_expert, moe_b_expert, moe_w_gate, moe_w_up, moe_w_down):
    depth = w_in.shape[0]
    bp, tp, _ = x_prompt.shape
    bs, ts, _ = x_sample.shape
    cfg_p, qpos_p, kpos_p = _group_cfg(tp, 0, None)
    cfg_s, qpos_s, kpos_s = _group_cfg(ts, PAST_LEN, cache_swa_k.shape[2])
    bias_p = _stacked_bias(t5_table, qpos_p, kpos_p)
    bias_s = _stacked_bias(t5_table, qpos_s, kpos_s)
    hp, hs = x_prompt.astype(F32), x_sample.astype(F32)
    moe_wg = moe_w_gate.astype(F32).reshape(depth * N_EXPERTS, D_MODEL, EXPERT_FF)
    moe_wu = moe_w_up.astype(F32).reshape(depth * N_EXPERTS, D_MODEL, EXPERT_FF)
    moe_wd = moe_w_down.astype(F32).reshape(depth * N_EXPERTS, EXPERT_FF, D_MODEL)
    outs = [[] for _ in range(10)]
    for l in range(depth):
        lw = {
            "norm_mix": norm_mix[l], "w_in": w_in[l], "ssm_a_re": ssm_a_re[l], "ssm_a_im": ssm_a_im[l],
            "ssm_log_dt": ssm_log_dt[l], "ssm_b_re": ssm_b_re[l], "ssm_b_im": ssm_b_im[l], "ssm_c_re": ssm_c_re[l],
            "ssm_c_im": ssm_c_im[l], "ssm_d": ssm_d[l], "ssm_w_glu": ssm_w_glu[l], "gla_w_gate": gla_w_gate[l],
            "gla_b_gate": gla_b_gate[l], "gla_norm": gla_norm[l], "swa_q_norm": swa_q_norm[l],
            "swa_k_norm": swa_k_norm[l], "swa_sinks": swa_sinks[l], "w_out": w_out[l], "norm_ffn": norm_ffn[l],
            "moe_w_group": moe_w_group[l], "moe_b_group": moe_b_group[l], "moe_w_expert": moe_w_expert[l],
            "moe_b_expert": moe_b_expert[l],
        }
        pw = _prep_layer(lw, {cfg_p["ssm"], cfg_s["ssm"]}, {cfg_p["gla"], cfg_s["gla"]})
        pw.update(layer=l, moe_wg=moe_wg, moe_wu=moe_wu, moe_wd=moe_wd)
        zs = jnp.zeros((bp, SSM_GROUPS, SSM_STATE), F32)
        zg = jnp.zeros((bp, GLA_HEADS, GLA_DK, GLA_DV), F32)
        hp, nk, nv, ng, nr, ni = _layer(hp, pw, bias_p, zs, zs, zg, None, None, cfg_p)
        for slot, val in zip((0, 1, 4, 6, 7), (nk, nv, ng, nr, ni)):
            outs[slot].append(val)
        hs, nk, nv, ng, nr, ni = _layer(hs, pw, bias_s, state_ssm_re[l], state_ssm_im[l], state_gla[l],
                                        cache_swa_k[l], cache_swa_v[l], cfg_s)
        for slot, val in zip((2, 3, 5, 8, 9), (nk, nv, ng, nr, ni)):
            outs[slot].append(val)
    return (hp, hs) + tuple(jnp.stack(o) for o in outs)
```

```python
import functools
import math

import jax
import jax.numpy as jnp
from jax import lax
from jax.experimental import pallas as pl
from jax.experimental.pallas import tpu as pltpu

F32 = jnp.float32
BF16 = jnp.bfloat16

D_MODEL = 1024
CHUNK = 64
RMS_EPS = 1e-6
SSM_GROUPS = 16
SSM_GC = 16
SSM_STATE = 64
SSM_W = SSM_GROUPS * SSM_GC
SSM_FLAT = SSM_GROUPS * SSM_STATE
GLA_HEADS = 6
GLA_DK = 32
GLA_DV = 64
GLA_GATE_RANK = 16
GLA_GATE_NORM = 16.0
GLA_K = GLA_HEADS * GLA_DK
GLA_KP = 256
GLA_W = GLA_HEADS * GLA_DV
SWA_HEADS = 6
SWA_KV_HEADS = 2
SWA_REP = SWA_HEADS // SWA_KV_HEADS
SWA_HEAD_DIM = 64
SWA_WINDOW = 128
SWA_W = SWA_HEADS * SWA_HEAD_DIM
SWA_KV_W = SWA_KV_HEADS * SWA_HEAD_DIM
MIX_W = SSM_W + GLA_W + SWA_W
N_BUCKETS = 32
T5_MAX_DIST = 128
N_EXPERT_GROUPS = 4
EXPERTS_PER_GROUP = 4
N_EXPERTS = 16
EXPERT_FF = 512
LANES = 128
ROUTER_GROUP_LANE0 = N_EXPERTS
AUX_W1, AUX_W2, AUX_D1, AUX_D2 = 0, 1, 2, 3
AUX_ROWS = 8
MOE_TB = 256
MOE_PIECE = 16
MOE_RLOC = 2 * MOE_TB + N_EXPERTS * MOE_PIECE
MOE_PPB = MOE_RLOC // MOE_PIECE

GLA_Q0, GLA_K0, GLA_V0, GLA_OG0, GLA_Z0, GLA_PW = 0, 256, 512, 896, 1280, 1408
SWA_PW = SWA_W + 2 * SWA_KV_W
P_GLA0, P_SWA0, P_SSM0, P_TOTAL = 0, GLA_PW, GLA_PW + SWA_PW, GLA_PW + SWA_PW + SSM_W

VMEM_LIMIT = 48 * 1024 * 1024


def _cparams(*sem):
    return pltpu.CompilerParams(dimension_semantics=sem, vmem_limit_bytes=VMEM_LIMIT)


def _dot(a, b):
    return jnp.dot(a, b, preferred_element_type=F32)


def _dot_nt(a, b):
    return lax.dot_general(a, b, (((1,), (1,)), ((), ())), preferred_element_type=F32)


def _dot_tn(a, b):
    return lax.dot_general(a, b, (((0,), (0,)), ((), ())), preferred_element_type=F32)


def _hi_lo(x):
    hi = x.astype(BF16)
    return hi, (x - hi.astype(F32)).astype(BF16)


def _dot_f32_rhs(a_bf16, x):
    hi, lo = _hi_lo(x)
    return _dot(a_bf16, hi) + _dot(a_bf16, lo)


def _dot_f32_lhs(x, b_bf16):
    hi, lo = _hi_lo(x)
    return _dot(hi, b_bf16) + _dot(lo, b_bf16)


def _log2(n):
    assert n & (n - 1) == 0
    return n.bit_length() - 1


def _sigmoid(x):
    return 1.0 / (1.0 + jnp.exp(-x))


def _const_spec(shape):
    nd = len(shape)
    return pl.BlockSpec(shape, lambda *_: (0,) * nd)


def _in_proj_body(x_ref, g_ref, w_ref, mavg_ref, qkg_ref, gla_ref, swa_ref, ssm_ref):
    x = x_ref[...]
    xn = x * lax.rsqrt(jnp.mean(x * x, axis=-1, keepdims=True) + RMS_EPS) * g_ref[...]
    xb = xn.astype(BF16)
    gla_ref[...] = _dot(xb, w_ref[:, P_GLA0:P_SWA0])
    ssm_ref[...] = _dot(xb, w_ref[:, P_SSM0:P_TOTAL])
    s = _dot(xb, w_ref[:, P_SWA0:P_SSM0])
    qk = s[:, :SWA_W + SWA_KV_W]
    ms = _dot_f32_lhs(qk * qk, mavg_ref[...])
    swa_ref[:, :SWA_W + SWA_KV_W] = qk * lax.rsqrt(ms + RMS_EPS) * qkg_ref[...]
    swa_ref[:, SWA_W + SWA_KV_W:] = s[:, SWA_W + SWA_KV_W:]


def _in_proj(x, g, w, mavg, qkg, tm):
    n = x.shape[0]
    return pl.pallas_call(
        _in_proj_body,
        grid=(n // tm,),
        in_specs=[pl.BlockSpec((tm, D_MODEL), lambda i: (i, 0)), _const_spec(g.shape), _const_spec(w.shape),
                  _const_spec(mavg.shape), _const_spec(qkg.shape)],
        out_specs=[pl.BlockSpec((tm, GLA_PW), lambda i: (i, 0)), pl.BlockSpec((tm, SWA_PW), lambda i: (i, 0)),
                   pl.BlockSpec((tm, SSM_W), lambda i: (i, 0))],
        out_shape=[jax.ShapeDtypeStruct((n, GLA_PW), F32), jax.ShapeDtypeStruct((n, SWA_PW), F32),
                   jax.ShapeDtypeStruct((n, SSM_W), F32)],
        compiler_params=_cparams("parallel"),
        name="in_proj",
    )(x, g, w, mavg, qkg)


def _ssm_body(u_ref, h0r_ref, h0i_ref, ab_ref, bd_ref, cd_ref, pwp_ref, pwn_ref, tril_ref, d_ref, wglu_ref,
              y_ref, hr_ref, hi_ref, carry_ref, *, n_part, n_sub, lc):
    @pl.when(pl.program_id(1) == 0)
    def _():
        carry_ref[0:1, :] = h0r_ref[0]
        carry_ref[1:2, :] = h0i_ref[0]

    hr = carry_ref[0:1, :]
    hi = carry_ref[1:2, :]
    ab_re = ab_ref[0:1, :]
    ab_im = ab_ref[1:2, :]
    nr, ni = pwn_ref[0], pwn_ref[1]
    pr, pi = pwp_ref[0], pwp_ref[1]
    span = n_sub * lc
    for part in range(n_part):
        rows = slice(part * span, (part + 1) * span)
        u = u_ref[0, rows, :]
        bu = _dot(u.astype(BF16), bd_ref[...])
        sr, si = [], []
        for c in range(n_sub):
            bur = bu[c * lc:(c + 1) * lc, :SSM_FLAT]
            bui = bu[c * lc:(c + 1) * lc, SSM_FLAT:]
            sr.append(nr * bur - ni * bui)
            si.append(nr * bui + ni * bur)
        scaled = jnp.concatenate([jnp.concatenate(sr, axis=0), jnp.concatenate(si, axis=0)], axis=1)
        cs = _dot_f32_rhs(tril_ref[...], scaled)
        h_r, h_i = [], []
        for c in range(n_sub):
            cr = cs[c * lc:(c + 1) * lc, :SSM_FLAT] + (ab_re * hr - ab_im * hi)
            ci = cs[c * lc:(c + 1) * lc, SSM_FLAT:] + (ab_re * hi + ab_im * hr)
            h_r.append(pr * cr - pi * ci)
            h_i.append(pr * ci + pi * cr)
            hr = h_r[-1][lc - 1:lc, :]
            hi = h_i[-1][lc - 1:lc, :]
        hcat = jnp.concatenate([jnp.concatenate(h_r, axis=0), jnp.concatenate(h_i, axis=0)], axis=1).astype(BF16)
        y = _dot(hcat, cd_ref[...]) + d_ref[...] * u
        g = 0.5 * y * (1.0 + jnp.tanh(math.sqrt(2.0 / math.pi) * (y + 0.044715 * (y * y * y))))
        y_ref[0, rows, :] = g * _sigmoid(_dot(g.astype(BF16), wglu_ref[...]))
    carry_ref[0:1, :] = hr
    carry_ref[1:2, :] = hi
    hr_ref[0] = hr
    hi_ref[0] = hi


def _ssm(u, h0r, h0i, sw, lblk, span, lc):
    b, t, _ = u.shape
    consts = [sw["ab"], sw["bd"], sw["cd"], sw["pwp"], sw["pwn"], sw["tril"], sw["d"], sw["wglu"]]
    state_spec = pl.BlockSpec((1, 1, SSM_FLAT), lambda i, j: (i, 0, 0))
    return pl.pallas_call(
        functools.partial(_ssm_body, n_part=lblk // span, n_sub=span // lc, lc=lc),
        grid=(b, t // lblk),
        in_specs=[pl.BlockSpec((1, lblk, SSM_W), lambda i, j: (i, j, 0)), state_spec, state_spec]
        + [_const_spec(c.shape) for c in consts],
        out_specs=[pl.BlockSpec((1, lblk, SSM_W), lambda i, j: (i, j, 0)), state_spec, state_spec],
        out_shape=[jax.ShapeDtypeStruct((b, t, SSM_W), F32), jax.ShapeDtypeStruct((b, 1, SSM_FLAT), F32),
                   jax.ShapeDtypeStruct((b, 1, SSM_FLAT), F32)],
        scratch_shapes=[pltpu.VMEM((8, SSM_FLAT), F32)],
        compiler_params=_cparams("parallel", "arbitrary"),
        name="ssm",
    )(u, h0r, h0i, *consts)


def _gla_body(p_ref, s0_ref, wg_ref, bg_ref, tril_ref, mavg_ref, gn_ref, spread_ref, o_ref, sfin_ref, s_ref, *,
              n_part, n_sub, lc):
    row_v = lax.broadcasted_iota(jnp.int32, (GLA_W, GLA_KP), 0)
    col_k = lax.broadcasted_iota(jnp.int32, (GLA_W, GLA_KP), 1)
    same_head = ((row_v >> _log2(GLA_DV)) == (col_k >> _log2(GLA_DK))).astype(F32)

    @pl.when(pl.program_id(1) == 0)
    def _():
        s_ref[...] = _dot_f32_lhs(s0_ref[0], spread_ref[...]) * same_head

    lane_k = lax.broadcasted_iota(jnp.int32, (1, GLA_KP), 1)
    lane_v = lax.broadcasted_iota(jnp.int32, (1, GLA_W), 1)
    head_k = [((lane_k >= h * GLA_DK) & (lane_k < (h + 1) * GLA_DK)).astype(F32) for h in range(GLA_HEADS)]
    head_v = [((lane_v >= h * GLA_DV) & (lane_v < (h + 1) * GLA_DV)).astype(F32) for h in range(GLA_HEADS)]
    row_t = lax.broadcasted_iota(jnp.int32, (GLA_HEADS * lc, lc), 0)
    col_s = lax.broadcasted_iota(jnp.int32, (GLA_HEADS * lc, lc), 1)
    causal = (row_t & (lc - 1)) >= col_s
    mid = lc // 2 - 1
    span = n_sub * lc
    st = s_ref[...]
    for part in range(n_part):
        rows = slice(part * span, (part + 1) * span)
        z = p_ref[0, rows, GLA_Z0:GLA_Z0 + LANES]
        gin = _dot(z.astype(BF16), wg_ref[...]) + bg_ref[...]
        glog = (jnp.minimum(gin, 0.0) - jnp.log(1.0 + jnp.exp(-jnp.abs(gin)))) / GLA_GATE_NORM
        g_all = _dot_f32_rhs(tril_ref[...], glog)
        outs = []
        for c in range(n_sub):
            crow = slice(part * span + c * lc, part * span + (c + 1) * lc)
            q = p_ref[0, crow, GLA_Q0:GLA_Q0 + GLA_KP] * (GLA_DK ** -0.5)
            k = p_ref[0, crow, GLA_K0:GLA_K0 + GLA_KP]
            vb = p_ref[0, crow, GLA_V0:GLA_V0 + GLA_W].astype(BF16)
            gc = g_all[c * lc:(c + 1) * lc, :]
            gl = gc[lc - 1:lc, :]
            gm = gc[mid:mid + 1, :]
            qc = (q * jnp.exp(gc - gm))
            ke = (k * jnp.exp(gm - gc)).astype(BF16)
            kd = (k * jnp.exp(gl - gc)).astype(BF16)
            qs = jnp.concatenate([qc * m for m in head_k], axis=0).astype(BF16)
            attn = jnp.where(causal, _dot_nt(qs, ke), 0.0)
            o2 = _dot(attn.astype(BF16), vb)
            o = _dot_nt((q * jnp.exp(gc)).astype(BF16), st.astype(BF16))
            for h in range(GLA_HEADS):
                o = o + head_v[h] * o2[h * lc:(h + 1) * lc, :]
            st = st * jnp.exp(gl) + _dot_tn(vb, kd) * same_head
            outs.append(o)
        o = jnp.concatenate(outs, axis=0) if n_sub > 1 else outs[0]
        og = p_ref[0, rows, GLA_OG0:GLA_OG0 + GLA_W]
        ms = _dot_f32_lhs(o * o, mavg_ref[...])
        on = o * lax.rsqrt(ms + RMS_EPS) * gn_ref[...]
        o_ref[0, rows, :] = on * (og * _sigmoid(og))
    s_ref[...] = st

    @pl.when(pl.program_id(1) == pl.num_programs(1) - 1)
    def _():
        hi, lo = _hi_lo(st)
        sfin_ref[0] = _dot_nt(hi, spread_ref[...]) + _dot_nt(lo, spread_ref[...])


def _gla(p, s0, gw, lblk, span, lc):
    b, t, _ = p.shape
    consts = [gw["wg"], gw["bg"], gw["tril"], gw["mavg"], gw["gn"], gw["spread"]]
    st_spec = pl.BlockSpec((1, GLA_W, GLA_DK), lambda i, j: (i, 0, 0))
    return pl.pallas_call(
        functools.partial(_gla_body, n_part=lblk // span, n_sub=span // lc, lc=lc),
        grid=(b, t // lblk),
        in_specs=[pl.BlockSpec((1, lblk, GLA_PW), lambda i, j: (i, j, 0)), st_spec]
        + [_const_spec(c.shape) for c in consts],
        out_specs=[pl.BlockSpec((1, lblk, GLA_W), lambda i, j: (i, j, 0)), st_spec],
        out_shape=[jax.ShapeDtypeStruct((b, t, GLA_W), F32), jax.ShapeDtypeStruct((b, GLA_W, GLA_DK), F32)],
        scratch_shapes=[pltpu.VMEM((GLA_W, GLA_KP), F32)],
        compiler_params=_cparams("parallel", "arbitrary"),
        name="gla",
    )(p, s0, *consts)


def _t5_bias(t5_table, q_pos, k_pos):
    rel = k_pos[None, :] - q_pos[:, None]
    half = N_BUCKETS // 2
    max_exact = half // 2
    n = jnp.abs(rel)
    far = max_exact + (jnp.log(jnp.maximum(n, 1).astype(jnp.float32) / max_exact)
                       / math.log(T5_MAX_DIST / max_exact) * (half - max_exact)).astype(jnp.int32)
    bucket = jnp.where(rel > 0, half, 0) + jnp.where(n < max_exact, n, jnp.minimum(far, half - 1))
    return jnp.moveaxis(t5_table.astype(F32)[bucket], -1, -3)


def _swa_body(sink_ref, q_ref, kp_ref, vp_ref, kc_ref, vc_ref, bias_ref, o_ref, *, qb, cq, win, pos0):
    blk = pl.program_id(1)
    kwin = jnp.concatenate([kp_ref[0], kc_ref[0]], axis=0)
    vwin = jnp.concatenate([vp_ref[0], vc_ref[0]], axis=0)
    col = lax.broadcasted_iota(jnp.int32, (1, win), 1)
    row = lax.broadcasted_iota(jnp.int32, (SWA_REP * cq, 1), 0)
    for j in range(qb // cq):
        valid = (pos0 + blk * qb + j * cq + col) >= 0
        for g in range(SWA_KV_HEADS):
            heads = [SWA_REP * g + r for r in range(SWA_REP)]
            qs = jnp.concatenate(
                [q_ref[0, j * cq:(j + 1) * cq, h * SWA_HEAD_DIM:(h + 1) * SWA_HEAD_DIM] for h in heads], axis=0)
            kk = kwin[j * cq:j * cq + win, g * SWA_HEAD_DIM:(g + 1) * SWA_HEAD_DIM]
            vv = vwin[j * cq:j * cq + win, g * SWA_HEAD_DIM:(g + 1) * SWA_HEAD_DIM]
            s = _dot_nt(qs.astype(BF16), kk.astype(BF16)) * (SWA_HEAD_DIM ** -0.5) + bias_ref[g]
            s = jnp.where(valid, s, -1e30)
            sink = jnp.where(row < cq, sink_ref[heads[0]], jnp.where(row < 2 * cq, sink_ref[heads[1]], sink_ref[heads[2]]))
            m = jnp.maximum(jnp.max(s, axis=-1, keepdims=True), sink)
            e = jnp.exp(s - m)
            den = jnp.sum(e, axis=-1, keepdims=True) + jnp.exp(sink - m)
            o = _dot(e.astype(BF16), vv.astype(BF16)) * (1.0 / den)
            for r, h in enumerate(heads):
                o_ref[0, j * cq:(j + 1) * cq, h * SWA_HEAD_DIM:(h + 1) * SWA_HEAD_DIM] = o[r * cq:(r + 1) * cq, :]


def _swa(sinks, p_swa, k_prev, v_prev, prev_k_map, prev_v_map, bias, qb, cq, win, pos0):
    b, t, _ = p_swa.shape
    kcol, vcol = SWA_W // SWA_KV_W, SWA_W // SWA_KV_W + 1
    return pl.pallas_call(
        functools.partial(_swa_body, qb=qb, cq=cq, win=win, pos0=pos0),
        grid=(b, t // qb),
        in_specs=[pl.BlockSpec(memory_space=pltpu.SMEM),
                  pl.BlockSpec((1, qb, SWA_W), lambda i, j: (i, j, 0)),
                  pl.BlockSpec((1, SWA_WINDOW, SWA_KV_W), prev_k_map),
                  pl.BlockSpec((1, SWA_WINDOW, SWA_KV_W), prev_v_map),
                  pl.BlockSpec((1, qb, SWA_KV_W), lambda i, j: (i, j, kcol)),
                  pl.BlockSpec((1, qb, SWA_KV_W), lambda i, j: (i, j, vcol)),
                  _const_spec(bias.shape)],
        out_specs=pl.BlockSpec((1, qb, SWA_W), lambda i, j: (i, j, 0)),
        out_shape=jax.ShapeDtypeStruct((b, t, SWA_W), F32),
        compiler_params=_cparams("parallel", "parallel"),
        name="swa",
    )(sinks, p_swa, k_prev, v_prev, p_swa, p_swa, bias)


def _out_proj_body(x_ref, ys_ref, og_ref, os_ref, wo_ref, gf_ref, wr_ref, br_ref, x1_ref, xn_ref, sel_ref, aux_ref):
    x1 = (x_ref[...] + _dot(ys_ref[...].astype(BF16), wo_ref[0:SSM_W, :])
          + _dot(og_ref[...].astype(BF16), wo_ref[SSM_W:SSM_W + GLA_W, :])
          + _dot(os_ref[...].astype(BF16), wo_ref[SSM_W + GLA_W:MIX_W, :]))
    x1_ref[...] = x1
    xn = x1 * lax.rsqrt(jnp.mean(x1 * x1, axis=-1, keepdims=True) + RMS_EPS) * gf_ref[...]
    xb = xn.astype(BF16)
    xn_ref[...] = xb
    logit = _dot(xb, wr_ref[...]) + br_ref[...]
    lt = jnp.transpose(logit)[:2 * N_EXPERTS, :]
    row_i = lax.broadcasted_iota(jnp.int32, lt.shape, 0)
    row = row_i.astype(F32)
    neg = -jnp.inf
    big = float(LANES)
    g_row = row_i - ROUTER_GROUP_LANE0
    lg = jnp.where((g_row >= 0) & (g_row < N_EXPERT_GROUPS), lt, neg)
    gmax = jnp.max(lg, axis=0, keepdims=True)
    p_sel = 1.0 / jnp.sum(jnp.exp(lg - gmax), axis=0, keepdims=True)
    g_idx = jnp.min(jnp.where(lg == gmax, g_row.astype(F32), big), axis=0, keepdims=True)
    row_group = (row_i >> _log2(EXPERTS_PER_GROUP)).astype(F32)
    in_group = (row_i < N_EXPERTS) & (row_group == g_idx)
    le = jnp.where(in_group, lt, neg)
    m1 = jnp.max(le, axis=0, keepdims=True)
    i1 = jnp.min(jnp.where(le == m1, row, big), axis=0, keepdims=True)
    le2 = jnp.where(row == i1, neg, le)
    m2 = jnp.max(le2, axis=0, keepdims=True)
    i2 = jnp.min(jnp.where(le2 == m2, row, big), axis=0, keepdims=True)
    r = jnp.exp(m2 - m1)
    w1 = 1.0 / (1.0 + r)
    aux_row = lax.broadcasted_iota(jnp.int32, aux_ref.shape, 0)
    aux_ref[...] = p_sel * (jnp.where(aux_row == AUX_W1, w1, 0.0) + jnp.where(aux_row == AUX_W2, r * w1, 0.0))
    sel_ref[...] = jnp.where(row == i1, 1.0, 0.0) + jnp.where((row - float(N_EXPERTS)) == i2, 1.0, 0.0)


def _out_proj(x, ys, og, osw, wo, gf, wr, br, tm):
    n = x.shape[0]
    row = lambda w: pl.BlockSpec((tm, w), lambda i: (i, 0))
    return pl.pallas_call(
        _out_proj_body,
        grid=(n // tm,),
        in_specs=[row(D_MODEL), row(SSM_W), row(GLA_W), row(SWA_W), _const_spec(wo.shape), _const_spec(gf.shape),
                  _const_spec(wr.shape), _const_spec(br.shape)],
        out_specs=[row(D_MODEL), row(D_MODEL), pl.BlockSpec((2 * N_EXPERTS, tm), lambda i: (0, i)),
                   pl.BlockSpec((AUX_ROWS, tm), lambda i: (0, i))],
        out_shape=[jax.ShapeDtypeStruct((n, D_MODEL), F32), jax.ShapeDtypeStruct((n, D_MODEL), BF16),
                   jax.ShapeDtypeStruct((2 * N_EXPERTS, n), F32), jax.ShapeDtypeStruct((AUX_ROWS, n), F32)],
        compiler_params=_cparams("parallel"),
        name="out_proj",
    )(x, ys, og, osw, wo, gf, wr, br)


def _moe_sort_body(xn_ref, sel_ref, lo_ref, triu_ref, aux_ref, xs_ref, auxo_ref):
    s1 = sel_ref[0:N_EXPERTS, :]
    s2 = sel_ref[N_EXPERTS:2 * N_EXPERTS, :]
    rank = _dot((s1 + s2).astype(BF16), triu_ref[...])
    pos = lo_ref[0][:, 0:1] + rank
    d1 = jnp.sum(s1 * pos, axis=0, keepdims=True)
    d2 = jnp.sum(s2 * pos, axis=0, keepdims=True)
    r = lax.broadcasted_iota(jnp.int32, (MOE_RLOC, MOE_TB), 0).astype(F32)
    perm = jnp.where((r == d1) | (r == d2), 1.0, 0.0).astype(BF16)
    xs_ref[...] = _dot(perm, xn_ref[...]).astype(BF16)
    row = lax.broadcasted_iota(jnp.int32, (LANES, MOE_TB), 0)
    at = jnp.where(row == AUX_W1, aux_ref[AUX_W1:AUX_W1 + 1, :], jnp.where(row == AUX_W2, aux_ref[AUX_W2:AUX_W2 + 1, :],
                   jnp.where(row == AUX_D1, d1, jnp.where(row == AUX_D2, d2, 0.0))))
    auxo_ref[...] = jnp.transpose(at)


def _moe_sort(xn, sel, lo_rows, triu, aux):
    n = xn.shape[0]
    nblk = n // MOE_TB
    return pl.pallas_call(
        _moe_sort_body,
        grid=(nblk,),
        in_specs=[pl.BlockSpec((MOE_TB, D_MODEL), lambda i: (i, 0)), pl.BlockSpec((2 * N_EXPERTS, MOE_TB), lambda i: (0, i)),
                  pl.BlockSpec((1, N_EXPERTS, LANES), lambda i: (i, 0, 0)), _const_spec(triu.shape),
                  pl.BlockSpec((AUX_ROWS, MOE_TB), lambda i: (0, i))],
        out_specs=[pl.BlockSpec((MOE_RLOC, D_MODEL), lambda i: (i, 0)), pl.BlockSpec((MOE_TB, LANES), lambda i: (i, 0))],
        out_shape=[jax.ShapeDtypeStruct((nblk * MOE_RLOC, D_MODEL), BF16), jax.ShapeDtypeStruct((n, LANES), F32)],
        compiler_params=_cparams("parallel"),
        name="moe_sort",
    )(xn, sel, lo_rows, triu, aux)


def _piece_copy(hbm_ref, piece, buf_ref, slot, p, sem, to_hbm):
    start = lambda i: i * MOE_PIECE if isinstance(i, int) else pl.multiple_of(i * MOE_PIECE, MOE_PIECE)
    rows = pl.ds(start(piece), MOE_PIECE)
    vm = buf_ref.at[slot, pl.ds(start(p), MOE_PIECE)]
    if to_hbm:
        return pltpu.make_async_copy(vm, hbm_ref.at[rows], sem)
    return pltpu.make_async_copy(hbm_ref.at[rows], vm, sem)


def _moe_expert_body(texp_ref, piece_ref, nv_ref, xs_hbm, wg_ref, wu_ref, wd_ref, ys_hbm,
                     xbuf, ybuf, wgb, wub, wdb, sem_in, sem_out, *, pt, n_steps):
    i = pl.program_id(0)
    slot = i % 2

    def for_pieces(tile, s, hbm_ref, buf_ref, sem, to_hbm, wait):
        def one(p):
            piece = 0 if wait else piece_ref[tile * pt + p]
            c = _piece_copy(hbm_ref, piece, buf_ref, s, p, sem.at[s], to_hbm)
            c.wait() if wait else c.start()

        nv = nv_ref[tile]

        @pl.when(nv == pt)
        def _():
            for p in range(pt):
                one(p)

        @pl.when(nv < pt)
        def _():
            def body(p, carry):
                one(p)
                return carry
            lax.fori_loop(0, nv, body, 0)

    gather = lambda tile, s, wait: for_pieces(tile, s, xs_hbm, xbuf, sem_in, False, wait)
    scatter = lambda tile, s, wait: for_pieces(tile, s, ys_hbm, ybuf, sem_out, True, wait)

    @pl.when(i == 0)
    def _():
        xbuf[...] = jnp.zeros_like(xbuf)
        gather(0, 0, False)

    @pl.when(i + 1 < n_steps)
    def _():
        gather(i + 1, 1 - slot, False)

    gather(i, slot, True)

    @pl.when((i == 0) | (texp_ref[i] != texp_ref[jnp.maximum(i - 1, 0)]))
    def _():
        wgb[...] = wg_ref[0].astype(BF16)
        wub[...] = wu_ref[0].astype(BF16)
        wdb[...] = wd_ref[0].astype(BF16)

    @pl.when(nv_ref[i] > 0)
    def _():
        xb = xbuf[slot]
        a = _dot(xb, wgb[...])
        h = (a * _sigmoid(a)) * _dot(xb, wub[...])
        ybuf[slot] = _dot(h.astype(BF16), wdb[...]).astype(BF16)

    scatter(i, slot, False)

    @pl.when(i >= 1)
    def _():
        scatter(i - 1, 1 - slot, True)

    @pl.when(i == n_steps - 1)
    def _():
        scatter(i, slot, True)


def _moe_experts(tile_expert, piece, nvalid, xs, wg, wu, wd, layer, pt):
    n_steps = tile_expert.shape[0]
    tm = pt * MOE_PIECE
    wspec = lambda shape: pl.BlockSpec(shape, lambda i, te, pc, nv: (te[i] + layer * N_EXPERTS, 0, 0))
    grid_spec = pltpu.PrefetchScalarGridSpec(
        num_scalar_prefetch=3,
        grid=(n_steps,),
        in_specs=[pl.BlockSpec(memory_space=pl.ANY), wspec((1, D_MODEL, EXPERT_FF)), wspec((1, D_MODEL, EXPERT_FF)),
                  wspec((1, EXPERT_FF, D_MODEL))],
        out_specs=pl.BlockSpec(memory_space=pl.ANY),
        scratch_shapes=[pltpu.VMEM((2, tm, D_MODEL), BF16), pltpu.VMEM((2, tm, D_MODEL), BF16),
                        pltpu.VMEM((D_MODEL, EXPERT_FF), BF16), pltpu.VMEM((D_MODEL, EXPERT_FF), BF16),
                        pltpu.VMEM((EXPERT_FF, D_MODEL), BF16),
                        pltpu.SemaphoreType.DMA((2,)), pltpu.SemaphoreType.DMA((2,))],
    )
    return pl.pallas_call(
        functools.partial(_moe_expert_body, pt=pt, n_steps=n_steps),
        grid_spec=grid_spec,
        out_shape=jax.ShapeDtypeStruct(xs.shape, BF16),
        input_output_aliases={3: 0},
        compiler_params=_cparams("arbitrary"),
        name="moe_experts",
    )(tile_expert, piece, nvalid, xs, wg, wu, wd)


def _moe_combine_body(ys_ref, aux_ref, x1_ref, o_ref):
    col = lax.broadcasted_iota(jnp.int32, (MOE_TB, MOE_RLOC), 1).astype(F32)
    y = ys_ref[...]
    pick = lambda lane: _dot(jnp.where(col == aux_ref[:, lane:lane + 1], 1.0, 0.0).astype(BF16), y)
    o_ref[...] = (x1_ref[...] + aux_ref[:, AUX_W1:AUX_W1 + 1] * pick(AUX_D1)
                  + aux_ref[:, AUX_W2:AUX_W2 + 1] * pick(AUX_D2))


def _moe_combine(ys, aux, x1):
    n = x1.shape[0]
    return pl.pallas_call(
        _moe_combine_body,
        grid=(n // MOE_TB,),
        in_specs=[pl.BlockSpec((MOE_RLOC, D_MODEL), lambda i: (i, 0)), pl.BlockSpec((MOE_TB, LANES), lambda i: (i, 0)),
                  pl.BlockSpec((MOE_TB, D_MODEL), lambda i: (i, 0))],
        out_specs=pl.BlockSpec((MOE_TB, D_MODEL), lambda i: (i, 0)),
        out_shape=jax.ShapeDtypeStruct((n, D_MODEL), F32),
        compiler_params=_cparams("parallel"),
        name="moe_combine",
    )(ys, aux, x1)


def _route_tables(sel, pt):
    n = sel.shape[1]
    nblk = n // MOE_TB
    cnt = (sel[:N_EXPERTS] + sel[N_EXPERTS:]).reshape(N_EXPERTS, nblk, MOE_TB).sum(-1).astype(jnp.int32)
    pc = (cnt + MOE_PIECE - 1) // MOE_PIECE
    lo_p = jnp.cumsum(pc, axis=0) - pc
    lo_rows = jnp.broadcast_to((lo_p.T * MOE_PIECE).astype(F32)[:, :, None], (nblk, N_EXPERTS, LANES))
    pe = pc.sum(1)
    tiles_e = (pe + pt - 1) // pt
    tile_start = jnp.cumsum(tiles_e) - tiles_e
    seg_start = (tile_start[:, None] * pt + jnp.cumsum(pc, axis=1) - pc).reshape(-1)
    pcs = pc.reshape(-1)
    seg_src = (jnp.arange(nblk, dtype=jnp.int32)[None, :] * MOE_PPB + lo_p).reshape(-1)
    n_steps = -(-(2 * n // MOE_PIECE + nblk * N_EXPERTS + N_EXPERTS * (pt - 1)) // pt)
    slot = jnp.arange(n_steps * pt, dtype=jnp.int32)[:, None]
    in_seg = (slot >= seg_start[None, :]) & (slot < (seg_start + pcs)[None, :])
    piece = jnp.sum(jnp.where(in_seg, seg_src[None, :] + slot - seg_start[None, :], 0), axis=1).astype(jnp.int32)
    tile = jnp.arange(n_steps, dtype=jnp.int32)[:, None]
    in_exp = (tile >= tile_start[None, :]) & (tile < (tile_start + tiles_e)[None, :])
    tile_expert = jnp.sum(jnp.where(in_exp, jnp.arange(N_EXPERTS, dtype=jnp.int32)[None, :], 0), axis=1)
    tile_expert = jnp.where(tile[:, 0] < tiles_e.sum(), tile_expert, N_EXPERTS - 1).astype(jnp.int32)
    nvalid = jnp.sum(jnp.where(in_exp, jnp.clip(pe[None, :] - (tile - tile_start[None, :]) * pt, 0, pt), 0),
                     axis=1).astype(jnp.int32)
    return lo_rows, tile_expert, piece, nvalid


def _moe(xn, sel, aux, x1, wg, wu, wd, layer, triu, pt):
    lo_rows, tile_expert, piece, nvalid = _route_tables(sel, pt)
    xs, aux2 = _moe_sort(xn, sel, lo_rows, triu, aux)
    ys = _moe_experts(tile_expert, piece, nvalid, xs, wg, wu, wd, layer, pt)
    return _moe_combine(ys, aux2, x1)


def _seg_mean_matrix(width, seg):
    i = jnp.arange(width)
    return jnp.where((i[:, None] // seg) == (i[None, :] // seg), 1.0 / seg, 0.0).astype(BF16)


def _chunk_tril(span, lc):
    i = jnp.arange(span)
    return ((i[:, None] >= i[None, :]) & ((i[:, None] // lc) == (i[None, :] // lc))).astype(BF16)


def _prep_layer(lw, ssm_shapes, gla_shapes):
    w_in = lw["w_in"].astype(F32)
    cols = {}
    off = 0
    for name, wdt in (("u", SSM_W), ("qg", GLA_K), ("kg", GLA_K), ("vg", GLA_W), ("z", GLA_GATE_RANK), ("og", GLA_W),
                      ("qs", SWA_W), ("ks", SWA_KV_W), ("vs", SWA_KV_W)):
        cols[name] = w_in[:, off:off + wdt]
        off += wdt
    zpad = lambda wdt: jnp.zeros((D_MODEL, wdt), F32)
    w_all = jnp.concatenate(
        [cols["qg"], zpad(GLA_KP - GLA_K), cols["kg"], zpad(GLA_KP - GLA_K), cols["vg"], cols["og"], cols["z"],
         zpad(LANES - GLA_GATE_RANK), cols["qs"], cols["ks"], cols["vs"], cols["u"]], axis=1).astype(BF16)
    out = {"norm_mix": lw["norm_mix"].astype(F32)[None, :], "w_all": w_all,
           "mavg_qk": _seg_mean_matrix(SWA_W + SWA_KV_W, SWA_HEAD_DIM),
           "qk_gain": jnp.concatenate([jnp.tile(lw["swa_q_norm"].astype(F32), SWA_HEADS),
                                       jnp.tile(lw["swa_k_norm"].astype(F32), SWA_KV_HEADS)])[None, :]}
    a_re = lw["ssm_a_re"].astype(F32)
    a_im = lw["ssm_a_im"].astype(F32)
    dt = jnp.exp(lw["ssm_log_dt"].astype(F32))[:, None]
    mag = jnp.exp(a_re * dt)
    ab_re = mag * jnp.cos(a_im * dt)
    ab_im = mag * jnp.sin(a_im * dt)
    den = a_re * a_re + a_im * a_im
    nr = ab_re - 1.0
    f_re = (nr * a_re + ab_im * a_im) / den
    f_im = (ab_im * a_re - nr * a_im) / den
    b_re = lw["ssm_b_re"].astype(F32)
    b_im = lw["ssm_b_im"].astype(F32)
    bb_re = f_re[..., None] * b_re - f_im[..., None] * b_im
    bb_im = f_re[..., None] * b_im + f_im[..., None] * b_re
    eye_g = jnp.eye(SSM_GROUPS, dtype=F32)
    blockdiag_in = lambda bb: jnp.einsum("gpc,gh->gchp", bb, eye_g).reshape(SSM_W, SSM_FLAT)
    blockdiag_out = lambda cc: jnp.einsum("gcp,gh->gphc", cc, eye_g).reshape(SSM_FLAT, SSM_W)
    bd = jnp.concatenate([blockdiag_in(bb_re), blockdiag_in(bb_im)], axis=1).astype(BF16)
    cd = jnp.concatenate([blockdiag_out(lw["ssm_c_re"].astype(F32)),
                          -blockdiag_out(lw["ssm_c_im"].astype(F32))], axis=0).astype(BF16)
    ssm = {"ab": jnp.concatenate([ab_re.reshape(1, SSM_FLAT), ab_im.reshape(1, SSM_FLAT),
                                  jnp.zeros((6, SSM_FLAT), F32)], axis=0),
           "bd": bd, "cd": cd, "d": lw["ssm_d"].astype(F32)[None, :], "wglu": lw["ssm_w_glu"].astype(BF16)}
    def powers(lc, sign):
        tt = jnp.arange(lc, dtype=F32)[:, None, None]
        m = jnp.exp(sign * tt * (a_re * dt)[None])
        ang = sign * tt * (a_im * dt)[None]
        return jnp.stack([(m * jnp.cos(ang)).reshape(lc, SSM_FLAT), (m * jnp.sin(ang)).reshape(lc, SSM_FLAT)])

    out["ssm"] = {(span, lc): dict(ssm, pwp=powers(lc, 1.0), pwn=powers(lc, -1.0), tril=_chunk_tril(span, lc))
                  for span, lc in ssm_shapes}
    wg = jnp.zeros((LANES, GLA_KP), F32).at[:GLA_GATE_RANK, :GLA_K].set(lw["gla_w_gate"].astype(F32)).astype(BF16)
    bg = jnp.zeros((1, GLA_KP), F32).at[0, :GLA_K].set(lw["gla_b_gate"].astype(F32))
    kk = jnp.arange(GLA_KP)
    spread = ((kk[None, :] % GLA_DK) == jnp.arange(GLA_DK)[:, None]) & (kk[None, :] < GLA_K)
    gla = {"wg": wg, "bg": bg, "mavg": _seg_mean_matrix(GLA_W, GLA_DV),
           "gn": jnp.tile(lw["gla_norm"].astype(F32), GLA_HEADS)[None, :], "spread": spread.astype(BF16)}
    out["gla"] = {(span, lc): dict(gla, tril=_chunk_tril(span, lc)) for span, lc in gla_shapes}
    out["sinks"] = lw["swa_sinks"].astype(F32)
    out["w_out"] = lw["w_out"].astype(BF16)
    out["norm_ffn"] = lw["norm_ffn"].astype(F32)[None, :]
    wr = jnp.zeros((D_MODEL, LANES), F32)
    wr = wr.at[:, :N_EXPERTS].set(lw["moe_w_expert"].astype(F32))
    wr = wr.at[:, ROUTER_GROUP_LANE0:ROUTER_GROUP_LANE0 + N_EXPERT_GROUPS].set(lw["moe_w_group"].astype(F32))
    out["wr"] = wr.astype(BF16)
    br = jnp.zeros((1, LANES), F32)
    br = br.at[0, :N_EXPERTS].set(lw["moe_b_expert"].astype(F32))
    br = br.at[0, ROUTER_GROUP_LANE0:ROUTER_GROUP_LANE0 + N_EXPERT_GROUPS].set(lw["moe_b_group"].astype(F32))
    out["br"] = br
    ti = jnp.arange(MOE_TB)
    out["moe_triu"] = (ti[:, None] < ti[None, :]).astype(BF16)
    return out


def _gla_state_in(h0):
    return jnp.transpose(h0.astype(F32), (0, 1, 3, 2)).reshape(h0.shape[0], GLA_W, GLA_DK)


def _gla_state_out(st):
    return jnp.transpose(st.reshape(st.shape[0], GLA_HEADS, GLA_DV, GLA_DK), (0, 1, 3, 2))


def _layer(x, pw, bias, ssm_h0r, ssm_h0i, gla_h0, past_k, past_v, cfg):
    b, t, _ = x.shape
    n = b * t
    p_gla, p_swa, p_ssm = _in_proj(x.reshape(n, D_MODEL), pw["norm_mix"], pw["w_all"], pw["mavg_qk"], pw["qk_gain"],
                                   cfg["tm"])
    p_gla = p_gla.reshape(b, t, GLA_PW)
    p_swa = p_swa.reshape(b, t, SWA_PW)
    y_ssm, h_re, h_im = _ssm(p_ssm.reshape(b, t, SSM_W), ssm_h0r.reshape(b, 1, SSM_FLAT).astype(F32),
                             ssm_h0i.reshape(b, 1, SSM_FLAT).astype(F32), pw["ssm"][cfg["ssm"]], cfg["lblk"],
                             *cfg["ssm"])
    o_g, s_fin = _gla(p_gla, _gla_state_in(gla_h0), pw["gla"][cfg["gla"]], cfg["lblk"], *cfg["gla"])
    kcol, vcol = SWA_W // SWA_KV_W, SWA_W // SWA_KV_W + 1
    if past_k is None:
        per_blk = cfg["qb"] // SWA_WINDOW
        prev_k_map = lambda i, j: (i, jnp.maximum(j * per_blk - 1, 0), kcol)
        prev_v_map = lambda i, j: (i, jnp.maximum(j * per_blk - 1, 0), vcol)
        k_prev, v_prev = p_swa, p_swa
        keep = min(SWA_WINDOW, t)
        new_k = p_swa[:, t - keep:, SWA_W:SWA_W + SWA_KV_W]
        new_v = p_swa[:, t - keep:, SWA_W + SWA_KV_W:]
    else:
        prev_k_map = prev_v_map = lambda i, j: (i, 0, 0)
        k_prev = past_k.reshape(b, SWA_WINDOW, SWA_KV_W).astype(F32)
        v_prev = past_v.reshape(b, SWA_WINDOW, SWA_KV_W).astype(F32)
        new_k = p_swa[:, :, SWA_W:SWA_W + SWA_KV_W]
        new_v = p_swa[:, :, SWA_W + SWA_KV_W:]
    o_s = _swa(pw["sinks"], p_swa, k_prev, v_prev, prev_k_map, prev_v_map, bias, cfg["qb"], cfg["cq"], cfg["win"],
               cfg["pos0"])
    x1, xn, sel, aux = _out_proj(x.reshape(n, D_MODEL), y_ssm.reshape(n, SSM_W), o_g.reshape(n, GLA_W),
                                 o_s.reshape(n, SWA_W), pw["w_out"], pw["norm_ffn"], pw["wr"], pw["br"],
                                 cfg["tm"])
    x2 = _moe(xn, sel, aux, x1, pw["moe_wg"], pw["moe_wu"], pw["moe_wd"], pw["layer"], pw["moe_triu"], cfg["pt"])
    kv_shape = (b, new_k.shape[1], SWA_KV_HEADS, SWA_HEAD_DIM)
    return (x2.reshape(b, t, D_MODEL), new_k.reshape(kv_shape), new_v.reshape(kv_shape), _gla_state_out(s_fin),
            h_re.reshape(b, SSM_GROUPS, SSM_STATE), h_im.reshape(b, SSM_GROUPS, SSM_STATE))


def _group_cfg(t, past_len, n_past):
    if n_past is None:
        assert t % (8 * CHUNK) == 0
        cfg = dict(ssm=(4 * CHUNK, CHUNK), gla=(4 * CHUNK, 2 * CHUNK), lblk=8 * CHUNK, qb=8 * CHUNK, cq=CHUNK,
                   win=SWA_WINDOW + CHUNK, pos0=-SWA_WINDOW, tm=512, pt=32)
        q_pos = jnp.arange(CHUNK, dtype=jnp.int32) + SWA_WINDOW
        k_pos = jnp.arange(SWA_WINDOW + CHUNK, dtype=jnp.int32)
    else:
        first_key, last_q = past_len - n_past, past_len + t - 1
        assert n_past == SWA_WINDOW and t <= CHUNK and t % 8 == 0
        assert past_len // CHUNK == last_q // CHUNK and first_key // CHUNK >= past_len // CHUNK - SWA_WINDOW // CHUNK
        cfg = dict(ssm=(t, t), gla=(t, t), lblk=t, qb=t, cq=t, win=n_past + t, pos0=first_key, tm=512, pt=16)
        q_pos = past_len + jnp.arange(t, dtype=jnp.int32)
        k_pos = first_key + jnp.arange(n_past + t, dtype=jnp.int32)
    return cfg, q_pos, k_pos


def _stacked_bias(t5_table, q_pos, k_pos):
    bias = _t5_bias(t5_table, q_pos, k_pos)
    nq, nk = bias.shape[1:]
    return bias.reshape(SWA_KV_HEADS, SWA_REP * nq, nk)


PAST_LEN = 1024


def kernel(x_prompt, x_sample, cache_swa_k, cache_swa_v, state_gla, state_ssm_re, state_ssm_im, norm_mix, w_in, ssm_a_re, ssm_a_im, ssm_log_dt, ssm_b_re, ssm_b_im, ssm_c_re, ssm_c_im, ssm_d, ssm_w_glu, gla_w_gate, gla_b_gate, gla_norm, swa_q_norm, swa_k_norm, swa_sinks, t5_table, w_out, norm_ffn, moe_w_group, moe_b_group, moe_w_expert, moe_b_expert, moe_w_gate, moe_w_up, moe_w_down):
    depth = w_in.shape[0]
    bp, tp, _ = x_prompt.shape
    bs, ts, _ = x_sample.shape
    cfg_p, qpos_p, kpos_p = _group_cfg(tp, 0, None)
    cfg_s, qpos_s, kpos_s = _group_cfg(ts, PAST_LEN, cache_swa_k.shape[2])
    bias_p = _stacked_bias(t5_table, qpos_p, kpos_p)
    bias_s = _stacked_bias(t5_table, qpos_s, kpos_s)
    hp, hs = x_prompt.astype(F32), x_sample.astype(F32)
    moe_wg = moe_w_gate.astype(F32).reshape(depth * N_EXPERTS, D_MODEL, EXPERT_FF)
    moe_wu = moe_w_up.astype(F32).reshape(depth * N_EXPERTS, D_MODEL, EXPERT_FF)
    moe_wd = moe_w_down.astype(F32).reshape(depth * N_EXPERTS, EXPERT_FF, D_MODEL)
    outs = [[] for _ in range(10)]
    for l in range(depth):
        lw = {
            "norm_mix": norm_mix[l], "w_in": w_in[l], "ssm_a_re": ssm_a_re[l], "ssm_a_im": ssm_a_im[l],
            "ssm_log_dt": ssm_log_dt[l], "ssm_b_re": ssm_b_re[l], "ssm_b_im": ssm_b_im[l], "ssm_c_re": ssm_c_re[l],
            "ssm_c_im": ssm_c_im[l], "ssm_d": ssm_d[l], "ssm_w_glu": ssm_w_glu[l], "gla_w_gate": gla_w_gate[l],
            "gla_b_gate": gla_b_gate[l], "gla_norm": gla_norm[l], "swa_q_norm": swa_q_norm[l],
            "swa_k_norm": swa_k_norm[l], "swa_sinks": swa_sinks[l], "w_out": w_out[l], "norm_ffn": norm_ffn[l],
            "moe_w_group": moe_w_group[l], "moe_b_group": moe_b_group[l], "moe_w_expert": moe_w_expert[l],
            "moe_b_expert": moe_b_expert[l],
        }
        pw = _prep_layer(lw, {cfg_p["ssm"], cfg_s["ssm"]}, {cfg_p["gla"], cfg_s["gla"]})
        pw.update(layer=l, moe_wg=moe_wg, moe_wu=moe_wu, moe_wd=moe_wd)
        zs = jnp.zeros((bp, SSM_GROUPS, SSM_STATE), F32)
        zg = jnp.zeros((bp, GLA_HEADS, GLA_DK, GLA_DV), F32)
        hp, nk, nv, ng, nr, ni = _layer(hp, pw, bias_p, zs, zs, zg, None, None, cfg_p)
        for slot, val in zip((0, 1, 4, 6, 7), (nk, nv, ng, nr, ni)):
            outs[slot].append(val)
        hs, nk, nv, ng, nr, ni = _layer(hs, pw, bias_s, state_ssm_re[l], state_ssm_im[l], state_gla[l],
                                        cache_swa_k[l], cache_swa_v[l], cfg_s)
        for slot, val in zip((2, 3, 5, 8, 9), (nk, nv, ng, nr, ni)):
            outs[slot].append(val)
    return (hp, hs) + tuple(jnp.stack(o) for o in outs)
```

```python
import functools
import math

import jax
import jax.numpy as jnp
from jax import lax
from jax.experimental import pallas as pl
from jax.experimental.pallas import tpu as pltpu

F32 = jnp.float32
BF16 = jnp.bfloat16

D_MODEL = 1024
CHUNK = 64
RMS_EPS = 1e-6
SSM_GROUPS = 16
SSM_GC = 16
SSM_STATE = 64
SSM_W = SSM_GROUPS * SSM_GC
SSM_FLAT = SSM_GROUPS * SSM_STATE
GLA_HEADS = 6
GLA_DK = 32
GLA_DV = 64
GLA_GATE_RANK = 16
GLA_GATE_NORM = 16.0
GLA_K = GLA_HEADS * GLA_DK
GLA_KP = 256
GLA_W = GLA_HEADS * GLA_DV
SWA_HEADS = 6
SWA_KV_HEADS = 2
SWA_REP = SWA_HEADS // SWA_KV_HEADS
SWA_HEAD_DIM = 64
SWA_WINDOW = 128
SWA_W = SWA_HEADS * SWA_HEAD_DIM
SWA_KV_W = SWA_KV_HEADS * SWA_HEAD_DIM
MIX_W = SSM_W + GLA_W + SWA_W
N_BUCKETS = 32
T5_MAX_DIST = 128
N_EXPERT_GROUPS = 4
EXPERTS_PER_GROUP = 4
N_EXPERTS = 16
EXPERT_FF = 512
LANES = 128
ROUTER_GROUP_LANE0 = N_EXPERTS
AUX_W1, AUX_W2, AUX_D1, AUX_D2 = 0, 1, 2, 3
AUX_ROWS = 8
MOE_TB = 512
MOE_PIECE = 16
MOE_RLOC = 2 * MOE_TB + N_EXPERTS * MOE_PIECE
MOE_PPB = MOE_RLOC // MOE_PIECE

GLA_Q0, GLA_K0, GLA_V0, GLA_OG0, GLA_Z0, GLA_PW = 0, 256, 512, 896, 1280, 1408
SWA_PW = SWA_W + 2 * SWA_KV_W
P_GLA0, P_SWA0, P_SSM0, P_TOTAL = 0, GLA_PW, GLA_PW + SWA_PW, GLA_PW + SWA_PW + SSM_W

VMEM_LIMIT = 48 * 1024 * 1024


def _cparams(*sem):
    return pltpu.CompilerParams(dimension_semantics=sem, vmem_limit_bytes=VMEM_LIMIT)


def _dot(a, b):
    return jnp.dot(a, b, preferred_element_type=F32)


def _dot_nt(a, b):
    return lax.dot_general(a, b, (((1,), (1,)), ((), ())), preferred_element_type=F32)


def _dot_tn(a, b):
    return lax.dot_general(a, b, (((0,), (0,)), ((), ())), preferred_element_type=F32)


def _hi_lo(x):
    hi = x.astype(BF16)
    return hi, (x - hi.astype(F32)).astype(BF16)


def _dot_f32_rhs(a_bf16, x):
    hi, lo = _hi_lo(x)
    return _dot(a_bf16, hi) + _dot(a_bf16, lo)


def _dot_f32_lhs(x, b_bf16):
    hi, lo = _hi_lo(x)
    return _dot(hi, b_bf16) + _dot(lo, b_bf16)


def _log2(n):
    assert n & (n - 1) == 0
    return n.bit_length() - 1


def _sigmoid(x):
    return 1.0 / (1.0 + jnp.exp(-x))


def _const_spec(shape):
    nd = len(shape)
    return pl.BlockSpec(shape, lambda *_: (0,) * nd)


def _in_proj_body(x_ref, g_ref, w_ref, mavg_ref, qkg_ref, gla_ref, swa_ref, ssm_ref):
    x = x_ref[...]
    xn = x * lax.rsqrt(jnp.mean(x * x, axis=-1, keepdims=True) + RMS_EPS) * g_ref[...]
    xb = xn.astype(BF16)
    gla_ref[...] = _dot(xb, w_ref[:, P_GLA0:P_SWA0])
    ssm_ref[...] = _dot(xb, w_ref[:, P_SSM0:P_TOTAL])
    s = _dot(xb, w_ref[:, P_SWA0:P_SSM0])
    qk = s[:, :SWA_W + SWA_KV_W]
    ms = _dot_f32_lhs(qk * qk, mavg_ref[...])
    swa_ref[:, :SWA_W + SWA_KV_W] = qk * lax.rsqrt(ms + RMS_EPS) * qkg_ref[...]
    swa_ref[:, SWA_W + SWA_KV_W:] = s[:, SWA_W + SWA_KV_W:]


def _in_proj(x, g, w, mavg, qkg, tm):
    n = x.shape[0]
    return pl.pallas_call(
        _in_proj_body,
        grid=(n // tm,),
        in_specs=[pl.BlockSpec((tm, D_MODEL), lambda i: (i, 0)), _const_spec(g.shape), _const_spec(w.shape),
                  _const_spec(mavg.shape), _const_spec(qkg.shape)],
        out_specs=[pl.BlockSpec((tm, GLA_PW), lambda i: (i, 0)), pl.BlockSpec((tm, SWA_PW), lambda i: (i, 0)),
                   pl.BlockSpec((tm, SSM_W), lambda i: (i, 0))],
        out_shape=[jax.ShapeDtypeStruct((n, GLA_PW), F32), jax.ShapeDtypeStruct((n, SWA_PW), F32),
                   jax.ShapeDtypeStruct((n, SSM_W), F32)],
        compiler_params=_cparams("parallel"),
        name="in_proj",
    )(x, g, w, mavg, qkg)


def _ssm_body(u_ref, h0r_ref, h0i_ref, ab_ref, bd_ref, cd_ref, pwp_ref, pwn_ref, tril_ref, d_ref, wglu_ref,
              y_ref, hr_ref, hi_ref, carry_ref, *, n_part, n_sub, lc):
    @pl.when(pl.program_id(1) == 0)
    def _():
        carry_ref[0:1, :] = h0r_ref[0]
        carry_ref[1:2, :] = h0i_ref[0]

    hr = carry_ref[0:1, :]
    hi = carry_ref[1:2, :]
    ab_re = ab_ref[0:1, :]
    ab_im = ab_ref[1:2, :]
    nr, ni = pwn_ref[0], pwn_ref[1]
    pr, pi = pwp_ref[0], pwp_ref[1]
    span = n_sub * lc
    for part in range(n_part):
        rows = slice(part * span, (part + 1) * span)
        u = u_ref[0, rows, :]
        bu = _dot(u.astype(BF16), bd_ref[...])
        sr, si = [], []
        for c in range(n_sub):
            bur = bu[c * lc:(c + 1) * lc, :SSM_FLAT]
            bui = bu[c * lc:(c + 1) * lc, SSM_FLAT:]
            sr.append(nr * bur - ni * bui)
            si.append(nr * bui + ni * bur)
        scaled = jnp.concatenate([jnp.concatenate(sr, axis=0), jnp.concatenate(si, axis=0)], axis=1)
        cs = _dot_f32_rhs(tril_ref[...], scaled)
        h_r, h_i = [], []
        for c in range(n_sub):
            cr = cs[c * lc:(c + 1) * lc, :SSM_FLAT] + (ab_re * hr - ab_im * hi)
            ci = cs[c * lc:(c + 1) * lc, SSM_FLAT:] + (ab_re * hi + ab_im * hr)
            h_r.append(pr * cr - pi * ci)
            h_i.append(pr * ci + pi * cr)
            hr = h_r[-1][lc - 1:lc, :]
            hi = h_i[-1][lc - 1:lc, :]
        hcat = jnp.concatenate([jnp.concatenate(h_r, axis=0), jnp.concatenate(h_i, axis=0)], axis=1).astype(BF16)
        y = _dot(hcat, cd_ref[...]) + d_ref[...] * u
        g = 0.5 * y * (1.0 + jnp.tanh(math.sqrt(2.0 / math.pi) * (y + 0.044715 * (y * y * y))))
        y_ref[0, rows, :] = g * _sigmoid(_dot(g.astype(BF16), wglu_ref[...]))
    carry_ref[0:1, :] = hr
    carry_ref[1:2, :] = hi
    hr_ref[0] = hr
    hi_ref[0] = hi


def _ssm(u, h0r, h0i, sw, lblk, span, lc):
    b, t, _ = u.shape
    consts = [sw["ab"], sw["bd"], sw["cd"], sw["pwp"], sw["pwn"], sw["tril"], sw["d"], sw["wglu"]]
    state_spec = pl.BlockSpec((1, 1, SSM_FLAT), lambda i, j: (i, 0, 0))
    return pl.pallas_call(
        functools.partial(_ssm_body, n_part=lblk // span, n_sub=span // lc, lc=lc),
        grid=(b, t // lblk),
        in_specs=[pl.BlockSpec((1, lblk, SSM_W), lambda i, j: (i, j, 0)), state_spec, state_spec]
        + [_const_spec(c.shape) for c in consts],
        out_specs=[pl.BlockSpec((1, lblk, SSM_W), lambda i, j: (i, j, 0)), state_spec, state_spec],
        out_shape=[jax.ShapeDtypeStruct((b, t, SSM_W), F32), jax.ShapeDtypeStruct((b, 1, SSM_FLAT), F32),
                   jax.ShapeDtypeStruct((b, 1, SSM_FLAT), F32)],
        scratch_shapes=[pltpu.VMEM((8, SSM_FLAT), F32)],
        compiler_params=_cparams("parallel", "arbitrary"),
        name="ssm",
    )(u, h0r, h0i, *consts)


def _gla_body(p_ref, s0_ref, wg_ref, bg_ref, tril_ref, mavg_ref, gn_ref, spread_ref, o_ref, sfin_ref, s_ref, *,
              n_part, n_sub, lc):
    row_v = lax.broadcasted_iota(jnp.int32, (GLA_W, GLA_KP), 0)
    col_k = lax.broadcasted_iota(jnp.int32, (GLA_W, GLA_KP), 1)
    same_head = ((row_v >> _log2(GLA_DV)) == (col_k >> _log2(GLA_DK))).astype(F32)

    @pl.when(pl.program_id(1) == 0)
    def _():
        s_ref[...] = _dot_f32_lhs(s0_ref[0], spread_ref[...]) * same_head

    lane_k = lax.broadcasted_iota(jnp.int32, (1, GLA_KP), 1)
    lane_v = lax.broadcasted_iota(jnp.int32, (1, GLA_W), 1)
    head_k = [((lane_k >= h * GLA_DK) & (lane_k < (h + 1) * GLA_DK)).astype(F32) for h in range(GLA_HEADS)]
    head_v = [((lane_v >= h * GLA_DV) & (lane_v < (h + 1) * GLA_DV)).astype(F32) for h in range(GLA_HEADS)]
    row_t = lax.broadcasted_iota(jnp.int32, (GLA_HEADS * lc, lc), 0)
    col_s = lax.broadcasted_iota(jnp.int32, (GLA_HEADS * lc, lc), 1)
    causal = (row_t & (lc - 1)) >= col_s
    mid = lc // 2 - 1
    span = n_sub * lc
    st = s_ref[...]
    for part in range(n_part):
        rows = slice(part * span, (part + 1) * span)
        z = p_ref[0, rows, GLA_Z0:GLA_Z0 + LANES]
        gin = _dot(z.astype(BF16), wg_ref[...]) + bg_ref[...]
        glog = (jnp.minimum(gin, 0.0) - jnp.log(1.0 + jnp.exp(-jnp.abs(gin)))) / GLA_GATE_NORM
        g_all = _dot_f32_rhs(tril_ref[...], glog)
        outs = []
        for c in range(n_sub):
            crow = slice(part * span + c * lc, part * span + (c + 1) * lc)
            q = p_ref[0, crow, GLA_Q0:GLA_Q0 + GLA_KP] * (GLA_DK ** -0.5)
            k = p_ref[0, crow, GLA_K0:GLA_K0 + GLA_KP]
            vb = p_ref[0, crow, GLA_V0:GLA_V0 + GLA_W].astype(BF16)
            gc = g_all[c * lc:(c + 1) * lc, :]
            gl = gc[lc - 1:lc, :]
            gm = gc[mid:mid + 1, :]
            qc = (q * jnp.exp(gc - gm))
            ke = (k * jnp.exp(gm - gc)).astype(BF16)
            kd = (k * jnp.exp(gl - gc)).astype(BF16)
            qs = jnp.concatenate([qc * m for m in head_k], axis=0).astype(BF16)
            attn = jnp.where(causal, _dot_nt(qs, ke), 0.0)
            o2 = _dot(attn.astype(BF16), vb)
            o = _dot_nt((q * jnp.exp(gc)).astype(BF16), st.astype(BF16))
            for h in range(GLA_HEADS):
                o = o + head_v[h] * o2[h * lc:(h + 1) * lc, :]
            st = st * jnp.exp(gl) + _dot_tn(vb, kd) * same_head
            outs.append(o)
        o = jnp.concatenate(outs, axis=0) if n_sub > 1 else outs[0]
        og = p_ref[0, rows, GLA_OG0:GLA_OG0 + GLA_W]
        ms = _dot_f32_lhs(o * o, mavg_ref[...])
        on = o * lax.rsqrt(ms + RMS_EPS) * gn_ref[...]
        o_ref[0, rows, :] = on * (og * _sigmoid(og))
    s_ref[...] = st

    @pl.when(pl.program_id(1) == pl.num_programs(1) - 1)
    def _():
        hi, lo = _hi_lo(st)
        sfin_ref[0] = _dot_nt(hi, spread_ref[...]) + _dot_nt(lo, spread_ref[...])


def _gla(p, s0, gw, lblk, span, lc):
    b, t, _ = p.shape
    consts = [gw["wg"], gw["bg"], gw["tril"], gw["mavg"], gw["gn"], gw["spread"]]
    st_spec = pl.BlockSpec((1, GLA_W, GLA_DK), lambda i, j: (i, 0, 0))
    return pl.pallas_call(
        functools.partial(_gla_body, n_part=lblk // span, n_sub=span // lc, lc=lc),
        grid=(b, t // lblk),
        in_specs=[pl.BlockSpec((1, lblk, GLA_PW), lambda i, j: (i, j, 0)), st_spec]
        + [_const_spec(c.shape) for c in consts],
        out_specs=[pl.BlockSpec((1, lblk, GLA_W), lambda i, j: (i, j, 0)), st_spec],
        out_shape=[jax.ShapeDtypeStruct((b, t, GLA_W), F32), jax.ShapeDtypeStruct((b, GLA_W, GLA_DK), F32)],
        scratch_shapes=[pltpu.VMEM((GLA_W, GLA_KP), F32)],
        compiler_params=_cparams("parallel", "arbitrary"),
        name="gla",
    )(p, s0, *consts)


def _t5_bias(t5_table, q_pos, k_pos):
    nq, nk = q_pos.shape[0], k_pos.shape[0]
    rel = (k_pos[0] - q_pos[0]) + jnp.arange(-(nq - 1), nk, dtype=jnp.int32)
    half = N_BUCKETS // 2
    max_exact = half // 2
    n = jnp.abs(rel)
    far = max_exact + (jnp.log(jnp.maximum(n, 1).astype(jnp.float32) / max_exact)
                       / math.log(T5_MAX_DIST / max_exact) * (half - max_exact)).astype(jnp.int32)
    bucket = jnp.where(rel > 0, half, 0) + jnp.where(n < max_exact, n, jnp.minimum(far, half - 1))
    by_dist = t5_table.astype(F32)[bucket].T
    return jnp.stack([by_dist[:, nq - 1 - i:nq - 1 - i + nk] for i in range(nq)], axis=1)


def _swa_body(sink_ref, q_ref, kp_ref, vp_ref, kc_ref, vc_ref, bias_ref, o_ref, *, qb, cq, win, pos0):
    blk = pl.program_id(1)
    kwin = jnp.concatenate([kp_ref[0], kc_ref[0]], axis=0)
    vwin = jnp.concatenate([vp_ref[0], vc_ref[0]], axis=0)
    col = lax.broadcasted_iota(jnp.int32, (1, win), 1)
    row = lax.broadcasted_iota(jnp.int32, (SWA_REP * cq, 1), 0)
    for j in range(qb // cq):
        valid = (pos0 + blk * qb + j * cq + col) >= 0
        for g in range(SWA_KV_HEADS):
            heads = [SWA_REP * g + r for r in range(SWA_REP)]
            qs = jnp.concatenate(
                [q_ref[0, j * cq:(j + 1) * cq, h * SWA_HEAD_DIM:(h + 1) * SWA_HEAD_DIM] for h in heads], axis=0)
            kk = kwin[j * cq:j * cq + win, g * SWA_HEAD_DIM:(g + 1) * SWA_HEAD_DIM]
            vv = vwin[j * cq:j * cq + win, g * SWA_HEAD_DIM:(g + 1) * SWA_HEAD_DIM]
            s = _dot_nt(qs.astype(BF16), kk.astype(BF16)) * (SWA_HEAD_DIM ** -0.5) + bias_ref[g]
            s = jnp.where(valid, s, -1e30)
            sink = jnp.where(row < cq, sink_ref[heads[0]], jnp.where(row < 2 * cq, sink_ref[heads[1]], sink_ref[heads[2]]))
            m = jnp.maximum(jnp.max(s, axis=-1, keepdims=True), sink)
            e = jnp.exp(s - m)
            den = jnp.sum(e, axis=-1, keepdims=True) + jnp.exp(sink - m)
            o = _dot(e.astype(BF16), vv.astype(BF16)) * (1.0 / den)
            for r, h in enumerate(heads):
                o_ref[0, j * cq:(j + 1) * cq, h * SWA_HEAD_DIM:(h + 1) * SWA_HEAD_DIM] = o[r * cq:(r + 1) * cq, :]


def _swa(sinks, p_swa, k_prev, v_prev, prev_k_map, prev_v_map, bias, qb, cq, win, pos0):
    b, t, _ = p_swa.shape
    kcol, vcol = SWA_W // SWA_KV_W, SWA_W // SWA_KV_W + 1
    return pl.pallas_call(
        functools.partial(_swa_body, qb=qb, cq=cq, win=win, pos0=pos0),
        grid=(b, t // qb),
        in_specs=[pl.BlockSpec(memory_space=pltpu.SMEM),
                  pl.BlockSpec((1, qb, SWA_W), lambda i, j: (i, j, 0)),
                  pl.BlockSpec((1, SWA_WINDOW, SWA_KV_W), prev_k_map),
                  pl.BlockSpec((1, SWA_WINDOW, SWA_KV_W), prev_v_map),
                  pl.BlockSpec((1, qb, SWA_KV_W), lambda i, j: (i, j, kcol)),
                  pl.BlockSpec((1, qb, SWA_KV_W), lambda i, j: (i, j, vcol)),
                  _const_spec(bias.shape)],
        out_specs=pl.BlockSpec((1, qb, SWA_W), lambda i, j: (i, j, 0)),
        out_shape=jax.ShapeDtypeStruct((b, t, SWA_W), F32),
        compiler_params=_cparams("parallel", "parallel"),
        name="swa",
    )(sinks, p_swa, k_prev, v_prev, p_swa, p_swa, bias)


def _out_proj_body(x_ref, ys_ref, og_ref, os_ref, wo_ref, gf_ref, wr_ref, br_ref, triu_ref, before_ref, x1_ref, xs_ref,
                   aux_ref, cnt_ref):
    x1 = (x_ref[...] + _dot(ys_ref[...].astype(BF16), wo_ref[0:SSM_W, :])
          + _dot(og_ref[...].astype(BF16), wo_ref[SSM_W:SSM_W + GLA_W, :])
          + _dot(os_ref[...].astype(BF16), wo_ref[SSM_W + GLA_W:MIX_W, :]))
    x1_ref[...] = x1
    xn = x1 * lax.rsqrt(jnp.mean(x1 * x1, axis=-1, keepdims=True) + RMS_EPS) * gf_ref[...]
    xb = xn.astype(BF16)
    logit = _dot(xb, wr_ref[...]) + br_ref[...]
    lt = jnp.transpose(logit)[:2 * N_EXPERTS, :]
    row_i = lax.broadcasted_iota(jnp.int32, lt.shape, 0)
    row = row_i.astype(F32)
    neg = -jnp.inf
    big = float(LANES)
    g_row = row_i - ROUTER_GROUP_LANE0
    lg = jnp.where((g_row >= 0) & (g_row < N_EXPERT_GROUPS), lt, neg)
    gmax = jnp.max(lg, axis=0, keepdims=True)
    p_sel = 1.0 / jnp.sum(jnp.exp(lg - gmax), axis=0, keepdims=True)
    g_idx = jnp.min(jnp.where(lg == gmax, g_row.astype(F32), big), axis=0, keepdims=True)
    row_group = (row_i >> _log2(EXPERTS_PER_GROUP)).astype(F32)
    in_group = (row_i < N_EXPERTS) & (row_group == g_idx)
    le = jnp.where(in_group, lt, neg)
    m1 = jnp.max(le, axis=0, keepdims=True)
    i1 = jnp.min(jnp.where(le == m1, row, big), axis=0, keepdims=True)
    le2 = jnp.where(row == i1, neg, le)
    m2 = jnp.max(le2, axis=0, keepdims=True)
    i2 = jnp.min(jnp.where(le2 == m2, row, big), axis=0, keepdims=True)
    r = jnp.exp(m2 - m1)
    w1 = 1.0 / (1.0 + r)
    exp_row = lax.broadcasted_iota(jnp.int32, (N_EXPERTS, MOE_TB), 0).astype(F32)
    s1 = jnp.where(exp_row == i1, 1.0, 0.0)
    s2 = jnp.where(exp_row == i2, 1.0, 0.0)
    both = s1 + s2
    cnt = jnp.sum(both, axis=1, keepdims=True) + jnp.zeros_like(both)
    seg = jnp.floor((cnt + (MOE_PIECE - 1)) * (1.0 / MOE_PIECE)) * MOE_PIECE
    seg_k = jnp.concatenate([seg, jnp.zeros((LANES - N_EXPERTS, MOE_TB), F32)], axis=0).astype(BF16)
    lo = _dot(before_ref[...], seg_k)
    rank = _dot(both.astype(BF16), triu_ref[...])
    pos = lo + rank
    d1 = jnp.sum(s1 * pos, axis=0, keepdims=True)
    d2 = jnp.sum(s2 * pos, axis=0, keepdims=True)
    r_iota = lax.broadcasted_iota(jnp.int32, (MOE_RLOC, MOE_TB), 0).astype(F32)
    perm = jnp.where((r_iota == d1) | (r_iota == d2), 1.0, 0.0).astype(BF16)
    xs_ref[...] = _dot(perm, xb).astype(BF16)
    arow = lax.broadcasted_iota(jnp.int32, (LANES, MOE_TB), 0)
    at = jnp.where(arow == AUX_W1, p_sel * w1, jnp.where(arow == AUX_W2, p_sel * (r * w1),
                   jnp.where(arow == AUX_D1, d1, jnp.where(arow == AUX_D2, d2, 0.0))))
    aux_ref[...] = jnp.transpose(at)
    cnt_ref[0] = cnt[:, :LANES]


def _out_proj(x, ys, og, osw, wo, gf, wr, br, triu, before):
    n = x.shape[0]
    nblk = n // MOE_TB
    row = lambda w: pl.BlockSpec((MOE_TB, w), lambda i: (i, 0))
    return pl.pallas_call(
        _out_proj_body,
        grid=(nblk,),
        in_specs=[row(D_MODEL), row(SSM_W), row(GLA_W), row(SWA_W), _const_spec(wo.shape), _const_spec(gf.shape),
                  _const_spec(wr.shape), _const_spec(br.shape), _const_spec(triu.shape), _const_spec(before.shape)],
        out_specs=[row(D_MODEL), pl.BlockSpec((MOE_RLOC, D_MODEL), lambda i: (i, 0)), row(LANES),
                   pl.BlockSpec((1, N_EXPERTS, LANES), lambda i: (i, 0, 0))],
        out_shape=[jax.ShapeDtypeStruct((n, D_MODEL), F32), jax.ShapeDtypeStruct((nblk * MOE_RLOC, D_MODEL), BF16),
                   jax.ShapeDtypeStruct((n, LANES), F32), jax.ShapeDtypeStruct((nblk, N_EXPERTS, LANES), F32)],
        compiler_params=_cparams("parallel"),
        name="out_proj",
    )(x, ys, og, osw, wo, gf, wr, br, triu, before)


def _piece_copy(hbm_ref, piece, buf_ref, slot, p, sem, to_hbm):
    start = lambda i: i * MOE_PIECE if isinstance(i, int) else pl.multiple_of(i * MOE_PIECE, MOE_PIECE)
    rows = pl.ds(start(piece), MOE_PIECE)
    vm = buf_ref.at[slot, pl.ds(start(p), MOE_PIECE)]
    if to_hbm:
        return pltpu.make_async_copy(vm, hbm_ref.at[rows], sem)
    return pltpu.make_async_copy(hbm_ref.at[rows], vm, sem)


def _moe_expert_body(texp_ref, piece_ref, nv_ref, xs_hbm, wg_ref, wu_ref, wd_ref, ys_hbm,
                     xbuf, ybuf, wgb, wub, wdb, sem_in, sem_out, *, pt, n_steps):
    i = pl.program_id(0)
    slot = i % 2

    def for_pieces(tile, s, hbm_ref, buf_ref, sem, to_hbm, wait):
        def one(p):
            piece = 0 if wait else piece_ref[tile * pt + p]
            c = _piece_copy(hbm_ref, piece, buf_ref, s, p, sem.at[s], to_hbm)
            c.wait() if wait else c.start()

        nv = nv_ref[tile]

        @pl.when(nv == pt)
        def _():
            for p in range(pt):
                one(p)

        @pl.when(nv < pt)
        def _():
            def body(p, carry):
                one(p)
                return carry
            lax.fori_loop(0, nv, body, 0)

    gather = lambda tile, s, wait: for_pieces(tile, s, xs_hbm, xbuf, sem_in, False, wait)
    scatter = lambda tile, s, wait: for_pieces(tile, s, ys_hbm, ybuf, sem_out, True, wait)

    @pl.when(i == 0)
    def _():
        xbuf[...] = jnp.zeros_like(xbuf)
        gather(0, 0, False)

    @pl.when(i + 1 < n_steps)
    def _():
        gather(i + 1, 1 - slot, False)

    gather(i, slot, True)

    @pl.when((i == 0) | (texp_ref[i] != texp_ref[jnp.maximum(i - 1, 0)]))
    def _():
        wgb[...] = wg_ref[0].astype(BF16)
        wub[...] = wu_ref[0].astype(BF16)
        wdb[...] = wd_ref[0].astype(BF16)

    @pl.when(nv_ref[i] > 0)
    def _():
        xb = xbuf[slot]
        a = _dot(xb, wgb[...])
        h = (a * _sigmoid(a)) * _dot(xb, wub[...])
        ybuf[slot] = _dot(h.astype(BF16), wdb[...]).astype(BF16)

    scatter(i, slot, False)

    @pl.when(i >= 1)
    def _():
        scatter(i - 1, 1 - slot, True)

    @pl.when(i == n_steps - 1)
    def _():
        scatter(i, slot, True)


def _moe_experts(tile_expert, piece, nvalid, xs, wg, wu, wd, layer, pt):
    n_steps = tile_expert.shape[0]
    tm = pt * MOE_PIECE
    wspec = lambda shape: pl.BlockSpec(shape, lambda i, te, pc, nv: (te[i] + layer * N_EXPERTS, 0, 0))
    grid_spec = pltpu.PrefetchScalarGridSpec(
        num_scalar_prefetch=3,
        grid=(n_steps,),
        in_specs=[pl.BlockSpec(memory_space=pl.ANY), wspec((1, D_MODEL, EXPERT_FF)), wspec((1, D_MODEL, EXPERT_FF)),
                  wspec((1, EXPERT_FF, D_MODEL))],
        out_specs=pl.BlockSpec(memory_space=pl.ANY),
        scratch_shapes=[pltpu.VMEM((2, tm, D_MODEL), BF16), pltpu.VMEM((2, tm, D_MODEL), BF16),
                        pltpu.VMEM((D_MODEL, EXPERT_FF), BF16), pltpu.VMEM((D_MODEL, EXPERT_FF), BF16),
                        pltpu.VMEM((EXPERT_FF, D_MODEL), BF16),
                        pltpu.SemaphoreType.DMA((2,)), pltpu.SemaphoreType.DMA((2,))],
    )
    return pl.pallas_call(
        functools.partial(_moe_expert_body, pt=pt, n_steps=n_steps),
        grid_spec=grid_spec,
        out_shape=jax.ShapeDtypeStruct(xs.shape, BF16),
        input_output_aliases={3: 0},
        compiler_params=_cparams("arbitrary"),
        name="moe_experts",
    )(tile_expert, piece, nvalid, xs, wg, wu, wd)


def _moe_combine_body(ys_ref, aux_ref, x1_ref, o_ref):
    col = lax.broadcasted_iota(jnp.int32, (MOE_TB, MOE_RLOC), 1).astype(F32)
    mix = jnp.where(col == aux_ref[:, AUX_D1:AUX_D1 + 1], aux_ref[:, AUX_W1:AUX_W1 + 1],
                    jnp.where(col == aux_ref[:, AUX_D2:AUX_D2 + 1], aux_ref[:, AUX_W2:AUX_W2 + 1], 0.0))
    o_ref[...] = x1_ref[...] + _dot(mix.astype(BF16), ys_ref[...])


def _moe_combine(ys, aux, x1):
    n = x1.shape[0]
    return pl.pallas_call(
        _moe_combine_body,
        grid=(n // MOE_TB,),
        in_specs=[pl.BlockSpec((MOE_RLOC, D_MODEL), lambda i: (i, 0)), pl.BlockSpec((MOE_TB, LANES), lambda i: (i, 0)),
                  pl.BlockSpec((MOE_TB, D_MODEL), lambda i: (i, 0))],
        out_specs=pl.BlockSpec((MOE_TB, D_MODEL), lambda i: (i, 0)),
        out_shape=jax.ShapeDtypeStruct((n, D_MODEL), F32),
        compiler_params=_cparams("parallel"),
        name="moe_combine",
    )(ys, aux, x1)


def _route_tables(counts, pt):
    nblk = counts.shape[0]
    n = nblk * MOE_TB
    cnt = counts[:, :, 0].T.astype(jnp.int32)
    pc = (cnt + MOE_PIECE - 1) // MOE_PIECE
    lo_p = jnp.cumsum(pc, axis=0) - pc
    pe = pc.sum(1)
    tiles_e = (pe + pt - 1) // pt
    tile_start = jnp.cumsum(tiles_e) - tiles_e
    seg_start = (tile_start[:, None] * pt + jnp.cumsum(pc, axis=1) - pc).reshape(-1)
    pcs = pc.reshape(-1)
    seg_src = (jnp.arange(nblk, dtype=jnp.int32)[None, :] * MOE_PPB + lo_p).reshape(-1)
    n_steps = -(-(2 * n // MOE_PIECE + nblk * N_EXPERTS + N_EXPERTS * (pt - 1)) // pt)
    slot = jnp.arange(n_steps * pt, dtype=jnp.int32)[:, None]
    in_seg = (slot >= seg_start[None, :]) & (slot < (seg_start + pcs)[None, :])
    piece = jnp.sum(jnp.where(in_seg, seg_src[None, :] + slot - seg_start[None, :], 0), axis=1).astype(jnp.int32)
    tile = jnp.arange(n_steps, dtype=jnp.int32)[:, None]
    in_exp = (tile >= tile_start[None, :]) & (tile < (tile_start + tiles_e)[None, :])
    tile_expert = jnp.sum(jnp.where(in_exp, jnp.arange(N_EXPERTS, dtype=jnp.int32)[None, :], 0), axis=1)
    tile_expert = jnp.where(tile[:, 0] < tiles_e.sum(), tile_expert, N_EXPERTS - 1).astype(jnp.int32)
    nvalid = jnp.sum(jnp.where(in_exp, jnp.clip(pe[None, :] - (tile - tile_start[None, :]) * pt, 0, pt), 0),
                     axis=1).astype(jnp.int32)
    return tile_expert, piece, nvalid


def _moe(xs, aux, x1, counts, wg, wu, wd, layer, pt):
    tile_expert, piece, nvalid = _route_tables(counts, pt)
    ys = _moe_experts(tile_expert, piece, nvalid, xs, wg, wu, wd, layer, pt)
    return _moe_combine(ys, aux, x1)


def _seg_mean_matrix(width, seg):
    i = jnp.arange(width)
    return jnp.where((i[:, None] // seg) == (i[None, :] // seg), 1.0 / seg, 0.0).astype(BF16)


def _chunk_tril(span, lc):
    i = jnp.arange(span)
    return ((i[:, None] >= i[None, :]) & ((i[:, None] // lc) == (i[None, :] // lc))).astype(BF16)


def _prep_layer(lw, ssm_shapes, gla_shapes):
    w_in = lw["w_in"].astype(F32)
    cols = {}
    off = 0
    for name, wdt in (("u", SSM_W), ("qg", GLA_K), ("kg", GLA_K), ("vg", GLA_W), ("z", GLA_GATE_RANK), ("og", GLA_W),
                      ("qs", SWA_W), ("ks", SWA_KV_W), ("vs", SWA_KV_W)):
        cols[name] = w_in[:, off:off + wdt]
        off += wdt
    zpad = lambda wdt: jnp.zeros((D_MODEL, wdt), F32)
    w_all = jnp.concatenate(
        [cols["qg"], zpad(GLA_KP - GLA_K), cols["kg"], zpad(GLA_KP - GLA_K), cols["vg"], cols["og"], cols["z"],
         zpad(LANES - GLA_GATE_RANK), cols["qs"], cols["ks"], cols["vs"], cols["u"]], axis=1).astype(BF16)
    out = {"norm_mix": lw["norm_mix"].astype(F32)[None, :], "w_all": w_all,
           "mavg_qk": _seg_mean_matrix(SWA_W + SWA_KV_W, SWA_HEAD_DIM),
           "qk_gain": jnp.concatenate([jnp.tile(lw["swa_q_norm"].astype(F32), SWA_HEADS),
                                       jnp.tile(lw["swa_k_norm"].astype(F32), SWA_KV_HEADS)])[None, :]}
    a_re = lw["ssm_a_re"].astype(F32)
    a_im = lw["ssm_a_im"].astype(F32)
    dt = jnp.exp(lw["ssm_log_dt"].astype(F32))[:, None]
    mag = jnp.exp(a_re * dt)
    ab_re = mag * jnp.cos(a_im * dt)
    ab_im = mag * jnp.sin(a_im * dt)
    den = a_re * a_re + a_im * a_im
    nr = ab_re - 1.0
    f_re = (nr * a_re + ab_im * a_im) / den
    f_im = (ab_im * a_re - nr * a_im) / den
    b_re = lw["ssm_b_re"].astype(F32)
    b_im = lw["ssm_b_im"].astype(F32)
    bb_re = f_re[..., None] * b_re - f_im[..., None] * b_im
    bb_im = f_re[..., None] * b_im + f_im[..., None] * b_re
    eye_g = jnp.eye(SSM_GROUPS, dtype=F32)
    blockdiag_in = lambda bb: jnp.einsum("gpc,gh->gchp", bb, eye_g).reshape(SSM_W, SSM_FLAT)
    blockdiag_out = lambda cc: jnp.einsum("gcp,gh->gphc", cc, eye_g).reshape(SSM_FLAT, SSM_W)
    bd = jnp.concatenate([blockdiag_in(bb_re), blockdiag_in(bb_im)], axis=1).astype(BF16)
    cd = jnp.concatenate([blockdiag_out(lw["ssm_c_re"].astype(F32)),
                          -blockdiag_out(lw["ssm_c_im"].astype(F32))], axis=0).astype(BF16)
    ssm = {"ab": jnp.concatenate([ab_re.reshape(1, SSM_FLAT), ab_im.reshape(1, SSM_FLAT),
                                  jnp.zeros((6, SSM_FLAT), F32)], axis=0),
           "bd": bd, "cd": cd, "d": lw["ssm_d"].astype(F32)[None, :], "wglu": lw["ssm_w_glu"].astype(BF16)}
    def powers(lc, sign):
        tt = jnp.arange(lc, dtype=F32)[:, None, None]
        m = jnp.exp(sign * tt * (a_re * dt)[None])
        ang = sign * tt * (a_im * dt)[None]
        return jnp.stack([(m * jnp.cos(ang)).reshape(lc, SSM_FLAT), (m * jnp.sin(ang)).reshape(lc, SSM_FLAT)])

    out["ssm"] = {(span, lc): dict(ssm, pwp=powers(lc, 1.0), pwn=powers(lc, -1.0), tril=_chunk_tril(span, lc))
                  for span, lc in ssm_shapes}
    wg = jnp.zeros((LANES, GLA_KP), F32).at[:GLA_GATE_RANK, :GLA_K].set(lw["gla_w_gate"].astype(F32)).astype(BF16)
    bg = jnp.zeros((1, GLA_KP), F32).at[0, :GLA_K].set(lw["gla_b_gate"].astype(F32))
    kk = jnp.arange(GLA_KP)
    spread = ((kk[None, :] % GLA_DK) == jnp.arange(GLA_DK)[:, None]) & (kk[None, :] < GLA_K)
    gla = {"wg": wg, "bg": bg, "mavg": _seg_mean_matrix(GLA_W, GLA_DV),
           "gn": jnp.tile(lw["gla_norm"].astype(F32), GLA_HEADS)[None, :], "spread": spread.astype(BF16)}
    out["gla"] = {(span, lc): dict(gla, tril=_chunk_tril(span, lc)) for span, lc in gla_shapes}
    out["sinks"] = lw["swa_sinks"].astype(F32)
    out["w_out"] = lw["w_out"].astype(BF16)
    out["norm_ffn"] = lw["norm_ffn"].astype(F32)[None, :]
    wr = jnp.zeros((D_MODEL, LANES), F32)
    wr = wr.at[:, :N_EXPERTS].set(lw["moe_w_expert"].astype(F32))
    wr = wr.at[:, ROUTER_GROUP_LANE0:ROUTER_GROUP_LANE0 + N_EXPERT_GROUPS].set(lw["moe_w_group"].astype(F32))
    out["wr"] = wr.astype(BF16)
    br = jnp.zeros((1, LANES), F32)
    br = br.at[0, :N_EXPERTS].set(lw["moe_b_expert"].astype(F32))
    br = br.at[0, ROUTER_GROUP_LANE0:ROUTER_GROUP_LANE0 + N_EXPERT_GROUPS].set(lw["moe_b_group"].astype(F32))
    out["br"] = br
    ti = jnp.arange(MOE_TB)
    out["moe_triu"] = (ti[:, None] < ti[None, :]).astype(BF16)
    out["moe_before"] = (jnp.arange(LANES)[None, :] < jnp.arange(N_EXPERTS)[:, None]).astype(BF16)
    return out


def _gla_state_in(h0):
    return jnp.transpose(h0.astype(F32), (0, 1, 3, 2)).reshape(h0.shape[0], GLA_W, GLA_DK)


def _gla_state_out(st):
    return jnp.transpose(st.reshape(st.shape[0], GLA_HEADS, GLA_DV, GLA_DK), (0, 1, 3, 2))


def _layer(x, pw, bias, ssm_h0r, ssm_h0i, gla_h0, past_k, past_v, cfg):
    b, t, _ = x.shape
    n = b * t
    p_gla, p_swa, p_ssm = _in_proj(x.reshape(n, D_MODEL), pw["norm_mix"], pw["w_all"], pw["mavg_qk"], pw["qk_gain"],
                                   cfg["tm"])
    p_gla = p_gla.reshape(b, t, GLA_PW)
    p_swa = p_swa.reshape(b, t, SWA_PW)
    y_ssm, h_re, h_im = _ssm(p_ssm.reshape(b, t, SSM_W), ssm_h0r.reshape(b, 1, SSM_FLAT).astype(F32),
                             ssm_h0i.reshape(b, 1, SSM_FLAT).astype(F32), pw["ssm"][cfg["ssm"]], cfg["lblk"],
                             *cfg["ssm"])
    o_g, s_fin = _gla(p_gla, _gla_state_in(gla_h0), pw["gla"][cfg["gla"]], cfg["lblk"], *cfg["gla"])
    kcol, vcol = SWA_W // SWA_KV_W, SWA_W // SWA_KV_W + 1
    if past_k is None:
        per_blk = cfg["qb"] // SWA_WINDOW
        prev_k_map = lambda i, j: (i, jnp.maximum(j * per_blk - 1, 0), kcol)
        prev_v_map = lambda i, j: (i, jnp.maximum(j * per_blk - 1, 0), vcol)
        k_prev, v_prev = p_swa, p_swa
        keep = min(SWA_WINDOW, t)
        new_k = p_swa[:, t - keep:, SWA_W:SWA_W + SWA_KV_W]
        new_v = p_swa[:, t - keep:, SWA_W + SWA_KV_W:]
    else:
        prev_k_map = prev_v_map = lambda i, j: (i, 0, 0)
        k_prev = past_k.reshape(b, SWA_WINDOW, SWA_KV_W).astype(F32)
        v_prev = past_v.reshape(b, SWA_WINDOW, SWA_KV_W).astype(F32)
        new_k = p_swa[:, :, SWA_W:SWA_W + SWA_KV_W]
        new_v = p_swa[:, :, SWA_W + SWA_KV_W:]
    o_s = _swa(pw["sinks"], p_swa, k_prev, v_prev, prev_k_map, prev_v_map, bias, cfg["qb"], cfg["cq"], cfg["win"],
               cfg["pos0"])
    x1, xs, aux, counts = _out_proj(x.reshape(n, D_MODEL), y_ssm.reshape(n, SSM_W), o_g.reshape(n, GLA_W),
                                    o_s.reshape(n, SWA_W), pw["w_out"], pw["norm_ffn"], pw["wr"], pw["br"],
                                    pw["moe_triu"], pw["moe_before"])
    x2 = _moe(xs, aux, x1, counts, pw["moe_wg"], pw["moe_wu"], pw["moe_wd"], pw["layer"], cfg["pt"])
    kv_shape = (b, new_k.shape[1], SWA_KV_HEADS, SWA_HEAD_DIM)
    return (x2.reshape(b, t, D_MODEL), new_k.reshape(kv_shape), new_v.reshape(kv_shape), _gla_state_out(s_fin),
            h_re.reshape(b, SSM_GROUPS, SSM_STATE), h_im.reshape(b, SSM_GROUPS, SSM_STATE))


def _group_cfg(t, past_len, n_past):
    if n_past is None:
        assert t % (16 * CHUNK) == 0
        cfg = dict(ssm=(4 * CHUNK, CHUNK), gla=(4 * CHUNK, 2 * CHUNK), lblk=16 * CHUNK, qb=8 * CHUNK, cq=CHUNK,
                   win=SWA_WINDOW + CHUNK, pos0=-SWA_WINDOW, tm=512, pt=32)
        q_pos = jnp.arange(CHUNK, dtype=jnp.int32) + SWA_WINDOW
        k_pos = jnp.arange(SWA_WINDOW + CHUNK, dtype=jnp.int32)
    else:
        first_key, last_q = past_len - n_past, past_len + t - 1
        assert n_past == SWA_WINDOW and t <= CHUNK and t % 8 == 0
        assert past_len // CHUNK == last_q // CHUNK and first_key // CHUNK >= past_len // CHUNK - SWA_WINDOW // CHUNK
        cfg = dict(ssm=(t, t), gla=(t, t), lblk=t, qb=t, cq=t, win=n_past + t, pos0=first_key, tm=512, pt=16)
        q_pos = past_len + jnp.arange(t, dtype=jnp.int32)
        k_pos = first_key + jnp.arange(n_past + t, dtype=jnp.int32)
    return cfg, q_pos, k_pos


def _stacked_bias(t5_table, q_pos, k_pos):
    bias = _t5_bias(t5_table, q_pos, k_pos)
    nq, nk = bias.shape[1:]
    return bias.reshape(SWA_KV_HEADS, SWA_REP * nq, nk)


PAST_LEN = 1024


def kernel(x_prompt, x_sample, cache_swa_k, cache_swa_v, state_gla, state_ssm_re, state_ssm_im, norm_mix, w_in, ssm_a_re, ssm_a_im, ssm_log_dt, ssm_b_re, ssm_b_im, ssm_c_re, ssm_c_im, ssm_d, ssm_w_glu, gla_w_gate, gla_b_gate, gla_norm, swa_q_norm, swa_k_norm, swa_sinks, t5_table, w_out, norm_ffn, moe_w_group, moe_b_group, moe_w_expert, moe_b_expert, moe_w_gate, moe_w_up, moe_w_down):
    depth = w_in.shape[0]
    bp, tp, _ = x_prompt.shape
    bs, ts, _ = x_sample.shape
    cfg_p, qpos_p, kpos_p = _group_cfg(tp, 0, None)
    cfg_s, qpos_s, kpos_s = _group_cfg(ts, PAST_LEN, cache_swa_k.shape[2])
    bias_p = _stacked_bias(t5_table, qpos_p, kpos_p)
    bias_s = _stacked_bias(t5_table, qpos_s, kpos_s)
    hp, hs = x_prompt.astype(F32), x_sample.astype(F32)
    moe_wg = moe_w_gate.astype(F32).reshape(depth * N_EXPERTS, D_MODEL, EXPERT_FF)
    moe_wu = moe_w_up.astype(F32).reshape(depth * N_EXPERTS, D_MODEL, EXPERT_FF)
    moe_wd = moe_w_down.astype(F32).reshape(depth * N_EXPERTS, EXPERT_FF, D_MODEL)
    outs = [[] for _ in range(10)]
    for l in range(depth):
        lw = {
            "norm_mix": norm_mix[l], "w_in": w_in[l], "ssm_a_re": ssm_a_re[l], "ssm_a_im": ssm_a_im[l],
            "ssm_log_dt": ssm_log_dt[l], "ssm_b_re": ssm_b_re[l], "ssm_b_im": ssm_b_im[l], "ssm_c_re": ssm_c_re[l],
            "ssm_c_im": ssm_c_im[l], "ssm_d": ssm_d[l], "ssm_w_glu": ssm_w_glu[l], "gla_w_gate": gla_w_gate[l],
            "gla_b_gate": gla_b_gate[l], "gla_norm": gla_norm[l], "swa_q_norm": swa_q_norm[l],
            "swa_k_norm": swa_k_norm[l], "swa_sinks": swa_sinks[l], "w_out": w_out[l], "norm_ffn": norm_ffn[l],
            "moe_w_group": moe_w_group[l], "moe_b_group": moe_b_group[l], "moe_w_expert": moe_w_expert[l],
            "moe_b_expert": moe_b_expert[l],
        }
        pw = _prep_layer(lw, {cfg_p["ssm"], cfg_s["ssm"]}, {cfg_p["gla"], cfg_s["gla"]})
        pw.update(layer=l, moe_wg=moe_wg, moe_wu=moe_wu, moe_wd=moe_wd)
        zs = jnp.zeros((bp, SSM_GROUPS, SSM_STATE), F32)
        zg = jnp.zeros((bp, GLA_HEADS, GLA_DK, GLA_DV), F32)
        hp, nk, nv, ng, nr, ni = _layer(hp, pw, bias_p, zs, zs, zg, None, None, cfg_p)
        for slot, val in zip((0, 1, 4, 6, 7), (nk, nv, ng, nr, ni)):
            outs[slot].append(val)
        hs, nk, nv, ng, nr, ni = _layer(hs, pw, bias_s, state_ssm_re[l], state_ssm_im[l], state_gla[l],
                                        cache_swa_k[l], cache_swa_v[l], cfg_s)
        for slot, val in zip((2, 3, 5, 8, 9), (nk, nv, ng, nr, ni)):
            outs[slot].append(val)
    return (hp, hs) + tuple(jnp.stack(o) for o in outs)
```

```python
import functools
import math

import jax
import jax.numpy as jnp
from jax import lax
from jax.experimental import pallas as pl
from jax.experimental.pallas import tpu as pltpu

F32 = jnp.float32
BF16 = jnp.bfloat16

D_MODEL = 1024
CHUNK = 64
RMS_EPS = 1e-6
SSM_GROUPS = 16
SSM_GC = 16
SSM_STATE = 64
SSM_W = SSM_GROUPS * SSM_GC
SSM_FLAT = SSM_GROUPS * SSM_STATE
GLA_HEADS = 6
GLA_DK = 32
GLA_DV = 64
GLA_GATE_RANK = 16
GLA_GATE_NORM = 16.0
GLA_K = GLA_HEADS * GLA_DK
GLA_KP = 256
GLA_W = GLA_HEADS * GLA_DV
SWA_HEADS = 6
SWA_KV_HEADS = 2
SWA_REP = SWA_HEADS // SWA_KV_HEADS
SWA_HEAD_DIM = 64
SWA_WINDOW = 128
SWA_W = SWA_HEADS * SWA_HEAD_DIM
SWA_KV_W = SWA_KV_HEADS * SWA_HEAD_DIM
MIX_W = SSM_W + GLA_W + SWA_W
N_BUCKETS = 32
T5_MAX_DIST = 128
N_EXPERT_GROUPS = 4
EXPERTS_PER_GROUP = 4
N_EXPERTS = 16
EXPERT_FF = 512
LANES = 128
ROUTER_GROUP_LANE0 = N_EXPERTS
AUX_W1, AUX_W2, AUX_D1, AUX_D2 = 0, 1, 2, 3
AUX_ROWS = 8
DECODE_STACK = 8
MOE_TB = 512
MOE_PIECE = 16
MOE_RLOC = 2 * MOE_TB + N_EXPERTS * MOE_PIECE
MOE_PPB = MOE_RLOC // MOE_PIECE

GLA_Q0, GLA_K0, GLA_V0, GLA_OG0, GLA_Z0, GLA_PW = 0, 256, 512, 896, 1280, 1408
SWA_PW = SWA_W + 2 * SWA_KV_W
P_GLA0, P_SWA0, P_SSM0, P_TOTAL = 0, GLA_PW, GLA_PW + SWA_PW, GLA_PW + SWA_PW + SSM_W

VMEM_LIMIT = 48 * 1024 * 1024


def _cparams(*sem):
    return pltpu.CompilerParams(dimension_semantics=sem, vmem_limit_bytes=VMEM_LIMIT)


def _dot(a, b):
    return jnp.dot(a, b, preferred_element_type=F32)


def _dot_nt(a, b):
    return lax.dot_general(a, b, (((1,), (1,)), ((), ())), preferred_element_type=F32)


def _dot_tn(a, b):
    return lax.dot_general(a, b, (((0,), (0,)), ((), ())), preferred_element_type=F32)


def _hi_lo(x):
    hi = x.astype(BF16)
    return hi, (x - hi.astype(F32)).astype(BF16)


def _dot_f32_rhs(a_bf16, x):
    hi, lo = _hi_lo(x)
    return _dot(a_bf16, hi) + _dot(a_bf16, lo)


def _dot_f32_lhs(x, b_bf16):
    hi, lo = _hi_lo(x)
    return _dot(hi, b_bf16) + _dot(lo, b_bf16)


def _log2(n):
    assert n & (n - 1) == 0
    return n.bit_length() - 1


def _sigmoid(x):
    return 1.0 / (1.0 + jnp.exp(-x))


def _const_spec(shape):
    nd = len(shape)
    return pl.BlockSpec(shape, lambda *_: (0,) * nd)


def _in_proj_body(x_ref, g_ref, w_ref, mavg_ref, qkg_ref, gla_ref, swa_ref, ssm_ref):
    _in_proj_math(x_ref[...], g_ref, w_ref, mavg_ref, qkg_ref, gla_ref, swa_ref, ssm_ref)


def _in_proj_math(x, g_ref, w_ref, mavg_ref, qkg_ref, gla_ref, swa_ref, ssm_ref):
    xn = x * lax.rsqrt(jnp.mean(x * x, axis=-1, keepdims=True) + RMS_EPS) * g_ref[...]
    xb = xn.astype(BF16)
    gla_ref[...] = _dot(xb, w_ref[:, P_GLA0:P_SWA0])
    ssm_ref[...] = _dot(xb, w_ref[:, P_SSM0:P_TOTAL])
    s = _dot(xb, w_ref[:, P_SWA0:P_SSM0])
    qk = s[:, :SWA_W + SWA_KV_W]
    ms = _dot_f32_lhs(qk * qk, mavg_ref[...])
    swa_ref[:, :SWA_W + SWA_KV_W] = qk * lax.rsqrt(ms + RMS_EPS) * qkg_ref[...]
    swa_ref[:, SWA_W + SWA_KV_W:] = s[:, SWA_W + SWA_KV_W:]


def _in_proj(x, g, w, mavg, qkg, tm):
    n = x.shape[0]
    return pl.pallas_call(
        _in_proj_body,
        grid=(n // tm,),
        in_specs=[pl.BlockSpec((tm, D_MODEL), lambda i: (i, 0)), _const_spec(g.shape), _const_spec(w.shape),
                  _const_spec(mavg.shape), _const_spec(qkg.shape)],
        out_specs=[pl.BlockSpec((tm, GLA_PW), lambda i: (i, 0)), pl.BlockSpec((tm, SWA_PW), lambda i: (i, 0)),
                   pl.BlockSpec((tm, SSM_W), lambda i: (i, 0))],
        out_shape=[jax.ShapeDtypeStruct((n, GLA_PW), F32), jax.ShapeDtypeStruct((n, SWA_PW), F32),
                   jax.ShapeDtypeStruct((n, SSM_W), F32)],
        compiler_params=_cparams("parallel"),
        name="in_proj",
    )(x, g, w, mavg, qkg)


def _ssm_body(u_ref, h0r_ref, h0i_ref, ab_ref, bd_ref, cd_ref, pwp_ref, pwn_ref, tril_ref, d_ref, wglu_ref,
              y_ref, hr_ref, hi_ref, carry_ref, *, n_part, n_sub, lc, indep):
    if not indep:
        @pl.when(pl.program_id(1) == 0)
        def _():
            carry_ref[0:1, :] = h0r_ref[0]
            carry_ref[1:2, :] = h0i_ref[0]

        hr = carry_ref[0:1, :]
        hi = carry_ref[1:2, :]
    ab_re = ab_ref[0:1, :]
    ab_im = ab_ref[1:2, :]
    nr, ni = pwn_ref[0], pwn_ref[1]
    pr, pi = pwp_ref[0], pwp_ref[1]
    span = n_sub * lc
    for part in range(n_part):
        rows = slice(part * span, (part + 1) * span)
        u = u_ref[0, rows, :]
        bu = _dot(u.astype(BF16), bd_ref[...])
        sr, si = [], []
        for c in range(n_sub):
            bur = bu[c * lc:(c + 1) * lc, :SSM_FLAT]
            bui = bu[c * lc:(c + 1) * lc, SSM_FLAT:]
            sr.append(nr * bur - ni * bui)
            si.append(nr * bui + ni * bur)
        scaled = jnp.concatenate([jnp.concatenate(sr, axis=0), jnp.concatenate(si, axis=0)], axis=1)
        cs = _dot_f32_rhs(tril_ref[...], scaled)
        h_r, h_i = [], []
        for c in range(n_sub):
            if indep:
                hr, hi = h0r_ref[0, c:c + 1, :], h0i_ref[0, c:c + 1, :]
            cr = cs[c * lc:(c + 1) * lc, :SSM_FLAT] + (ab_re * hr - ab_im * hi)
            ci = cs[c * lc:(c + 1) * lc, SSM_FLAT:] + (ab_re * hi + ab_im * hr)
            h_r.append(pr * cr - pi * ci)
            h_i.append(pr * ci + pi * cr)
            hr = h_r[-1][lc - 1:lc, :]
            hi = h_i[-1][lc - 1:lc, :]
            if indep:
                hr_ref[0, c:c + 1, :] = hr
                hi_ref[0, c:c + 1, :] = hi
        hcat = jnp.concatenate([jnp.concatenate(h_r, axis=0), jnp.concatenate(h_i, axis=0)], axis=1).astype(BF16)
        y = _dot(hcat, cd_ref[...]) + d_ref[...] * u
        g = 0.5 * y * (1.0 + jnp.tanh(math.sqrt(2.0 / math.pi) * (y + 0.044715 * (y * y * y))))
        y_ref[0, rows, :] = g * _sigmoid(_dot(g.astype(BF16), wglu_ref[...]))
    if not indep:
        carry_ref[0:1, :] = hr
        carry_ref[1:2, :] = hi
        hr_ref[0] = hr
        hi_ref[0] = hi


def _ssm(u, h0r, h0i, sw, lblk, span, lc, indep):
    b, t, _ = u.shape
    n_state = span // lc if indep else 1
    assert not indep or (t == lblk == span)
    consts = [sw["ab"], sw["bd"], sw["cd"], sw["pwp"], sw["pwn"], sw["tril"], sw["d"], sw["wglu"]]
    state_spec = pl.BlockSpec((1, n_state, SSM_FLAT), lambda i, j: (i, 0, 0))
    return pl.pallas_call(
        functools.partial(_ssm_body, n_part=lblk // span, n_sub=span // lc, lc=lc, indep=indep),
        grid=(b, t // lblk),
        in_specs=[pl.BlockSpec((1, lblk, SSM_W), lambda i, j: (i, j, 0)), state_spec, state_spec]
        + [_const_spec(c.shape) for c in consts],
        out_specs=[pl.BlockSpec((1, lblk, SSM_W), lambda i, j: (i, j, 0)), state_spec, state_spec],
        out_shape=[jax.ShapeDtypeStruct((b, t, SSM_W), F32), jax.ShapeDtypeStruct((b, n_state, SSM_FLAT), F32),
                   jax.ShapeDtypeStruct((b, n_state, SSM_FLAT), F32)],
        scratch_shapes=[pltpu.VMEM((8, SSM_FLAT), F32)],
        compiler_params=_cparams("parallel", "arbitrary"),
        name="ssm",
    )(u, h0r, h0i, *consts)


def _gla_body(p_ref, s0_ref, wg_ref, bg_ref, tril_ref, mavg_ref, gn_ref, spread_ref, o_ref, sfin_ref, s_ref, *,
              n_part, n_sub, lc, indep):
    row_v = lax.broadcasted_iota(jnp.int32, (GLA_W, GLA_KP), 0)
    col_k = lax.broadcasted_iota(jnp.int32, (GLA_W, GLA_KP), 1)
    same_head = ((row_v >> _log2(GLA_DV)) == (col_k >> _log2(GLA_DK))).astype(F32)

    def spread(compact):
        return _dot_f32_lhs(compact, spread_ref[...]) * same_head

    def gather(full):
        hi, lo = _hi_lo(full)
        return _dot_nt(hi, spread_ref[...]) + _dot_nt(lo, spread_ref[...])

    if not indep:
        @pl.when(pl.program_id(1) == 0)
        def _():
            s_ref[...] = spread(s0_ref[0, 0])

    lane_k = lax.broadcasted_iota(jnp.int32, (1, GLA_KP), 1)
    lane_v = lax.broadcasted_iota(jnp.int32, (1, GLA_W), 1)
    head_k = [((lane_k >= h * GLA_DK) & (lane_k < (h + 1) * GLA_DK)).astype(F32) for h in range(GLA_HEADS)]
    head_v = [((lane_v >= h * GLA_DV) & (lane_v < (h + 1) * GLA_DV)).astype(F32) for h in range(GLA_HEADS)]
    row_t = lax.broadcasted_iota(jnp.int32, (GLA_HEADS * lc, lc), 0)
    col_s = lax.broadcasted_iota(jnp.int32, (GLA_HEADS * lc, lc), 1)
    causal = (row_t & (lc - 1)) >= col_s
    mid = lc // 2 - 1
    span = n_sub * lc
    st = None if indep else s_ref[...]
    for part in range(n_part):
        rows = slice(part * span, (part + 1) * span)
        z = p_ref[0, rows, GLA_Z0:GLA_Z0 + LANES]
        gin = _dot(z.astype(BF16), wg_ref[...]) + bg_ref[...]
        glog = (jnp.minimum(gin, 0.0) - jnp.log(1.0 + jnp.exp(-jnp.abs(gin)))) / GLA_GATE_NORM
        g_all = _dot_f32_rhs(tril_ref[...], glog)
        outs = []
        for c in range(n_sub):
            crow = slice(part * span + c * lc, part * span + (c + 1) * lc)
            q = p_ref[0, crow, GLA_Q0:GLA_Q0 + GLA_KP] * (GLA_DK ** -0.5)
            k = p_ref[0, crow, GLA_K0:GLA_K0 + GLA_KP]
            vb = p_ref[0, crow, GLA_V0:GLA_V0 + GLA_W].astype(BF16)
            if indep:
                st = spread(s0_ref[0, c])
            gc = g_all[c * lc:(c + 1) * lc, :]
            gl = gc[lc - 1:lc, :]
            gm = gc[mid:mid + 1, :]
            qc = (q * jnp.exp(gc - gm))
            ke = (k * jnp.exp(gm - gc)).astype(BF16)
            kd = (k * jnp.exp(gl - gc)).astype(BF16)
            qs = jnp.concatenate([qc * m for m in head_k], axis=0).astype(BF16)
            attn = jnp.where(causal, _dot_nt(qs, ke), 0.0)
            o2 = _dot(attn.astype(BF16), vb)
            o = _dot_nt((q * jnp.exp(gc)).astype(BF16), st.astype(BF16))
            for h in range(GLA_HEADS):
                o = o + head_v[h] * o2[h * lc:(h + 1) * lc, :]
            st = st * jnp.exp(gl) + _dot_tn(vb, kd) * same_head
            if indep:
                sfin_ref[0, c] = gather(st)
            outs.append(o)
        o = jnp.concatenate(outs, axis=0) if n_sub > 1 else outs[0]
        og = p_ref[0, rows, GLA_OG0:GLA_OG0 + GLA_W]
        ms = _dot_f32_lhs(o * o, mavg_ref[...])
        on = o * lax.rsqrt(ms + RMS_EPS) * gn_ref[...]
        o_ref[0, rows, :] = on * (og * _sigmoid(og))
    if not indep:
        s_ref[...] = st

        @pl.when(pl.program_id(1) == pl.num_programs(1) - 1)
        def _():
            sfin_ref[0, 0] = gather(st)


def _gla(p, s0, gw, lblk, span, lc, indep):
    b, t, _ = p.shape
    n_state = span // lc if indep else 1
    assert not indep or (t == lblk == span)
    consts = [gw["wg"], gw["bg"], gw["tril"], gw["mavg"], gw["gn"], gw["spread"]]
    st_spec = pl.BlockSpec((1, n_state, GLA_W, GLA_DK), lambda i, j: (i, 0, 0, 0))
    return pl.pallas_call(
        functools.partial(_gla_body, n_part=lblk // span, n_sub=span // lc, lc=lc, indep=indep),
        grid=(b, t // lblk),
        in_specs=[pl.BlockSpec((1, lblk, GLA_PW), lambda i, j: (i, j, 0)), st_spec]
        + [_const_spec(c.shape) for c in consts],
        out_specs=[pl.BlockSpec((1, lblk, GLA_W), lambda i, j: (i, j, 0)), st_spec],
        out_shape=[jax.ShapeDtypeStruct((b, t, GLA_W), F32),
                   jax.ShapeDtypeStruct((b, n_state, GLA_W, GLA_DK), F32)],
        scratch_shapes=[pltpu.VMEM((GLA_W, GLA_KP), F32)],
        compiler_params=_cparams("parallel", "arbitrary"),
        name="gla",
    )(p, s0, *consts)


def _t5_bias(t5_table, q_pos, k_pos):
    nq, nk = q_pos.shape[0], k_pos.shape[0]
    rel = (k_pos[0] - q_pos[0]) + jnp.arange(-(nq - 1), nk, dtype=jnp.int32)
    half = N_BUCKETS // 2
    max_exact = half // 2
    n = jnp.abs(rel)
    far = max_exact + (jnp.log(jnp.maximum(n, 1).astype(jnp.float32) / max_exact)
                       / math.log(T5_MAX_DIST / max_exact) * (half - max_exact)).astype(jnp.int32)
    bucket = jnp.where(rel > 0, half, 0) + jnp.where(n < max_exact, n, jnp.minimum(far, half - 1))
    by_dist = t5_table.astype(F32)[bucket].T
    return jnp.stack([by_dist[:, nq - 1 - i:nq - 1 - i + nk] for i in range(nq)], axis=1)


def _swa_body(sink_ref, q_ref, kp_ref, vp_ref, kc_ref, vc_ref, bias_ref, o_ref, *, bb, qb, cq, win, pos0):
    blk = pl.program_id(1)
    col = lax.broadcasted_iota(jnp.int32, (1, win), 1)
    row = lax.broadcasted_iota(jnp.int32, (SWA_REP * cq, 1), 0)
    for b in range(bb):
        kwin = jnp.concatenate([kp_ref[b], kc_ref[b]], axis=0)
        vwin = jnp.concatenate([vp_ref[b], vc_ref[b]], axis=0)
        for j in range(qb // cq):
            valid = (pos0 + blk * qb + j * cq + col) >= 0
            for g in range(SWA_KV_HEADS):
                heads = [SWA_REP * g + r for r in range(SWA_REP)]
                qs = jnp.concatenate(
                    [q_ref[b, j * cq:(j + 1) * cq, h * SWA_HEAD_DIM:(h + 1) * SWA_HEAD_DIM] for h in heads], axis=0)
                kk = kwin[j * cq:j * cq + win, g * SWA_HEAD_DIM:(g + 1) * SWA_HEAD_DIM]
                vv = vwin[j * cq:j * cq + win, g * SWA_HEAD_DIM:(g + 1) * SWA_HEAD_DIM]
                s = _dot_nt(qs.astype(BF16), kk.astype(BF16)) * (SWA_HEAD_DIM ** -0.5) + bias_ref[g]
                s = jnp.where(valid, s, -1e30)
                sink = jnp.where(row < cq, sink_ref[heads[0]],
                                 jnp.where(row < 2 * cq, sink_ref[heads[1]], sink_ref[heads[2]]))
                m = jnp.maximum(jnp.max(s, axis=-1, keepdims=True), sink)
                e = jnp.exp(s - m)
                den = jnp.sum(e, axis=-1, keepdims=True) + jnp.exp(sink - m)
                o = _dot(e.astype(BF16), vv.astype(BF16)) * (1.0 / den)
                for r, h in enumerate(heads):
                    o_ref[b, j * cq:(j + 1) * cq, h * SWA_HEAD_DIM:(h + 1) * SWA_HEAD_DIM] = o[r * cq:(r + 1) * cq, :]


def _swa(sinks, p_swa, k_prev, v_prev, prev_k_map, prev_v_map, bias, bb, qb, cq, win, pos0):
    b, t, _ = p_swa.shape
    kcol, vcol = SWA_W // SWA_KV_W, SWA_W // SWA_KV_W + 1
    return pl.pallas_call(
        functools.partial(_swa_body, bb=bb, qb=qb, cq=cq, win=win, pos0=pos0),
        grid=(b // bb, t // qb),
        in_specs=[pl.BlockSpec(memory_space=pltpu.SMEM),
                  pl.BlockSpec((bb, qb, SWA_W), lambda i, j: (i, j, 0)),
                  pl.BlockSpec((bb, SWA_WINDOW, SWA_KV_W), prev_k_map),
                  pl.BlockSpec((bb, SWA_WINDOW, SWA_KV_W), prev_v_map),
                  pl.BlockSpec((bb, qb, SWA_KV_W), lambda i, j: (i, j, kcol)),
                  pl.BlockSpec((bb, qb, SWA_KV_W), lambda i, j: (i, j, vcol)),
                  _const_spec(bias.shape)],
        out_specs=pl.BlockSpec((bb, qb, SWA_W), lambda i, j: (i, j, 0)),
        out_shape=jax.ShapeDtypeStruct((b, t, SWA_W), F32),
        compiler_params=_cparams("parallel", "parallel"),
        name="swa",
    )(sinks, p_swa, k_prev, v_prev, p_swa, p_swa, bias)


def _out_proj_body(x_ref, ys_ref, og_ref, os_ref, wo_ref, gf_ref, wr_ref, br_ref, triu_ref, before_ref, x1_ref, xs_ref,
                   aux_ref, cnt_ref):
    x1 = (x_ref[...] + _dot(ys_ref[...].astype(BF16), wo_ref[0:SSM_W, :])
          + _dot(og_ref[...].astype(BF16), wo_ref[SSM_W:SSM_W + GLA_W, :])
          + _dot(os_ref[...].astype(BF16), wo_ref[SSM_W + GLA_W:MIX_W, :]))
    x1_ref[...] = x1
    xn = x1 * lax.rsqrt(jnp.mean(x1 * x1, axis=-1, keepdims=True) + RMS_EPS) * gf_ref[...]
    xb = xn.astype(BF16)
    logit = _dot(xb, wr_ref[...]) + br_ref[...]
    lt = jnp.transpose(logit)[:2 * N_EXPERTS, :]
    row_i = lax.broadcasted_iota(jnp.int32, lt.shape, 0)
    row = row_i.astype(F32)
    neg = -jnp.inf
    big = float(LANES)
    g_row = row_i - ROUTER_GROUP_LANE0
    lg = jnp.where((g_row >= 0) & (g_row < N_EXPERT_GROUPS), lt, neg)
    gmax = jnp.max(lg, axis=0, keepdims=True)
    p_sel = 1.0 / jnp.sum(jnp.exp(lg - gmax), axis=0, keepdims=True)
    g_idx = jnp.min(jnp.where(lg == gmax, g_row.astype(F32), big), axis=0, keepdims=True)
    row_group = (row_i >> _log2(EXPERTS_PER_GROUP)).astype(F32)
    in_group = (row_i < N_EXPERTS) & (row_group == g_idx)
    le = jnp.where(in_group, lt, neg)
    m1 = jnp.max(le, axis=0, keepdims=True)
    i1 = jnp.min(jnp.where(le == m1, row, big), axis=0, keepdims=True)
    le2 = jnp.where(row == i1, neg, le)
    m2 = jnp.max(le2, axis=0, keepdims=True)
    i2 = jnp.min(jnp.where(le2 == m2, row, big), axis=0, keepdims=True)
    r = jnp.exp(m2 - m1)
    w1 = 1.0 / (1.0 + r)
    exp_row = lax.broadcasted_iota(jnp.int32, (N_EXPERTS, MOE_TB), 0).astype(F32)
    s1 = jnp.where(exp_row == i1, 1.0, 0.0)
    s2 = jnp.where(exp_row == i2, 1.0, 0.0)
    both = s1 + s2
    cnt = jnp.sum(both, axis=1, keepdims=True) + jnp.zeros_like(both)
    seg = jnp.floor((cnt + (MOE_PIECE - 1)) * (1.0 / MOE_PIECE)) * MOE_PIECE
    seg_k = jnp.concatenate([seg, jnp.zeros((LANES - N_EXPERTS, MOE_TB), F32)], axis=0).astype(BF16)
    lo = _dot(before_ref[...], seg_k)
    rank = _dot(both.astype(BF16), triu_ref[...])
    pos = lo + rank
    d1 = jnp.sum(s1 * pos, axis=0, keepdims=True)
    d2 = jnp.sum(s2 * pos, axis=0, keepdims=True)
    r_iota = lax.broadcasted_iota(jnp.int32, (MOE_RLOC, MOE_TB), 0).astype(F32)
    perm = jnp.where((r_iota == d1) | (r_iota == d2), 1.0, 0.0).astype(BF16)
    xs_ref[...] = _dot(perm, xb).astype(BF16)
    arow = lax.broadcasted_iota(jnp.int32, (LANES, MOE_TB), 0)
    at = jnp.where(arow == AUX_W1, p_sel * w1, jnp.where(arow == AUX_W2, p_sel * (r * w1),
                   jnp.where(arow == AUX_D1, d1, jnp.where(arow == AUX_D2, d2, 0.0))))
    aux_ref[...] = jnp.transpose(at)
    cnt_ref[0] = cnt[:, :LANES]


def _out_proj(x, ys, og, osw, wo, gf, wr, br, triu, before):
    n = x.shape[0]
    nblk = n // MOE_TB
    row = lambda w: pl.BlockSpec((MOE_TB, w), lambda i: (i, 0))
    return pl.pallas_call(
        _out_proj_body,
        grid=(nblk,),
        in_specs=[row(D_MODEL), row(SSM_W), row(GLA_W), row(SWA_W), _const_spec(wo.shape), _const_spec(gf.shape),
                  _const_spec(wr.shape), _const_spec(br.shape), _const_spec(triu.shape), _const_spec(before.shape)],
        out_specs=[row(D_MODEL), pl.BlockSpec((MOE_RLOC, D_MODEL), lambda i: (i, 0)), row(LANES),
                   pl.BlockSpec((1, N_EXPERTS, LANES), lambda i: (i, 0, 0))],
        out_shape=[jax.ShapeDtypeStruct((n, D_MODEL), F32), jax.ShapeDtypeStruct((nblk * MOE_RLOC, D_MODEL), BF16),
                   jax.ShapeDtypeStruct((n, LANES), F32), jax.ShapeDtypeStruct((nblk, N_EXPERTS, LANES), F32)],
        compiler_params=_cparams("parallel"),
        name="out_proj",
    )(x, ys, og, osw, wo, gf, wr, br, triu, before)


def _piece_copy(hbm_ref, piece, buf_ref, slot, p, sem, to_hbm):
    start = lambda i: i * MOE_PIECE if isinstance(i, int) else pl.multiple_of(i * MOE_PIECE, MOE_PIECE)
    rows = pl.ds(start(piece), MOE_PIECE)
    vm = buf_ref.at[slot, pl.ds(start(p), MOE_PIECE)]
    if to_hbm:
        return pltpu.make_async_copy(vm, hbm_ref.at[rows], sem)
    return pltpu.make_async_copy(hbm_ref.at[rows], vm, sem)


def _moe_expert_body(texp_ref, piece_ref, nv_ref, xs_hbm, wg_ref, wu_ref, wd_ref, ys_hbm,
                     xbuf, ybuf, wgb, wub, wdb, sem_in, sem_out, *, pt, n_steps):
    i = pl.program_id(0)
    slot = i % 2

    def for_pieces(tile, s, hbm_ref, buf_ref, sem, to_hbm, wait):
        def one(p):
            piece = 0 if wait else piece_ref[tile * pt + p]
            c = _piece_copy(hbm_ref, piece, buf_ref, s, p, sem.at[s], to_hbm)
            c.wait() if wait else c.start()

        nv = nv_ref[tile]

        @pl.when(nv == pt)
        def _():
            for p in range(pt):
                one(p)

        @pl.when(nv < pt)
        def _():
            def body(p, carry):
                one(p)
                return carry
            lax.fori_loop(0, nv, body, 0)

    gather = lambda tile, s, wait: for_pieces(tile, s, xs_hbm, xbuf, sem_in, False, wait)
    scatter = lambda tile, s, wait: for_pieces(tile, s, ys_hbm, ybuf, sem_out, True, wait)

    @pl.when(i == 0)
    def _():
        xbuf[...] = jnp.zeros_like(xbuf)
        gather(0, 0, False)

    @pl.when(i + 1 < n_steps)
    def _():
        gather(i + 1, 1 - slot, False)

    gather(i, slot, True)

    @pl.when((i == 0) | (texp_ref[i] != texp_ref[jnp.maximum(i - 1, 0)]))
    def _():
        wgb[...] = wg_ref[0].astype(BF16)
        wub[...] = wu_ref[0].astype(BF16)
        wdb[...] = wd_ref[0].astype(BF16)

    @pl.when(nv_ref[i] > 0)
    def _():
        xb = xbuf[slot]
        a = _dot(xb, wgb[...])
        h = (a * _sigmoid(a)) * _dot(xb, wub[...])
        ybuf[slot] = _dot(h.astype(BF16), wdb[...]).astype(BF16)

    scatter(i, slot, False)

    @pl.when(i >= 1)
    def _():
        scatter(i - 1, 1 - slot, True)

    @pl.when(i == n_steps - 1)
    def _():
        scatter(i, slot, True)


def _moe_experts(tile_expert, piece, nvalid, xs, wg, wu, wd, layer, pt):
    n_steps = tile_expert.shape[0]
    tm = pt * MOE_PIECE
    wspec = lambda shape: pl.BlockSpec(shape, lambda i, te, pc, nv: (te[i] + layer * N_EXPERTS, 0, 0))
    grid_spec = pltpu.PrefetchScalarGridSpec(
        num_scalar_prefetch=3,
        grid=(n_steps,),
        in_specs=[pl.BlockSpec(memory_space=pl.ANY), wspec((1, D_MODEL, EXPERT_FF)), wspec((1, D_MODEL, EXPERT_FF)),
                  wspec((1, EXPERT_FF, D_MODEL))],
        out_specs=pl.BlockSpec(memory_space=pl.ANY),
        scratch_shapes=[pltpu.VMEM((2, tm, D_MODEL), BF16), pltpu.VMEM((2, tm, D_MODEL), BF16),
                        pltpu.VMEM((D_MODEL, EXPERT_FF), BF16), pltpu.VMEM((D_MODEL, EXPERT_FF), BF16),
                        pltpu.VMEM((EXPERT_FF, D_MODEL), BF16),
                        pltpu.SemaphoreType.DMA((2,)), pltpu.SemaphoreType.DMA((2,))],
    )
    return pl.pallas_call(
        functools.partial(_moe_expert_body, pt=pt, n_steps=n_steps),
        grid_spec=grid_spec,
        out_shape=jax.ShapeDtypeStruct(xs.shape, BF16),
        input_output_aliases={3: 0},
        compiler_params=_cparams("arbitrary"),
        name="moe_experts",
    )(tile_expert, piece, nvalid, xs, wg, wu, wd)


def _combine_math(ys_ref, aux_ref, x1_ref):
    col = lax.broadcasted_iota(jnp.int32, (MOE_TB, MOE_RLOC), 1).astype(F32)
    mix = jnp.where(col == aux_ref[:, AUX_D1:AUX_D1 + 1], aux_ref[:, AUX_W1:AUX_W1 + 1],
                    jnp.where(col == aux_ref[:, AUX_D2:AUX_D2 + 1], aux_ref[:, AUX_W2:AUX_W2 + 1], 0.0))
    return x1_ref[...] + _dot(mix.astype(BF16), ys_ref[...])


def _moe_combine_body(ys_ref, aux_ref, x1_ref, o_ref):
    o_ref[...] = _combine_math(ys_ref, aux_ref, x1_ref)


def _combine_in_proj_body(ys_ref, aux_ref, x1_ref, g_ref, w_ref, mavg_ref, qkg_ref, o_ref, gla_ref, swa_ref, ssm_ref):
    x = _combine_math(ys_ref, aux_ref, x1_ref)
    o_ref[...] = x
    _in_proj_math(x, g_ref, w_ref, mavg_ref, qkg_ref, gla_ref, swa_ref, ssm_ref)


def _combine_in_proj(ys, aux, x1, g, w, mavg, qkg):
    n = x1.shape[0]
    row = lambda wdt: pl.BlockSpec((MOE_TB, wdt), lambda i: (i, 0))
    return pl.pallas_call(
        _combine_in_proj_body,
        grid=(n // MOE_TB,),
        in_specs=[pl.BlockSpec((MOE_RLOC, D_MODEL), lambda i: (i, 0)), row(LANES), row(D_MODEL), _const_spec(g.shape),
                  _const_spec(w.shape), _const_spec(mavg.shape), _const_spec(qkg.shape)],
        out_specs=[row(D_MODEL), row(GLA_PW), row(SWA_PW), row(SSM_W)],
        out_shape=[jax.ShapeDtypeStruct((n, D_MODEL), F32), jax.ShapeDtypeStruct((n, GLA_PW), F32),
                   jax.ShapeDtypeStruct((n, SWA_PW), F32), jax.ShapeDtypeStruct((n, SSM_W), F32)],
        compiler_params=_cparams("parallel"),
        name="combine_in_proj",
    )(ys, aux, x1, g, w, mavg, qkg)


def _moe_combine(ys, aux, x1):
    n = x1.shape[0]
    return pl.pallas_call(
        _moe_combine_body,
        grid=(n // MOE_TB,),
        in_specs=[pl.BlockSpec((MOE_RLOC, D_MODEL), lambda i: (i, 0)), pl.BlockSpec((MOE_TB, LANES), lambda i: (i, 0)),
                  pl.BlockSpec((MOE_TB, D_MODEL), lambda i: (i, 0))],
        out_specs=pl.BlockSpec((MOE_TB, D_MODEL), lambda i: (i, 0)),
        out_shape=jax.ShapeDtypeStruct((n, D_MODEL), F32),
        compiler_params=_cparams("parallel"),
        name="moe_combine",
    )(ys, aux, x1)


def _route_tables(counts, pt):
    nblk = counts.shape[0]
    n = nblk * MOE_TB
    cnt = counts[:, :, 0].T.astype(jnp.int32)
    pc = (cnt + MOE_PIECE - 1) // MOE_PIECE
    lo_p = jnp.cumsum(pc, axis=0) - pc
    pe = pc.sum(1)
    tiles_e = (pe + pt - 1) // pt
    tile_start = jnp.cumsum(tiles_e) - tiles_e
    seg_start = (tile_start[:, None] * pt + jnp.cumsum(pc, axis=1) - pc).reshape(-1)
    pcs = pc.reshape(-1)
    seg_src = (jnp.arange(nblk, dtype=jnp.int32)[None, :] * MOE_PPB + lo_p).reshape(-1)
    n_steps = -(-(2 * n // MOE_PIECE + nblk * N_EXPERTS + N_EXPERTS * (pt - 1)) // pt)
    slot = jnp.arange(n_steps * pt, dtype=jnp.int32)[:, None]
    in_seg = (slot >= seg_start[None, :]) & (slot < (seg_start + pcs)[None, :])
    piece = jnp.sum(jnp.where(in_seg, seg_src[None, :] + slot - seg_start[None, :], 0), axis=1).astype(jnp.int32)
    tile = jnp.arange(n_steps, dtype=jnp.int32)[:, None]
    in_exp = (tile >= tile_start[None, :]) & (tile < (tile_start + tiles_e)[None, :])
    tile_expert = jnp.sum(jnp.where(in_exp, jnp.arange(N_EXPERTS, dtype=jnp.int32)[None, :], 0), axis=1)
    tile_expert = jnp.where(tile[:, 0] < tiles_e.sum(), tile_expert, N_EXPERTS - 1).astype(jnp.int32)
    nvalid = jnp.sum(jnp.where(in_exp, jnp.clip(pe[None, :] - (tile - tile_start[None, :]) * pt, 0, pt), 0),
                     axis=1).astype(jnp.int32)
    return tile_expert, piece, nvalid


def _moe(xs, counts, wg, wu, wd, layer, pt):
    tile_expert, piece, nvalid = _route_tables(counts, pt)
    return _moe_experts(tile_expert, piece, nvalid, xs, wg, wu, wd, layer, pt)


def _seg_mean_matrix(width, seg):
    i = jnp.arange(width)
    return jnp.where((i[:, None] // seg) == (i[None, :] // seg), 1.0 / seg, 0.0).astype(BF16)


def _chunk_tril(span, lc):
    i = jnp.arange(span)
    return ((i[:, None] >= i[None, :]) & ((i[:, None] // lc) == (i[None, :] // lc))).astype(BF16)


def _prep_layer(lw, ssm_shapes, gla_shapes):
    w_in = lw["w_in"].astype(F32)
    cols = {}
    off = 0
    for name, wdt in (("u", SSM_W), ("qg", GLA_K), ("kg", GLA_K), ("vg", GLA_W), ("z", GLA_GATE_RANK), ("og", GLA_W),
                      ("qs", SWA_W), ("ks", SWA_KV_W), ("vs", SWA_KV_W)):
        cols[name] = w_in[:, off:off + wdt]
        off += wdt
    zpad = lambda wdt: jnp.zeros((D_MODEL, wdt), F32)
    w_all = jnp.concatenate(
        [cols["qg"], zpad(GLA_KP - GLA_K), cols["kg"], zpad(GLA_KP - GLA_K), cols["vg"], cols["og"], cols["z"],
         zpad(LANES - GLA_GATE_RANK), cols["qs"], cols["ks"], cols["vs"], cols["u"]], axis=1).astype(BF16)
    out = {"norm_mix": lw["norm_mix"].astype(F32)[None, :], "w_all": w_all,
           "mavg_qk": _seg_mean_matrix(SWA_W + SWA_KV_W, SWA_HEAD_DIM),
           "qk_gain": jnp.concatenate([jnp.tile(lw["swa_q_norm"].astype(F32), SWA_HEADS),
                                       jnp.tile(lw["swa_k_norm"].astype(F32), SWA_KV_HEADS)])[None, :]}
    a_re = lw["ssm_a_re"].astype(F32)
    a_im = lw["ssm_a_im"].astype(F32)
    dt = jnp.exp(lw["ssm_log_dt"].astype(F32))[:, None]
    mag = jnp.exp(a_re * dt)
    ab_re = mag * jnp.cos(a_im * dt)
    ab_im = mag * jnp.sin(a_im * dt)
    den = a_re * a_re + a_im * a_im
    nr = ab_re - 1.0
    f_re = (nr * a_re + ab_im * a_im) / den
    f_im = (ab_im * a_re - nr * a_im) / den
    b_re = lw["ssm_b_re"].astype(F32)
    b_im = lw["ssm_b_im"].astype(F32)
    bb_re = f_re[..., None] * b_re - f_im[..., None] * b_im
    bb_im = f_re[..., None] * b_im + f_im[..., None] * b_re
    eye_g = jnp.eye(SSM_GROUPS, dtype=F32)
    blockdiag_in = lambda bb: jnp.einsum("gpc,gh->gchp", bb, eye_g).reshape(SSM_W, SSM_FLAT)
    blockdiag_out = lambda cc: jnp.einsum("gcp,gh->gphc", cc, eye_g).reshape(SSM_FLAT, SSM_W)
    bd = jnp.concatenate([blockdiag_in(bb_re), blockdiag_in(bb_im)], axis=1).astype(BF16)
    cd = jnp.concatenate([blockdiag_out(lw["ssm_c_re"].astype(F32)),
                          -blockdiag_out(lw["ssm_c_im"].astype(F32))], axis=0).astype(BF16)
    ssm = {"ab": jnp.concatenate([ab_re.reshape(1, SSM_FLAT), ab_im.reshape(1, SSM_FLAT),
                                  jnp.zeros((6, SSM_FLAT), F32)], axis=0),
           "bd": bd, "cd": cd, "d": lw["ssm_d"].astype(F32)[None, :], "wglu": lw["ssm_w_glu"].astype(BF16)}
    def powers(lc, sign):
        tt = jnp.arange(lc, dtype=F32)[:, None, None]
        m = jnp.exp(sign * tt * (a_re * dt)[None])
        ang = sign * tt * (a_im * dt)[None]
        return jnp.stack([(m * jnp.cos(ang)).reshape(lc, SSM_FLAT), (m * jnp.sin(ang)).reshape(lc, SSM_FLAT)])

    out["ssm"] = {(span, lc): dict(ssm, pwp=powers(lc, 1.0), pwn=powers(lc, -1.0), tril=_chunk_tril(span, lc))
                  for span, lc in ssm_shapes}
    wg = jnp.zeros((LANES, GLA_KP), F32).at[:GLA_GATE_RANK, :GLA_K].set(lw["gla_w_gate"].astype(F32)).astype(BF16)
    bg = jnp.zeros((1, GLA_KP), F32).at[0, :GLA_K].set(lw["gla_b_gate"].astype(F32))
    kk = jnp.arange(GLA_KP)
    spread = ((kk[None, :] % GLA_DK) == jnp.arange(GLA_DK)[:, None]) & (kk[None, :] < GLA_K)
    gla = {"wg": wg, "bg": bg, "mavg": _seg_mean_matrix(GLA_W, GLA_DV),
           "gn": jnp.tile(lw["gla_norm"].astype(F32), GLA_HEADS)[None, :], "spread": spread.astype(BF16)}
    out["gla"] = {(span, lc): dict(gla, tril=_chunk_tril(span, lc)) for span, lc in gla_shapes}
    out["sinks"] = lw["swa_sinks"].astype(F32)
    out["w_out"] = lw["w_out"].astype(BF16)
    out["norm_ffn"] = lw["norm_ffn"].astype(F32)[None, :]
    wr = jnp.zeros((D_MODEL, LANES), F32)
    wr = wr.at[:, :N_EXPERTS].set(lw["moe_w_expert"].astype(F32))
    wr = wr.at[:, ROUTER_GROUP_LANE0:ROUTER_GROUP_LANE0 + N_EXPERT_GROUPS].set(lw["moe_w_group"].astype(F32))
    out["wr"] = wr.astype(BF16)
    br = jnp.zeros((1, LANES), F32)
    br = br.at[0, :N_EXPERTS].set(lw["moe_b_expert"].astype(F32))
    br = br.at[0, ROUTER_GROUP_LANE0:ROUTER_GROUP_LANE0 + N_EXPERT_GROUPS].set(lw["moe_b_group"].astype(F32))
    out["br"] = br
    ti = jnp.arange(MOE_TB)
    out["moe_triu"] = (ti[:, None] < ti[None, :]).astype(BF16)
    out["moe_before"] = (jnp.arange(LANES)[None, :] < jnp.arange(N_EXPERTS)[:, None]).astype(BF16)
    return out


def _gla_state_in(h0):
    return jnp.transpose(h0.astype(F32), (0, 1, 3, 2)).reshape(h0.shape[0], GLA_W, GLA_DK)


def _gla_state_out(st):
    return jnp.transpose(st.reshape(st.shape[0], GLA_HEADS, GLA_DV, GLA_DK), (0, 1, 3, 2))


def _layer(x, pending, pw, bias, ssm_h0r, ssm_h0i, gla_h0, past_k, past_v, cfg, b, t):
    n = b * t
    in_w = (pw["norm_mix"], pw["w_all"], pw["mavg_qk"], pw["qk_gain"])
    if pending is None:
        p_gla, p_swa, p_ssm = _in_proj(x, *in_w, cfg["tm"])
    else:
        x, p_gla, p_swa, p_ssm = _combine_in_proj(*pending, *in_w)
    p_gla = p_gla.reshape(b, t, GLA_PW)
    p_swa = p_swa.reshape(b, t, SWA_PW)
    stack = cfg["stack"]
    indep = stack > 1
    grp = b // stack
    y_ssm, h_re, h_im = _ssm(p_ssm.reshape(grp, stack * t, SSM_W), ssm_h0r.reshape(grp, stack, SSM_FLAT).astype(F32),
                             ssm_h0i.reshape(grp, stack, SSM_FLAT).astype(F32), pw["ssm"][cfg["ssm"]], cfg["lblk"],
                             *cfg["ssm"], indep)
    o_g, s_fin = _gla(p_gla.reshape(grp, stack * t, GLA_PW), _gla_state_in(gla_h0).reshape(grp, stack, GLA_W, GLA_DK),
                      pw["gla"][cfg["gla"]], cfg["lblk"], *cfg["gla"], indep)
    s_fin = s_fin.reshape(b, GLA_W, GLA_DK)
    kcol, vcol = SWA_W // SWA_KV_W, SWA_W // SWA_KV_W + 1
    if past_k is None:
        per_blk = cfg["qb"] // SWA_WINDOW
        prev_k_map = lambda i, j: (i, jnp.maximum(j * per_blk - 1, 0), kcol)
        prev_v_map = lambda i, j: (i, jnp.maximum(j * per_blk - 1, 0), vcol)
        k_prev, v_prev = p_swa, p_swa
        keep = min(SWA_WINDOW, t)
        new_k = p_swa[:, t - keep:, SWA_W:SWA_W + SWA_KV_W]
        new_v = p_swa[:, t - keep:, SWA_W + SWA_KV_W:]
    else:
        prev_k_map = prev_v_map = lambda i, j: (i, 0, 0)
        k_prev = past_k.reshape(b, SWA_WINDOW, SWA_KV_W).astype(F32)
        v_prev = past_v.reshape(b, SWA_WINDOW, SWA_KV_W).astype(F32)
        new_k = p_swa[:, :, SWA_W:SWA_W + SWA_KV_W]
        new_v = p_swa[:, :, SWA_W + SWA_KV_W:]
    o_s = _swa(pw["sinks"], p_swa, k_prev, v_prev, prev_k_map, prev_v_map, bias, stack, cfg["qb"], cfg["cq"],
               cfg["win"], cfg["pos0"])
    x1, xs, aux, counts = _out_proj(x, y_ssm.reshape(n, SSM_W), o_g.reshape(n, GLA_W),
                                    o_s.reshape(n, SWA_W), pw["w_out"], pw["norm_ffn"], pw["wr"], pw["br"],
                                    pw["moe_triu"], pw["moe_before"])
    ys = _moe(xs, counts, pw["moe_wg"], pw["moe_wu"], pw["moe_wd"], pw["layer"], cfg["pt"])
    kv_shape = (b, new_k.shape[1], SWA_KV_HEADS, SWA_HEAD_DIM)
    return ((ys, aux, x1), new_k.reshape(kv_shape), new_v.reshape(kv_shape), _gla_state_out(s_fin),
            h_re.reshape(b, SSM_GROUPS, SSM_STATE), h_im.reshape(b, SSM_GROUPS, SSM_STATE))


def _group_cfg(t, past_len, n_past):
    if n_past is None:
        assert t % (16 * CHUNK) == 0
        cfg = dict(ssm=(4 * CHUNK, CHUNK), gla=(4 * CHUNK, 2 * CHUNK), lblk=16 * CHUNK, stack=1, qb=8 * CHUNK, cq=CHUNK,
                   win=SWA_WINDOW + CHUNK, pos0=-SWA_WINDOW, tm=512, pt=32)
        q_pos = jnp.arange(CHUNK, dtype=jnp.int32) + SWA_WINDOW
        k_pos = jnp.arange(SWA_WINDOW + CHUNK, dtype=jnp.int32)
    else:
        first_key, last_q = past_len - n_past, past_len + t - 1
        assert n_past == SWA_WINDOW and t <= CHUNK and t % 8 == 0
        assert past_len // CHUNK == last_q // CHUNK and first_key // CHUNK >= past_len // CHUNK - SWA_WINDOW // CHUNK
        stack = DECODE_STACK
        cfg = dict(ssm=(stack * t, t), gla=(stack * t, t), lblk=stack * t, stack=stack, qb=t, cq=t, win=n_past + t,
                   pos0=first_key, tm=512, pt=16)
        q_pos = past_len + jnp.arange(t, dtype=jnp.int32)
        k_pos = first_key + jnp.arange(n_past + t, dtype=jnp.int32)
    return cfg, q_pos, k_pos


def _stacked_bias(t5_table, q_pos, k_pos):
    bias = _t5_bias(t5_table, q_pos, k_pos)
    nq, nk = bias.shape[1:]
    return bias.reshape(SWA_KV_HEADS, SWA_REP * nq, nk)


PAST_LEN = 1024


def kernel(x_prompt, x_sample, cache_swa_k, cache_swa_v, state_gla, state_ssm_re, state_ssm_im, norm_mix, w_in, ssm_a_re, ssm_a_im, ssm_log_dt, ssm_b_re, ssm_b_im, ssm_c_re, ssm_c_im, ssm_d, ssm_w_glu, gla_w_gate, gla_b_gate, gla_norm, swa_q_norm, swa_k_norm, swa_sinks, t5_table, w_out, norm_ffn, moe_w_group, moe_b_group, moe_w_expert, moe_b_expert, moe_w_gate, moe_w_up, moe_w_down):
    depth = w_in.shape[0]
    bp, tp, _ = x_prompt.shape
    bs, ts, _ = x_sample.shape
    cfg_p, qpos_p, kpos_p = _group_cfg(tp, 0, None)
    cfg_s, qpos_s, kpos_s = _group_cfg(ts, PAST_LEN, cache_swa_k.shape[2])
    bias_p = _stacked_bias(t5_table, qpos_p, kpos_p)
    bias_s = _stacked_bias(t5_table, qpos_s, kpos_s)
    xp = x_prompt.astype(F32).reshape(bp * tp, D_MODEL)
    xs = x_sample.astype(F32).reshape(bs * ts, D_MODEL)
    pend_p = pend_s = None
    moe_wg = moe_w_gate.astype(F32).reshape(depth * N_EXPERTS, D_MODEL, EXPERT_FF)
    moe_wu = moe_w_up.astype(F32).reshape(depth * N_EXPERTS, D_MODEL, EXPERT_FF)
    moe_wd = moe_w_down.astype(F32).reshape(depth * N_EXPERTS, EXPERT_FF, D_MODEL)
    outs = [[] for _ in range(10)]
    for l in range(depth):
        lw = {
            "norm_mix": norm_mix[l], "w_in": w_in[l], "ssm_a_re": ssm_a_re[l], "ssm_a_im": ssm_a_im[l],
            "ssm_log_dt": ssm_log_dt[l], "ssm_b_re": ssm_b_re[l], "ssm_b_im": ssm_b_im[l], "ssm_c_re": ssm_c_re[l],
            "ssm_c_im": ssm_c_im[l], "ssm_d": ssm_d[l], "ssm_w_glu": ssm_w_glu[l], "gla_w_gate": gla_w_gate[l],
            "gla_b_gate": gla_b_gate[l], "gla_norm": gla_norm[l], "swa_q_norm": swa_q_norm[l],
            "swa_k_norm": swa_k_norm[l], "swa_sinks": swa_sinks[l], "w_out": w_out[l], "norm_ffn": norm_ffn[l],
            "moe_w_group": moe_w_group[l], "moe_b_group": moe_b_group[l], "moe_w_expert": moe_w_expert[l],
            "moe_b_expert": moe_b_expert[l],
        }
        pw = _prep_layer(lw, {cfg_p["ssm"], cfg_s["ssm"]}, {cfg_p["gla"], cfg_s["gla"]})
        pw.update(layer=l, moe_wg=moe_wg, moe_wu=moe_wu, moe_wd=moe_wd)
        zs = jnp.zeros((bp, SSM_GROUPS, SSM_STATE), F32)
        zg = jnp.zeros((bp, GLA_HEADS, GLA_DK, GLA_DV), F32)
        pend_p, nk, nv, ng, nr, ni = _layer(xp, pend_p, pw, bias_p, zs, zs, zg, None, None, cfg_p, bp, tp)
        for slot, val in zip((0, 1, 4, 6, 7), (nk, nv, ng, nr, ni)):
            outs[slot].append(val)
        pend_s, nk, nv, ng, nr, ni = _layer(xs, pend_s, pw, bias_s, state_ssm_re[l], state_ssm_im[l], state_gla[l],
                                            cache_swa_k[l], cache_swa_v[l], cfg_s, bs, ts)
        for slot, val in zip((2, 3, 5, 8, 9), (nk, nv, ng, nr, ni)):
            outs[slot].append(val)
    hp = _moe_combine(*pend_p).reshape(bp, tp, D_MODEL)
    hs = _moe_combine(*pend_s).reshape(bs, ts, D_MODEL)
    return (hp, hs) + tuple(jnp.stack(o) for o in outs)
```

```python
import functools
import math

import jax
import jax.numpy as jnp
from jax import lax
from jax.experimental import pallas as pl
from jax.experimental.pallas import tpu as pltpu

F32 = jnp.float32
BF16 = jnp.bfloat16

D_MODEL = 1024
CHUNK = 64
RMS_EPS = 1e-6
SSM_GROUPS = 16
SSM_GC = 16
SSM_STATE = 64
SSM_W = SSM_GROUPS * SSM_GC
SSM_FLAT = SSM_GROUPS * SSM_STATE
GLA_HEADS = 6
GLA_DK = 32
GLA_DV = 64
GLA_GATE_RANK = 16
GLA_GATE_NORM = 16.0
GLA_K = GLA_HEADS * GLA_DK
GLA_KP = 256
GLA_W = GLA_HEADS * GLA_DV
SWA_HEADS = 6
SWA_KV_HEADS = 2
SWA_REP = SWA_HEADS // SWA_KV_HEADS
SWA_HEAD_DIM = 64
SWA_WINDOW = 128
SWA_W = SWA_HEADS * SWA_HEAD_DIM
SWA_KV_W = SWA_KV_HEADS * SWA_HEAD_DIM
MIX_W = SSM_W + GLA_W + SWA_W
N_BUCKETS = 32
T5_MAX_DIST = 128
N_EXPERT_GROUPS = 4
EXPERTS_PER_GROUP = 4
N_EXPERTS = 16
EXPERT_FF = 512
LANES = 128
ROUTER_GROUP_LANE0 = N_EXPERTS
AUX_W1, AUX_W2, AUX_D1, AUX_D2 = 0, 1, 2, 3
AUX_ROWS = 8
DECODE_STACK = 8
MOE_TB = 512
MOE_PIECE = 16
MOE_RLOC = 2 * MOE_TB + N_EXPERTS * MOE_PIECE
MOE_PPB = MOE_RLOC // MOE_PIECE

GLA_Q0, GLA_K0, GLA_V0, GLA_OG0, GLA_Z0, GLA_PW = 0, 256, 512, 896, 1280, 1408
SWA_PW = SWA_W + 2 * SWA_KV_W
P_GLA0, P_SWA0, P_SSM0, P_TOTAL = 0, GLA_PW, GLA_PW + SWA_PW, GLA_PW + SWA_PW + SSM_W

VMEM_LIMIT = 48 * 1024 * 1024


def _cparams(*sem):
    return pltpu.CompilerParams(dimension_semantics=sem, vmem_limit_bytes=VMEM_LIMIT)


def _dot(a, b):
    return jnp.dot(a, b, preferred_element_type=F32)


def _dot_nt(a, b):
    return lax.dot_general(a, b, (((1,), (1,)), ((), ())), preferred_element_type=F32)


def _dot_tn(a, b):
    return lax.dot_general(a, b, (((0,), (0,)), ((), ())), preferred_element_type=F32)


def _hi_lo(x):
    hi = x.astype(BF16)
    return hi, (x - hi.astype(F32)).astype(BF16)


def _dot_f32_rhs(a_bf16, x):
    hi, lo = _hi_lo(x)
    return _dot(a_bf16, hi) + _dot(a_bf16, lo)


def _dot_f32_lhs(x, b_bf16):
    hi, lo = _hi_lo(x)
    return _dot(hi, b_bf16) + _dot(lo, b_bf16)


def _log2(n):
    assert n & (n - 1) == 0
    return n.bit_length() - 1


def _sigmoid(x):
    return 1.0 / (1.0 + jnp.exp(-x))


def _const_spec(shape):
    nd = len(shape)
    return pl.BlockSpec(shape, lambda *_: (0,) * nd)


def _in_proj_body(x_ref, g_ref, w_ref, mavg_ref, qkg_ref, gla_ref, swa_ref, ssm_ref):
    _in_proj_math(x_ref[...], g_ref, w_ref, mavg_ref, qkg_ref, gla_ref, swa_ref, ssm_ref)


def _in_proj_math(x, g_ref, w_ref, mavg_ref, qkg_ref, gla_ref, swa_ref, ssm_ref):
    xn = x * lax.rsqrt(jnp.mean(x * x, axis=-1, keepdims=True) + RMS_EPS) * g_ref[...]
    xb = xn.astype(BF16)
    gla_ref[...] = _dot(xb, w_ref[:, P_GLA0:P_SWA0])
    ssm_ref[...] = _dot(xb, w_ref[:, P_SSM0:P_TOTAL])
    s = _dot(xb, w_ref[:, P_SWA0:P_SSM0])
    qk = s[:, :SWA_W + SWA_KV_W]
    ms = _dot((qk * qk).astype(BF16), mavg_ref[...])
    swa_ref[:, :SWA_W + SWA_KV_W] = qk * lax.rsqrt(ms + RMS_EPS) * qkg_ref[...]
    swa_ref[:, SWA_W + SWA_KV_W:] = s[:, SWA_W + SWA_KV_W:]


def _in_proj(x, g, w, mavg, qkg, tm):
    n = x.shape[0]
    return pl.pallas_call(
        _in_proj_body,
        grid=(n // tm,),
        in_specs=[pl.BlockSpec((tm, D_MODEL), lambda i: (i, 0)), _const_spec(g.shape), _const_spec(w.shape),
                  _const_spec(mavg.shape), _const_spec(qkg.shape)],
        out_specs=[pl.BlockSpec((tm, GLA_PW), lambda i: (i, 0)), pl.BlockSpec((tm, SWA_PW), lambda i: (i, 0)),
                   pl.BlockSpec((tm, SSM_W), lambda i: (i, 0))],
        out_shape=[jax.ShapeDtypeStruct((n, GLA_PW), F32), jax.ShapeDtypeStruct((n, SWA_PW), F32),
                   jax.ShapeDtypeStruct((n, SSM_W), F32)],
        compiler_params=_cparams("parallel"),
        name="in_proj",
    )(x, g, w, mavg, qkg)


def _ssm_body(u_ref, h0r_ref, h0i_ref, ab_ref, bd_ref, cd_ref, pwp_ref, pwn_ref, tril_ref, d_ref, wglu_ref,
              y_ref, hr_ref, hi_ref, carry_ref, *, n_part, n_sub, lc, indep):
    if not indep:
        @pl.when(pl.program_id(1) == 0)
        def _():
            carry_ref[0:1, :] = h0r_ref[0]
            carry_ref[1:2, :] = h0i_ref[0]

        hr = carry_ref[0:1, :]
        hi = carry_ref[1:2, :]
    ab_re = ab_ref[0:1, :]
    ab_im = ab_ref[1:2, :]
    nr, ni = pwn_ref[0], pwn_ref[1]
    pr, pi = pwp_ref[0], pwp_ref[1]
    span = n_sub * lc
    for part in range(n_part):
        rows = slice(part * span, (part + 1) * span)
        u = u_ref[0, rows, :]
        bu = _dot(u.astype(BF16), bd_ref[...])
        sr, si = [], []
        for c in range(n_sub):
            bur = bu[c * lc:(c + 1) * lc, :SSM_FLAT]
            bui = bu[c * lc:(c + 1) * lc, SSM_FLAT:]
            sr.append(nr * bur - ni * bui)
            si.append(nr * bui + ni * bur)
        scaled = jnp.concatenate([jnp.concatenate(sr, axis=0), jnp.concatenate(si, axis=0)], axis=1)
        cs = _dot(tril_ref[...], scaled.astype(BF16))
        h_r, h_i = [], []
        for c in range(n_sub):
            if indep:
                hr, hi = h0r_ref[0, c:c + 1, :], h0i_ref[0, c:c + 1, :]
            cr = cs[c * lc:(c + 1) * lc, :SSM_FLAT] + (ab_re * hr - ab_im * hi)
            ci = cs[c * lc:(c + 1) * lc, SSM_FLAT:] + (ab_re * hi + ab_im * hr)
            h_r.append(pr * cr - pi * ci)
            h_i.append(pr * ci + pi * cr)
            hr = h_r[-1][lc - 1:lc, :]
            hi = h_i[-1][lc - 1:lc, :]
            if indep:
                hr_ref[0, c:c + 1, :] = hr
                hi_ref[0, c:c + 1, :] = hi
        hcat = jnp.concatenate([jnp.concatenate(h_r, axis=0), jnp.concatenate(h_i, axis=0)], axis=1).astype(BF16)
        y = _dot(hcat, cd_ref[...]) + d_ref[...] * u
        g = 0.5 * y * (1.0 + jnp.tanh(math.sqrt(2.0 / math.pi) * (y + 0.044715 * (y * y * y))))
        y_ref[0, rows, :] = g * _sigmoid(_dot(g.astype(BF16), wglu_ref[...]))
    if not indep:
        carry_ref[0:1, :] = hr
        carry_ref[1:2, :] = hi
        hr_ref[0] = hr
        hi_ref[0] = hi


def _ssm(u, h0r, h0i, sw, lblk, span, lc, indep):
    b, t, _ = u.shape
    n_state = span // lc if indep else 1
    assert not indep or (t == lblk == span)
    consts = [sw["ab"], sw["bd"], sw["cd"], sw["pwp"], sw["pwn"], sw["tril"], sw["d"], sw["wglu"]]
    state_spec = pl.BlockSpec((1, n_state, SSM_FLAT), lambda i, j: (i, 0, 0))
    return pl.pallas_call(
        functools.partial(_ssm_body, n_part=lblk // span, n_sub=span // lc, lc=lc, indep=indep),
        grid=(b, t // lblk),
        in_specs=[pl.BlockSpec((1, lblk, SSM_W), lambda i, j: (i, j, 0)), state_spec, state_spec]
        + [_const_spec(c.shape) for c in consts],
        out_specs=[pl.BlockSpec((1, lblk, SSM_W), lambda i, j: (i, j, 0)), state_spec, state_spec],
        out_shape=[jax.ShapeDtypeStruct((b, t, SSM_W), F32), jax.ShapeDtypeStruct((b, n_state, SSM_FLAT), F32),
                   jax.ShapeDtypeStruct((b, n_state, SSM_FLAT), F32)],
        scratch_shapes=[pltpu.VMEM((8, SSM_FLAT), F32)],
        compiler_params=_cparams("parallel", "arbitrary"),
        name="ssm",
    )(u, h0r, h0i, *consts)


def _gla_body(p_ref, s0_ref, wg_ref, bg_ref, tril_ref, mavg_ref, gn_ref, spread_ref, o_ref, sfin_ref, s_ref, *,
              n_part, n_sub, lc, indep):
    row_v = lax.broadcasted_iota(jnp.int32, (GLA_W, GLA_KP), 0)
    col_k = lax.broadcasted_iota(jnp.int32, (GLA_W, GLA_KP), 1)
    same_head = ((row_v >> _log2(GLA_DV)) == (col_k >> _log2(GLA_DK))).astype(F32)

    def spread(compact):
        return _dot_f32_lhs(compact, spread_ref[...]) * same_head

    def gather(full):
        hi, lo = _hi_lo(full)
        return _dot_nt(hi, spread_ref[...]) + _dot_nt(lo, spread_ref[...])

    if not indep:
        @pl.when(pl.program_id(1) == 0)
        def _():
            s_ref[...] = spread(s0_ref[0, 0])

    lane_k = lax.broadcasted_iota(jnp.int32, (1, GLA_KP), 1)
    lane_v = lax.broadcasted_iota(jnp.int32, (1, GLA_W), 1)
    head_k = [((lane_k >= h * GLA_DK) & (lane_k < (h + 1) * GLA_DK)).astype(F32) for h in range(GLA_HEADS)]
    head_v = [((lane_v >= h * GLA_DV) & (lane_v < (h + 1) * GLA_DV)).astype(F32) for h in range(GLA_HEADS)]
    row_t = lax.broadcasted_iota(jnp.int32, (GLA_HEADS * lc, lc), 0)
    col_s = lax.broadcasted_iota(jnp.int32, (GLA_HEADS * lc, lc), 1)
    causal = (row_t & (lc - 1)) >= col_s
    mid = lc // 2 - 1
    span = n_sub * lc
    st = None if indep else s_ref[...]
    for part in range(n_part):
        rows = slice(part * span, (part + 1) * span)
        z = p_ref[0, rows, GLA_Z0:GLA_Z0 + LANES]
        gin = _dot(z.astype(BF16), wg_ref[...]) + bg_ref[...]
        glog = (jnp.minimum(gin, 0.0) - jnp.log(1.0 + jnp.exp(-jnp.abs(gin)))) / GLA_GATE_NORM
        g_all = _dot_f32_rhs(tril_ref[...], glog)
        outs = []
        for c in range(n_sub):
            crow = slice(part * span + c * lc, part * span + (c + 1) * lc)
            q = p_ref[0, crow, GLA_Q0:GLA_Q0 + GLA_KP] * (GLA_DK ** -0.5)
            k = p_ref[0, crow, GLA_K0:GLA_K0 + GLA_KP]
            vb = p_ref[0, crow, GLA_V0:GLA_V0 + GLA_W].astype(BF16)
            if indep:
                st = spread(s0_ref[0, c])
            gc = g_all[c * lc:(c + 1) * lc, :]
            gl = gc[lc - 1:lc, :]
            gm = gc[mid:mid + 1, :]
            qc = (q * jnp.exp(gc - gm))
            ke = (k * jnp.exp(gm - gc)).astype(BF16)
            kd = (k * jnp.exp(gl - gc)).astype(BF16)
            qs = jnp.concatenate([qc * m for m in head_k], axis=0).astype(BF16)
            attn = jnp.where(causal, _dot_nt(qs, ke), 0.0)
            o2 = _dot(attn.astype(BF16), vb)
            o = _dot_nt((q * jnp.exp(gc)).astype(BF16), st.astype(BF16))
            for h in range(GLA_HEADS):
                o = o + head_v[h] * o2[h * lc:(h + 1) * lc, :]
            st = st * jnp.exp(gl) + _dot_tn(vb, kd) * same_head
            if indep:
                sfin_ref[0, c] = gather(st)
            outs.append(o)
        o = jnp.concatenate(outs, axis=0) if n_sub > 1 else outs[0]
        og = p_ref[0, rows, GLA_OG0:GLA_OG0 + GLA_W]
        ms = _dot((o * o).astype(BF16), mavg_ref[...])
        on = o * lax.rsqrt(ms + RMS_EPS) * gn_ref[...]
        o_ref[0, rows, :] = on * (og * _sigmoid(og))
    if not indep:
        s_ref[...] = st

        @pl.when(pl.program_id(1) == pl.num_programs(1) - 1)
        def _():
            sfin_ref[0, 0] = gather(st)


def _gla(p, s0, gw, lblk, span, lc, indep):
    b, t, _ = p.shape
    n_state = span // lc if indep else 1
    assert not indep or (t == lblk == span)
    consts = [gw["wg"], gw["bg"], gw["tril"], gw["mavg"], gw["gn"], gw["spread"]]
    st_spec = pl.BlockSpec((1, n_state, GLA_W, GLA_DK), lambda i, j: (i, 0, 0, 0))
    return pl.pallas_call(
        functools.partial(_gla_body, n_part=lblk // span, n_sub=span // lc, lc=lc, indep=indep),
        grid=(b, t // lblk),
        in_specs=[pl.BlockSpec((1, lblk, GLA_PW), lambda i, j: (i, j, 0)), st_spec]
        + [_const_spec(c.shape) for c in consts],
        out_specs=[pl.BlockSpec((1, lblk, GLA_W), lambda i, j: (i, j, 0)), st_spec],
        out_shape=[jax.ShapeDtypeStruct((b, t, GLA_W), F32),
                   jax.ShapeDtypeStruct((b, n_state, GLA_W, GLA_DK), F32)],
        scratch_shapes=[pltpu.VMEM((GLA_W, GLA_KP), F32)],
        compiler_params=_cparams("parallel", "arbitrary"),
        name="gla",
    )(p, s0, *consts)


def _t5_bias(t5_table, q_pos, k_pos):
    nq, nk = q_pos.shape[0], k_pos.shape[0]
    rel = (k_pos[0] - q_pos[0]) + jnp.arange(-(nq - 1), nk, dtype=jnp.int32)
    half = N_BUCKETS // 2
    max_exact = half // 2
    n = jnp.abs(rel)
    far = max_exact + (jnp.log(jnp.maximum(n, 1).astype(jnp.float32) / max_exact)
                       / math.log(T5_MAX_DIST / max_exact) * (half - max_exact)).astype(jnp.int32)
    bucket = jnp.where(rel > 0, half, 0) + jnp.where(n < max_exact, n, jnp.minimum(far, half - 1))
    by_dist = t5_table.astype(F32)[bucket].T
    return jnp.stack([by_dist[:, nq - 1 - i:nq - 1 - i + nk] for i in range(nq)], axis=1)


def _swa_body(sink_ref, q_ref, kp_ref, vp_ref, kc_ref, vc_ref, bias_ref, o_ref, *, bb, qb, cq, win, pos0):
    blk = pl.program_id(1)
    col = lax.broadcasted_iota(jnp.int32, (1, win), 1)
    row = lax.broadcasted_iota(jnp.int32, (SWA_REP * cq, 1), 0)
    for b in range(bb):
        kwin = jnp.concatenate([kp_ref[b], kc_ref[b]], axis=0)
        vwin = jnp.concatenate([vp_ref[b], vc_ref[b]], axis=0)
        for j in range(qb // cq):
            valid = (pos0 + blk * qb + j * cq + col) >= 0
            for g in range(SWA_KV_HEADS):
                heads = [SWA_REP * g + r for r in range(SWA_REP)]
                qs = jnp.concatenate(
                    [q_ref[b, j * cq:(j + 1) * cq, h * SWA_HEAD_DIM:(h + 1) * SWA_HEAD_DIM] for h in heads], axis=0)
                kk = kwin[j * cq:j * cq + win, g * SWA_HEAD_DIM:(g + 1) * SWA_HEAD_DIM]
                vv = vwin[j * cq:j * cq + win, g * SWA_HEAD_DIM:(g + 1) * SWA_HEAD_DIM]
                s = _dot_nt(qs.astype(BF16), kk.astype(BF16)) + bias_ref[g]
                if pos0 + j * cq < 0:
                    s = jnp.where(valid, s, -1e30)
                sink = jnp.where(row < cq, sink_ref[heads[0]],
                                 jnp.where(row < 2 * cq, sink_ref[heads[1]], sink_ref[heads[2]]))
                m = jnp.maximum(jnp.max(s, axis=-1, keepdims=True), sink)
                e = jnp.exp(s - m)
                den = jnp.sum(e, axis=-1, keepdims=True) + jnp.exp(sink - m)
                o = _dot(e.astype(BF16), vv.astype(BF16)) * (1.0 / den)
                for r, h in enumerate(heads):
                    o_ref[b, j * cq:(j + 1) * cq, h * SWA_HEAD_DIM:(h + 1) * SWA_HEAD_DIM] = o[r * cq:(r + 1) * cq, :]


def _swa(sinks, p_swa, k_prev, v_prev, prev_k_map, prev_v_map, bias, bb, qb, cq, win, pos0):
    b, t, _ = p_swa.shape
    kcol, vcol = SWA_W // SWA_KV_W, SWA_W // SWA_KV_W + 1
    return pl.pallas_call(
        functools.partial(_swa_body, bb=bb, qb=qb, cq=cq, win=win, pos0=pos0),
        grid=(b // bb, t // qb),
        in_specs=[pl.BlockSpec(memory_space=pltpu.SMEM),
                  pl.BlockSpec((bb, qb, SWA_W), lambda i, j: (i, j, 0)),
                  pl.BlockSpec((bb, SWA_WINDOW, SWA_KV_W), prev_k_map),
                  pl.BlockSpec((bb, SWA_WINDOW, SWA_KV_W), prev_v_map),
                  pl.BlockSpec((bb, qb, SWA_KV_W), lambda i, j: (i, j, kcol)),
                  pl.BlockSpec((bb, qb, SWA_KV_W), lambda i, j: (i, j, vcol)),
                  _const_spec(bias.shape)],
        out_specs=pl.BlockSpec((bb, qb, SWA_W), lambda i, j: (i, j, 0)),
        out_shape=jax.ShapeDtypeStruct((b, t, SWA_W), F32),
        compiler_params=_cparams("parallel", "parallel"),
        name="swa",
    )(sinks, p_swa, k_prev, v_prev, p_swa, p_swa, bias)


def _out_proj_body(x_ref, ys_ref, og_ref, os_ref, wo_ref, gf_ref, wr_ref, br_ref, triu_ref, before_ref, x1_ref, xs_ref,
                   aux_ref, cnt_ref):
    x1 = (x_ref[...] + _dot(ys_ref[...].astype(BF16), wo_ref[0:SSM_W, :])
          + _dot(og_ref[...].astype(BF16), wo_ref[SSM_W:SSM_W + GLA_W, :])
          + _dot(os_ref[...].astype(BF16), wo_ref[SSM_W + GLA_W:MIX_W, :]))
    x1_ref[...] = x1
    xn = x1 * lax.rsqrt(jnp.mean(x1 * x1, axis=-1, keepdims=True) + RMS_EPS) * gf_ref[...]
    xb = xn.astype(BF16)
    logit = _dot(xb, wr_ref[...]) + br_ref[...]
    lt = jnp.transpose(logit)[:2 * N_EXPERTS, :]
    row_i = lax.broadcasted_iota(jnp.int32, lt.shape, 0)
    row = row_i.astype(F32)
    neg = -jnp.inf
    big = float(LANES)
    g_row = row_i - ROUTER_GROUP_LANE0
    lg = jnp.where((g_row >= 0) & (g_row < N_EXPERT_GROUPS), lt, neg)
    gmax = jnp.max(lg, axis=0, keepdims=True)
    p_sel = 1.0 / jnp.sum(jnp.exp(lg - gmax), axis=0, keepdims=True)
    g_idx = jnp.min(jnp.where(lg == gmax, g_row.astype(F32), big), axis=0, keepdims=True)
    row_group = (row_i >> _log2(EXPERTS_PER_GROUP)).astype(F32)
    in_group = (row_i < N_EXPERTS) & (row_group == g_idx)
    le = jnp.where(in_group, lt, neg)
    m1 = jnp.max(le, axis=0, keepdims=True)
    i1 = jnp.min(jnp.where(le == m1, row, big), axis=0, keepdims=True)
    le2 = jnp.where(row == i1, neg, le)
    m2 = jnp.max(le2, axis=0, keepdims=True)
    i2 = jnp.min(jnp.where(le2 == m2, row, big), axis=0, keepdims=True)
    r = jnp.exp(m2 - m1)
    w1 = 1.0 / (1.0 + r)
    exp_row = lax.broadcasted_iota(jnp.int32, (N_EXPERTS, MOE_TB), 0).astype(F32)
    s1 = jnp.where(exp_row == i1, 1.0, 0.0)
    s2 = jnp.where(exp_row == i2, 1.0, 0.0)
    both = s1 + s2
    cnt = jnp.sum(both, axis=1, keepdims=True) + jnp.zeros_like(both)
    seg = jnp.floor((cnt + (MOE_PIECE - 1)) * (1.0 / MOE_PIECE)) * MOE_PIECE
    seg_k = jnp.concatenate([seg, jnp.zeros((LANES - N_EXPERTS, MOE_TB), F32)], axis=0).astype(BF16)
    lo = _dot(before_ref[...], seg_k)
    rank = _dot(both.astype(BF16), triu_ref[...])
    pos = lo + rank
    d1 = jnp.sum(s1 * pos, axis=0, keepdims=True)
    d2 = jnp.sum(s2 * pos, axis=0, keepdims=True)
    r_iota = lax.broadcasted_iota(jnp.int32, (MOE_RLOC, MOE_TB), 0).astype(F32)
    perm = jnp.where((r_iota == d1) | (r_iota == d2), 1.0, 0.0).astype(BF16)
    xs_ref[...] = _dot(perm, xb).astype(BF16)
    arow = lax.broadcasted_iota(jnp.int32, (LANES, MOE_TB), 0)
    at = jnp.where(arow == AUX_W1, p_sel * w1, jnp.where(arow == AUX_W2, p_sel * (r * w1),
                   jnp.where(arow == AUX_D1, d1, jnp.where(arow == AUX_D2, d2, 0.0))))
    aux_ref[...] = jnp.transpose(at)
    cnt_ref[0] = cnt[:, :LANES]


def _out_proj(x, ys, og, osw, wo, gf, wr, br, triu, before):
    n = x.shape[0]
    nblk = n // MOE_TB
    row = lambda w: pl.BlockSpec((MOE_TB, w), lambda i: (i, 0))
    return pl.pallas_call(
        _out_proj_body,
        grid=(nblk,),
        in_specs=[row(D_MODEL), row(SSM_W), row(GLA_W), row(SWA_W), _const_spec(wo.shape), _const_spec(gf.shape),
                  _const_spec(wr.shape), _const_spec(br.shape), _const_spec(triu.shape), _const_spec(before.shape)],
        out_specs=[row(D_MODEL), pl.BlockSpec((MOE_RLOC, D_MODEL), lambda i: (i, 0)), row(LANES),
                   pl.BlockSpec((1, N_EXPERTS, LANES), lambda i: (i, 0, 0))],
        out_shape=[jax.ShapeDtypeStruct((n, D_MODEL), F32), jax.ShapeDtypeStruct((nblk * MOE_RLOC, D_MODEL), BF16),
                   jax.ShapeDtypeStruct((n, LANES), F32), jax.ShapeDtypeStruct((nblk, N_EXPERTS, LANES), F32)],
        compiler_params=_cparams("parallel"),
        name="out_proj",
    )(x, ys, og, osw, wo, gf, wr, br, triu, before)


def _piece_copy(hbm_ref, piece, buf_ref, slot, p, sem, to_hbm):
    start = lambda i: i * MOE_PIECE if isinstance(i, int) else pl.multiple_of(i * MOE_PIECE, MOE_PIECE)
    rows = pl.ds(start(piece), MOE_PIECE)
    vm = buf_ref.at[slot, pl.ds(start(p), MOE_PIECE)]
    if to_hbm:
        return pltpu.make_async_copy(vm, hbm_ref.at[rows], sem)
    return pltpu.make_async_copy(hbm_ref.at[rows], vm, sem)


def _moe_expert_body(texp_ref, piece_ref, nv_ref, xs_hbm, wg_ref, wu_ref, wd_ref, ys_hbm,
                     xbuf, ybuf, wgb, wub, wdb, sem_in, sem_out, *, pt, n_steps):
    i = pl.program_id(0)
    slot = i % 2

    def for_pieces(tile, s, hbm_ref, buf_ref, sem, to_hbm, wait):
        def one(p):
            piece = 0 if wait else piece_ref[tile * pt + p]
            c = _piece_copy(hbm_ref, piece, buf_ref, s, p, sem.at[s], to_hbm)
            c.wait() if wait else c.start()

        nv = nv_ref[tile]

        @pl.when(nv == pt)
        def _():
            for p in range(pt):
                one(p)

        @pl.when(nv < pt)
        def _():
            def body(p, carry):
                one(p)
                return carry
            lax.fori_loop(0, nv, body, 0)

    gather = lambda tile, s, wait: for_pieces(tile, s, xs_hbm, xbuf, sem_in, False, wait)
    scatter = lambda tile, s, wait: for_pieces(tile, s, ys_hbm, ybuf, sem_out, True, wait)

    @pl.when(i == 0)
    def _():
        xbuf[...] = jnp.zeros_like(xbuf)
        gather(0, 0, False)

    @pl.when(i + 1 < n_steps)
    def _():
        gather(i + 1, 1 - slot, False)

    gather(i, slot, True)

    @pl.when((i == 0) | (texp_ref[i] != texp_ref[jnp.maximum(i - 1, 0)]))
    def _():
        wgb[...] = wg_ref[0].astype(BF16)
        wub[...] = wu_ref[0].astype(BF16)
        wdb[...] = wd_ref[0].astype(BF16)

    @pl.when(nv_ref[i] > 0)
    def _():
        xb = xbuf[slot]
        a = _dot(xb, wgb[...])
        h = (a * _sigmoid(a)) * _dot(xb, wub[...])
        ybuf[slot] = _dot(h.astype(BF16), wdb[...]).astype(BF16)

    scatter(i, slot, False)

    @pl.when(i >= 1)
    def _():
        scatter(i - 1, 1 - slot, True)

    @pl.when(i == n_steps - 1)
    def _():
        scatter(i, slot, True)


def _moe_experts(tile_expert, piece, nvalid, xs, wg, wu, wd, layer, pt):
    n_steps = tile_expert.shape[0]
    tm = pt * MOE_PIECE
    wspec = lambda shape: pl.BlockSpec(shape, lambda i, te, pc, nv: (te[i] + layer * N_EXPERTS, 0, 0))
    grid_spec = pltpu.PrefetchScalarGridSpec(
        num_scalar_prefetch=3,
        grid=(n_steps,),
        in_specs=[pl.BlockSpec(memory_space=pl.ANY), wspec((1, D_MODEL, EXPERT_FF)), wspec((1, D_MODEL, EXPERT_FF)),
                  wspec((1, EXPERT_FF, D_MODEL))],
        out_specs=pl.BlockSpec(memory_space=pl.ANY),
        scratch_shapes=[pltpu.VMEM((2, tm, D_MODEL), BF16), pltpu.VMEM((2, tm, D_MODEL), BF16),
                        pltpu.VMEM((D_MODEL, EXPERT_FF), BF16), pltpu.VMEM((D_MODEL, EXPERT_FF), BF16),
                        pltpu.VMEM((EXPERT_FF, D_MODEL), BF16),
                        pltpu.SemaphoreType.DMA((2,)), pltpu.SemaphoreType.DMA((2,))],
    )
    return pl.pallas_call(
        functools.partial(_moe_expert_body, pt=pt, n_steps=n_steps),
        grid_spec=grid_spec,
        out_shape=jax.ShapeDtypeStruct(xs.shape, BF16),
        input_output_aliases={3: 0},
        compiler_params=_cparams("arbitrary"),
        name="moe_experts",
    )(tile_expert, piece, nvalid, xs, wg, wu, wd)


def _combine_math(ys_ref, aux_ref, x1_ref):
    col = lax.broadcasted_iota(jnp.int32, (MOE_TB, MOE_RLOC), 1).astype(F32)
    mix = jnp.where(col == aux_ref[:, AUX_D1:AUX_D1 + 1], aux_ref[:, AUX_W1:AUX_W1 + 1],
                    jnp.where(col == aux_ref[:, AUX_D2:AUX_D2 + 1], aux_ref[:, AUX_W2:AUX_W2 + 1], 0.0))
    return x1_ref[...] + _dot(mix.astype(BF16), ys_ref[...])


def _moe_combine_body(ys_ref, aux_ref, x1_ref, o_ref):
    o_ref[...] = _combine_math(ys_ref, aux_ref, x1_ref)


def _combine_in_proj_body(ys_ref, aux_ref, x1_ref, g_ref, w_ref, mavg_ref, qkg_ref, o_ref, gla_ref, swa_ref, ssm_ref):
    x = _combine_math(ys_ref, aux_ref, x1_ref)
    o_ref[...] = x
    _in_proj_math(x, g_ref, w_ref, mavg_ref, qkg_ref, gla_ref, swa_ref, ssm_ref)


def _combine_in_proj(ys, aux, x1, g, w, mavg, qkg):
    n = x1.shape[0]
    row = lambda wdt: pl.BlockSpec((MOE_TB, wdt), lambda i: (i, 0))
    return pl.pallas_call(
        _combine_in_proj_body,
        grid=(n // MOE_TB,),
        in_specs=[pl.BlockSpec((MOE_RLOC, D_MODEL), lambda i: (i, 0)), row(LANES), row(D_MODEL), _const_spec(g.shape),
                  _const_spec(w.shape), _const_spec(mavg.shape), _const_spec(qkg.shape)],
        out_specs=[row(D_MODEL), row(GLA_PW), row(SWA_PW), row(SSM_W)],
        out_shape=[jax.ShapeDtypeStruct((n, D_MODEL), F32), jax.ShapeDtypeStruct((n, GLA_PW), F32),
                   jax.ShapeDtypeStruct((n, SWA_PW), F32), jax.ShapeDtypeStruct((n, SSM_W), F32)],
        compiler_params=_cparams("parallel"),
        name="combine_in_proj",
    )(ys, aux, x1, g, w, mavg, qkg)


def _moe_combine(ys, aux, x1):
    n = x1.shape[0]
    return pl.pallas_call(
        _moe_combine_body,
        grid=(n // MOE_TB,),
        in_specs=[pl.BlockSpec((MOE_RLOC, D_MODEL), lambda i: (i, 0)), pl.BlockSpec((MOE_TB, LANES), lambda i: (i, 0)),
                  pl.BlockSpec((MOE_TB, D_MODEL), lambda i: (i, 0))],
        out_specs=pl.BlockSpec((MOE_TB, D_MODEL), lambda i: (i, 0)),
        out_shape=jax.ShapeDtypeStruct((n, D_MODEL), F32),
        compiler_params=_cparams("parallel"),
        name="moe_combine",
    )(ys, aux, x1)


def _route_tables(counts, pt):
    nblk = counts.shape[0]
    n = nblk * MOE_TB
    cnt = counts[:, :, 0].T.astype(jnp.int32)
    pc = (cnt + MOE_PIECE - 1) // MOE_PIECE
    lo_p = jnp.cumsum(pc, axis=0) - pc
    pe = pc.sum(1)
    tiles_e = (pe + pt - 1) // pt
    tile_start = jnp.cumsum(tiles_e) - tiles_e
    seg_start = (tile_start[:, None] * pt + jnp.cumsum(pc, axis=1) - pc).reshape(-1)
    pcs = pc.reshape(-1)
    seg_src = (jnp.arange(nblk, dtype=jnp.int32)[None, :] * MOE_PPB + lo_p).reshape(-1)
    n_steps = -(-(2 * n // MOE_PIECE + nblk * N_EXPERTS + N_EXPERTS * (pt - 1)) // pt)
    slot = jnp.arange(n_steps * pt, dtype=jnp.int32)[:, None]
    in_seg = (slot >= seg_start[None, :]) & (slot < (seg_start + pcs)[None, :])
    piece = jnp.sum(jnp.where(in_seg, seg_src[None, :] + slot - seg_start[None, :], 0), axis=1).astype(jnp.int32)
    tile = jnp.arange(n_steps, dtype=jnp.int32)[:, None]
    in_exp = (tile >= tile_start[None, :]) & (tile < (tile_start + tiles_e)[None, :])
    tile_expert = jnp.sum(jnp.where(in_exp, jnp.arange(N_EXPERTS, dtype=jnp.int32)[None, :], 0), axis=1)
    tile_expert = jnp.where(tile[:, 0] < tiles_e.sum(), tile_expert, N_EXPERTS - 1).astype(jnp.int32)
    nvalid = jnp.sum(jnp.where(in_exp, jnp.clip(pe[None, :] - (tile - tile_start[None, :]) * pt, 0, pt), 0),
                     axis=1).astype(jnp.int32)
    return tile_expert, piece, nvalid


def _moe(xs, counts, wg, wu, wd, layer, pt):
    tile_expert, piece, nvalid = _route_tables(counts, pt)
    return _moe_experts(tile_expert, piece, nvalid, xs, wg, wu, wd, layer, pt)


def _seg_mean_matrix(width, seg):
    i = jnp.arange(width)
    return jnp.where((i[:, None] // seg) == (i[None, :] // seg), 1.0 / seg, 0.0).astype(BF16)


def _chunk_tril(span, lc):
    i = jnp.arange(span)
    return ((i[:, None] >= i[None, :]) & ((i[:, None] // lc) == (i[None, :] // lc))).astype(BF16)


def _prep_layer(lw, ssm_shapes, gla_shapes):
    w_in = lw["w_in"].astype(F32)
    cols = {}
    off = 0
    for name, wdt in (("u", SSM_W), ("qg", GLA_K), ("kg", GLA_K), ("vg", GLA_W), ("z", GLA_GATE_RANK), ("og", GLA_W),
                      ("qs", SWA_W), ("ks", SWA_KV_W), ("vs", SWA_KV_W)):
        cols[name] = w_in[:, off:off + wdt]
        off += wdt
    zpad = lambda wdt: jnp.zeros((D_MODEL, wdt), F32)
    w_all = jnp.concatenate(
        [cols["qg"], zpad(GLA_KP - GLA_K), cols["kg"], zpad(GLA_KP - GLA_K), cols["vg"], cols["og"], cols["z"],
         zpad(LANES - GLA_GATE_RANK), cols["qs"], cols["ks"], cols["vs"], cols["u"]], axis=1).astype(BF16)
    out = {"norm_mix": lw["norm_mix"].astype(F32)[None, :], "w_all": w_all,
           "mavg_qk": _seg_mean_matrix(SWA_W + SWA_KV_W, SWA_HEAD_DIM),
           "qk_gain": jnp.concatenate([jnp.tile(lw["swa_q_norm"].astype(F32) * SWA_HEAD_DIM ** -0.5, SWA_HEADS),
                                       jnp.tile(lw["swa_k_norm"].astype(F32), SWA_KV_HEADS)])[None, :]}
    a_re = lw["ssm_a_re"].astype(F32)
    a_im = lw["ssm_a_im"].astype(F32)
    dt = jnp.exp(lw["ssm_log_dt"].astype(F32))[:, None]
    mag = jnp.exp(a_re * dt)
    ab_re = mag * jnp.cos(a_im * dt)
    ab_im = mag * jnp.sin(a_im * dt)
    den = a_re * a_re + a_im * a_im
    nr = ab_re - 1.0
    f_re = (nr * a_re + ab_im * a_im) / den
    f_im = (ab_im * a_re - nr * a_im) / den
    b_re = lw["ssm_b_re"].astype(F32)
    b_im = lw["ssm_b_im"].astype(F32)
    bb_re = f_re[..., None] * b_re - f_im[..., None] * b_im
    bb_im = f_re[..., None] * b_im + f_im[..., None] * b_re
    eye_g = jnp.eye(SSM_GROUPS, dtype=F32)
    blockdiag_in = lambda bb: jnp.einsum("gpc,gh->gchp", bb, eye_g).reshape(SSM_W, SSM_FLAT)
    blockdiag_out = lambda cc: jnp.einsum("gcp,gh->gphc", cc, eye_g).reshape(SSM_FLAT, SSM_W)
    bd = jnp.concatenate([blockdiag_in(bb_re), blockdiag_in(bb_im)], axis=1).astype(BF16)
    cd = jnp.concatenate([blockdiag_out(lw["ssm_c_re"].astype(F32)),
                          -blockdiag_out(lw["ssm_c_im"].astype(F32))], axis=0).astype(BF16)
    ssm = {"ab": jnp.concatenate([ab_re.reshape(1, SSM_FLAT), ab_im.reshape(1, SSM_FLAT),
                                  jnp.zeros((6, SSM_FLAT), F32)], axis=0),
           "bd": bd, "cd": cd, "d": lw["ssm_d"].astype(F32)[None, :], "wglu": lw["ssm_w_glu"].astype(BF16)}
    def powers(lc, sign):
        tt = jnp.arange(lc, dtype=F32)[:, None, None]
        m = jnp.exp(sign * tt * (a_re * dt)[None])
        ang = sign * tt * (a_im * dt)[None]
        return jnp.stack([(m * jnp.cos(ang)).reshape(lc, SSM_FLAT), (m * jnp.sin(ang)).reshape(lc, SSM_FLAT)])

    out["ssm"] = {(span, lc): dict(ssm, pwp=powers(lc, 1.0), pwn=powers(lc, -1.0), tril=_chunk_tril(span, lc))
                  for span, lc in ssm_shapes}
    wg = jnp.zeros((LANES, GLA_KP), F32).at[:GLA_GATE_RANK, :GLA_K].set(lw["gla_w_gate"].astype(F32)).astype(BF16)
    bg = jnp.zeros((1, GLA_KP), F32).at[0, :GLA_K].set(lw["gla_b_gate"].astype(F32))
    kk = jnp.arange(GLA_KP)
    spread = ((kk[None, :] % GLA_DK) == jnp.arange(GLA_DK)[:, None]) & (kk[None, :] < GLA_K)
    gla = {"wg": wg, "bg": bg, "mavg": _seg_mean_matrix(GLA_W, GLA_DV),
           "gn": jnp.tile(lw["gla_norm"].astype(F32), GLA_HEADS)[None, :], "spread": spread.astype(BF16)}
    out["gla"] = {(span, lc): dict(gla, tril=_chunk_tril(span, lc)) for span, lc in gla_shapes}
    out["sinks"] = lw["swa_sinks"].astype(F32)
    out["w_out"] = lw["w_out"].astype(BF16)
    out["norm_ffn"] = lw["norm_ffn"].astype(F32)[None, :]
    wr = jnp.zeros((D_MODEL, LANES), F32)
    wr = wr.at[:, :N_EXPERTS].set(lw["moe_w_expert"].astype(F32))
    wr = wr.at[:, ROUTER_GROUP_LANE0:ROUTER_GROUP_LANE0 + N_EXPERT_GROUPS].set(lw["moe_w_group"].astype(F32))
    out["wr"] = wr.astype(BF16)
    br = jnp.zeros((1, LANES), F32)
    br = br.at[0, :N_EXPERTS].set(lw["moe_b_expert"].astype(F32))
    br = br.at[0, ROUTER_GROUP_LANE0:ROUTER_GROUP_LANE0 + N_EXPERT_GROUPS].set(lw["moe_b_group"].astype(F32))
    out["br"] = br
    ti = jnp.arange(MOE_TB)
    out["moe_triu"] = (ti[:, None] < ti[None, :]).astype(BF16)
    out["moe_before"] = (jnp.arange(LANES)[None, :] < jnp.arange(N_EXPERTS)[:, None]).astype(BF16)
    return out


def _gla_state_in(h0):
    return jnp.transpose(h0.astype(F32), (0, 1, 3, 2)).reshape(h0.shape[0], GLA_W, GLA_DK)


def _gla_state_out(st):
    return jnp.transpose(st.reshape(st.shape[0], GLA_HEADS, GLA_DV, GLA_DK), (0, 1, 3, 2))


def _layer(x, pending, pw, bias, ssm_h0r, ssm_h0i, gla_h0, past_k, past_v, cfg, b, t):
    n = b * t
    in_w = (pw["norm_mix"], pw["w_all"], pw["mavg_qk"], pw["qk_gain"])
    if pending is None:
        p_gla, p_swa, p_ssm = _in_proj(x, *in_w, cfg["tm"])
    else:
        x, p_gla, p_swa, p_ssm = _combine_in_proj(*pending, *in_w)
    p_gla = p_gla.reshape(b, t, GLA_PW)
    p_swa = p_swa.reshape(b, t, SWA_PW)
    stack = cfg["stack"]
    indep = stack > 1
    grp = b // stack
    y_ssm, h_re, h_im = _ssm(p_ssm.reshape(grp, stack * t, SSM_W), ssm_h0r.reshape(grp, stack, SSM_FLAT).astype(F32),
                             ssm_h0i.reshape(grp, stack, SSM_FLAT).astype(F32), pw["ssm"][cfg["ssm"]], cfg["lblk"],
                             *cfg["ssm"], indep)
    o_g, s_fin = _gla(p_gla.reshape(grp, stack * t, GLA_PW), _gla_state_in(gla_h0).reshape(grp, stack, GLA_W, GLA_DK),
                      pw["gla"][cfg["gla"]], cfg["lblk"], *cfg["gla"], indep)
    s_fin = s_fin.reshape(b, GLA_W, GLA_DK)
    kcol, vcol = SWA_W // SWA_KV_W, SWA_W // SWA_KV_W + 1
    if past_k is None:
        per_blk = cfg["qb"] // SWA_WINDOW
        prev_k_map = lambda i, j: (i, jnp.maximum(j * per_blk - 1, 0), kcol)
        prev_v_map = lambda i, j: (i, jnp.maximum(j * per_blk - 1, 0), vcol)
        k_prev, v_prev = p_swa, p_swa
        keep = min(SWA_WINDOW, t)
        new_k = p_swa[:, t - keep:, SWA_W:SWA_W + SWA_KV_W]
        new_v = p_swa[:, t - keep:, SWA_W + SWA_KV_W:]
    else:
        prev_k_map = prev_v_map = lambda i, j: (i, 0, 0)
        k_prev = past_k.reshape(b, SWA_WINDOW, SWA_KV_W).astype(F32)
        v_prev = past_v.reshape(b, SWA_WINDOW, SWA_KV_W).astype(F32)
        new_k = p_swa[:, :, SWA_W:SWA_W + SWA_KV_W]
        new_v = p_swa[:, :, SWA_W + SWA_KV_W:]
    o_s = _swa(pw["sinks"], p_swa, k_prev, v_prev, prev_k_map, prev_v_map, bias, stack, cfg["qb"], cfg["cq"],
               cfg["win"], cfg["pos0"])
    x1, xs, aux, counts = _out_proj(x, y_ssm.reshape(n, SSM_W), o_g.reshape(n, GLA_W),
                                    o_s.reshape(n, SWA_W), pw["w_out"], pw["norm_ffn"], pw["wr"], pw["br"],
                                    pw["moe_triu"], pw["moe_before"])
    ys = _moe(xs, counts, pw["moe_wg"], pw["moe_wu"], pw["moe_wd"], pw["layer"], cfg["pt"])
    kv_shape = (b, new_k.shape[1], SWA_KV_HEADS, SWA_HEAD_DIM)
    return ((ys, aux, x1), new_k.reshape(kv_shape), new_v.reshape(kv_shape), _gla_state_out(s_fin),
            h_re.reshape(b, SSM_GROUPS, SSM_STATE), h_im.reshape(b, SSM_GROUPS, SSM_STATE))


def _group_cfg(t, past_len, n_past):
    if n_past is None:
        assert t % (16 * CHUNK) == 0
        cfg = dict(ssm=(4 * CHUNK, CHUNK), gla=(4 * CHUNK, 2 * CHUNK), lblk=16 * CHUNK, stack=1, qb=8 * CHUNK, cq=CHUNK,
                   win=SWA_WINDOW + CHUNK, pos0=-SWA_WINDOW, tm=512, pt=32)
        q_pos = jnp.arange(CHUNK, dtype=jnp.int32) + SWA_WINDOW
        k_pos = jnp.arange(SWA_WINDOW + CHUNK, dtype=jnp.int32)
    else:
        first_key, last_q = past_len - n_past, past_len + t - 1
        assert n_past == SWA_WINDOW and t <= CHUNK and t % 8 == 0
        assert past_len // CHUNK == last_q // CHUNK and first_key // CHUNK >= past_len // CHUNK - SWA_WINDOW // CHUNK
        stack = DECODE_STACK
        cfg = dict(ssm=(stack * t, t), gla=(stack * t, t), lblk=stack * t, stack=stack, qb=t, cq=t, win=n_past + t,
                   pos0=first_key, tm=512, pt=16)
        q_pos = past_len + jnp.arange(t, dtype=jnp.int32)
        k_pos = first_key + jnp.arange(n_past + t, dtype=jnp.int32)
    return cfg, q_pos, k_pos


def _stacked_bias(t5_table, q_pos, k_pos):
    bias = _t5_bias(t5_table, q_pos, k_pos)
    nq, nk = bias.shape[1:]
    return bias.reshape(SWA_KV_HEADS, SWA_REP * nq, nk)


PAST_LEN = 1024


def kernel(x_prompt, x_sample, cache_swa_k, cache_swa_v, state_gla, state_ssm_re, state_ssm_im, norm_mix, w_in, ssm_a_re, ssm_a_im, ssm_log_dt, ssm_b_re, ssm_b_im, ssm_c_re, ssm_c_im, ssm_d, ssm_w_glu, gla_w_gate, gla_b_gate, gla_norm, swa_q_norm, swa_k_norm, swa_sinks, t5_table, w_out, norm_ffn, moe_w_group, moe_b_group, moe_w_expert, moe_b_expert, moe_w_gate, moe_w_up, moe_w_down):
    depth = w_in.shape[0]
    bp, tp, _ = x_prompt.shape
    bs, ts, _ = x_sample.shape
    cfg_p, qpos_p, kpos_p = _group_cfg(tp, 0, None)
    cfg_s, qpos_s, kpos_s = _group_cfg(ts, PAST_LEN, cache_swa_k.shape[2])
    bias_p = _stacked_bias(t5_table, qpos_p, kpos_p)
    bias_s = _stacked_bias(t5_table, qpos_s, kpos_s)
    xp = x_prompt.astype(F32).reshape(bp * tp, D_MODEL)
    xs = x_sample.astype(F32).reshape(bs * ts, D_MODEL)
    pend_p = pend_s = None
    moe_wg = moe_w_gate.astype(F32).reshape(depth * N_EXPERTS, D_MODEL, EXPERT_FF)
    moe_wu = moe_w_up.astype(F32).reshape(depth * N_EXPERTS, D_MODEL, EXPERT_FF)
    moe_wd = moe_w_down.astype(F32).reshape(depth * N_EXPERTS, EXPERT_FF, D_MODEL)
    outs = [[] for _ in range(10)]
    for l in range(depth):
        lw = {
            "norm_mix": norm_mix[l], "w_in": w_in[l], "ssm_a_re": ssm_a_re[l], "ssm_a_im": ssm_a_im[l],
            "ssm_log_dt": ssm_log_dt[l], "ssm_b_re": ssm_b_re[l], "ssm_b_im": ssm_b_im[l], "ssm_c_re": ssm_c_re[l],
            "ssm_c_im": ssm_c_im[l], "ssm_d": ssm_d[l], "ssm_w_glu": ssm_w_glu[l], "gla_w_gate": gla_w_gate[l],
            "gla_b_gate": gla_b_gate[l], "gla_norm": gla_norm[l], "swa_q_norm": swa_q_norm[l],
            "swa_k_norm": swa_k_norm[l], "swa_sinks": swa_sinks[l], "w_out": w_out[l], "norm_ffn": norm_ffn[l],
            "moe_w_group": moe_w_group[l], "moe_b_group": moe_b_group[l], "moe_w_expert": moe_w_expert[l],
            "moe_b_expert": moe_b_expert[l],
        }
        pw = _prep_layer(lw, {cfg_p["ssm"], cfg_s["ssm"]}, {cfg_p["gla"], cfg_s["gla"]})
        pw.update(layer=l, moe_wg=moe_wg, moe_wu=moe_wu, moe_wd=moe_wd)
        zs = jnp.zeros((bp, SSM_GROUPS, SSM_STATE), F32)
        zg = jnp.zeros((bp, GLA_HEADS, GLA_DK, GLA_DV), F32)
        pend_p, nk, nv, ng, nr, ni = _layer(xp, pend_p, pw, bias_p, zs, zs, zg, None, None, cfg_p, bp, tp)
        for slot, val in zip((0, 1, 4, 6, 7), (nk, nv, ng, nr, ni)):
            outs[slot].append(val)
        pend_s, nk, nv, ng, nr, ni = _layer(xs, pend_s, pw, bias_s, state_ssm_re[l], state_ssm_im[l], state_gla[l],
                                            cache_swa_k[l], cache_swa_v[l], cfg_s, bs, ts)
        for slot, val in zip((2, 3, 5, 8, 9), (nk, nv, ng, nr, ni)):
            outs[slot].append(val)
    hp = _moe_combine(*pend_p).reshape(bp, tp, D_MODEL)
    hs = _moe_combine(*pend_s).reshape(bs, ts, D_MODEL)
    return (hp, hs) + tuple(jnp.stack(o) for o in outs)
```

```python
import functools
import math

import jax
import jax.numpy as jnp
from jax import lax
from jax.experimental import pallas as pl
from jax.experimental.pallas import tpu as pltpu

F32 = jnp.float32
BF16 = jnp.bfloat16

D_MODEL = 1024
CHUNK = 64
RMS_EPS = 1e-6
SSM_GROUPS = 16
SSM_GC = 16
SSM_STATE = 64
SSM_W = SSM_GROUPS * SSM_GC
SSM_FLAT = SSM_GROUPS * SSM_STATE
GLA_HEADS = 6
GLA_DK = 32
GLA_DV = 64
GLA_GATE_RANK = 16
GLA_GATE_NORM = 16.0
GLA_K = GLA_HEADS * GLA_DK
GLA_KP = 256
GLA_W = GLA_HEADS * GLA_DV
SWA_HEADS = 6
SWA_KV_HEADS = 2
SWA_REP = SWA_HEADS // SWA_KV_HEADS
SWA_HEAD_DIM = 64
SWA_WINDOW = 128
SWA_W = SWA_HEADS * SWA_HEAD_DIM
SWA_KV_W = SWA_KV_HEADS * SWA_HEAD_DIM
MIX_W = SSM_W + GLA_W + SWA_W
N_BUCKETS = 32
T5_MAX_DIST = 128
N_EXPERT_GROUPS = 4
EXPERTS_PER_GROUP = 4
N_EXPERTS = 16
EXPERT_FF = 512
LANES = 128
ROUTER_GROUP_LANE0 = N_EXPERTS
AUX_W1, AUX_W2, AUX_D1, AUX_D2 = 0, 1, 2, 3
AUX_ROWS = 8
DECODE_STACK = 8
MOE_TB = 512
MOE_PIECE = 16
MOE_RLOC = 2 * MOE_TB + N_EXPERTS * MOE_PIECE
MOE_PPB = MOE_RLOC // MOE_PIECE

GLA_Q0, GLA_K0, GLA_V0, GLA_OG0, GLA_Z0, GLA_PW = 0, 256, 512, 896, 1280, 1408
SWA_PW = SWA_W + 2 * SWA_KV_W
P_GLA0, P_SWA0, P_SSM0, P_TOTAL = 0, GLA_PW, GLA_PW + SWA_PW, GLA_PW + SWA_PW + SSM_W

VMEM_LIMIT = 48 * 1024 * 1024


def _cparams(*sem):
    return pltpu.CompilerParams(dimension_semantics=sem, vmem_limit_bytes=VMEM_LIMIT)


def _dot(a, b):
    return jnp.dot(a, b, preferred_element_type=F32)


def _dot_nt(a, b):
    return lax.dot_general(a, b, (((1,), (1,)), ((), ())), preferred_element_type=F32)


def _dot_tn(a, b):
    return lax.dot_general(a, b, (((0,), (0,)), ((), ())), preferred_element_type=F32)


def _hi_lo(x):
    hi = x.astype(BF16)
    return hi, (x - hi.astype(F32)).astype(BF16)


def _dot_f32_rhs(a_bf16, x):
    hi, lo = _hi_lo(x)
    return _dot(a_bf16, hi) + _dot(a_bf16, lo)


def _dot_f32_lhs(x, b_bf16):
    hi, lo = _hi_lo(x)
    return _dot(hi, b_bf16) + _dot(lo, b_bf16)


def _log2(n):
    assert n & (n - 1) == 0
    return n.bit_length() - 1


def _sigmoid(x):
    return 1.0 / (1.0 + jnp.exp(-x))


def _const_spec(shape):
    nd = len(shape)
    return pl.BlockSpec(shape, lambda *_: (0,) * nd)


def _in_proj_body(x_ref, g_ref, w_ref, mavg_ref, qkg_ref, gla_ref, swa_ref, ssm_ref):
    _in_proj_math(x_ref[...], g_ref, w_ref, mavg_ref, qkg_ref, gla_ref, swa_ref, ssm_ref)


def _in_proj_math(x, g_ref, w_ref, mavg_ref, qkg_ref, gla_ref, swa_ref, ssm_ref):
    xn = x * lax.rsqrt(jnp.mean(x * x, axis=-1, keepdims=True) + RMS_EPS) * g_ref[...]
    xb = xn.astype(BF16)
    gla_ref[...] = _dot(xb, w_ref[:, P_GLA0:P_SWA0])
    ssm_ref[...] = _dot(xb, w_ref[:, P_SSM0:P_TOTAL])
    s = _dot(xb, w_ref[:, P_SWA0:P_SSM0])
    qk = s[:, :SWA_W + SWA_KV_W]
    ms = _dot((qk * qk).astype(BF16), mavg_ref[...])
    swa_ref[:, :SWA_W + SWA_KV_W] = qk * lax.rsqrt(ms + RMS_EPS) * qkg_ref[...]
    swa_ref[:, SWA_W + SWA_KV_W:] = s[:, SWA_W + SWA_KV_W:]


def _in_proj(x, g, w, mavg, qkg, tm):
    n = x.shape[0]
    return pl.pallas_call(
        _in_proj_body,
        grid=(n // tm,),
        in_specs=[pl.BlockSpec((tm, D_MODEL), lambda i: (i, 0)), _const_spec(g.shape), _const_spec(w.shape),
                  _const_spec(mavg.shape), _const_spec(qkg.shape)],
        out_specs=[pl.BlockSpec((tm, GLA_PW), lambda i: (i, 0)), pl.BlockSpec((tm, SWA_PW), lambda i: (i, 0)),
                   pl.BlockSpec((tm, SSM_W), lambda i: (i, 0))],
        out_shape=[jax.ShapeDtypeStruct((n, GLA_PW), F32), jax.ShapeDtypeStruct((n, SWA_PW), F32),
                   jax.ShapeDtypeStruct((n, SSM_W), F32)],
        compiler_params=_cparams("parallel"),
        name="in_proj",
    )(x, g, w, mavg, qkg)


def _ssm_body(u_ref, h0r_ref, h0i_ref, ab_ref, bd_ref, cd_ref, pwp_ref, pwn_ref, tril_ref, d_ref, wglu_ref,
              y_ref, hr_ref, hi_ref, carry_ref, *, n_part, n_sub, lc, indep):
    if not indep:
        @pl.when(pl.program_id(1) == 0)
        def _():
            carry_ref[0:1, :] = h0r_ref[0]
            carry_ref[1:2, :] = h0i_ref[0]

        hr = carry_ref[0:1, :]
        hi = carry_ref[1:2, :]
    ab_re = ab_ref[0:1, :]
    ab_im = ab_ref[1:2, :]
    nr, ni = pwn_ref[0], pwn_ref[1]
    pr, pi = pwp_ref[0], pwp_ref[1]
    span = n_sub * lc
    for part in range(n_part):
        rows = slice(part * span, (part + 1) * span)
        u = u_ref[0, rows, :]
        bu = _dot(u.astype(BF16), bd_ref[...])
        sr, si = [], []
        for c in range(n_sub):
            bur = bu[c * lc:(c + 1) * lc, :SSM_FLAT]
            bui = bu[c * lc:(c + 1) * lc, SSM_FLAT:]
            sr.append(nr * bur - ni * bui)
            si.append(nr * bui + ni * bur)
        scaled = jnp.concatenate([jnp.concatenate(sr, axis=0), jnp.concatenate(si, axis=0)], axis=1)
        cs = _dot(tril_ref[...], scaled.astype(BF16))
        h_r, h_i = [], []
        for c in range(n_sub):
            if indep:
                hr, hi = h0r_ref[0, c:c + 1, :], h0i_ref[0, c:c + 1, :]
            cr = cs[c * lc:(c + 1) * lc, :SSM_FLAT] + (ab_re * hr - ab_im * hi)
            ci = cs[c * lc:(c + 1) * lc, SSM_FLAT:] + (ab_re * hi + ab_im * hr)
            h_r.append(pr * cr - pi * ci)
            h_i.append(pr * ci + pi * cr)
            hr = h_r[-1][lc - 1:lc, :]
            hi = h_i[-1][lc - 1:lc, :]
            if indep:
                hr_ref[0, c:c + 1, :] = hr
                hi_ref[0, c:c + 1, :] = hi
        hcat = jnp.concatenate([jnp.concatenate(h_r, axis=0), jnp.concatenate(h_i, axis=0)], axis=1).astype(BF16)
        y = _dot(hcat, cd_ref[...]) + d_ref[...] * u
        g = 0.5 * y * (1.0 + jnp.tanh(math.sqrt(2.0 / math.pi) * (y + 0.044715 * (y * y * y))))
        y_ref[0, rows, :] = g * _sigmoid(_dot(g.astype(BF16), wglu_ref[...]))
    if not indep:
        carry_ref[0:1, :] = hr
        carry_ref[1:2, :] = hi
        hr_ref[0] = hr
        hi_ref[0] = hi


def _ssm(u, h0r, h0i, sw, lblk, span, lc, indep):
    b, t, _ = u.shape
    n_state = span // lc if indep else 1
    assert not indep or (t == lblk == span)
    consts = [sw["ab"], sw["bd"], sw["cd"], sw["pwp"], sw["pwn"], sw["tril"], sw["d"], sw["wglu"]]
    state_spec = pl.BlockSpec((1, n_state, SSM_FLAT), lambda i, j: (i, 0, 0))
    return pl.pallas_call(
        functools.partial(_ssm_body, n_part=lblk // span, n_sub=span // lc, lc=lc, indep=indep),
        grid=(b, t // lblk),
        in_specs=[pl.BlockSpec((1, lblk, SSM_W), lambda i, j: (i, j, 0)), state_spec, state_spec]
        + [_const_spec(c.shape) for c in consts],
        out_specs=[pl.BlockSpec((1, lblk, SSM_W), lambda i, j: (i, j, 0)), state_spec, state_spec],
        out_shape=[jax.ShapeDtypeStruct((b, t, SSM_W), F32), jax.ShapeDtypeStruct((b, n_state, SSM_FLAT), F32),
                   jax.ShapeDtypeStruct((b, n_state, SSM_FLAT), F32)],
        scratch_shapes=[pltpu.VMEM((8, SSM_FLAT), F32)],
        compiler_params=_cparams("parallel", "arbitrary"),
        name="ssm",
    )(u, h0r, h0i, *consts)


def _gla_body(p_ref, s0_ref, wg_ref, bg_ref, tril_ref, mavg_ref, gn_ref, spread_ref, o_ref, sfin_ref, s_ref, *,
              n_part, n_sub, lc, indep):
    row_v = lax.broadcasted_iota(jnp.int32, (GLA_W, GLA_KP), 0)
    col_k = lax.broadcasted_iota(jnp.int32, (GLA_W, GLA_KP), 1)
    same_head = ((row_v >> _log2(GLA_DV)) == (col_k >> _log2(GLA_DK))).astype(F32)

    def spread(compact):
        return _dot_f32_lhs(compact, spread_ref[...]) * same_head

    def gather(full):
        hi, lo = _hi_lo(full)
        return _dot_nt(hi, spread_ref[...]) + _dot_nt(lo, spread_ref[...])

    if not indep:
        @pl.when(pl.program_id(1) == 0)
        def _():
            s_ref[...] = spread(s0_ref[0, 0])

    lane_k = lax.broadcasted_iota(jnp.int32, (1, GLA_KP), 1)
    lane_v = lax.broadcasted_iota(jnp.int32, (1, GLA_W), 1)
    head_k = [((lane_k >= h * GLA_DK) & (lane_k < (h + 1) * GLA_DK)).astype(F32) for h in range(GLA_HEADS)]
    head_v = [((lane_v >= h * GLA_DV) & (lane_v < (h + 1) * GLA_DV)).astype(F32) for h in range(GLA_HEADS)]
    row_t = lax.broadcasted_iota(jnp.int32, (GLA_HEADS * lc, lc), 0)
    col_s = lax.broadcasted_iota(jnp.int32, (GLA_HEADS * lc, lc), 1)
    causal = (row_t & (lc - 1)) >= col_s
    mid = lc // 2 - 1
    span = n_sub * lc
    st = None if indep else s_ref[...]
    for part in range(n_part):
        rows = slice(part * span, (part + 1) * span)
        z = p_ref[0, rows, GLA_Z0:GLA_Z0 + LANES]
        gin = _dot(z.astype(BF16), wg_ref[...]) + bg_ref[...]
        glog = (jnp.minimum(gin, 0.0) - jnp.log(1.0 + jnp.exp(-jnp.abs(gin)))) / GLA_GATE_NORM
        g_all = _dot_f32_rhs(tril_ref[...], glog)
        outs = []
        for c in range(n_sub):
            crow = slice(part * span + c * lc, part * span + (c + 1) * lc)
            q = p_ref[0, crow, GLA_Q0:GLA_Q0 + GLA_KP] * (GLA_DK ** -0.5)
            k = p_ref[0, crow, GLA_K0:GLA_K0 + GLA_KP]
            vb = p_ref[0, crow, GLA_V0:GLA_V0 + GLA_W].astype(BF16)
            if indep:
                st = spread(s0_ref[0, c])
            gc = g_all[c * lc:(c + 1) * lc, :]
            gl = gc[lc - 1:lc, :]
            gm = gc[mid:mid + 1, :]
            qc = (q * jnp.exp(gc - gm))
            ke = (k * jnp.exp(gm - gc)).astype(BF16)
            kd = (k * jnp.exp(gl - gc)).astype(BF16)
            qs = jnp.concatenate([qc * m for m in head_k], axis=0).astype(BF16)
            attn = jnp.where(causal, _dot_nt(qs, ke), 0.0)
            o2 = _dot(attn.astype(BF16), vb)
            o = _dot_nt((q * jnp.exp(gc)).astype(BF16), st.astype(BF16))
            for h in range(GLA_HEADS):
                o = o + head_v[h] * o2[h * lc:(h + 1) * lc, :]
            st = st * jnp.exp(gl) + _dot_tn(vb, kd) * same_head
            if indep:
                sfin_ref[0, c] = gather(st)
            outs.append(o)
        o = jnp.concatenate(outs, axis=0) if n_sub > 1 else outs[0]
        og = p_ref[0, rows, GLA_OG0:GLA_OG0 + GLA_W]
        ms = _dot((o * o).astype(BF16), mavg_ref[...])
        on = o * lax.rsqrt(ms + RMS_EPS) * gn_ref[...]
        o_ref[0, rows, :] = on * (og * _sigmoid(og))
    if not indep:
        s_ref[...] = st

        @pl.when(pl.program_id(1) == pl.num_programs(1) - 1)
        def _():
            sfin_ref[0, 0] = gather(st)


def _gla(p, s0, gw, lblk, span, lc, indep):
    b, t, _ = p.shape
    n_state = span // lc if indep else 1
    assert not indep or (t == lblk == span)
    consts = [gw["wg"], gw["bg"], gw["tril"], gw["mavg"], gw["gn"], gw["spread"]]
    st_spec = pl.BlockSpec((1, n_state, GLA_W, GLA_DK), lambda i, j: (i, 0, 0, 0))
    return pl.pallas_call(
        functools.partial(_gla_body, n_part=lblk // span, n_sub=span // lc, lc=lc, indep=indep),
        grid=(b, t // lblk),
        in_specs=[pl.BlockSpec((1, lblk, GLA_PW), lambda i, j: (i, j, 0)), st_spec]
        + [_const_spec(c.shape) for c in consts],
        out_specs=[pl.BlockSpec((1, lblk, GLA_W), lambda i, j: (i, j, 0)), st_spec],
        out_shape=[jax.ShapeDtypeStruct((b, t, GLA_W), F32),
                   jax.ShapeDtypeStruct((b, n_state, GLA_W, GLA_DK), F32)],
        scratch_shapes=[pltpu.VMEM((GLA_W, GLA_KP), F32)],
        compiler_params=_cparams("parallel", "arbitrary"),
        name="gla",
    )(p, s0, *consts)


def _t5_bias(t5_table, q_pos, k_pos):
    nq, nk = q_pos.shape[0], k_pos.shape[0]
    rel = (k_pos[0] - q_pos[0]) + jnp.arange(-(nq - 1), nk, dtype=jnp.int32)
    half = N_BUCKETS // 2
    max_exact = half // 2
    n = jnp.abs(rel)
    far = max_exact + (jnp.log(jnp.maximum(n, 1).astype(jnp.float32) / max_exact)
                       / math.log(T5_MAX_DIST / max_exact) * (half - max_exact)).astype(jnp.int32)
    bucket = jnp.where(rel > 0, half, 0) + jnp.where(n < max_exact, n, jnp.minimum(far, half - 1))
    by_dist = t5_table.astype(F32)[bucket].T
    return jnp.stack([by_dist[:, nq - 1 - i:nq - 1 - i + nk] for i in range(nq)], axis=1)


def _swa_body(sink_ref, q_ref, kp_ref, vp_ref, kc_ref, vc_ref, bias_ref, o_ref, *, bb, qb, cq, win, pos0):
    blk = pl.program_id(1)
    col = lax.broadcasted_iota(jnp.int32, (1, win), 1)
    row = lax.broadcasted_iota(jnp.int32, (SWA_REP * cq, 1), 0)
    for b in range(bb):
        kwin = jnp.concatenate([kp_ref[b], kc_ref[b]], axis=0)
        vwin = jnp.concatenate([vp_ref[b], vc_ref[b]], axis=0)
        for j in range(qb // cq):
            valid = (pos0 + blk * qb + j * cq + col) >= 0
            for g in range(SWA_KV_HEADS):
                heads = [SWA_REP * g + r for r in range(SWA_REP)]
                qs = jnp.concatenate(
                    [q_ref[b, j * cq:(j + 1) * cq, h * SWA_HEAD_DIM:(h + 1) * SWA_HEAD_DIM] for h in heads], axis=0)
                kk = kwin[j * cq:j * cq + win, g * SWA_HEAD_DIM:(g + 1) * SWA_HEAD_DIM]
                vv = vwin[j * cq:j * cq + win, g * SWA_HEAD_DIM:(g + 1) * SWA_HEAD_DIM]
                s = _dot_nt(qs.astype(BF16), kk.astype(BF16)) + bias_ref[g]
                if pos0 + j * cq < 0:
                    s = jnp.where(valid, s, -1e30)
                sink = jnp.where(row < cq, sink_ref[heads[0]],
                                 jnp.where(row < 2 * cq, sink_ref[heads[1]], sink_ref[heads[2]]))
                m = jnp.maximum(jnp.max(s, axis=-1, keepdims=True), sink)
                e = jnp.exp(s - m)
                den = jnp.sum(e, axis=-1, keepdims=True) + jnp.exp(sink - m)
                o = _dot(e.astype(BF16), vv.astype(BF16)) * (1.0 / den)
                for r, h in enumerate(heads):
                    o_ref[b, j * cq:(j + 1) * cq, h * SWA_HEAD_DIM:(h + 1) * SWA_HEAD_DIM] = o[r * cq:(r + 1) * cq, :]


def _swa(sinks, p_swa, k_prev, v_prev, prev_k_map, prev_v_map, bias, bb, qb, cq, win, pos0):
    b, t, _ = p_swa.shape
    kcol, vcol = SWA_W // SWA_KV_W, SWA_W // SWA_KV_W + 1
    return pl.pallas_call(
        functools.partial(_swa_body, bb=bb, qb=qb, cq=cq, win=win, pos0=pos0),
        grid=(b // bb, t // qb),
        in_specs=[pl.BlockSpec(memory_space=pltpu.SMEM),
                  pl.BlockSpec((bb, qb, SWA_W), lambda i, j: (i, j, 0)),
                  pl.BlockSpec((bb, SWA_WINDOW, SWA_KV_W), prev_k_map),
                  pl.BlockSpec((bb, SWA_WINDOW, SWA_KV_W), prev_v_map),
                  pl.BlockSpec((bb, qb, SWA_KV_W), lambda i, j: (i, j, kcol)),
                  pl.BlockSpec((bb, qb, SWA_KV_W), lambda i, j: (i, j, vcol)),
                  _const_spec(bias.shape)],
        out_specs=pl.BlockSpec((bb, qb, SWA_W), lambda i, j: (i, j, 0)),
        out_shape=jax.ShapeDtypeStruct((b, t, SWA_W), F32),
        compiler_params=_cparams("parallel", "parallel"),
        name="swa",
    )(sinks, p_swa, k_prev, v_prev, p_swa, p_swa, bias)


def _out_proj_body(x_ref, ys_ref, og_ref, os_ref, wo_ref, gf_ref, wr_ref, br_ref, triu_ref, before_ref, x1_ref, xs_ref,
                   aux_ref, cnt_ref):
    x1 = (x_ref[...] + _dot(ys_ref[...].astype(BF16), wo_ref[0:SSM_W, :])
          + _dot(og_ref[...].astype(BF16), wo_ref[SSM_W:SSM_W + GLA_W, :])
          + _dot(os_ref[...].astype(BF16), wo_ref[SSM_W + GLA_W:MIX_W, :]))
    x1_ref[...] = x1
    xn = x1 * lax.rsqrt(jnp.mean(x1 * x1, axis=-1, keepdims=True) + RMS_EPS) * gf_ref[...]
    xb = xn.astype(BF16)
    logit = _dot(xb, wr_ref[...]) + br_ref[...]
    lt = jnp.transpose(logit)[:2 * N_EXPERTS, :]
    row_i = lax.broadcasted_iota(jnp.int32, lt.shape, 0)
    row = row_i.astype(F32)
    neg = -jnp.inf
    big = float(LANES)
    g_row = row_i - ROUTER_GROUP_LANE0
    lg = jnp.where((g_row >= 0) & (g_row < N_EXPERT_GROUPS), lt, neg)
    gmax = jnp.max(lg, axis=0, keepdims=True)
    p_sel = 1.0 / jnp.sum(jnp.exp(lg - gmax), axis=0, keepdims=True)
    g_idx = jnp.min(jnp.where(lg == gmax, g_row.astype(F32), big), axis=0, keepdims=True)
    row_group = (row_i >> _log2(EXPERTS_PER_GROUP)).astype(F32)
    in_group = (row_i < N_EXPERTS) & (row_group == g_idx)
    le = jnp.where(in_group, lt, neg)
    m1 = jnp.max(le, axis=0, keepdims=True)
    i1 = jnp.min(jnp.where(le == m1, row, big), axis=0, keepdims=True)
    le2 = jnp.where(row == i1, neg, le)
    m2 = jnp.max(le2, axis=0, keepdims=True)
    i2 = jnp.min(jnp.where(le2 == m2, row, big), axis=0, keepdims=True)
    r = jnp.exp(m2 - m1)
    w1 = 1.0 / (1.0 + r)
    exp_row = lax.broadcasted_iota(jnp.int32, (N_EXPERTS, MOE_TB), 0).astype(F32)
    s1 = jnp.where(exp_row == i1, 1.0, 0.0)
    s2 = jnp.where(exp_row == i2, 1.0, 0.0)
    both = s1 + s2
    cnt = jnp.sum(both, axis=1, keepdims=True) + jnp.zeros_like(both)
    seg = jnp.floor((cnt + (MOE_PIECE - 1)) * (1.0 / MOE_PIECE)) * MOE_PIECE
    seg_k = jnp.concatenate([seg, jnp.zeros((LANES - N_EXPERTS, MOE_TB), F32)], axis=0).astype(BF16)
    lo = _dot(before_ref[...], seg_k)
    rank = _dot(both.astype(BF16), triu_ref[...])
    pos = lo + rank
    d1 = jnp.sum(s1 * pos, axis=0, keepdims=True)
    d2 = jnp.sum(s2 * pos, axis=0, keepdims=True)
    r_iota = lax.broadcasted_iota(jnp.int32, (MOE_RLOC, MOE_TB), 0).astype(F32)
    perm = jnp.where((r_iota == d1) | (r_iota == d2), 1.0, 0.0).astype(BF16)
    xs_ref[...] = _dot(perm, xb).astype(BF16)
    arow = lax.broadcasted_iota(jnp.int32, (LANES, MOE_TB), 0)
    at = jnp.where(arow == AUX_W1, p_sel * w1, jnp.where(arow == AUX_W2, p_sel * (r * w1),
                   jnp.where(arow == AUX_D1, d1, jnp.where(arow == AUX_D2, d2, 0.0))))
    aux_ref[...] = jnp.transpose(at)
    cnt_ref[0] = cnt[:, :LANES]


def _out_proj(x, ys, og, osw, wo, gf, wr, br, triu, before):
    n = x.shape[0]
    nblk = n // MOE_TB
    row = lambda w: pl.BlockSpec((MOE_TB, w), lambda i: (i, 0))
    return pl.pallas_call(
        _out_proj_body,
        grid=(nblk,),
        in_specs=[row(D_MODEL), row(SSM_W), row(GLA_W), row(SWA_W), _const_spec(wo.shape), _const_spec(gf.shape),
                  _const_spec(wr.shape), _const_spec(br.shape), _const_spec(triu.shape), _const_spec(before.shape)],
        out_specs=[row(D_MODEL), pl.BlockSpec((MOE_RLOC, D_MODEL), lambda i: (i, 0)), row(LANES),
                   pl.BlockSpec((1, N_EXPERTS, LANES), lambda i: (i, 0, 0))],
        out_shape=[jax.ShapeDtypeStruct((n, D_MODEL), F32), jax.ShapeDtypeStruct((nblk * MOE_RLOC, D_MODEL), BF16),
                   jax.ShapeDtypeStruct((n, LANES), F32), jax.ShapeDtypeStruct((nblk, N_EXPERTS, LANES), F32)],
        compiler_params=_cparams("parallel"),
        name="out_proj",
    )(x, ys, og, osw, wo, gf, wr, br, triu, before)


def _piece_copy(hbm_ref, piece, buf_ref, slot, p, sem, to_hbm):
    start = lambda i: i * MOE_PIECE if isinstance(i, int) else pl.multiple_of(i * MOE_PIECE, MOE_PIECE)
    rows = pl.ds(start(piece), MOE_PIECE)
    vm = buf_ref.at[slot, pl.ds(start(p), MOE_PIECE)]
    if to_hbm:
        return pltpu.make_async_copy(vm, hbm_ref.at[rows], sem)
    return pltpu.make_async_copy(hbm_ref.at[rows], vm, sem)


def _moe_expert_body(texp_ref, piece_ref, nv_ref, xs_hbm, wg_ref, wu_ref, wd_ref, ys_hbm,
                     xbuf, ybuf, wgb, wub, wdb, sem_in, sem_out, *, pt, n_steps):
    i = pl.program_id(0)
    slot = i % 2

    def for_pieces(tile, s, hbm_ref, buf_ref, sem, to_hbm, wait):
        def one(p):
            piece = 0 if wait else piece_ref[tile * pt + p]
            c = _piece_copy(hbm_ref, piece, buf_ref, s, p, sem.at[s], to_hbm)
            c.wait() if wait else c.start()

        nv = nv_ref[tile]

        @pl.when(nv == pt)
        def _():
            for p in range(pt):
                one(p)

        @pl.when(nv < pt)
        def _():
            def body(p, carry):
                one(p)
                return carry
            lax.fori_loop(0, nv, body, 0)

    gather = lambda tile, s, wait: for_pieces(tile, s, xs_hbm, xbuf, sem_in, False, wait)
    scatter = lambda tile, s, wait: for_pieces(tile, s, ys_hbm, ybuf, sem_out, True, wait)

    @pl.when(i == 0)
    def _():
        xbuf[...] = jnp.zeros_like(xbuf)
        gather(0, 0, False)

    @pl.when(i + 1 < n_steps)
    def _():
        gather(i + 1, 1 - slot, False)

    gather(i, slot, True)

    @pl.when((i == 0) | (texp_ref[i] != texp_ref[jnp.maximum(i - 1, 0)]))
    def _():
        wgb[...] = wg_ref[0].astype(BF16)
        wub[...] = wu_ref[0].astype(BF16)
        wdb[...] = wd_ref[0].astype(BF16)

    @pl.when(nv_ref[i] > 0)
    def _():
        xb = xbuf[slot]
        a = _dot(xb, wgb[...])
        h = (a * _sigmoid(a)) * _dot(xb, wub[...])
        ybuf[slot] = _dot(h.astype(BF16), wdb[...]).astype(BF16)

    scatter(i, slot, False)

    @pl.when(i >= 1)
    def _():
        scatter(i - 1, 1 - slot, True)

    @pl.when(i == n_steps - 1)
    def _():
        scatter(i, slot, True)


def _moe_experts(tile_expert, piece, nvalid, xs, wg, wu, wd, layer, pt):
    n_steps = tile_expert.shape[0]
    tm = pt * MOE_PIECE
    wspec = lambda shape: pl.BlockSpec(shape, lambda i, te, pc, nv: (te[i] + layer * N_EXPERTS, 0, 0))
    grid_spec = pltpu.PrefetchScalarGridSpec(
        num_scalar_prefetch=3,
        grid=(n_steps,),
        in_specs=[pl.BlockSpec(memory_space=pl.ANY), wspec((1, D_MODEL, EXPERT_FF)), wspec((1, D_MODEL, EXPERT_FF)),
                  wspec((1, EXPERT_FF, D_MODEL))],
        out_specs=pl.BlockSpec(memory_space=pl.ANY),
        scratch_shapes=[pltpu.VMEM((2, tm, D_MODEL), BF16), pltpu.VMEM((2, tm, D_MODEL), BF16),
                        pltpu.VMEM((D_MODEL, EXPERT_FF), BF16), pltpu.VMEM((D_MODEL, EXPERT_FF), BF16),
                        pltpu.VMEM((EXPERT_FF, D_MODEL), BF16),
                        pltpu.SemaphoreType.DMA((2,)), pltpu.SemaphoreType.DMA((2,))],
    )
    return pl.pallas_call(
        functools.partial(_moe_expert_body, pt=pt, n_steps=n_steps),
        grid_spec=grid_spec,
        out_shape=jax.ShapeDtypeStruct(xs.shape, BF16),
        input_output_aliases={3: 0},
        compiler_params=_cparams("arbitrary"),
        name="moe_experts",
    )(tile_expert, piece, nvalid, xs, wg, wu, wd)


def _combine_math(ys_ref, aux_ref, x1_ref):
    col = lax.broadcasted_iota(jnp.int32, (MOE_TB, MOE_RLOC), 1).astype(F32)
    mix = jnp.where(col == aux_ref[:, AUX_D1:AUX_D1 + 1], aux_ref[:, AUX_W1:AUX_W1 + 1],
                    jnp.where(col == aux_ref[:, AUX_D2:AUX_D2 + 1], aux_ref[:, AUX_W2:AUX_W2 + 1], 0.0))
    return x1_ref[...] + _dot(mix.astype(BF16), ys_ref[...])


def _moe_combine_body(ys_ref, aux_ref, x1_ref, o_ref):
    o_ref[...] = _combine_math(ys_ref, aux_ref, x1_ref)


def _combine_in_proj_body(ys_ref, aux_ref, x1_ref, g_ref, w_ref, mavg_ref, qkg_ref, o_ref, gla_ref, swa_ref, ssm_ref):
    x = _combine_math(ys_ref, aux_ref, x1_ref)
    o_ref[...] = x
    _in_proj_math(x, g_ref, w_ref, mavg_ref, qkg_ref, gla_ref, swa_ref, ssm_ref)


def _combine_in_proj(ys, aux, x1, g, w, mavg, qkg):
    n = x1.shape[0]
    row = lambda wdt: pl.BlockSpec((MOE_TB, wdt), lambda i: (i, 0))
    return pl.pallas_call(
        _combine_in_proj_body,
        grid=(n // MOE_TB,),
        in_specs=[pl.BlockSpec((MOE_RLOC, D_MODEL), lambda i: (i, 0)), row(LANES), row(D_MODEL), _const_spec(g.shape),
                  _const_spec(w.shape), _const_spec(mavg.shape), _const_spec(qkg.shape)],
        out_specs=[row(D_MODEL), row(GLA_PW), row(SWA_PW), row(SSM_W)],
        out_shape=[jax.ShapeDtypeStruct((n, D_MODEL), F32), jax.ShapeDtypeStruct((n, GLA_PW), F32),
                   jax.ShapeDtypeStruct((n, SWA_PW), F32), jax.ShapeDtypeStruct((n, SSM_W), F32)],
        compiler_params=_cparams("parallel"),
        name="combine_in_proj",
    )(ys, aux, x1, g, w, mavg, qkg)


def _moe_combine(ys, aux, x1):
    n = x1.shape[0]
    return pl.pallas_call(
        _moe_combine_body,
        grid=(n // MOE_TB,),
        in_specs=[pl.BlockSpec((MOE_RLOC, D_MODEL), lambda i: (i, 0)), pl.BlockSpec((MOE_TB, LANES), lambda i: (i, 0)),
                  pl.BlockSpec((MOE_TB, D_MODEL), lambda i: (i, 0))],
        out_specs=pl.BlockSpec((MOE_TB, D_MODEL), lambda i: (i, 0)),
        out_shape=jax.ShapeDtypeStruct((n, D_MODEL), F32),
        compiler_params=_cparams("parallel"),
        name="moe_combine",
    )(ys, aux, x1)


def _route_tables(counts, pt):
    nblk = counts.shape[0]
    n = nblk * MOE_TB
    cnt = counts[:, :, 0].T.astype(jnp.int32)
    pc = (cnt + MOE_PIECE - 1) // MOE_PIECE
    lo_p = jnp.cumsum(pc, axis=0) - pc
    pe = pc.sum(1)
    tiles_e = (pe + pt - 1) // pt
    tile_start = jnp.cumsum(tiles_e) - tiles_e
    seg_start = (tile_start[:, None] * pt + jnp.cumsum(pc, axis=1) - pc).reshape(-1)
    pcs = pc.reshape(-1)
    seg_src = (jnp.arange(nblk, dtype=jnp.int32)[None, :] * MOE_PPB + lo_p).reshape(-1)
    n_steps = -(-(2 * n // MOE_PIECE + nblk * N_EXPERTS + N_EXPERTS * (pt - 1)) // pt)
    slot = jnp.arange(n_steps * pt, dtype=jnp.int32)[:, None]
    in_seg = (slot >= seg_start[None, :]) & (slot < (seg_start + pcs)[None, :])
    piece = jnp.sum(jnp.where(in_seg, seg_src[None, :] + slot - seg_start[None, :], 0), axis=1).astype(jnp.int32)
    tile = jnp.arange(n_steps, dtype=jnp.int32)[:, None]
    in_exp = (tile >= tile_start[None, :]) & (tile < (tile_start + tiles_e)[None, :])
    tile_expert = jnp.sum(jnp.where(in_exp, jnp.arange(N_EXPERTS, dtype=jnp.int32)[None, :], 0), axis=1)
    tile_expert = jnp.where(tile[:, 0] < tiles_e.sum(), tile_expert, N_EXPERTS - 1).astype(jnp.int32)
    nvalid = jnp.sum(jnp.where(in_exp, jnp.clip(pe[None, :] - (tile - tile_start[None, :]) * pt, 0, pt), 0),
                     axis=1).astype(jnp.int32)
    return tile_expert, piece, nvalid


def _moe(xs, counts, wg, wu, wd, layer, pt):
    tile_expert, piece, nvalid = _route_tables(counts, pt)
    return _moe_experts(tile_expert, piece, nvalid, xs, wg, wu, wd, layer, pt)


def _seg_mean_matrix(width, seg):
    i = jnp.arange(width)
    return jnp.where((i[:, None] // seg) == (i[None, :] // seg), 1.0 / seg, 0.0).astype(BF16)


def _chunk_tril(span, lc):
    i = jnp.arange(span)
    return ((i[:, None] >= i[None, :]) & ((i[:, None] // lc) == (i[None, :] // lc))).astype(BF16)


def _prep_consts(ssm_shapes, gla_shapes):
    ti = jnp.arange(MOE_TB)
    kk = jnp.arange(GLA_KP)
    spread = ((kk[None, :] % GLA_DK) == jnp.arange(GLA_DK)[:, None]) & (kk[None, :] < GLA_K)
    return {"mavg_qk": _seg_mean_matrix(SWA_W + SWA_KV_W, SWA_HEAD_DIM), "mavg_gla": _seg_mean_matrix(GLA_W, GLA_DV),
            "spread": spread.astype(BF16),
            "ssm_tril": {s: _chunk_tril(*s) for s in ssm_shapes}, "gla_tril": {s: _chunk_tril(*s) for s in gla_shapes},
            "moe_triu": (ti[:, None] < ti[None, :]).astype(BF16),
            "moe_before": (jnp.arange(LANES)[None, :] < jnp.arange(N_EXPERTS)[:, None]).astype(BF16)}


def _prep_layer(lw, ssm_chunks):
    w_in = lw["w_in"].astype(F32)
    cols = {}
    off = 0
    for name, wdt in (("u", SSM_W), ("qg", GLA_K), ("kg", GLA_K), ("vg", GLA_W), ("z", GLA_GATE_RANK), ("og", GLA_W),
                      ("qs", SWA_W), ("ks", SWA_KV_W), ("vs", SWA_KV_W)):
        cols[name] = w_in[:, off:off + wdt]
        off += wdt
    zpad = lambda wdt: jnp.zeros((D_MODEL, wdt), F32)
    w_all = jnp.concatenate(
        [cols["qg"], zpad(GLA_KP - GLA_K), cols["kg"], zpad(GLA_KP - GLA_K), cols["vg"], cols["og"], cols["z"],
         zpad(LANES - GLA_GATE_RANK), cols["qs"], cols["ks"], cols["vs"], cols["u"]], axis=1).astype(BF16)
    out = {"norm_mix": lw["norm_mix"].astype(F32)[None, :], "w_all": w_all,
           "qk_gain": jnp.concatenate([jnp.tile(lw["swa_q_norm"].astype(F32) * SWA_HEAD_DIM ** -0.5, SWA_HEADS),
                                       jnp.tile(lw["swa_k_norm"].astype(F32), SWA_KV_HEADS)])[None, :]}
    a_re = lw["ssm_a_re"].astype(F32)
    a_im = lw["ssm_a_im"].astype(F32)
    dt = jnp.exp(lw["ssm_log_dt"].astype(F32))[:, None]
    mag = jnp.exp(a_re * dt)
    ab_re = mag * jnp.cos(a_im * dt)
    ab_im = mag * jnp.sin(a_im * dt)
    den = a_re * a_re + a_im * a_im
    nr = ab_re - 1.0
    f_re = (nr * a_re + ab_im * a_im) / den
    f_im = (ab_im * a_re - nr * a_im) / den
    b_re = lw["ssm_b_re"].astype(F32)
    b_im = lw["ssm_b_im"].astype(F32)
    bb_re = f_re[..., None] * b_re - f_im[..., None] * b_im
    bb_im = f_re[..., None] * b_im + f_im[..., None] * b_re
    eye_g = jnp.eye(SSM_GROUPS, dtype=F32)
    blockdiag_in = lambda bb: jnp.einsum("gpc,gh->gchp", bb, eye_g).reshape(SSM_W, SSM_FLAT)
    blockdiag_out = lambda cc: jnp.einsum("gcp,gh->gphc", cc, eye_g).reshape(SSM_FLAT, SSM_W)
    bd = jnp.concatenate([blockdiag_in(bb_re), blockdiag_in(bb_im)], axis=1).astype(BF16)
    cd = jnp.concatenate([blockdiag_out(lw["ssm_c_re"].astype(F32)),
                          -blockdiag_out(lw["ssm_c_im"].astype(F32))], axis=0).astype(BF16)
    ssm = {"ab": jnp.concatenate([ab_re.reshape(1, SSM_FLAT), ab_im.reshape(1, SSM_FLAT),
                                  jnp.zeros((6, SSM_FLAT), F32)], axis=0),
           "bd": bd, "cd": cd, "d": lw["ssm_d"].astype(F32)[None, :], "wglu": lw["ssm_w_glu"].astype(BF16)}
    def powers(lc, sign):
        tt = jnp.arange(lc, dtype=F32)[:, None, None]
        m = jnp.exp(sign * tt * (a_re * dt)[None])
        ang = sign * tt * (a_im * dt)[None]
        return jnp.stack([(m * jnp.cos(ang)).reshape(lc, SSM_FLAT), (m * jnp.sin(ang)).reshape(lc, SSM_FLAT)])

    out["ssm"] = ssm
    out["ssm_pw"] = {lc: {"pwp": powers(lc, 1.0), "pwn": powers(lc, -1.0)} for lc in ssm_chunks}
    wg = jnp.zeros((LANES, GLA_KP), F32).at[:GLA_GATE_RANK, :GLA_K].set(lw["gla_w_gate"].astype(F32)).astype(BF16)
    bg = jnp.zeros((1, GLA_KP), F32).at[0, :GLA_K].set(lw["gla_b_gate"].astype(F32))
    out["gla"] = {"wg": wg, "bg": bg, "gn": jnp.tile(lw["gla_norm"].astype(F32), GLA_HEADS)[None, :]}
    out["sinks"] = lw["swa_sinks"].astype(F32)
    out["w_out"] = lw["w_out"].astype(BF16)
    out["norm_ffn"] = lw["norm_ffn"].astype(F32)[None, :]
    wr = jnp.zeros((D_MODEL, LANES), F32)
    wr = wr.at[:, :N_EXPERTS].set(lw["moe_w_expert"].astype(F32))
    wr = wr.at[:, ROUTER_GROUP_LANE0:ROUTER_GROUP_LANE0 + N_EXPERT_GROUPS].set(lw["moe_w_group"].astype(F32))
    out["wr"] = wr.astype(BF16)
    br = jnp.zeros((1, LANES), F32)
    br = br.at[0, :N_EXPERTS].set(lw["moe_b_expert"].astype(F32))
    br = br.at[0, ROUTER_GROUP_LANE0:ROUTER_GROUP_LANE0 + N_EXPERT_GROUPS].set(lw["moe_b_group"].astype(F32))
    out["br"] = br
    return out


def _gla_state_in(h0):
    return jnp.transpose(h0.astype(F32), (0, 1, 3, 2)).reshape(h0.shape[0], GLA_W, GLA_DK)


def _gla_state_out(st):
    return jnp.transpose(st.reshape(st.shape[0], GLA_HEADS, GLA_DV, GLA_DK), (0, 1, 3, 2))


def _layer(x, pending, pw, bias, ssm_h0r, ssm_h0i, gla_h0, past_k, past_v, cfg, b, t):
    n = b * t
    pc = pw["consts"]
    in_w = (pw["norm_mix"], pw["w_all"], pc["mavg_qk"], pw["qk_gain"])
    ssm_w = dict(pw["ssm"], **pw["ssm_pw"][cfg["ssm"][1]], tril=pc["ssm_tril"][cfg["ssm"]])
    gla_w = dict(pw["gla"], tril=pc["gla_tril"][cfg["gla"]], mavg=pc["mavg_gla"], spread=pc["spread"])
    if pending is None:
        p_gla, p_swa, p_ssm = _in_proj(x, *in_w, cfg["tm"])
    else:
        x, p_gla, p_swa, p_ssm = _combine_in_proj(*pending, *in_w)
    p_gla = p_gla.reshape(b, t, GLA_PW)
    p_swa = p_swa.reshape(b, t, SWA_PW)
    stack = cfg["stack"]
    indep = stack > 1
    grp = b // stack
    y_ssm, h_re, h_im = _ssm(p_ssm.reshape(grp, stack * t, SSM_W), ssm_h0r.reshape(grp, stack, SSM_FLAT).astype(F32),
                             ssm_h0i.reshape(grp, stack, SSM_FLAT).astype(F32), ssm_w, cfg["lblk"],
                             *cfg["ssm"], indep)
    o_g, s_fin = _gla(p_gla.reshape(grp, stack * t, GLA_PW), _gla_state_in(gla_h0).reshape(grp, stack, GLA_W, GLA_DK),
                      gla_w, cfg["lblk"], *cfg["gla"], indep)
    s_fin = s_fin.reshape(b, GLA_W, GLA_DK)
    kcol, vcol = SWA_W // SWA_KV_W, SWA_W // SWA_KV_W + 1
    if past_k is None:
        per_blk = cfg["qb"] // SWA_WINDOW
        prev_k_map = lambda i, j: (i, jnp.maximum(j * per_blk - 1, 0), kcol)
        prev_v_map = lambda i, j: (i, jnp.maximum(j * per_blk - 1, 0), vcol)
        k_prev, v_prev = p_swa, p_swa
        keep = min(SWA_WINDOW, t)
        new_k = p_swa[:, t - keep:, SWA_W:SWA_W + SWA_KV_W]
        new_v = p_swa[:, t - keep:, SWA_W + SWA_KV_W:]
    else:
        prev_k_map = prev_v_map = lambda i, j: (i, 0, 0)
        k_prev = past_k.reshape(b, SWA_WINDOW, SWA_KV_W).astype(F32)
        v_prev = past_v.reshape(b, SWA_WINDOW, SWA_KV_W).astype(F32)
        new_k = p_swa[:, :, SWA_W:SWA_W + SWA_KV_W]
        new_v = p_swa[:, :, SWA_W + SWA_KV_W:]
    o_s = _swa(pw["sinks"], p_swa, k_prev, v_prev, prev_k_map, prev_v_map, bias, stack, cfg["qb"], cfg["cq"],
               cfg["win"], cfg["pos0"])
    x1, xs, aux, counts = _out_proj(x, y_ssm.reshape(n, SSM_W), o_g.reshape(n, GLA_W),
                                    o_s.reshape(n, SWA_W), pw["w_out"], pw["norm_ffn"], pw["wr"], pw["br"],
                                    pc["moe_triu"], pc["moe_before"])
    ys = _moe(xs, counts, pw["moe_wg"], pw["moe_wu"], pw["moe_wd"], pw["layer"], cfg["pt"])
    kv_shape = (b, new_k.shape[1], SWA_KV_HEADS, SWA_HEAD_DIM)
    return ((ys, aux, x1), new_k.reshape(kv_shape), new_v.reshape(kv_shape), _gla_state_out(s_fin),
            h_re.reshape(b, SSM_GROUPS, SSM_STATE), h_im.reshape(b, SSM_GROUPS, SSM_STATE))


def _group_cfg(t, past_len, n_past):
    if n_past is None:
        assert t % (16 * CHUNK) == 0
        cfg = dict(ssm=(4 * CHUNK, CHUNK), gla=(4 * CHUNK, 2 * CHUNK), lblk=16 * CHUNK, stack=1, qb=8 * CHUNK, cq=CHUNK,
                   win=SWA_WINDOW + CHUNK, pos0=-SWA_WINDOW, tm=512, pt=64)
        q_pos = jnp.arange(CHUNK, dtype=jnp.int32) + SWA_WINDOW
        k_pos = jnp.arange(SWA_WINDOW + CHUNK, dtype=jnp.int32)
    else:
        first_key, last_q = past_len - n_past, past_len + t - 1
        assert n_past == SWA_WINDOW and t <= CHUNK and t % 8 == 0
        assert past_len // CHUNK == last_q // CHUNK and first_key // CHUNK >= past_len // CHUNK - SWA_WINDOW // CHUNK
        stack = DECODE_STACK
        cfg = dict(ssm=(stack * t, t), gla=(stack * t, t), lblk=stack * t, stack=stack, qb=t, cq=t, win=n_past + t,
                   pos0=first_key, tm=512, pt=16)
        q_pos = past_len + jnp.arange(t, dtype=jnp.int32)
        k_pos = first_key + jnp.arange(n_past + t, dtype=jnp.int32)
    return cfg, q_pos, k_pos


def _stacked_bias(t5_table, q_pos, k_pos):
    bias = _t5_bias(t5_table, q_pos, k_pos)
    nq, nk = bias.shape[1:]
    return bias.reshape(SWA_KV_HEADS, SWA_REP * nq, nk)


PAST_LEN = 1024


def kernel(x_prompt, x_sample, cache_swa_k, cache_swa_v, state_gla, state_ssm_re, state_ssm_im, norm_mix, w_in, ssm_a_re, ssm_a_im, ssm_log_dt, ssm_b_re, ssm_b_im, ssm_c_re, ssm_c_im, ssm_d, ssm_w_glu, gla_w_gate, gla_b_gate, gla_norm, swa_q_norm, swa_k_norm, swa_sinks, t5_table, w_out, norm_ffn, moe_w_group, moe_b_group, moe_w_expert, moe_b_expert, moe_w_gate, moe_w_up, moe_w_down):
    depth = w_in.shape[0]
    bp, tp, _ = x_prompt.shape
    bs, ts, _ = x_sample.shape
    cfg_p, qpos_p, kpos_p = _group_cfg(tp, 0, None)
    cfg_s, qpos_s, kpos_s = _group_cfg(ts, PAST_LEN, cache_swa_k.shape[2])
    bias_p = _stacked_bias(t5_table, qpos_p, kpos_p)
    bias_s = _stacked_bias(t5_table, qpos_s, kpos_s)
    xp = x_prompt.astype(F32).reshape(bp * tp, D_MODEL)
    xs = x_sample.astype(F32).reshape(bs * ts, D_MODEL)
    pend_p = pend_s = None
    moe_wg = moe_w_gate.astype(F32).reshape(depth * N_EXPERTS, D_MODEL, EXPERT_FF)
    moe_wu = moe_w_up.astype(F32).reshape(depth * N_EXPERTS, D_MODEL, EXPERT_FF)
    moe_wd = moe_w_down.astype(F32).reshape(depth * N_EXPERTS, EXPERT_FF, D_MODEL)
    outs = [[] for _ in range(10)]
    lw_all = {
        "norm_mix": norm_mix, "w_in": w_in, "ssm_a_re": ssm_a_re, "ssm_a_im": ssm_a_im, "ssm_log_dt": ssm_log_dt,
        "ssm_b_re": ssm_b_re, "ssm_b_im": ssm_b_im, "ssm_c_re": ssm_c_re, "ssm_c_im": ssm_c_im, "ssm_d": ssm_d,
        "ssm_w_glu": ssm_w_glu, "gla_w_gate": gla_w_gate, "gla_b_gate": gla_b_gate, "gla_norm": gla_norm,
        "swa_q_norm": swa_q_norm, "swa_k_norm": swa_k_norm, "swa_sinks": swa_sinks, "w_out": w_out,
        "norm_ffn": norm_ffn, "moe_w_group": moe_w_group, "moe_b_group": moe_b_group, "moe_w_expert": moe_w_expert,
        "moe_b_expert": moe_b_expert,
    }
    pw_all = jax.vmap(lambda lw: _prep_layer(lw, (cfg_p["ssm"][1], cfg_s["ssm"][1])))(lw_all)
    consts = _prep_consts({cfg_p["ssm"], cfg_s["ssm"]}, {cfg_p["gla"], cfg_s["gla"]})
    for l in range(depth):
        pw = jax.tree.map(lambda a: a[l], pw_all)
        pw.update(consts=consts, layer=l, moe_wg=moe_wg, moe_wu=moe_wu, moe_wd=moe_wd)
        zs = jnp.zeros((bp, SSM_GROUPS, SSM_STATE), F32)
        zg = jnp.zeros((bp, GLA_HEADS, GLA_DK, GLA_DV), F32)
        pend_p, nk, nv, ng, nr, ni = _layer(xp, pend_p, pw, bias_p, zs, zs, zg, None, None, cfg_p, bp, tp)
        for slot, val in zip((0, 1, 4, 6, 7), (nk, nv, ng, nr, ni)):
            outs[slot].append(val)
        pend_s, nk, nv, ng, nr, ni = _layer(xs, pend_s, pw, bias_s, state_ssm_re[l], state_ssm_im[l], state_gla[l],
                                            cache_swa_k[l], cache_swa_v[l], cfg_s, bs, ts)
        for slot, val in zip((2, 3, 5, 8, 9), (nk, nv, ng, nr, ni)):
            outs[slot].append(val)
    hp = _moe_combine(*pend_p).reshape(bp, tp, D_MODEL)
    hs = _moe_combine(*pend_s).reshape(bs, ts, D_MODEL)
    return (hp, hs) + tuple(jnp.stack(o) for o in outs)
```

```python
import functools
import math

import jax
import jax.numpy as jnp
from jax import lax
from jax.experimental import pallas as pl
from jax.experimental.pallas import tpu as pltpu

F32 = jnp.float32
BF16 = jnp.bfloat16

D_MODEL = 1024
CHUNK = 64
RMS_EPS = 1e-6
SSM_GROUPS = 16
SSM_GC = 16
SSM_STATE = 64
SSM_W = SSM_GROUPS * SSM_GC
SSM_FLAT = SSM_GROUPS * SSM_STATE
GLA_HEADS = 6
GLA_DK = 32
GLA_DV = 64
GLA_GATE_RANK = 16
GLA_GATE_NORM = 16.0
GLA_K = GLA_HEADS * GLA_DK
GLA_KP = 256
GLA_W = GLA_HEADS * GLA_DV
SWA_HEADS = 6
SWA_KV_HEADS = 2
SWA_REP = SWA_HEADS // SWA_KV_HEADS
SWA_HEAD_DIM = 64
SWA_WINDOW = 128
SWA_W = SWA_HEADS * SWA_HEAD_DIM
SWA_KV_W = SWA_KV_HEADS * SWA_HEAD_DIM
MIX_W = SSM_W + GLA_W + SWA_W
N_BUCKETS = 32
T5_MAX_DIST = 128
N_EXPERT_GROUPS = 4
EXPERTS_PER_GROUP = 4
N_EXPERTS = 16
EXPERT_FF = 512
LANES = 128
ROUTER_GROUP_LANE0 = N_EXPERTS
AUX_W1, AUX_W2, AUX_D1, AUX_D2 = 0, 1, 2, 3
AUX_ROWS = 8
DECODE_STACK = 8
MOE_TB = 512
MOE_PIECE = 16
MOE_RLOC = 2 * MOE_TB + N_EXPERTS * MOE_PIECE
MOE_PPB = MOE_RLOC // MOE_PIECE

GLA_Q0, GLA_K0, GLA_V0, GLA_OG0, GLA_Z0, GLA_PW = 0, 256, 512, 896, 1280, 1408
SWA_PW = SWA_W + 2 * SWA_KV_W
P_GLA0, P_SWA0, P_SSM0, P_TOTAL = 0, GLA_PW, GLA_PW + SWA_PW, GLA_PW + SWA_PW + SSM_W

VMEM_LIMIT = 48 * 1024 * 1024


def _cparams(*sem):
    return pltpu.CompilerParams(dimension_semantics=sem, vmem_limit_bytes=VMEM_LIMIT)


def _dot(a, b):
    return jnp.dot(a, b, preferred_element_type=F32)


def _dot_nt(a, b):
    return lax.dot_general(a, b, (((1,), (1,)), ((), ())), preferred_element_type=F32)


def _dot_tn(a, b):
    return lax.dot_general(a, b, (((0,), (0,)), ((), ())), preferred_element_type=F32)


def _hi_lo(x):
    hi = x.astype(BF16)
    return hi, (x - hi.astype(F32)).astype(BF16)


def _dot_f32_rhs(a_bf16, x):
    hi, lo = _hi_lo(x)
    return _dot(a_bf16, hi) + _dot(a_bf16, lo)


def _dot_f32_lhs(x, b_bf16):
    hi, lo = _hi_lo(x)
    return _dot(hi, b_bf16) + _dot(lo, b_bf16)


def _log2(n):
    assert n & (n - 1) == 0
    return n.bit_length() - 1


def _sigmoid(x):
    return 1.0 / (1.0 + jnp.exp(-x))


def _const_spec(shape):
    nd = len(shape)
    return pl.BlockSpec(shape, lambda *_: (0,) * nd)


def _in_proj_body(x_ref, g_ref, w_ref, mavg_ref, qkg_ref, gla_ref, swa_ref, ssm_ref):
    _in_proj_math(x_ref[...], g_ref, w_ref, mavg_ref, qkg_ref, gla_ref, swa_ref, ssm_ref)


def _in_proj_math(x, g_ref, w_ref, mavg_ref, qkg_ref, gla_ref, swa_ref, ssm_ref):
    xn = x * lax.rsqrt(jnp.mean(x * x, axis=-1, keepdims=True) + RMS_EPS) * g_ref[...]
    xb = xn.astype(BF16)
    gla_ref[...] = _dot(xb, w_ref[:, P_GLA0:P_SWA0])
    ssm_ref[...] = _dot(xb, w_ref[:, P_SSM0:P_TOTAL])
    s = _dot(xb, w_ref[:, P_SWA0:P_SSM0])
    qk = s[:, :SWA_W + SWA_KV_W]
    ms = _dot((qk * qk).astype(BF16), mavg_ref[...])
    swa_ref[:, :SWA_W + SWA_KV_W] = qk * lax.rsqrt(ms + RMS_EPS) * qkg_ref[...]
    swa_ref[:, SWA_W + SWA_KV_W:] = s[:, SWA_W + SWA_KV_W:]


def _in_proj(x, g, w, mavg, qkg, tm):
    n = x.shape[0]
    return pl.pallas_call(
        _in_proj_body,
        grid=(n // tm,),
        in_specs=[pl.BlockSpec((tm, D_MODEL), lambda i: (i, 0)), _const_spec(g.shape), _const_spec(w.shape),
                  _const_spec(mavg.shape), _const_spec(qkg.shape)],
        out_specs=[pl.BlockSpec((tm, GLA_PW), lambda i: (i, 0)), pl.BlockSpec((tm, SWA_PW), lambda i: (i, 0)),
                   pl.BlockSpec((tm, SSM_W), lambda i: (i, 0))],
        out_shape=[jax.ShapeDtypeStruct((n, GLA_PW), F32), jax.ShapeDtypeStruct((n, SWA_PW), F32),
                   jax.ShapeDtypeStruct((n, SSM_W), F32)],
        compiler_params=_cparams("parallel"),
        name="in_proj",
    )(x, g, w, mavg, qkg)


def _ssm_body(u_ref, h0r_ref, h0i_ref, ab_ref, bd_ref, cd_ref, pwp_ref, pwn_ref, tril_ref, d_ref, wglu_ref,
              y_ref, hr_ref, hi_ref, carry_ref, *, n_part, n_sub, lc, indep):
    if not indep:
        @pl.when(pl.program_id(1) == 0)
        def _():
            carry_ref[0:1, :] = h0r_ref[0]
            carry_ref[1:2, :] = h0i_ref[0]

    n_cb = SSM_FLAT // LANES
    span = n_sub * lc
    carries = None if indep else [(carry_ref[0:1, b * LANES:(b + 1) * LANES], carry_ref[1:2, b * LANES:(b + 1) * LANES])
                                  for b in range(n_cb)]
    for part in range(n_part):
        rows = slice(part * span, (part + 1) * span)
        u = u_ref[0, rows, :]
        ub = u.astype(BF16)
        y = d_ref[...] * u
        for b in range(n_cb):
            cols = slice(b * LANES, (b + 1) * LANES)
            wcols = slice(2 * b * LANES, 2 * (b + 1) * LANES)
            ab_re, ab_im = ab_ref[0:1, cols], ab_ref[1:2, cols]
            nr, ni = pwn_ref[0, :, cols], pwn_ref[1, :, cols]
            pr, pi = pwp_ref[0, :, cols], pwp_ref[1, :, cols]
            bu = _dot(ub, bd_ref[:, wcols])
            sr, si = [], []
            for c in range(n_sub):
                bur = bu[c * lc:(c + 1) * lc, :LANES]
                bui = bu[c * lc:(c + 1) * lc, LANES:]
                sr.append(nr * bur - ni * bui)
                si.append(nr * bui + ni * bur)
            scaled = jnp.concatenate([jnp.concatenate(sr, axis=0), jnp.concatenate(si, axis=0)], axis=1)
            cs = _dot(tril_ref[...], scaled.astype(BF16))
            h_r, h_i = [], []
            for c in range(n_sub):
                if indep:
                    hr, hi = h0r_ref[0, c:c + 1, cols], h0i_ref[0, c:c + 1, cols]
                else:
                    hr, hi = carries[b]
                cr = cs[c * lc:(c + 1) * lc, :LANES] + (ab_re * hr - ab_im * hi)
                ci = cs[c * lc:(c + 1) * lc, LANES:] + (ab_re * hi + ab_im * hr)
                h_r.append(pr * cr - pi * ci)
                h_i.append(pr * ci + pi * cr)
                hr = h_r[-1][lc - 1:lc, :]
                hi = h_i[-1][lc - 1:lc, :]
                if indep:
                    hr_ref[0, c:c + 1, cols] = hr
                    hi_ref[0, c:c + 1, cols] = hi
                else:
                    carries[b] = (hr, hi)
            hcat = jnp.concatenate([jnp.concatenate(h_r, axis=0), jnp.concatenate(h_i, axis=0)], axis=1)
            y = y + _dot(hcat.astype(BF16), cd_ref[wcols, :])
        g = 0.5 * y * (1.0 + jnp.tanh(math.sqrt(2.0 / math.pi) * (y + 0.044715 * (y * y * y))))
        y_ref[0, rows, :] = g * _sigmoid(_dot(g.astype(BF16), wglu_ref[...]))
    if not indep:
        for b in range(n_cb):
            cols = slice(b * LANES, (b + 1) * LANES)
            carry_ref[0:1, cols], carry_ref[1:2, cols] = carries[b]
            hr_ref[0, :, cols], hi_ref[0, :, cols] = carries[b]


def _ssm(u, h0r, h0i, sw, lblk, span, lc, indep):
    b, t, _ = u.shape
    n_state = span // lc if indep else 1
    assert not indep or (t == lblk == span)
    consts = [sw["ab"], sw["bd"], sw["cd"], sw["pwp"], sw["pwn"], sw["tril"], sw["d"], sw["wglu"]]
    state_spec = pl.BlockSpec((1, n_state, SSM_FLAT), lambda i, j: (i, 0, 0))
    return pl.pallas_call(
        functools.partial(_ssm_body, n_part=lblk // span, n_sub=span // lc, lc=lc, indep=indep),
        grid=(b, t // lblk),
        in_specs=[pl.BlockSpec((1, lblk, SSM_W), lambda i, j: (i, j, 0)), state_spec, state_spec]
        + [_const_spec(c.shape) for c in consts],
        out_specs=[pl.BlockSpec((1, lblk, SSM_W), lambda i, j: (i, j, 0)), state_spec, state_spec],
        out_shape=[jax.ShapeDtypeStruct((b, t, SSM_W), F32), jax.ShapeDtypeStruct((b, n_state, SSM_FLAT), F32),
                   jax.ShapeDtypeStruct((b, n_state, SSM_FLAT), F32)],
        scratch_shapes=[pltpu.VMEM((8, SSM_FLAT), F32)],
        compiler_params=_cparams("parallel", "arbitrary"),
        name="ssm",
    )(u, h0r, h0i, *consts)


def _gla_body(p_ref, s0_ref, wg_ref, bg_ref, tril_ref, mavg_ref, gn_ref, spread_ref, o_ref, sfin_ref, s_ref, *,
              n_part, n_sub, lc, indep):
    row_v = lax.broadcasted_iota(jnp.int32, (GLA_W, GLA_KP), 0)
    col_k = lax.broadcasted_iota(jnp.int32, (GLA_W, GLA_KP), 1)
    same_head = ((row_v >> _log2(GLA_DV)) == (col_k >> _log2(GLA_DK))).astype(F32)

    def spread(compact):
        return _dot_f32_lhs(compact, spread_ref[...]) * same_head

    def gather(full):
        hi, lo = _hi_lo(full)
        return _dot_nt(hi, spread_ref[...]) + _dot_nt(lo, spread_ref[...])

    if not indep:
        @pl.when(pl.program_id(1) == 0)
        def _():
            s_ref[...] = spread(s0_ref[0, 0])

    lane_k = lax.broadcasted_iota(jnp.int32, (1, GLA_KP), 1)
    lane_v = lax.broadcasted_iota(jnp.int32, (1, GLA_W), 1)
    head_k = [((lane_k >= h * GLA_DK) & (lane_k < (h + 1) * GLA_DK)).astype(F32) for h in range(GLA_HEADS)]
    head_v = [((lane_v >= h * GLA_DV) & (lane_v < (h + 1) * GLA_DV)).astype(F32) for h in range(GLA_HEADS)]
    row_t = lax.broadcasted_iota(jnp.int32, (GLA_HEADS * lc, lc), 0)
    col_s = lax.broadcasted_iota(jnp.int32, (GLA_HEADS * lc, lc), 1)
    causal = (row_t & (lc - 1)) >= col_s
    mid = lc // 2 - 1
    span = n_sub * lc
    st = None if indep else s_ref[...]
    for part in range(n_part):
        rows = slice(part * span, (part + 1) * span)
        z = p_ref[0, rows, GLA_Z0:GLA_Z0 + LANES]
        gin = _dot(z.astype(BF16), wg_ref[...]) + bg_ref[...]
        glog = (jnp.minimum(gin, 0.0) - jnp.log(1.0 + jnp.exp(-jnp.abs(gin)))) / GLA_GATE_NORM
        g_all = _dot_f32_rhs(tril_ref[...], glog)
        outs = []
        for c in range(n_sub):
            crow = slice(part * span + c * lc, part * span + (c + 1) * lc)
            q = p_ref[0, crow, GLA_Q0:GLA_Q0 + GLA_KP] * (GLA_DK ** -0.5)
            k = p_ref[0, crow, GLA_K0:GLA_K0 + GLA_KP]
            vb = p_ref[0, crow, GLA_V0:GLA_V0 + GLA_W].astype(BF16)
            if indep:
                st = spread(s0_ref[0, c])
            gc = g_all[c * lc:(c + 1) * lc, :]
            gl = gc[lc - 1:lc, :]
            gm = gc[mid:mid + 1, :]
            qc = (q * jnp.exp(gc - gm))
            ke = (k * jnp.exp(gm - gc)).astype(BF16)
            kd = (k * jnp.exp(gl - gc)).astype(BF16)
            qs = jnp.concatenate([qc * m for m in head_k], axis=0).astype(BF16)
            attn = jnp.where(causal, _dot_nt(qs, ke), 0.0)
            o2 = _dot(attn.astype(BF16), vb)
            o = _dot_nt((q * jnp.exp(gc)).astype(BF16), st.astype(BF16))
            for h in range(GLA_HEADS):
                o = o + head_v[h] * o2[h * lc:(h + 1) * lc, :]
            st = st * jnp.exp(gl) + _dot_tn(vb, kd) * same_head
            if indep:
                sfin_ref[0, c] = gather(st)
            outs.append(o)
        o = jnp.concatenate(outs, axis=0) if n_sub > 1 else outs[0]
        og = p_ref[0, rows, GLA_OG0:GLA_OG0 + GLA_W]
        ms = _dot((o * o).astype(BF16), mavg_ref[...])
        on = o * lax.rsqrt(ms + RMS_EPS) * gn_ref[...]
        o_ref[0, rows, :] = on * (og * _sigmoid(og))
    if not indep:
        s_ref[...] = st

        @pl.when(pl.program_id(1) == pl.num_programs(1) - 1)
        def _():
            sfin_ref[0, 0] = gather(st)


def _gla(p, s0, gw, lblk, span, lc, indep):
    b, t, _ = p.shape
    n_state = span // lc if indep else 1
    assert not indep or (t == lblk == span)
    consts = [gw["wg"], gw["bg"], gw["tril"], gw["mavg"], gw["gn"], gw["spread"]]
    st_spec = pl.BlockSpec((1, n_state, GLA_W, GLA_DK), lambda i, j: (i, 0, 0, 0))
    return pl.pallas_call(
        functools.partial(_gla_body, n_part=lblk // span, n_sub=span // lc, lc=lc, indep=indep),
        grid=(b, t // lblk),
        in_specs=[pl.BlockSpec((1, lblk, GLA_PW), lambda i, j: (i, j, 0)), st_spec]
        + [_const_spec(c.shape) for c in consts],
        out_specs=[pl.BlockSpec((1, lblk, GLA_W), lambda i, j: (i, j, 0)), st_spec],
        out_shape=[jax.ShapeDtypeStruct((b, t, GLA_W), F32),
                   jax.ShapeDtypeStruct((b, n_state, GLA_W, GLA_DK), F32)],
        scratch_shapes=[pltpu.VMEM((GLA_W, GLA_KP), F32)],
        compiler_params=_cparams("parallel", "arbitrary"),
        name="gla",
    )(p, s0, *consts)


def _t5_bias(t5_table, q_pos, k_pos):
    nq, nk = q_pos.shape[0], k_pos.shape[0]
    rel = (k_pos[0] - q_pos[0]) + jnp.arange(-(nq - 1), nk, dtype=jnp.int32)
    half = N_BUCKETS // 2
    max_exact = half // 2
    n = jnp.abs(rel)
    far = max_exact + (jnp.log(jnp.maximum(n, 1).astype(jnp.float32) / max_exact)
                       / math.log(T5_MAX_DIST / max_exact) * (half - max_exact)).astype(jnp.int32)
    bucket = jnp.where(rel > 0, half, 0) + jnp.where(n < max_exact, n, jnp.minimum(far, half - 1))
    by_dist = t5_table.astype(F32)[bucket].T
    return jnp.stack([by_dist[:, nq - 1 - i:nq - 1 - i + nk] for i in range(nq)], axis=1)


def _swa_body(sink_ref, q_ref, kp_ref, vp_ref, kc_ref, vc_ref, bias_ref, o_ref, *, bb, qb, cq, win, pos0):
    blk = pl.program_id(1)
    col = lax.broadcasted_iota(jnp.int32, (1, win), 1)
    row = lax.broadcasted_iota(jnp.int32, (SWA_REP * cq, 1), 0)
    for b in range(bb):
        kwin = jnp.concatenate([kp_ref[b], kc_ref[b]], axis=0)
        vwin = jnp.concatenate([vp_ref[b], vc_ref[b]], axis=0)
        for j in range(qb // cq):
            valid = (pos0 + blk * qb + j * cq + col) >= 0
            for g in range(SWA_KV_HEADS):
                heads = [SWA_REP * g + r for r in range(SWA_REP)]
                qs = jnp.concatenate(
                    [q_ref[b, j * cq:(j + 1) * cq, h * SWA_HEAD_DIM:(h + 1) * SWA_HEAD_DIM] for h in heads], axis=0)
                kk = kwin[j * cq:j * cq + win, g * SWA_HEAD_DIM:(g + 1) * SWA_HEAD_DIM]
                vv = vwin[j * cq:j * cq + win, g * SWA_HEAD_DIM:(g + 1) * SWA_HEAD_DIM]
                s = _dot_nt(qs.astype(BF16), kk.astype(BF16)) + bias_ref[g]
                if pos0 + j * cq < 0:
                    s = jnp.where(valid, s, -1e30)
                sink = jnp.where(row < cq, sink_ref[heads[0]],
                                 jnp.where(row < 2 * cq, sink_ref[heads[1]], sink_ref[heads[2]]))
                m = jnp.maximum(jnp.max(s, axis=-1, keepdims=True), sink)
                e = jnp.exp(s - m)
                den = jnp.sum(e, axis=-1, keepdims=True) + jnp.exp(sink - m)
                o = _dot(e.astype(BF16), vv.astype(BF16)) * (1.0 / den)
                for r, h in enumerate(heads):
                    o_ref[b, j * cq:(j + 1) * cq, h * SWA_HEAD_DIM:(h + 1) * SWA_HEAD_DIM] = o[r * cq:(r + 1) * cq, :]


def _swa(sinks, p_swa, k_prev, v_prev, prev_k_map, prev_v_map, bias, bb, qb, cq, win, pos0):
    b, t, _ = p_swa.shape
    kcol, vcol = SWA_W // SWA_KV_W, SWA_W // SWA_KV_W + 1
    return pl.pallas_call(
        functools.partial(_swa_body, bb=bb, qb=qb, cq=cq, win=win, pos0=pos0),
        grid=(b // bb, t // qb),
        in_specs=[pl.BlockSpec(memory_space=pltpu.SMEM),
                  pl.BlockSpec((bb, qb, SWA_W), lambda i, j: (i, j, 0)),
                  pl.BlockSpec((bb, SWA_WINDOW, SWA_KV_W), prev_k_map),
                  pl.BlockSpec((bb, SWA_WINDOW, SWA_KV_W), prev_v_map),
                  pl.BlockSpec((bb, qb, SWA_KV_W), lambda i, j: (i, j, kcol)),
                  pl.BlockSpec((bb, qb, SWA_KV_W), lambda i, j: (i, j, vcol)),
                  _const_spec(bias.shape)],
        out_specs=pl.BlockSpec((bb, qb, SWA_W), lambda i, j: (i, j, 0)),
        out_shape=jax.ShapeDtypeStruct((b, t, SWA_W), F32),
        compiler_params=_cparams("parallel", "parallel"),
        name="swa",
    )(sinks, p_swa, k_prev, v_prev, p_swa, p_swa, bias)


def _out_proj_body(x_ref, ys_ref, og_ref, os_ref, wo_ref, gf_ref, wr_ref, br_ref, triu_ref, before_ref, x1_ref, xs_ref,
                   aux_ref, cnt_ref):
    x1 = (x_ref[...] + _dot(ys_ref[...].astype(BF16), wo_ref[0:SSM_W, :])
          + _dot(og_ref[...].astype(BF16), wo_ref[SSM_W:SSM_W + GLA_W, :])
          + _dot(os_ref[...].astype(BF16), wo_ref[SSM_W + GLA_W:MIX_W, :]))
    x1_ref[...] = x1
    xn = x1 * lax.rsqrt(jnp.mean(x1 * x1, axis=-1, keepdims=True) + RMS_EPS) * gf_ref[...]
    xb = xn.astype(BF16)
    logit = _dot(xb, wr_ref[...]) + br_ref[...]
    lt = jnp.transpose(logit)[:2 * N_EXPERTS, :]
    row_i = lax.broadcasted_iota(jnp.int32, lt.shape, 0)
    row = row_i.astype(F32)
    neg = -jnp.inf
    big = float(LANES)
    g_row = row_i - ROUTER_GROUP_LANE0
    lg = jnp.where((g_row >= 0) & (g_row < N_EXPERT_GROUPS), lt, neg)
    gmax = jnp.max(lg, axis=0, keepdims=True)
    p_sel = 1.0 / jnp.sum(jnp.exp(lg - gmax), axis=0, keepdims=True)
    g_idx = jnp.min(jnp.where(lg == gmax, g_row.astype(F32), big), axis=0, keepdims=True)
    row_group = (row_i >> _log2(EXPERTS_PER_GROUP)).astype(F32)
    in_group = (row_i < N_EXPERTS) & (row_group == g_idx)
    le = jnp.where(in_group, lt, neg)
    m1 = jnp.max(le, axis=0, keepdims=True)
    i1 = jnp.min(jnp.where(le == m1, row, big), axis=0, keepdims=True)
    le2 = jnp.where(row == i1, neg, le)
    m2 = jnp.max(le2, axis=0, keepdims=True)
    i2 = jnp.min(jnp.where(le2 == m2, row, big), axis=0, keepdims=True)
    r = jnp.exp(m2 - m1)
    w1 = 1.0 / (1.0 + r)
    exp_row = lax.broadcasted_iota(jnp.int32, (N_EXPERTS, MOE_TB), 0).astype(F32)
    s1 = jnp.where(exp_row == i1, 1.0, 0.0)
    s2 = jnp.where(exp_row == i2, 1.0, 0.0)
    both = s1 + s2
    cnt = jnp.sum(both, axis=1, keepdims=True) + jnp.zeros_like(both)
    seg = jnp.floor((cnt + (MOE_PIECE - 1)) * (1.0 / MOE_PIECE)) * MOE_PIECE
    seg_k = jnp.concatenate([seg, jnp.zeros((LANES - N_EXPERTS, MOE_TB), F32)], axis=0).astype(BF16)
    lo = _dot(before_ref[...], seg_k)
    rank = _dot(both.astype(BF16), triu_ref[...])
    pos = lo + rank
    d1 = jnp.sum(s1 * pos, axis=0, keepdims=True)
    d2 = jnp.sum(s2 * pos, axis=0, keepdims=True)
    r_iota = lax.broadcasted_iota(jnp.int32, (MOE_RLOC, MOE_TB), 0).astype(F32)
    perm = jnp.where((r_iota == d1) | (r_iota == d2), 1.0, 0.0).astype(BF16)
    xs_ref[...] = _dot(perm, xb).astype(BF16)
    arow = lax.broadcasted_iota(jnp.int32, (LANES, MOE_TB), 0)
    at = jnp.where(arow == AUX_W1, p_sel * w1, jnp.where(arow == AUX_W2, p_sel * (r * w1),
                   jnp.where(arow == AUX_D1, d1, jnp.where(arow == AUX_D2, d2, 0.0))))
    aux_ref[...] = jnp.transpose(at)
    cnt_ref[0] = cnt[:, :LANES]


def _out_proj(x, ys, og, osw, wo, gf, wr, br, triu, before):
    n = x.shape[0]
    nblk = n // MOE_TB
    row = lambda w: pl.BlockSpec((MOE_TB, w), lambda i: (i, 0))
    return pl.pallas_call(
        _out_proj_body,
        grid=(nblk,),
        in_specs=[row(D_MODEL), row(SSM_W), row(GLA_W), row(SWA_W), _const_spec(wo.shape), _const_spec(gf.shape),
                  _const_spec(wr.shape), _const_spec(br.shape), _const_spec(triu.shape), _const_spec(before.shape)],
        out_specs=[row(D_MODEL), pl.BlockSpec((MOE_RLOC, D_MODEL), lambda i: (i, 0)), row(LANES),
                   pl.BlockSpec((1, N_EXPERTS, LANES), lambda i: (i, 0, 0))],
        out_shape=[jax.ShapeDtypeStruct((n, D_MODEL), F32), jax.ShapeDtypeStruct((nblk * MOE_RLOC, D_MODEL), BF16),
                   jax.ShapeDtypeStruct((n, LANES), F32), jax.ShapeDtypeStruct((nblk, N_EXPERTS, LANES), F32)],
        compiler_params=_cparams("parallel"),
        name="out_proj",
    )(x, ys, og, osw, wo, gf, wr, br, triu, before)


def _piece_copy(hbm_ref, piece, buf_ref, slot, p, sem, to_hbm):
    start = lambda i: i * MOE_PIECE if isinstance(i, int) else pl.multiple_of(i * MOE_PIECE, MOE_PIECE)
    rows = pl.ds(start(piece), MOE_PIECE)
    vm = buf_ref.at[slot, pl.ds(start(p), MOE_PIECE)]
    if to_hbm:
        return pltpu.make_async_copy(vm, hbm_ref.at[rows], sem)
    return pltpu.make_async_copy(hbm_ref.at[rows], vm, sem)


def _moe_expert_body(texp_ref, piece_ref, nv_ref, xs_hbm, wg_ref, wu_ref, wd_ref, ys_hbm,
                     xbuf, ybuf, wgb, wub, wdb, sem_in, sem_out, *, pt, n_steps):
    i = pl.program_id(0)
    slot = i % 2

    def for_pieces(tile, s, hbm_ref, buf_ref, sem, to_hbm, wait):
        def one(p):
            piece = 0 if wait else piece_ref[tile * pt + p]
            c = _piece_copy(hbm_ref, piece, buf_ref, s, p, sem.at[s], to_hbm)
            c.wait() if wait else c.start()

        nv = nv_ref[tile]

        @pl.when(nv == pt)
        def _():
            for p in range(pt):
                one(p)

        @pl.when(nv < pt)
        def _():
            def body(p, carry):
                one(p)
                return carry
            lax.fori_loop(0, nv, body, 0)

    gather = lambda tile, s, wait: for_pieces(tile, s, xs_hbm, xbuf, sem_in, False, wait)
    scatter = lambda tile, s, wait: for_pieces(tile, s, ys_hbm, ybuf, sem_out, True, wait)

    @pl.when(i == 0)
    def _():
        xbuf[...] = jnp.zeros_like(xbuf)
        gather(0, 0, False)

    @pl.when(i + 1 < n_steps)
    def _():
        gather(i + 1, 1 - slot, False)

    gather(i, slot, True)

    @pl.when((i == 0) | (texp_ref[i] != texp_ref[jnp.maximum(i - 1, 0)]))
    def _():
        wgb[...] = wg_ref[0].astype(BF16)
        wub[...] = wu_ref[0].astype(BF16)
        wdb[...] = wd_ref[0].astype(BF16)

    @pl.when(nv_ref[i] > 0)
    def _():
        xb = xbuf[slot]
        a = _dot(xb, wgb[...])
        h = (a * _sigmoid(a)) * _dot(xb, wub[...])
        ybuf[slot] = _dot(h.astype(BF16), wdb[...]).astype(BF16)

    scatter(i, slot, False)

    @pl.when(i >= 1)
    def _():
        scatter(i - 1, 1 - slot, True)

    @pl.when(i == n_steps - 1)
    def _():
        scatter(i, slot, True)


def _moe_experts(tile_expert, piece, nvalid, xs, wg, wu, wd, layer, pt):
    n_steps = tile_expert.shape[0]
    tm = pt * MOE_PIECE
    wspec = lambda shape: pl.BlockSpec(shape, lambda i, te, pc, nv: (te[i] + layer * N_EXPERTS, 0, 0))
    grid_spec = pltpu.PrefetchScalarGridSpec(
        num_scalar_prefetch=3,
        grid=(n_steps,),
        in_specs=[pl.BlockSpec(memory_space=pl.ANY), wspec((1, D_MODEL, EXPERT_FF)), wspec((1, D_MODEL, EXPERT_FF)),
                  wspec((1, EXPERT_FF, D_MODEL))],
        out_specs=pl.BlockSpec(memory_space=pl.ANY),
        scratch_shapes=[pltpu.VMEM((2, tm, D_MODEL), BF16), pltpu.VMEM((2, tm, D_MODEL), BF16),
                        pltpu.VMEM((D_MODEL, EXPERT_FF), BF16), pltpu.VMEM((D_MODEL, EXPERT_FF), BF16),
                        pltpu.VMEM((EXPERT_FF, D_MODEL), BF16),
                        pltpu.SemaphoreType.DMA((2,)), pltpu.SemaphoreType.DMA((2,))],
    )
    return pl.pallas_call(
        functools.partial(_moe_expert_body, pt=pt, n_steps=n_steps),
        grid_spec=grid_spec,
        out_shape=jax.ShapeDtypeStruct(xs.shape, BF16),
        input_output_aliases={3: 0},
        compiler_params=_cparams("arbitrary"),
        name="moe_experts",
    )(tile_expert, piece, nvalid, xs, wg, wu, wd)


def _combine_math(ys_ref, aux_ref, x1_ref):
    col = lax.broadcasted_iota(jnp.int32, (MOE_TB, MOE_RLOC), 1).astype(F32)
    mix = jnp.where(col == aux_ref[:, AUX_D1:AUX_D1 + 1], aux_ref[:, AUX_W1:AUX_W1 + 1],
                    jnp.where(col == aux_ref[:, AUX_D2:AUX_D2 + 1], aux_ref[:, AUX_W2:AUX_W2 + 1], 0.0))
    return x1_ref[...] + _dot(mix.astype(BF16), ys_ref[...])


def _moe_combine_body(ys_ref, aux_ref, x1_ref, o_ref):
    o_ref[...] = _combine_math(ys_ref, aux_ref, x1_ref)


def _combine_in_proj_body(ys_ref, aux_ref, x1_ref, g_ref, w_ref, mavg_ref, qkg_ref, o_ref, gla_ref, swa_ref, ssm_ref):
    x = _combine_math(ys_ref, aux_ref, x1_ref)
    o_ref[...] = x
    _in_proj_math(x, g_ref, w_ref, mavg_ref, qkg_ref, gla_ref, swa_ref, ssm_ref)


def _combine_in_proj(ys, aux, x1, g, w, mavg, qkg):
    n = x1.shape[0]
    row = lambda wdt: pl.BlockSpec((MOE_TB, wdt), lambda i: (i, 0))
    return pl.pallas_call(
        _combine_in_proj_body,
        grid=(n // MOE_TB,),
        in_specs=[pl.BlockSpec((MOE_RLOC, D_MODEL), lambda i: (i, 0)), row(LANES), row(D_MODEL), _const_spec(g.shape),
                  _const_spec(w.shape), _const_spec(mavg.shape), _const_spec(qkg.shape)],
        out_specs=[row(D_MODEL), row(GLA_PW), row(SWA_PW), row(SSM_W)],
        out_shape=[jax.ShapeDtypeStruct((n, D_MODEL), F32), jax.ShapeDtypeStruct((n, GLA_PW), F32),
                   jax.ShapeDtypeStruct((n, SWA_PW), F32), jax.ShapeDtypeStruct((n, SSM_W), F32)],
        compiler_params=_cparams("parallel"),
        name="combine_in_proj",
    )(ys, aux, x1, g, w, mavg, qkg)


def _moe_combine(ys, aux, x1):
    n = x1.shape[0]
    return pl.pallas_call(
        _moe_combine_body,
        grid=(n // MOE_TB,),
        in_specs=[pl.BlockSpec((MOE_RLOC, D_MODEL), lambda i: (i, 0)), pl.BlockSpec((MOE_TB, LANES), lambda i: (i, 0)),
                  pl.BlockSpec((MOE_TB, D_MODEL), lambda i: (i, 0))],
        out_specs=pl.BlockSpec((MOE_TB, D_MODEL), lambda i: (i, 0)),
        out_shape=jax.ShapeDtypeStruct((n, D_MODEL), F32),
        compiler_params=_cparams("parallel"),
        name="moe_combine",
    )(ys, aux, x1)


def _route_tables(counts, pt):
    nblk = counts.shape[0]
    n = nblk * MOE_TB
    cnt = counts[:, :, 0].T.astype(jnp.int32)
    pc = (cnt + MOE_PIECE - 1) // MOE_PIECE
    lo_p = jnp.cumsum(pc, axis=0) - pc
    pe = pc.sum(1)
    tiles_e = (pe + pt - 1) // pt
    tile_start = jnp.cumsum(tiles_e) - tiles_e
    seg_start = (tile_start[:, None] * pt + jnp.cumsum(pc, axis=1) - pc).reshape(-1)
    pcs = pc.reshape(-1)
    seg_src = (jnp.arange(nblk, dtype=jnp.int32)[None, :] * MOE_PPB + lo_p).reshape(-1)
    n_steps = -(-(2 * n // MOE_PIECE + nblk * N_EXPERTS + N_EXPERTS * (pt - 1)) // pt)
    slot = jnp.arange(n_steps * pt, dtype=jnp.int32)[:, None]
    in_seg = (slot >= seg_start[None, :]) & (slot < (seg_start + pcs)[None, :])
    piece = jnp.sum(jnp.where(in_seg, seg_src[None, :] + slot - seg_start[None, :], 0), axis=1).astype(jnp.int32)
    tile = jnp.arange(n_steps, dtype=jnp.int32)[:, None]
    in_exp = (tile >= tile_start[None, :]) & (tile < (tile_start + tiles_e)[None, :])
    tile_expert = jnp.sum(jnp.where(in_exp, jnp.arange(N_EXPERTS, dtype=jnp.int32)[None, :], 0), axis=1)
    tile_expert = jnp.where(tile[:, 0] < tiles_e.sum(), tile_expert, N_EXPERTS - 1).astype(jnp.int32)
    nvalid = jnp.sum(jnp.where(in_exp, jnp.clip(pe[None, :] - (tile - tile_start[None, :]) * pt, 0, pt), 0),
                     axis=1).astype(jnp.int32)
    return tile_expert, piece, nvalid


def _moe(xs, counts, wg, wu, wd, layer, pt):
    tile_expert, piece, nvalid = _route_tables(counts, pt)
    return _moe_experts(tile_expert, piece, nvalid, xs, wg, wu, wd, layer, pt)


def _seg_mean_matrix(width, seg):
    i = jnp.arange(width)
    return jnp.where((i[:, None] // seg) == (i[None, :] // seg), 1.0 / seg, 0.0).astype(BF16)


def _chunk_tril(span, lc):
    i = jnp.arange(span)
    return ((i[:, None] >= i[None, :]) & ((i[:, None] // lc) == (i[None, :] // lc))).astype(BF16)


def _prep_consts(ssm_shapes, gla_shapes):
    ti = jnp.arange(MOE_TB)
    kk = jnp.arange(GLA_KP)
    spread = ((kk[None, :] % GLA_DK) == jnp.arange(GLA_DK)[:, None]) & (kk[None, :] < GLA_K)
    return {"mavg_qk": _seg_mean_matrix(SWA_W + SWA_KV_W, SWA_HEAD_DIM), "mavg_gla": _seg_mean_matrix(GLA_W, GLA_DV),
            "spread": spread.astype(BF16),
            "ssm_tril": {s: _chunk_tril(*s) for s in ssm_shapes}, "gla_tril": {s: _chunk_tril(*s) for s in gla_shapes},
            "moe_triu": (ti[:, None] < ti[None, :]).astype(BF16),
            "moe_before": (jnp.arange(LANES)[None, :] < jnp.arange(N_EXPERTS)[:, None]).astype(BF16)}


def _prep_layer(lw, ssm_chunks):
    w_in = lw["w_in"].astype(F32)
    cols = {}
    off = 0
    for name, wdt in (("u", SSM_W), ("qg", GLA_K), ("kg", GLA_K), ("vg", GLA_W), ("z", GLA_GATE_RANK), ("og", GLA_W),
                      ("qs", SWA_W), ("ks", SWA_KV_W), ("vs", SWA_KV_W)):
        cols[name] = w_in[:, off:off + wdt]
        off += wdt
    zpad = lambda wdt: jnp.zeros((D_MODEL, wdt), F32)
    w_all = jnp.concatenate(
        [cols["qg"], zpad(GLA_KP - GLA_K), cols["kg"], zpad(GLA_KP - GLA_K), cols["vg"], cols["og"], cols["z"],
         zpad(LANES - GLA_GATE_RANK), cols["qs"], cols["ks"], cols["vs"], cols["u"]], axis=1).astype(BF16)
    out = {"norm_mix": lw["norm_mix"].astype(F32)[None, :], "w_all": w_all,
           "qk_gain": jnp.concatenate([jnp.tile(lw["swa_q_norm"].astype(F32) * SWA_HEAD_DIM ** -0.5, SWA_HEADS),
                                       jnp.tile(lw["swa_k_norm"].astype(F32), SWA_KV_HEADS)])[None, :]}
    a_re = lw["ssm_a_re"].astype(F32)
    a_im = lw["ssm_a_im"].astype(F32)
    dt = jnp.exp(lw["ssm_log_dt"].astype(F32))[:, None]
    mag = jnp.exp(a_re * dt)
    ab_re = mag * jnp.cos(a_im * dt)
    ab_im = mag * jnp.sin(a_im * dt)
    den = a_re * a_re + a_im * a_im
    nr = ab_re - 1.0
    f_re = (nr * a_re + ab_im * a_im) / den
    f_im = (ab_im * a_re - nr * a_im) / den
    b_re = lw["ssm_b_re"].astype(F32)
    b_im = lw["ssm_b_im"].astype(F32)
    bb_re = f_re[..., None] * b_re - f_im[..., None] * b_im
    bb_im = f_re[..., None] * b_im + f_im[..., None] * b_re
    eye_g = jnp.eye(SSM_GROUPS, dtype=F32)
    blockdiag_in = lambda bb: jnp.einsum("gpc,gh->gchp", bb, eye_g).reshape(SSM_W, SSM_FLAT)
    blockdiag_out = lambda cc: jnp.einsum("gcp,gh->gphc", cc, eye_g).reshape(SSM_FLAT, SSM_W)
    n_cb = SSM_FLAT // LANES
    bd = jnp.stack([blockdiag_in(bb_re).reshape(SSM_W, n_cb, LANES), blockdiag_in(bb_im).reshape(SSM_W, n_cb, LANES)],
                   axis=2).reshape(SSM_W, 2 * SSM_FLAT).astype(BF16)
    cd = jnp.stack([blockdiag_out(lw["ssm_c_re"].astype(F32)).reshape(n_cb, LANES, SSM_W),
                    -blockdiag_out(lw["ssm_c_im"].astype(F32)).reshape(n_cb, LANES, SSM_W)],
                   axis=1).reshape(2 * SSM_FLAT, SSM_W).astype(BF16)
    ssm = {"ab": jnp.concatenate([ab_re.reshape(1, SSM_FLAT), ab_im.reshape(1, SSM_FLAT),
                                  jnp.zeros((6, SSM_FLAT), F32)], axis=0),
           "bd": bd, "cd": cd, "d": lw["ssm_d"].astype(F32)[None, :], "wglu": lw["ssm_w_glu"].astype(BF16)}
    def powers(lc, sign):
        tt = jnp.arange(lc, dtype=F32)[:, None, None]
        m = jnp.exp(sign * tt * (a_re * dt)[None])
        ang = sign * tt * (a_im * dt)[None]
        return jnp.stack([(m * jnp.cos(ang)).reshape(lc, SSM_FLAT), (m * jnp.sin(ang)).reshape(lc, SSM_FLAT)])

    out["ssm"] = ssm
    out["ssm_pw"] = {lc: {"pwp": powers(lc, 1.0), "pwn": powers(lc, -1.0)} for lc in ssm_chunks}
    wg = jnp.zeros((LANES, GLA_KP), F32).at[:GLA_GATE_RANK, :GLA_K].set(lw["gla_w_gate"].astype(F32)).astype(BF16)
    bg = jnp.zeros((1, GLA_KP), F32).at[0, :GLA_K].set(lw["gla_b_gate"].astype(F32))
    out["gla"] = {"wg": wg, "bg": bg, "gn": jnp.tile(lw["gla_norm"].astype(F32), GLA_HEADS)[None, :]}
    out["sinks"] = lw["swa_sinks"].astype(F32)
    out["w_out"] = lw["w_out"].astype(BF16)
    out["norm_ffn"] = lw["norm_ffn"].astype(F32)[None, :]
    wr = jnp.zeros((D_MODEL, LANES), F32)
    wr = wr.at[:, :N_EXPERTS].set(lw["moe_w_expert"].astype(F32))
    wr = wr.at[:, ROUTER_GROUP_LANE0:ROUTER_GROUP_LANE0 + N_EXPERT_GROUPS].set(lw["moe_w_group"].astype(F32))
    out["wr"] = wr.astype(BF16)
    br = jnp.zeros((1, LANES), F32)
    br = br.at[0, :N_EXPERTS].set(lw["moe_b_expert"].astype(F32))
    br = br.at[0, ROUTER_GROUP_LANE0:ROUTER_GROUP_LANE0 + N_EXPERT_GROUPS].set(lw["moe_b_group"].astype(F32))
    out["br"] = br
    return out


def _gla_state_in(h0):
    return jnp.transpose(h0.astype(F32), (0, 1, 3, 2)).reshape(h0.shape[0], GLA_W, GLA_DK)


def _gla_state_out(st):
    return jnp.transpose(st.reshape(st.shape[0], GLA_HEADS, GLA_DV, GLA_DK), (0, 1, 3, 2))


def _layer(x, pending, pw, bias, ssm_h0r, ssm_h0i, gla_h0, past_k, past_v, cfg, b, t):
    n = b * t
    pc = pw["consts"]
    in_w = (pw["norm_mix"], pw["w_all"], pc["mavg_qk"], pw["qk_gain"])
    ssm_w = dict(pw["ssm"], **pw["ssm_pw"][cfg["ssm"][1]], tril=pc["ssm_tril"][cfg["ssm"]])
    gla_w = dict(pw["gla"], tril=pc["gla_tril"][cfg["gla"]], mavg=pc["mavg_gla"], spread=pc["spread"])
    if pending is None:
        p_gla, p_swa, p_ssm = _in_proj(x, *in_w, cfg["tm"])
    else:
        x, p_gla, p_swa, p_ssm = _combine_in_proj(*pending, *in_w)
    p_gla = p_gla.reshape(b, t, GLA_PW)
    p_swa = p_swa.reshape(b, t, SWA_PW)
    stack = cfg["stack"]
    indep = stack > 1
    grp = b // stack
    y_ssm, h_re, h_im = _ssm(p_ssm.reshape(grp, stack * t, SSM_W), ssm_h0r.reshape(grp, stack, SSM_FLAT).astype(F32),
                             ssm_h0i.reshape(grp, stack, SSM_FLAT).astype(F32), ssm_w, cfg["lblk"],
                             *cfg["ssm"], indep)
    o_g, s_fin = _gla(p_gla.reshape(grp, stack * t, GLA_PW), _gla_state_in(gla_h0).reshape(grp, stack, GLA_W, GLA_DK),
                      gla_w, cfg["lblk"], *cfg["gla"], indep)
    s_fin = s_fin.reshape(b, GLA_W, GLA_DK)
    kcol, vcol = SWA_W // SWA_KV_W, SWA_W // SWA_KV_W + 1
    if past_k is None:
        per_blk = cfg["qb"] // SWA_WINDOW
        prev_k_map = lambda i, j: (i, jnp.maximum(j * per_blk - 1, 0), kcol)
        prev_v_map = lambda i, j: (i, jnp.maximum(j * per_blk - 1, 0), vcol)
        k_prev, v_prev = p_swa, p_swa
        keep = min(SWA_WINDOW, t)
        new_k = p_swa[:, t - keep:, SWA_W:SWA_W + SWA_KV_W]
        new_v = p_swa[:, t - keep:, SWA_W + SWA_KV_W:]
    else:
        prev_k_map = prev_v_map = lambda i, j: (i, 0, 0)
        k_prev = past_k.reshape(b, SWA_WINDOW, SWA_KV_W).astype(F32)
        v_prev = past_v.reshape(b, SWA_WINDOW, SWA_KV_W).astype(F32)
        new_k = p_swa[:, :, SWA_W:SWA_W + SWA_KV_W]
        new_v = p_swa[:, :, SWA_W + SWA_KV_W:]
    o_s = _swa(pw["sinks"], p_swa, k_prev, v_prev, prev_k_map, prev_v_map, bias, stack, cfg["qb"], cfg["cq"],
               cfg["win"], cfg["pos0"])
    x1, xs, aux, counts = _out_proj(x, y_ssm.reshape(n, SSM_W), o_g.reshape(n, GLA_W),
                                    o_s.reshape(n, SWA_W), pw["w_out"], pw["norm_ffn"], pw["wr"], pw["br"],
                                    pc["moe_triu"], pc["moe_before"])
    ys = _moe(xs, counts, pw["moe_wg"], pw["moe_wu"], pw["moe_wd"], pw["layer"], cfg["pt"])
    kv_shape = (b, new_k.shape[1], SWA_KV_HEADS, SWA_HEAD_DIM)
    return ((ys, aux, x1), new_k.reshape(kv_shape), new_v.reshape(kv_shape), _gla_state_out(s_fin),
            h_re.reshape(b, SSM_GROUPS, SSM_STATE), h_im.reshape(b, SSM_GROUPS, SSM_STATE))


def _group_cfg(t, past_len, n_past):
    if n_past is None:
        assert t % (16 * CHUNK) == 0
        cfg = dict(ssm=(4 * CHUNK, CHUNK), gla=(4 * CHUNK, 2 * CHUNK), lblk=16 * CHUNK, stack=1, qb=8 * CHUNK, cq=CHUNK,
                   win=SWA_WINDOW + CHUNK, pos0=-SWA_WINDOW, tm=512, pt=64)
        q_pos = jnp.arange(CHUNK, dtype=jnp.int32) + SWA_WINDOW
        k_pos = jnp.arange(SWA_WINDOW + CHUNK, dtype=jnp.int32)
    else:
        first_key, last_q = past_len - n_past, past_len + t - 1
        assert n_past == SWA_WINDOW and t <= CHUNK and t % 8 == 0
        assert past_len // CHUNK == last_q // CHUNK and first_key // CHUNK >= past_len // CHUNK - SWA_WINDOW // CHUNK
        stack = DECODE_STACK
        cfg = dict(ssm=(stack * t, t), gla=(stack * t, t), lblk=stack * t, stack=stack, qb=t, cq=t, win=n_past + t,
                   pos0=first_key, tm=512, pt=16)
        q_pos = past_len + jnp.arange(t, dtype=jnp.int32)
        k_pos = first_key + jnp.arange(n_past + t, dtype=jnp.int32)
    return cfg, q_pos, k_pos


def _stacked_bias(t5_table, q_pos, k_pos):
    bias = _t5_bias(t5_table, q_pos, k_pos)
    nq, nk = bias.shape[1:]
    return bias.reshape(SWA_KV_HEADS, SWA_REP * nq, nk)


PAST_LEN = 1024


def kernel(x_prompt, x_sample, cache_swa_k, cache_swa_v, state_gla, state_ssm_re, state_ssm_im, norm_mix, w_in, ssm_a_re, ssm_a_im, ssm_log_dt, ssm_b_re, ssm_b_im, ssm_c_re, ssm_c_im, ssm_d, ssm_w_glu, gla_w_gate, gla_b_gate, gla_norm, swa_q_norm, swa_k_norm, swa_sinks, t5_table, w_out, norm_ffn, moe_w_group, moe_b_group, moe_w_expert, moe_b_expert, moe_w_gate, moe_w_up, moe_w_down):
    depth = w_in.shape[0]
    bp, tp, _ = x_prompt.shape
    bs, ts, _ = x_sample.shape
    cfg_p, qpos_p, kpos_p = _group_cfg(tp, 0, None)
    cfg_s, qpos_s, kpos_s = _group_cfg(ts, PAST_LEN, cache_swa_k.shape[2])
    bias_p = _stacked_bias(t5_table, qpos_p, kpos_p)
    bias_s = _stacked_bias(t5_table, qpos_s, kpos_s)
    xp = x_prompt.astype(F32).reshape(bp * tp, D_MODEL)
    xs = x_sample.astype(F32).reshape(bs * ts, D_MODEL)
    pend_p = pend_s = None
    moe_wg = moe_w_gate.astype(F32).reshape(depth * N_EXPERTS, D_MODEL, EXPERT_FF)
    moe_wu = moe_w_up.astype(F32).reshape(depth * N_EXPERTS, D_MODEL, EXPERT_FF)
    moe_wd = moe_w_down.astype(F32).reshape(depth * N_EXPERTS, EXPERT_FF, D_MODEL)
    outs = [[] for _ in range(10)]
    lw_all = {
        "norm_mix": norm_mix, "w_in": w_in, "ssm_a_re": ssm_a_re, "ssm_a_im": ssm_a_im, "ssm_log_dt": ssm_log_dt,
        "ssm_b_re": ssm_b_re, "ssm_b_im": ssm_b_im, "ssm_c_re": ssm_c_re, "ssm_c_im": ssm_c_im, "ssm_d": ssm_d,
        "ssm_w_glu": ssm_w_glu, "gla_w_gate": gla_w_gate, "gla_b_gate": gla_b_gate, "gla_norm": gla_norm,
        "swa_q_norm": swa_q_norm, "swa_k_norm": swa_k_norm, "swa_sinks": swa_sinks, "w_out": w_out,
        "norm_ffn": norm_ffn, "moe_w_group": moe_w_group, "moe_b_group": moe_b_group, "moe_w_expert": moe_w_expert,
        "moe_b_expert": moe_b_expert,
    }
    consts = _prep_consts({cfg_p["ssm"], cfg_s["ssm"]}, {cfg_p["gla"], cfg_s["gla"]})
    for l in range(depth):
        pw = _prep_layer({name: w[l] for name, w in lw_all.items()}, (cfg_p["ssm"][1], cfg_s["ssm"][1]))
        pw.update(consts=consts, layer=l, moe_wg=moe_wg, moe_wu=moe_wu, moe_wd=moe_wd)
        zs = jnp.zeros((bp, SSM_GROUPS, SSM_STATE), F32)
        zg = jnp.zeros((bp, GLA_HEADS, GLA_DK, GLA_DV), F32)
        pend_p, nk, nv, ng, nr, ni = _layer(xp, pend_p, pw, bias_p, zs, zs, zg, None, None, cfg_p, bp, tp)
        for slot, val in zip((0, 1, 4, 6, 7), (nk, nv, ng, nr, ni)):
            outs[slot].append(val)
        pend_s, nk, nv, ng, nr, ni = _layer(xs, pend_s, pw, bias_s, state_ssm_re[l], state_ssm_im[l], state_gla[l],
                                            cache_swa_k[l], cache_swa_v[l], cfg_s, bs, ts)
        for slot, val in zip((2, 3, 5, 8, 9), (nk, nv, ng, nr, ni)):
            outs[slot].append(val)
    hp = _moe_combine(*pend_p).reshape(bp, tp, D_MODEL)
    hs = _moe_combine(*pend_s).reshape(bs, ts, D_MODEL)
    return (hp, hs) + tuple(jnp.stack(o) for o in outs)
```

```python
import functools
import math

import jax
import jax.numpy as jnp
from jax import lax
from jax.experimental import pallas as pl
from jax.experimental.pallas import tpu as pltpu

F32 = jnp.float32
BF16 = jnp.bfloat16

D_MODEL = 1024
CHUNK = 64
RMS_EPS = 1e-6
SSM_GROUPS = 16
SSM_GC = 16
SSM_STATE = 64
SSM_W = SSM_GROUPS * SSM_GC
SSM_FLAT = SSM_GROUPS * SSM_STATE
GLA_HEADS = 6
GLA_DK = 32
GLA_DV = 64
GLA_GATE_RANK = 16
GLA_GATE_NORM = 16.0
GLA_K = GLA_HEADS * GLA_DK
GLA_KP = 256
GLA_W = GLA_HEADS * GLA_DV
SWA_HEADS = 6
SWA_KV_HEADS = 2
SWA_REP = SWA_HEADS // SWA_KV_HEADS
SWA_HEAD_DIM = 64
SWA_WINDOW = 128
SWA_W = SWA_HEADS * SWA_HEAD_DIM
SWA_KV_W = SWA_KV_HEADS * SWA_HEAD_DIM
MIX_W = SSM_W + GLA_W + SWA_W
N_BUCKETS = 32
T5_MAX_DIST = 128
N_EXPERT_GROUPS = 4
EXPERTS_PER_GROUP = 4
N_EXPERTS = 16
EXPERT_FF = 512
LANES = 128
ROUTER_GROUP_LANE0 = N_EXPERTS
AUX_W1, AUX_W2, AUX_D1, AUX_D2 = 0, 1, 2, 3
AUX_ROWS = 8
DECODE_STACK = 16
MOE_TB = 512
MOE_PIECE = 16
MOE_RLOC = 2 * MOE_TB + N_EXPERTS * MOE_PIECE
MOE_PPB = MOE_RLOC // MOE_PIECE

GLA_Q0, GLA_K0, GLA_V0, GLA_OG0, GLA_Z0, GLA_PW = 0, 256, 512, 896, 1280, 1408
SWA_PW = SWA_W + 2 * SWA_KV_W
P_GLA0, P_SWA0, P_SSM0, P_TOTAL = 0, GLA_PW, GLA_PW + SWA_PW, GLA_PW + SWA_PW + SSM_W

VMEM_LIMIT = 48 * 1024 * 1024


def _cparams(*sem):
    return pltpu.CompilerParams(dimension_semantics=sem, vmem_limit_bytes=VMEM_LIMIT)


def _dot(a, b):
    return jnp.dot(a, b, preferred_element_type=F32)


def _dot_nt(a, b):
    return lax.dot_general(a, b, (((1,), (1,)), ((), ())), preferred_element_type=F32)


def _dot_tn(a, b):
    return lax.dot_general(a, b, (((0,), (0,)), ((), ())), preferred_element_type=F32)


def _hi_lo(x):
    hi = x.astype(BF16)
    return hi, (x - hi.astype(F32)).astype(BF16)


def _dot_f32_rhs(a_bf16, x):
    hi, lo = _hi_lo(x)
    return _dot(a_bf16, hi) + _dot(a_bf16, lo)


def _dot_f32_lhs(x, b_bf16):
    hi, lo = _hi_lo(x)
    return _dot(hi, b_bf16) + _dot(lo, b_bf16)


def _log2(n):
    assert n & (n - 1) == 0
    return n.bit_length() - 1


def _sigmoid(x):
    return 1.0 / (1.0 + jnp.exp(-x))


def _const_spec(shape):
    nd = len(shape)
    return pl.BlockSpec(shape, lambda *_: (0,) * nd)


def _in_proj_body(x_ref, g_ref, w_ref, mavg_ref, qkg_ref, gla_ref, swa_ref, ssm_ref):
    _in_proj_math(x_ref[...], g_ref, w_ref, mavg_ref, qkg_ref, gla_ref, swa_ref, ssm_ref)


def _in_proj_math(x, g_ref, w_ref, mavg_ref, qkg_ref, gla_ref, swa_ref, ssm_ref):
    xn = x * lax.rsqrt(jnp.mean(x * x, axis=-1, keepdims=True) + RMS_EPS) * g_ref[...]
    xb = xn.astype(BF16)
    gla_ref[...] = _dot(xb, w_ref[:, P_GLA0:P_SWA0])
    ssm_ref[...] = _dot(xb, w_ref[:, P_SSM0:P_TOTAL])
    s = _dot(xb, w_ref[:, P_SWA0:P_SSM0])
    qk = s[:, :SWA_W + SWA_KV_W]
    ms = _dot((qk * qk).astype(BF16), mavg_ref[...])
    swa_ref[:, :SWA_W + SWA_KV_W] = qk * lax.rsqrt(ms + RMS_EPS) * qkg_ref[...]
    swa_ref[:, SWA_W + SWA_KV_W:] = s[:, SWA_W + SWA_KV_W:]


def _in_proj(x, g, w, mavg, qkg, tm):
    n = x.shape[0]
    return pl.pallas_call(
        _in_proj_body,
        grid=(n // tm,),
        in_specs=[pl.BlockSpec((tm, D_MODEL), lambda i: (i, 0)), _const_spec(g.shape), _const_spec(w.shape),
                  _const_spec(mavg.shape), _const_spec(qkg.shape)],
        out_specs=[pl.BlockSpec((tm, GLA_PW), lambda i: (i, 0)), pl.BlockSpec((tm, SWA_PW), lambda i: (i, 0)),
                   pl.BlockSpec((tm, SSM_W), lambda i: (i, 0))],
        out_shape=[jax.ShapeDtypeStruct((n, GLA_PW), F32), jax.ShapeDtypeStruct((n, SWA_PW), F32),
                   jax.ShapeDtypeStruct((n, SSM_W), F32)],
        compiler_params=_cparams("parallel"),
        name="in_proj",
    )(x, g, w, mavg, qkg)


def _ssm_body(u_ref, h0r_ref, h0i_ref, ab_ref, bd_ref, cd_ref, pwp_ref, pwn_ref, tril_ref, d_ref, wglu_ref,
              y_ref, hr_ref, hi_ref, carry_ref, *, n_part, n_sub, lc, indep):
    if not indep:
        @pl.when(pl.program_id(1) == 0)
        def _():
            carry_ref[0:1, :] = h0r_ref[0]
            carry_ref[1:2, :] = h0i_ref[0]

        hr = carry_ref[0:1, :]
        hi = carry_ref[1:2, :]
    ab_re = ab_ref[0:1, :]
    ab_im = ab_ref[1:2, :]
    nr, ni = pwn_ref[0], pwn_ref[1]
    pr, pi = pwp_ref[0], pwp_ref[1]
    span = n_sub * lc
    for part in range(n_part):
        rows = slice(part * span, (part + 1) * span)
        u = u_ref[0, rows, :]
        bu = _dot(u.astype(BF16), bd_ref[...])
        sr, si = [], []
        for c in range(n_sub):
            bur = bu[c * lc:(c + 1) * lc, :SSM_FLAT]
            bui = bu[c * lc:(c + 1) * lc, SSM_FLAT:]
            sr.append(nr * bur - ni * bui)
            si.append(nr * bui + ni * bur)
        scaled = jnp.concatenate([jnp.concatenate(sr, axis=0), jnp.concatenate(si, axis=0)], axis=1)
        cs = _dot(tril_ref[...], scaled.astype(BF16))
        h_r, h_i = [], []
        for c in range(n_sub):
            if indep:
                hr, hi = h0r_ref[0, c:c + 1, :], h0i_ref[0, c:c + 1, :]
            cr = cs[c * lc:(c + 1) * lc, :SSM_FLAT] + (ab_re * hr - ab_im * hi)
            ci = cs[c * lc:(c + 1) * lc, SSM_FLAT:] + (ab_re * hi + ab_im * hr)
            h_r.append(pr * cr - pi * ci)
            h_i.append(pr * ci + pi * cr)
            hr = h_r[-1][lc - 1:lc, :]
            hi = h_i[-1][lc - 1:lc, :]
            if indep:
                hr_ref[0, c:c + 1, :] = hr
                hi_ref[0, c:c + 1, :] = hi
        hcat = jnp.concatenate([jnp.concatenate(h_r, axis=0), jnp.concatenate(h_i, axis=0)], axis=1).astype(BF16)
        y = _dot(hcat, cd_ref[...]) + d_ref[...] * u
        g = 0.5 * y * (1.0 + jnp.tanh(math.sqrt(2.0 / math.pi) * (y + 0.044715 * (y * y * y))))
        y_ref[0, rows, :] = g * _sigmoid(_dot(g.astype(BF16), wglu_ref[...]))
    if not indep:
        carry_ref[0:1, :] = hr
        carry_ref[1:2, :] = hi
        hr_ref[0] = hr
        hi_ref[0] = hi


def _ssm(u, h0r, h0i, sw, lblk, span, lc, indep):
    b, t, _ = u.shape
    n_state = span // lc if indep else 1
    assert not indep or (t == lblk == span)
    consts = [sw["ab"], sw["bd"], sw["cd"], sw["pwp"], sw["pwn"], sw["tril"], sw["d"], sw["wglu"]]
    state_spec = pl.BlockSpec((1, n_state, SSM_FLAT), lambda i, j: (i, 0, 0))
    return pl.pallas_call(
        functools.partial(_ssm_body, n_part=lblk // span, n_sub=span // lc, lc=lc, indep=indep),
        grid=(b, t // lblk),
        in_specs=[pl.BlockSpec((1, lblk, SSM_W), lambda i, j: (i, j, 0)), state_spec, state_spec]
        + [_const_spec(c.shape) for c in consts],
        out_specs=[pl.BlockSpec((1, lblk, SSM_W), lambda i, j: (i, j, 0)), state_spec, state_spec],
        out_shape=[jax.ShapeDtypeStruct((b, t, SSM_W), F32), jax.ShapeDtypeStruct((b, n_state, SSM_FLAT), F32),
                   jax.ShapeDtypeStruct((b, n_state, SSM_FLAT), F32)],
        scratch_shapes=[pltpu.VMEM((8, SSM_FLAT), F32)],
        compiler_params=_cparams("parallel", "arbitrary"),
        name="ssm",
    )(u, h0r, h0i, *consts)


def _gla_body(p_ref, s0_ref, wg_ref, bg_ref, tril_ref, mavg_ref, gn_ref, spread_ref, o_ref, sfin_ref, s_ref, *,
              n_part, n_sub, lc, indep):
    row_v = lax.broadcasted_iota(jnp.int32, (GLA_W, GLA_KP), 0)
    col_k = lax.broadcasted_iota(jnp.int32, (GLA_W, GLA_KP), 1)
    same_head = ((row_v >> _log2(GLA_DV)) == (col_k >> _log2(GLA_DK))).astype(F32)

    def spread(compact):
        return _dot_f32_lhs(compact, spread_ref[...]) * same_head

    def gather(full):
        hi, lo = _hi_lo(full)
        return _dot_nt(hi, spread_ref[...]) + _dot_nt(lo, spread_ref[...])

    if not indep:
        @pl.when(pl.program_id(1) == 0)
        def _():
            s_ref[...] = spread(s0_ref[0, 0])

    lane_k = lax.broadcasted_iota(jnp.int32, (1, GLA_KP), 1)
    lane_v = lax.broadcasted_iota(jnp.int32, (1, GLA_W), 1)
    head_k = [((lane_k >= h * GLA_DK) & (lane_k < (h + 1) * GLA_DK)).astype(F32) for h in range(GLA_HEADS)]
    head_v = [((lane_v >= h * GLA_DV) & (lane_v < (h + 1) * GLA_DV)).astype(F32) for h in range(GLA_HEADS)]
    row_t = lax.broadcasted_iota(jnp.int32, (GLA_HEADS * lc, lc), 0)
    col_s = lax.broadcasted_iota(jnp.int32, (GLA_HEADS * lc, lc), 1)
    causal = (row_t & (lc - 1)) >= col_s
    mid = lc // 2 - 1
    span = n_sub * lc
    st = None if indep else s_ref[...]
    for part in range(n_part):
        rows = slice(part * span, (part + 1) * span)
        z = p_ref[0, rows, GLA_Z0:GLA_Z0 + LANES]
        gin = _dot(z.astype(BF16), wg_ref[...]) + bg_ref[...]
        glog = (jnp.minimum(gin, 0.0) - jnp.log(1.0 + jnp.exp(-jnp.abs(gin)))) / GLA_GATE_NORM
        g_all = _dot_f32_rhs(tril_ref[...], glog)
        outs = []
        for c in range(n_sub):
            crow = slice(part * span + c * lc, part * span + (c + 1) * lc)
            q = p_ref[0, crow, GLA_Q0:GLA_Q0 + GLA_KP] * (GLA_DK ** -0.5)
            k = p_ref[0, crow, GLA_K0:GLA_K0 + GLA_KP]
            vb = p_ref[0, crow, GLA_V0:GLA_V0 + GLA_W].astype(BF16)
            if indep:
                st = spread(s0_ref[0, c])
            gc = g_all[c * lc:(c + 1) * lc, :]
            gl = gc[lc - 1:lc, :]
            gm = gc[mid:mid + 1, :]
            qc = (q * jnp.exp(gc - gm))
            ke = (k * jnp.exp(gm - gc)).astype(BF16)
            kd = (k * jnp.exp(gl - gc)).astype(BF16)
            qs = jnp.concatenate([qc * m for m in head_k], axis=0).astype(BF16)
            attn = jnp.where(causal, _dot_nt(qs, ke), 0.0)
            o2 = _dot(attn.astype(BF16), vb)
            o = _dot_nt((q * jnp.exp(gc)).astype(BF16), st.astype(BF16))
            for h in range(GLA_HEADS):
                o = o + head_v[h] * o2[h * lc:(h + 1) * lc, :]
            st = st * jnp.exp(gl) + _dot_tn(vb, kd) * same_head
            if indep:
                sfin_ref[0, c] = gather(st)
            outs.append(o)
        o = jnp.concatenate(outs, axis=0) if n_sub > 1 else outs[0]
        og = p_ref[0, rows, GLA_OG0:GLA_OG0 + GLA_W]
        ms = _dot((o * o).astype(BF16), mavg_ref[...])
        on = o * lax.rsqrt(ms + RMS_EPS) * gn_ref[...]
        o_ref[0, rows, :] = on * (og * _sigmoid(og))
    if not indep:
        s_ref[...] = st

        @pl.when(pl.program_id(1) == pl.num_programs(1) - 1)
        def _():
            sfin_ref[0, 0] = gather(st)


def _gla(p, s0, gw, lblk, span, lc, indep):
    b, t, _ = p.shape
    n_state = span // lc if indep else 1
    assert not indep or (t == lblk == span)
    consts = [gw["wg"], gw["bg"], gw["tril"], gw["mavg"], gw["gn"], gw["spread"]]
    st_spec = pl.BlockSpec((1, n_state, GLA_W, GLA_DK), lambda i, j: (i, 0, 0, 0))
    return pl.pallas_call(
        functools.partial(_gla_body, n_part=lblk // span, n_sub=span // lc, lc=lc, indep=indep),
        grid=(b, t // lblk),
        in_specs=[pl.BlockSpec((1, lblk, GLA_PW), lambda i, j: (i, j, 0)), st_spec]
        + [_const_spec(c.shape) for c in consts],
        out_specs=[pl.BlockSpec((1, lblk, GLA_W), lambda i, j: (i, j, 0)), st_spec],
        out_shape=[jax.ShapeDtypeStruct((b, t, GLA_W), F32),
                   jax.ShapeDtypeStruct((b, n_state, GLA_W, GLA_DK), F32)],
        scratch_shapes=[pltpu.VMEM((GLA_W, GLA_KP), F32)],
        compiler_params=_cparams("parallel", "arbitrary"),
        name="gla",
    )(p, s0, *consts)


def _t5_bias(t5_table, q_pos, k_pos):
    nq, nk = q_pos.shape[0], k_pos.shape[0]
    rel = (k_pos[0] - q_pos[0]) + jnp.arange(-(nq - 1), nk, dtype=jnp.int32)
    half = N_BUCKETS // 2
    max_exact = half // 2
    n = jnp.abs(rel)
    far = max_exact + (jnp.log(jnp.maximum(n, 1).astype(jnp.float32) / max_exact)
                       / math.log(T5_MAX_DIST / max_exact) * (half - max_exact)).astype(jnp.int32)
    bucket = jnp.where(rel > 0, half, 0) + jnp.where(n < max_exact, n, jnp.minimum(far, half - 1))
    by_dist = t5_table.astype(F32)[bucket].T
    return jnp.stack([by_dist[:, nq - 1 - i:nq - 1 - i + nk] for i in range(nq)], axis=1)


def _swa_body(sink_ref, q_ref, kp_ref, vp_ref, kc_ref, vc_ref, bias_ref, o_ref, *, bb, qb, cq, win, pos0):
    blk = pl.program_id(1)
    col = lax.broadcasted_iota(jnp.int32, (1, win), 1)
    row = lax.broadcasted_iota(jnp.int32, (SWA_REP * cq, 1), 0)
    for b in range(bb):
        kwin = jnp.concatenate([kp_ref[b], kc_ref[b]], axis=0)
        vwin = jnp.concatenate([vp_ref[b], vc_ref[b]], axis=0)
        for j in range(qb // cq):
            valid = (pos0 + blk * qb + j * cq + col) >= 0
            for g in range(SWA_KV_HEADS):
                heads = [SWA_REP * g + r for r in range(SWA_REP)]
                qs = jnp.concatenate(
                    [q_ref[b, j * cq:(j + 1) * cq, h * SWA_HEAD_DIM:(h + 1) * SWA_HEAD_DIM] for h in heads], axis=0)
                kk = kwin[j * cq:j * cq + win, g * SWA_HEAD_DIM:(g + 1) * SWA_HEAD_DIM]
                vv = vwin[j * cq:j * cq + win, g * SWA_HEAD_DIM:(g + 1) * SWA_HEAD_DIM]
                s = _dot_nt(qs.astype(BF16), kk.astype(BF16)) + bias_ref[g]
                if pos0 + j * cq < 0:
                    s = jnp.where(valid, s, -1e30)
                sink = jnp.where(row < cq, sink_ref[heads[0]],
                                 jnp.where(row < 2 * cq, sink_ref[heads[1]], sink_ref[heads[2]]))
                m = jnp.maximum(jnp.max(s, axis=-1, keepdims=True), sink)
                e = jnp.exp(s - m)
                den = jnp.sum(e, axis=-1, keepdims=True) + jnp.exp(sink - m)
                o = _dot(e.astype(BF16), vv.astype(BF16)) * (1.0 / den)
                for r, h in enumerate(heads):
                    o_ref[b, j * cq:(j + 1) * cq, h * SWA_HEAD_DIM:(h + 1) * SWA_HEAD_DIM] = o[r * cq:(r + 1) * cq, :]


def _swa(sinks, p_swa, k_prev, v_prev, prev_k_map, prev_v_map, bias, bb, qb, cq, win, pos0):
    b, t, _ = p_swa.shape
    kcol, vcol = SWA_W // SWA_KV_W, SWA_W // SWA_KV_W + 1
    return pl.pallas_call(
        functools.partial(_swa_body, bb=bb, qb=qb, cq=cq, win=win, pos0=pos0),
        grid=(b // bb, t // qb),
        in_specs=[pl.BlockSpec(memory_space=pltpu.SMEM),
                  pl.BlockSpec((bb, qb, SWA_W), lambda i, j: (i, j, 0)),
                  pl.BlockSpec((bb, SWA_WINDOW, SWA_KV_W), prev_k_map),
                  pl.BlockSpec((bb, SWA_WINDOW, SWA_KV_W), prev_v_map),
                  pl.BlockSpec((bb, qb, SWA_KV_W), lambda i, j: (i, j, kcol)),
                  pl.BlockSpec((bb, qb, SWA_KV_W), lambda i, j: (i, j, vcol)),
                  _const_spec(bias.shape)],
        out_specs=pl.BlockSpec((bb, qb, SWA_W), lambda i, j: (i, j, 0)),
        out_shape=jax.ShapeDtypeStruct((b, t, SWA_W), F32),
        compiler_params=_cparams("parallel", "parallel"),
        name="swa",
    )(sinks, p_swa, k_prev, v_prev, p_swa, p_swa, bias)


def _out_proj_body(x_ref, ys_ref, og_ref, os_ref, wo_ref, gf_ref, wr_ref, br_ref, triu_ref, before_ref, x1_ref, xs_ref,
                   aux_ref, cnt_ref):
    x1 = (x_ref[...] + _dot(ys_ref[...].astype(BF16), wo_ref[0:SSM_W, :])
          + _dot(og_ref[...].astype(BF16), wo_ref[SSM_W:SSM_W + GLA_W, :])
          + _dot(os_ref[...].astype(BF16), wo_ref[SSM_W + GLA_W:MIX_W, :]))
    x1_ref[...] = x1
    xn = x1 * lax.rsqrt(jnp.mean(x1 * x1, axis=-1, keepdims=True) + RMS_EPS) * gf_ref[...]
    xb = xn.astype(BF16)
    logit = _dot(xb, wr_ref[...]) + br_ref[...]
    lt = jnp.transpose(logit)[:2 * N_EXPERTS, :]
    row_i = lax.broadcasted_iota(jnp.int32, lt.shape, 0)
    row = row_i.astype(F32)
    neg = -jnp.inf
    big = float(LANES)
    g_row = row_i - ROUTER_GROUP_LANE0
    lg = jnp.where((g_row >= 0) & (g_row < N_EXPERT_GROUPS), lt, neg)
    gmax = jnp.max(lg, axis=0, keepdims=True)
    p_sel = 1.0 / jnp.sum(jnp.exp(lg - gmax), axis=0, keepdims=True)
    g_idx = jnp.min(jnp.where(lg == gmax, g_row.astype(F32), big), axis=0, keepdims=True)
    row_group = (row_i >> _log2(EXPERTS_PER_GROUP)).astype(F32)
    in_group = (row_i < N_EXPERTS) & (row_group == g_idx)
    le = jnp.where(in_group, lt, neg)
    m1 = jnp.max(le, axis=0, keepdims=True)
    i1 = jnp.min(jnp.where(le == m1, row, big), axis=0, keepdims=True)
    le2 = jnp.where(row == i1, neg, le)
    m2 = jnp.max(le2, axis=0, keepdims=True)
    i2 = jnp.min(jnp.where(le2 == m2, row, big), axis=0, keepdims=True)
    r = jnp.exp(m2 - m1)
    w1 = 1.0 / (1.0 + r)
    exp_row = lax.broadcasted_iota(jnp.int32, (N_EXPERTS, MOE_TB), 0).astype(F32)
    s1 = jnp.where(exp_row == i1, 1.0, 0.0)
    s2 = jnp.where(exp_row == i2, 1.0, 0.0)
    both = s1 + s2
    cnt = jnp.sum(both, axis=1, keepdims=True) + jnp.zeros_like(both)
    seg = jnp.floor((cnt + (MOE_PIECE - 1)) * (1.0 / MOE_PIECE)) * MOE_PIECE
    seg_k = jnp.concatenate([seg, jnp.zeros((LANES - N_EXPERTS, MOE_TB), F32)], axis=0).astype(BF16)
    lo = _dot(before_ref[...], seg_k)
    rank = _dot(both.astype(BF16), triu_ref[...])
    pos = lo + rank
    d1 = jnp.sum(s1 * pos, axis=0, keepdims=True)
    d2 = jnp.sum(s2 * pos, axis=0, keepdims=True)
    r_iota = lax.broadcasted_iota(jnp.int32, (MOE_RLOC, MOE_TB), 0).astype(F32)
    perm = jnp.where((r_iota == d1) | (r_iota == d2), 1.0, 0.0).astype(BF16)
    xs_ref[...] = _dot(perm, xb).astype(BF16)
    arow = lax.broadcasted_iota(jnp.int32, (LANES, MOE_TB), 0)
    at = jnp.where(arow == AUX_W1, p_sel * w1, jnp.where(arow == AUX_W2, p_sel * (r * w1),
                   jnp.where(arow == AUX_D1, d1, jnp.where(arow == AUX_D2, d2, 0.0))))
    aux_ref[...] = jnp.transpose(at)
    cnt_ref[0] = cnt[:, :LANES]


def _out_proj(x, ys, og, osw, wo, gf, wr, br, triu, before):
    n = x.shape[0]
    nblk = n // MOE_TB
    row = lambda w: pl.BlockSpec((MOE_TB, w), lambda i: (i, 0))
    return pl.pallas_call(
        _out_proj_body,
        grid=(nblk,),
        in_specs=[row(D_MODEL), row(SSM_W), row(GLA_W), row(SWA_W), _const_spec(wo.shape), _const_spec(gf.shape),
                  _const_spec(wr.shape), _const_spec(br.shape), _const_spec(triu.shape), _const_spec(before.shape)],
        out_specs=[row(D_MODEL), pl.BlockSpec((MOE_RLOC, D_MODEL), lambda i: (i, 0)), row(LANES),
                   pl.BlockSpec((1, N_EXPERTS, LANES), lambda i: (i, 0, 0))],
        out_shape=[jax.ShapeDtypeStruct((n, D_MODEL), F32), jax.ShapeDtypeStruct((nblk * MOE_RLOC, D_MODEL), BF16),
                   jax.ShapeDtypeStruct((n, LANES), F32), jax.ShapeDtypeStruct((nblk, N_EXPERTS, LANES), F32)],
        compiler_params=_cparams("parallel"),
        name="out_proj",
    )(x, ys, og, osw, wo, gf, wr, br, triu, before)


def _piece_copy(hbm_ref, piece, buf_ref, slot, p, sem, to_hbm):
    start = lambda i: i * MOE_PIECE if isinstance(i, int) else pl.multiple_of(i * MOE_PIECE, MOE_PIECE)
    rows = pl.ds(start(piece), MOE_PIECE)
    vm = buf_ref.at[slot, pl.ds(start(p), MOE_PIECE)]
    if to_hbm:
        return pltpu.make_async_copy(vm, hbm_ref.at[rows], sem)
    return pltpu.make_async_copy(hbm_ref.at[rows], vm, sem)


def _moe_expert_body(texp_ref, piece_ref, nv_ref, xs_hbm, wg_ref, wu_ref, wd_ref, ys_hbm,
                     xbuf, ybuf, wgb, wub, wdb, sem_in, sem_out, *, pt, n_steps):
    i = pl.program_id(0)
    slot = i % 2

    def for_pieces(tile, s, hbm_ref, buf_ref, sem, to_hbm, wait):
        def one(p):
            piece = 0 if wait else piece_ref[tile * pt + p]
            c = _piece_copy(hbm_ref, piece, buf_ref, s, p, sem.at[s], to_hbm)
            c.wait() if wait else c.start()

        nv = nv_ref[tile]

        @pl.when(nv == pt)
        def _():
            for p in range(pt):
                one(p)

        @pl.when(nv < pt)
        def _():
            def body(p, carry):
                one(p)
                return carry
            lax.fori_loop(0, nv, body, 0)

    gather = lambda tile, s, wait: for_pieces(tile, s, xs_hbm, xbuf, sem_in, False, wait)
    scatter = lambda tile, s, wait: for_pieces(tile, s, ys_hbm, ybuf, sem_out, True, wait)

    @pl.when(i == 0)
    def _():
        xbuf[...] = jnp.zeros_like(xbuf)
        gather(0, 0, False)

    @pl.when(i + 1 < n_steps)
    def _():
        gather(i + 1, 1 - slot, False)

    gather(i, slot, True)

    @pl.when((i == 0) | (texp_ref[i] != texp_ref[jnp.maximum(i - 1, 0)]))
    def _():
        wgb[...] = wg_ref[0].astype(BF16)
        wub[...] = wu_ref[0].astype(BF16)
        wdb[...] = wd_ref[0].astype(BF16)

    @pl.when(nv_ref[i] > 0)
    def _():
        xb = xbuf[slot]
        a = _dot(xb, wgb[...])
        h = (a * _sigmoid(a)) * _dot(xb, wub[...])
        ybuf[slot] = _dot(h.astype(BF16), wdb[...]).astype(BF16)

    scatter(i, slot, False)

    @pl.when(i >= 1)
    def _():
        scatter(i - 1, 1 - slot, True)

    @pl.when(i == n_steps - 1)
    def _():
        scatter(i, slot, True)


def _moe_experts(tile_expert, piece, nvalid, xs, wg, wu, wd, layer, pt):
    n_steps = tile_expert.shape[0]
    tm = pt * MOE_PIECE
    wspec = lambda shape: pl.BlockSpec(shape, lambda i, te, pc, nv: (te[i] + layer * N_EXPERTS, 0, 0))
    grid_spec = pltpu.PrefetchScalarGridSpec(
        num_scalar_prefetch=3,
        grid=(n_steps,),
        in_specs=[pl.BlockSpec(memory_space=pl.ANY), wspec((1, D_MODEL, EXPERT_FF)), wspec((1, D_MODEL, EXPERT_FF)),
                  wspec((1, EXPERT_FF, D_MODEL))],
        out_specs=pl.BlockSpec(memory_space=pl.ANY),
        scratch_shapes=[pltpu.VMEM((2, tm, D_MODEL), BF16), pltpu.VMEM((2, tm, D_MODEL), BF16),
                        pltpu.VMEM((D_MODEL, EXPERT_FF), BF16), pltpu.VMEM((D_MODEL, EXPERT_FF), BF16),
                        pltpu.VMEM((EXPERT_FF, D_MODEL), BF16),
                        pltpu.SemaphoreType.DMA((2,)), pltpu.SemaphoreType.DMA((2,))],
    )
    return pl.pallas_call(
        functools.partial(_moe_expert_body, pt=pt, n_steps=n_steps),
        grid_spec=grid_spec,
        out_shape=jax.ShapeDtypeStruct(xs.shape, BF16),
        input_output_aliases={3: 0},
        compiler_params=_cparams("arbitrary"),
        name="moe_experts",
    )(tile_expert, piece, nvalid, xs, wg, wu, wd)


def _combine_math(ys_ref, aux_ref, x1_ref):
    col = lax.broadcasted_iota(jnp.int32, (MOE_TB, MOE_RLOC), 1).astype(F32)
    mix = jnp.where(col == aux_ref[:, AUX_D1:AUX_D1 + 1], aux_ref[:, AUX_W1:AUX_W1 + 1],
                    jnp.where(col == aux_ref[:, AUX_D2:AUX_D2 + 1], aux_ref[:, AUX_W2:AUX_W2 + 1], 0.0))
    return x1_ref[...] + _dot(mix.astype(BF16), ys_ref[...])


def _moe_combine_body(ys_ref, aux_ref, x1_ref, o_ref):
    o_ref[...] = _combine_math(ys_ref, aux_ref, x1_ref)


def _combine_in_proj_body(ys_ref, aux_ref, x1_ref, g_ref, w_ref, mavg_ref, qkg_ref, o_ref, gla_ref, swa_ref, ssm_ref):
    x = _combine_math(ys_ref, aux_ref, x1_ref)
    o_ref[...] = x
    _in_proj_math(x, g_ref, w_ref, mavg_ref, qkg_ref, gla_ref, swa_ref, ssm_ref)


def _combine_in_proj(ys, aux, x1, g, w, mavg, qkg):
    n = x1.shape[0]
    row = lambda wdt: pl.BlockSpec((MOE_TB, wdt), lambda i: (i, 0))
    return pl.pallas_call(
        _combine_in_proj_body,
        grid=(n // MOE_TB,),
        in_specs=[pl.BlockSpec((MOE_RLOC, D_MODEL), lambda i: (i, 0)), row(LANES), row(D_MODEL), _const_spec(g.shape),
                  _const_spec(w.shape), _const_spec(mavg.shape), _const_spec(qkg.shape)],
        out_specs=[row(D_MODEL), row(GLA_PW), row(SWA_PW), row(SSM_W)],
        out_shape=[jax.ShapeDtypeStruct((n, D_MODEL), F32), jax.ShapeDtypeStruct((n, GLA_PW), F32),
                   jax.ShapeDtypeStruct((n, SWA_PW), F32), jax.ShapeDtypeStruct((n, SSM_W), F32)],
        compiler_params=_cparams("parallel"),
        name="combine_in_proj",
    )(ys, aux, x1, g, w, mavg, qkg)


def _moe_combine(ys, aux, x1):
    n = x1.shape[0]
    return pl.pallas_call(
        _moe_combine_body,
        grid=(n // MOE_TB,),
        in_specs=[pl.BlockSpec((MOE_RLOC, D_MODEL), lambda i: (i, 0)), pl.BlockSpec((MOE_TB, LANES), lambda i: (i, 0)),
                  pl.BlockSpec((MOE_TB, D_MODEL), lambda i: (i, 0))],
        out_specs=pl.BlockSpec((MOE_TB, D_MODEL), lambda i: (i, 0)),
        out_shape=jax.ShapeDtypeStruct((n, D_MODEL), F32),
        compiler_params=_cparams("parallel"),
        name="moe_combine",
    )(ys, aux, x1)


def _route_tables(counts, pt):
    nblk = counts.shape[0]
    n = nblk * MOE_TB
    cnt = counts[:, :, 0].T.astype(jnp.int32)
    pc = (cnt + MOE_PIECE - 1) // MOE_PIECE
    lo_p = jnp.cumsum(pc, axis=0) - pc
    pe = pc.sum(1)
    tiles_e = (pe + pt - 1) // pt
    tile_start = jnp.cumsum(tiles_e) - tiles_e
    seg_start = (tile_start[:, None] * pt + jnp.cumsum(pc, axis=1) - pc).reshape(-1)
    pcs = pc.reshape(-1)
    seg_src = (jnp.arange(nblk, dtype=jnp.int32)[None, :] * MOE_PPB + lo_p).reshape(-1)
    n_steps = -(-(2 * n // MOE_PIECE + nblk * N_EXPERTS + N_EXPERTS * (pt - 1)) // pt)
    slot = jnp.arange(n_steps * pt, dtype=jnp.int32)[:, None]
    in_seg = (slot >= seg_start[None, :]) & (slot < (seg_start + pcs)[None, :])
    piece = jnp.sum(jnp.where(in_seg, seg_src[None, :] + slot - seg_start[None, :], 0), axis=1).astype(jnp.int32)
    tile = jnp.arange(n_steps, dtype=jnp.int32)[:, None]
    in_exp = (tile >= tile_start[None, :]) & (tile < (tile_start + tiles_e)[None, :])
    tile_expert = jnp.sum(jnp.where(in_exp, jnp.arange(N_EXPERTS, dtype=jnp.int32)[None, :], 0), axis=1)
    tile_expert = jnp.where(tile[:, 0] < tiles_e.sum(), tile_expert, N_EXPERTS - 1).astype(jnp.int32)
    nvalid = jnp.sum(jnp.where(in_exp, jnp.clip(pe[None, :] - (tile - tile_start[None, :]) * pt, 0, pt), 0),
                     axis=1).astype(jnp.int32)
    return tile_expert, piece, nvalid


def _moe(xs, counts, wg, wu, wd, layer, pt):
    tile_expert, piece, nvalid = _route_tables(counts, pt)
    return _moe_experts(tile_expert, piece, nvalid, xs, wg, wu, wd, layer, pt)


def _seg_mean_matrix(width, seg):
    i = jnp.arange(width)
    return jnp.where((i[:, None] // seg) == (i[None, :] // seg), 1.0 / seg, 0.0).astype(BF16)


def _chunk_tril(span, lc):
    i = jnp.arange(span)
    return ((i[:, None] >= i[None, :]) & ((i[:, None] // lc) == (i[None, :] // lc))).astype(BF16)


def _prep_consts(ssm_shapes, gla_shapes):
    ti = jnp.arange(MOE_TB)
    kk = jnp.arange(GLA_KP)
    spread = ((kk[None, :] % GLA_DK) == jnp.arange(GLA_DK)[:, None]) & (kk[None, :] < GLA_K)
    return {"mavg_qk": _seg_mean_matrix(SWA_W + SWA_KV_W, SWA_HEAD_DIM), "mavg_gla": _seg_mean_matrix(GLA_W, GLA_DV),
            "spread": spread.astype(BF16),
            "ssm_tril": {s: _chunk_tril(*s) for s in ssm_shapes}, "gla_tril": {s: _chunk_tril(*s) for s in gla_shapes},
            "moe_triu": (ti[:, None] < ti[None, :]).astype(BF16),
            "moe_before": (jnp.arange(LANES)[None, :] < jnp.arange(N_EXPERTS)[:, None]).astype(BF16)}


def _prep_layer(lw, ssm_chunks):
    w_in = lw["w_in"].astype(F32)
    cols = {}
    off = 0
    for name, wdt in (("u", SSM_W), ("qg", GLA_K), ("kg", GLA_K), ("vg", GLA_W), ("z", GLA_GATE_RANK), ("og", GLA_W),
                      ("qs", SWA_W), ("ks", SWA_KV_W), ("vs", SWA_KV_W)):
        cols[name] = w_in[:, off:off + wdt]
        off += wdt
    zpad = lambda wdt: jnp.zeros((D_MODEL, wdt), F32)
    w_all = jnp.concatenate(
        [cols["qg"], zpad(GLA_KP - GLA_K), cols["kg"], zpad(GLA_KP - GLA_K), cols["vg"], cols["og"], cols["z"],
         zpad(LANES - GLA_GATE_RANK), cols["qs"], cols["ks"], cols["vs"], cols["u"]], axis=1).astype(BF16)
    out = {"norm_mix": lw["norm_mix"].astype(F32)[None, :], "w_all": w_all,
           "qk_gain": jnp.concatenate([jnp.tile(lw["swa_q_norm"].astype(F32) * SWA_HEAD_DIM ** -0.5, SWA_HEADS),
                                       jnp.tile(lw["swa_k_norm"].astype(F32), SWA_KV_HEADS)])[None, :]}
    a_re = lw["ssm_a_re"].astype(F32)
    a_im = lw["ssm_a_im"].astype(F32)
    dt = jnp.exp(lw["ssm_log_dt"].astype(F32))[:, None]
    mag = jnp.exp(a_re * dt)
    ab_re = mag * jnp.cos(a_im * dt)
    ab_im = mag * jnp.sin(a_im * dt)
    den = a_re * a_re + a_im * a_im
    nr = ab_re - 1.0
    f_re = (nr * a_re + ab_im * a_im) / den
    f_im = (ab_im * a_re - nr * a_im) / den
    b_re = lw["ssm_b_re"].astype(F32)
    b_im = lw["ssm_b_im"].astype(F32)
    bb_re = f_re[..., None] * b_re - f_im[..., None] * b_im
    bb_im = f_re[..., None] * b_im + f_im[..., None] * b_re
    eye_g = jnp.eye(SSM_GROUPS, dtype=F32)
    blockdiag_in = lambda bb: jnp.einsum("gpc,gh->gchp", bb, eye_g).reshape(SSM_W, SSM_FLAT)
    blockdiag_out = lambda cc: jnp.einsum("gcp,gh->gphc", cc, eye_g).reshape(SSM_FLAT, SSM_W)
    bd = jnp.concatenate([blockdiag_in(bb_re), blockdiag_in(bb_im)], axis=1).astype(BF16)
    cd = jnp.concatenate([blockdiag_out(lw["ssm_c_re"].astype(F32)),
                          -blockdiag_out(lw["ssm_c_im"].astype(F32))], axis=0).astype(BF16)
    ssm = {"ab": jnp.concatenate([ab_re.reshape(1, SSM_FLAT), ab_im.reshape(1, SSM_FLAT),
                                  jnp.zeros((6, SSM_FLAT), F32)], axis=0),
           "bd": bd, "cd": cd, "d": lw["ssm_d"].astype(F32)[None, :], "wglu": lw["ssm_w_glu"].astype(BF16)}
    def powers(lc, sign):
        tt = jnp.arange(lc, dtype=F32)[:, None, None]
        m = jnp.exp(sign * tt * (a_re * dt)[None])
        ang = sign * tt * (a_im * dt)[None]
        return jnp.stack([(m * jnp.cos(ang)).reshape(lc, SSM_FLAT), (m * jnp.sin(ang)).reshape(lc, SSM_FLAT)])

    out["ssm"] = ssm
    out["ssm_pw"] = {lc: {"pwp": powers(lc, 1.0), "pwn": powers(lc, -1.0)} for lc in ssm_chunks}
    wg = jnp.zeros((LANES, GLA_KP), F32).at[:GLA_GATE_RANK, :GLA_K].set(lw["gla_w_gate"].astype(F32)).astype(BF16)
    bg = jnp.zeros((1, GLA_KP), F32).at[0, :GLA_K].set(lw["gla_b_gate"].astype(F32))
    out["gla"] = {"wg": wg, "bg": bg, "gn": jnp.tile(lw["gla_norm"].astype(F32), GLA_HEADS)[None, :]}
    out["sinks"] = lw["swa_sinks"].astype(F32)
    out["w_out"] = lw["w_out"].astype(BF16)
    out["norm_ffn"] = lw["norm_ffn"].astype(F32)[None, :]
    wr = jnp.zeros((D_MODEL, LANES), F32)
    wr = wr.at[:, :N_EXPERTS].set(lw["moe_w_expert"].astype(F32))
    wr = wr.at[:, ROUTER_GROUP_LANE0:ROUTER_GROUP_LANE0 + N_EXPERT_GROUPS].set(lw["moe_w_group"].astype(F32))
    out["wr"] = wr.astype(BF16)
    br = jnp.zeros((1, LANES), F32)
    br = br.at[0, :N_EXPERTS].set(lw["moe_b_expert"].astype(F32))
    br = br.at[0, ROUTER_GROUP_LANE0:ROUTER_GROUP_LANE0 + N_EXPERT_GROUPS].set(lw["moe_b_group"].astype(F32))
    out["br"] = br
    return out


def _gla_state_in(h0):
    return jnp.transpose(h0.astype(F32), (0, 1, 3, 2)).reshape(h0.shape[0], GLA_W, GLA_DK)


def _gla_state_out(st):
    return jnp.transpose(st.reshape(st.shape[0], GLA_HEADS, GLA_DV, GLA_DK), (0, 1, 3, 2))


def _layer(x, pending, pw, bias, ssm_h0r, ssm_h0i, gla_h0, past_k, past_v, cfg, b, t):
    n = b * t
    pc = pw["consts"]
    in_w = (pw["norm_mix"], pw["w_all"], pc["mavg_qk"], pw["qk_gain"])
    ssm_w = dict(pw["ssm"], **pw["ssm_pw"][cfg["ssm"][1]], tril=pc["ssm_tril"][cfg["ssm"]])
    gla_w = dict(pw["gla"], tril=pc["gla_tril"][cfg["gla"]], mavg=pc["mavg_gla"], spread=pc["spread"])
    if pending is None:
        p_gla, p_swa, p_ssm = _in_proj(x, *in_w, cfg["tm"])
    else:
        x, p_gla, p_swa, p_ssm = _combine_in_proj(*pending, *in_w)
    p_gla = p_gla.reshape(b, t, GLA_PW)
    p_swa = p_swa.reshape(b, t, SWA_PW)
    stack = cfg["stack"]
    indep = stack > 1
    grp = b // stack
    y_ssm, h_re, h_im = _ssm(p_ssm.reshape(grp, stack * t, SSM_W), ssm_h0r.reshape(grp, stack, SSM_FLAT).astype(F32),
                             ssm_h0i.reshape(grp, stack, SSM_FLAT).astype(F32), ssm_w, cfg["lblk"],
                             *cfg["ssm"], indep)
    o_g, s_fin = _gla(p_gla.reshape(grp, stack * t, GLA_PW), _gla_state_in(gla_h0).reshape(grp, stack, GLA_W, GLA_DK),
                      gla_w, cfg["lblk"], *cfg["gla"], indep)
    s_fin = s_fin.reshape(b, GLA_W, GLA_DK)
    kcol, vcol = SWA_W // SWA_KV_W, SWA_W // SWA_KV_W + 1
    if past_k is None:
        per_blk = cfg["qb"] // SWA_WINDOW
        prev_k_map = lambda i, j: (i, jnp.maximum(j * per_blk - 1, 0), kcol)
        prev_v_map = lambda i, j: (i, jnp.maximum(j * per_blk - 1, 0), vcol)
        k_prev, v_prev = p_swa, p_swa
        keep = min(SWA_WINDOW, t)
        new_k = p_swa[:, t - keep:, SWA_W:SWA_W + SWA_KV_W]
        new_v = p_swa[:, t - keep:, SWA_W + SWA_KV_W:]
    else:
        prev_k_map = prev_v_map = lambda i, j: (i, 0, 0)
        k_prev = past_k.reshape(b, SWA_WINDOW, SWA_KV_W).astype(F32)
        v_prev = past_v.reshape(b, SWA_WINDOW, SWA_KV_W).astype(F32)
        new_k = p_swa[:, :, SWA_W:SWA_W + SWA_KV_W]
        new_v = p_swa[:, :, SWA_W + SWA_KV_W:]
    o_s = _swa(pw["sinks"], p_swa, k_prev, v_prev, prev_k_map, prev_v_map, bias, stack, cfg["qb"], cfg["cq"],
               cfg["win"], cfg["pos0"])
    x1, xs, aux, counts = _out_proj(x, y_ssm.reshape(n, SSM_W), o_g.reshape(n, GLA_W),
                                    o_s.reshape(n, SWA_W), pw["w_out"], pw["norm_ffn"], pw["wr"], pw["br"],
                                    pc["moe_triu"], pc["moe_before"])
    ys = _moe(xs, counts, pw["moe_wg"], pw["moe_wu"], pw["moe_wd"], pw["layer"], cfg["pt"])
    kv_shape = (b, new_k.shape[1], SWA_KV_HEADS, SWA_HEAD_DIM)
    return ((ys, aux, x1), new_k.reshape(kv_shape), new_v.reshape(kv_shape), _gla_state_out(s_fin),
            h_re.reshape(b, SSM_GROUPS, SSM_STATE), h_im.reshape(b, SSM_GROUPS, SSM_STATE))


def _group_cfg(t, past_len, n_past):
    if n_past is None:
        assert t % (16 * CHUNK) == 0
        cfg = dict(ssm=(4 * CHUNK, CHUNK), gla=(4 * CHUNK, 2 * CHUNK), lblk=16 * CHUNK, stack=1, qb=16 * CHUNK, cq=CHUNK,
                   win=SWA_WINDOW + CHUNK, pos0=-SWA_WINDOW, tm=1024, pt=64)
        q_pos = jnp.arange(CHUNK, dtype=jnp.int32) + SWA_WINDOW
        k_pos = jnp.arange(SWA_WINDOW + CHUNK, dtype=jnp.int32)
    else:
        first_key, last_q = past_len - n_past, past_len + t - 1
        assert n_past == SWA_WINDOW and t <= CHUNK and t % 8 == 0
        assert past_len // CHUNK == last_q // CHUNK and first_key // CHUNK >= past_len // CHUNK - SWA_WINDOW // CHUNK
        stack = DECODE_STACK
        cfg = dict(ssm=(stack * t, t), gla=(stack * t, t), lblk=stack * t, stack=stack, qb=t, cq=t, win=n_past + t,
                   pos0=first_key, tm=512, pt=16)
        q_pos = past_len + jnp.arange(t, dtype=jnp.int32)
        k_pos = first_key + jnp.arange(n_past + t, dtype=jnp.int32)
    return cfg, q_pos, k_pos


def _stacked_bias(t5_table, q_pos, k_pos):
    bias = _t5_bias(t5_table, q_pos, k_pos)
    nq, nk = bias.shape[1:]
    return bias.reshape(SWA_KV_HEADS, SWA_REP * nq, nk)


PAST_LEN = 1024


def kernel(x_prompt, x_sample, cache_swa_k, cache_swa_v, state_gla, state_ssm_re, state_ssm_im, norm_mix, w_in, ssm_a_re, ssm_a_im, ssm_log_dt, ssm_b_re, ssm_b_im, ssm_c_re, ssm_c_im, ssm_d, ssm_w_glu, gla_w_gate, gla_b_gate, gla_norm, swa_q_norm, swa_k_norm, swa_sinks, t5_table, w_out, norm_ffn, moe_w_group, moe_b_group, moe_w_expert, moe_b_expert, moe_w_gate, moe_w_up, moe_w_down):
    depth = w_in.shape[0]
    bp, tp, _ = x_prompt.shape
    bs, ts, _ = x_sample.shape
    cfg_p, qpos_p, kpos_p = _group_cfg(tp, 0, None)
    cfg_s, qpos_s, kpos_s = _group_cfg(ts, PAST_LEN, cache_swa_k.shape[2])
    bias_p = _stacked_bias(t5_table, qpos_p, kpos_p)
    bias_s = _stacked_bias(t5_table, qpos_s, kpos_s)
    xp = x_prompt.astype(F32).reshape(bp * tp, D_MODEL)
    xs = x_sample.astype(F32).reshape(bs * ts, D_MODEL)
    pend_p = pend_s = None
    moe_wg = moe_w_gate.astype(F32).reshape(depth * N_EXPERTS, D_MODEL, EXPERT_FF)
    moe_wu = moe_w_up.astype(F32).reshape(depth * N_EXPERTS, D_MODEL, EXPERT_FF)
    moe_wd = moe_w_down.astype(F32).reshape(depth * N_EXPERTS, EXPERT_FF, D_MODEL)
    outs = [[] for _ in range(10)]
    lw_all = {
        "norm_mix": norm_mix, "w_in": w_in, "ssm_a_re": ssm_a_re, "ssm_a_im": ssm_a_im, "ssm_log_dt": ssm_log_dt,
        "ssm_b_re": ssm_b_re, "ssm_b_im": ssm_b_im, "ssm_c_re": ssm_c_re, "ssm_c_im": ssm_c_im, "ssm_d": ssm_d,
        "ssm_w_glu": ssm_w_glu, "gla_w_gate": gla_w_gate, "gla_b_gate": gla_b_gate, "gla_norm": gla_norm,
        "swa_q_norm": swa_q_norm, "swa_k_norm": swa_k_norm, "swa_sinks": swa_sinks, "w_out": w_out,
        "norm_ffn": norm_ffn, "moe_w_group": moe_w_group, "moe_b_group": moe_b_group, "moe_w_expert": moe_w_expert,
        "moe_b_expert": moe_b_expert,
    }
    consts = _prep_consts({cfg_p["ssm"], cfg_s["ssm"]}, {cfg_p["gla"], cfg_s["gla"]})
    for l in range(depth):
        pw = _prep_layer({name: w[l] for name, w in lw_all.items()}, (cfg_p["ssm"][1], cfg_s["ssm"][1]))
        pw.update(consts=consts, layer=l, moe_wg=moe_wg, moe_wu=moe_wu, moe_wd=moe_wd)
        zs = jnp.zeros((bp, SSM_GROUPS, SSM_STATE), F32)
        zg = jnp.zeros((bp, GLA_HEADS, GLA_DK, GLA_DV), F32)
        pend_p, nk, nv, ng, nr, ni = _layer(xp, pend_p, pw, bias_p, zs, zs, zg, None, None, cfg_p, bp, tp)
        for slot, val in zip((0, 1, 4, 6, 7), (nk, nv, ng, nr, ni)):
            outs[slot].append(val)
        pend_s, nk, nv, ng, nr, ni = _layer(xs, pend_s, pw, bias_s, state_ssm_re[l], state_ssm_im[l], state_gla[l],
                                            cache_swa_k[l], cache_swa_v[l], cfg_s, bs, ts)
        for slot, val in zip((2, 3, 5, 8, 9), (nk, nv, ng, nr, ni)):
            outs[slot].append(val)
    hp = _moe_combine(*pend_p).reshape(bp, tp, D_MODEL)
    hs = _moe_combine(*pend_s).reshape(bs, ts, D_MODEL)
    return (hp, hs) + tuple(jnp.stack(o) for o in outs)
```

```python
import functools
import math

import jax
import jax.numpy as jnp
from jax import lax
from jax.experimental import pallas as pl
from jax.experimental.pallas import tpu as pltpu

F32 = jnp.float32
BF16 = jnp.bfloat16

D_MODEL = 1024
CHUNK = 64
RMS_EPS = 1e-6
SSM_GROUPS = 16
SSM_GC = 16
SSM_STATE = 64
SSM_W = SSM_GROUPS * SSM_GC
SSM_FLAT = SSM_GROUPS * SSM_STATE
GLA_HEADS = 6
GLA_DK = 32
GLA_DV = 64
GLA_GATE_RANK = 16
GLA_GATE_NORM = 16.0
GLA_K = GLA_HEADS * GLA_DK
GLA_KP = 256
GLA_W = GLA_HEADS * GLA_DV
SWA_HEADS = 6
SWA_KV_HEADS = 2
SWA_REP = SWA_HEADS // SWA_KV_HEADS
SWA_HEAD_DIM = 64
SWA_WINDOW = 128
SWA_W = SWA_HEADS * SWA_HEAD_DIM
SWA_KV_W = SWA_KV_HEADS * SWA_HEAD_DIM
MIX_W = SSM_W + GLA_W + SWA_W
N_BUCKETS = 32
T5_MAX_DIST = 128
N_EXPERT_GROUPS = 4
EXPERTS_PER_GROUP = 4
N_EXPERTS = 16
EXPERT_FF = 512
LANES = 128
ROUTER_GROUP_LANE0 = N_EXPERTS
AUX_W1, AUX_W2, AUX_D1, AUX_D2 = 0, 1, 2, 3
AUX_ROWS = 8
DECODE_STACK = 16
MOE_TB = 512
MOE_PIECE = 16
MOE_RLOC = 2 * MOE_TB + N_EXPERTS * MOE_PIECE
MOE_PPB = MOE_RLOC // MOE_PIECE

GLA_Q0, GLA_K0, GLA_V0, GLA_OG0, GLA_Z0, GLA_PW = 0, 256, 512, 896, 1280, 1408
SWA_PW = SWA_W + 2 * SWA_KV_W
P_GLA0, P_SWA0, P_SSM0, P_TOTAL = 0, GLA_PW, GLA_PW + SWA_PW, GLA_PW + SWA_PW + SSM_W

VMEM_LIMIT = 48 * 1024 * 1024


def _cparams(*sem):
    return pltpu.CompilerParams(dimension_semantics=sem, vmem_limit_bytes=VMEM_LIMIT)


def _dot(a, b):
    return jnp.dot(a, b, preferred_element_type=F32)


def _dot_nt(a, b):
    return lax.dot_general(a, b, (((1,), (1,)), ((), ())), preferred_element_type=F32)


def _dot_tn(a, b):
    return lax.dot_general(a, b, (((0,), (0,)), ((), ())), preferred_element_type=F32)


def _hi_lo(x):
    hi = x.astype(BF16)
    return hi, (x - hi.astype(F32)).astype(BF16)


def _dot_f32_rhs(a_bf16, x):
    hi, lo = _hi_lo(x)
    return _dot(a_bf16, hi) + _dot(a_bf16, lo)


def _dot_f32_lhs(x, b_bf16):
    hi, lo = _hi_lo(x)
    return _dot(hi, b_bf16) + _dot(lo, b_bf16)


def _log2(n):
    assert n & (n - 1) == 0
    return n.bit_length() - 1


def _sigmoid(x):
    return 1.0 / (1.0 + jnp.exp(-x))


def _const_spec(shape):
    nd = len(shape)
    return pl.BlockSpec(shape, lambda *_: (0,) * nd)


def _in_proj_body(x_ref, g_ref, w_ref, mavg_ref, qkg_ref, gla_ref, swa_ref, ssm_ref):
    _in_proj_math(x_ref[...], g_ref, w_ref, mavg_ref, qkg_ref, gla_ref, swa_ref, ssm_ref)


def _in_proj_math(x, g_ref, w_ref, mavg_ref, qkg_ref, gla_ref, swa_ref, ssm_ref):
    xn = x * lax.rsqrt(jnp.mean(x * x, axis=-1, keepdims=True) + RMS_EPS) * g_ref[...]
    xb = xn.astype(BF16)
    gla_ref[...] = _dot(xb, w_ref[:, P_GLA0:P_SWA0])
    ssm_ref[...] = _dot(xb, w_ref[:, P_SSM0:P_TOTAL])
    s = _dot(xb, w_ref[:, P_SWA0:P_SSM0])
    qk = s[:, :SWA_W + SWA_KV_W]
    ms = _dot((qk * qk).astype(BF16), mavg_ref[...])
    swa_ref[:, :SWA_W + SWA_KV_W] = qk * lax.rsqrt(ms + RMS_EPS) * qkg_ref[...]
    swa_ref[:, SWA_W + SWA_KV_W:] = s[:, SWA_W + SWA_KV_W:]


def _in_proj(x, g, w, mavg, qkg, tm):
    n = x.shape[0]
    return pl.pallas_call(
        _in_proj_body,
        grid=(n // tm,),
        in_specs=[pl.BlockSpec((tm, D_MODEL), lambda i: (i, 0)), _const_spec(g.shape), _const_spec(w.shape),
                  _const_spec(mavg.shape), _const_spec(qkg.shape)],
        out_specs=[pl.BlockSpec((tm, GLA_PW), lambda i: (i, 0)), pl.BlockSpec((tm, SWA_PW), lambda i: (i, 0)),
                   pl.BlockSpec((tm, SSM_W), lambda i: (i, 0))],
        out_shape=[jax.ShapeDtypeStruct((n, GLA_PW), F32), jax.ShapeDtypeStruct((n, SWA_PW), F32),
                   jax.ShapeDtypeStruct((n, SSM_W), F32)],
        compiler_params=_cparams("parallel"),
        name="in_proj",
    )(x, g, w, mavg, qkg)


def _ssm_body(u_ref, h0r_ref, h0i_ref, ab_ref, bd_ref, cd_ref, pwp_ref, pwn_ref, tril_ref, d_ref, wglu_ref,
              y_ref, hr_ref, hi_ref, carry_ref, *, n_part, n_sub, lc, indep):
    if not indep:
        @pl.when(pl.program_id(1) == 0)
        def _():
            carry_ref[0:1, :] = h0r_ref[0]
            carry_ref[1:2, :] = h0i_ref[0]

        hr = carry_ref[0:1, :]
        hi = carry_ref[1:2, :]
    ab_re = ab_ref[0:1, :]
    ab_im = ab_ref[1:2, :]
    nr, ni = pwn_ref[0], pwn_ref[1]
    pr, pi = pwp_ref[0], pwp_ref[1]
    span = n_sub * lc
    for part in range(n_part):
        rows = slice(part * span, (part + 1) * span)
        u = u_ref[0, rows, :]
        bu = _dot(u.astype(BF16), bd_ref[...])
        sr, si = [], []
        for c in range(n_sub):
            bur = bu[c * lc:(c + 1) * lc, :SSM_FLAT]
            bui = bu[c * lc:(c + 1) * lc, SSM_FLAT:]
            sr.append(nr * bur - ni * bui)
            si.append(nr * bui + ni * bur)
        scaled = jnp.concatenate([jnp.concatenate(sr, axis=0), jnp.concatenate(si, axis=0)], axis=1)
        cs = _dot(tril_ref[...], scaled.astype(BF16))
        h_r, h_i = [], []
        for c in range(n_sub):
            if indep:
                hr, hi = h0r_ref[0, c:c + 1, :], h0i_ref[0, c:c + 1, :]
            cr = cs[c * lc:(c + 1) * lc, :SSM_FLAT] + (ab_re * hr - ab_im * hi)
            ci = cs[c * lc:(c + 1) * lc, SSM_FLAT:] + (ab_re * hi + ab_im * hr)
            h_r.append(pr * cr - pi * ci)
            h_i.append(pr * ci + pi * cr)
            hr = h_r[-1][lc - 1:lc, :]
            hi = h_i[-1][lc - 1:lc, :]
            if indep:
                hr_ref[0, c:c + 1, :] = hr
                hi_ref[0, c:c + 1, :] = hi
        hcat = jnp.concatenate([jnp.concatenate(h_r, axis=0), jnp.concatenate(h_i, axis=0)], axis=1).astype(BF16)
        y = _dot(hcat, cd_ref[...]) + d_ref[...] * u
        g = 0.5 * y * (1.0 + jnp.tanh(math.sqrt(2.0 / math.pi) * (y + 0.044715 * (y * y * y))))
        y_ref[0, rows, :] = g * _sigmoid(_dot(g.astype(BF16), wglu_ref[...]))
    if not indep:
        carry_ref[0:1, :] = hr
        carry_ref[1:2, :] = hi
        hr_ref[0] = hr
        hi_ref[0] = hi


def _ssm(u, h0r, h0i, sw, lblk, span, lc, indep):
    b, t, _ = u.shape
    n_state = span // lc if indep else 1
    assert not indep or (t == lblk == span)
    consts = [sw["ab"], sw["bd"], sw["cd"], sw["pwp"], sw["pwn"], sw["tril"], sw["d"], sw["wglu"]]
    state_spec = pl.BlockSpec((1, n_state, SSM_FLAT), lambda i, j: (i, 0, 0))
    return pl.pallas_call(
        functools.partial(_ssm_body, n_part=lblk // span, n_sub=span // lc, lc=lc, indep=indep),
        grid=(b, t // lblk),
        in_specs=[pl.BlockSpec((1, lblk, SSM_W), lambda i, j: (i, j, 0)), state_spec, state_spec]
        + [_const_spec(c.shape) for c in consts],
        out_specs=[pl.BlockSpec((1, lblk, SSM_W), lambda i, j: (i, j, 0)), state_spec, state_spec],
        out_shape=[jax.ShapeDtypeStruct((b, t, SSM_W), F32), jax.ShapeDtypeStruct((b, n_state, SSM_FLAT), F32),
                   jax.ShapeDtypeStruct((b, n_state, SSM_FLAT), F32)],
        scratch_shapes=[pltpu.VMEM((8, SSM_FLAT), F32)],
        compiler_params=_cparams("parallel", "arbitrary"),
        name="ssm",
    )(u, h0r, h0i, *consts)


def _gla_body(p_ref, s0_ref, wg_ref, bg_ref, tril_ref, mavg_ref, gn_ref, spread_ref, o_ref, sfin_ref, s_ref, *,
              n_part, n_sub, lc, indep):
    row_v = lax.broadcasted_iota(jnp.int32, (GLA_W, GLA_KP), 0)
    col_k = lax.broadcasted_iota(jnp.int32, (GLA_W, GLA_KP), 1)
    same_head = ((row_v >> _log2(GLA_DV)) == (col_k >> _log2(GLA_DK))).astype(F32)

    def spread(compact):
        return _dot_f32_lhs(compact, spread_ref[...]) * same_head

    def gather(full):
        hi, lo = _hi_lo(full)
        return _dot_nt(hi, spread_ref[...]) + _dot_nt(lo, spread_ref[...])

    if not indep:
        @pl.when(pl.program_id(1) == 0)
        def _():
            s_ref[...] = spread(s0_ref[0, 0])

    lane_k = lax.broadcasted_iota(jnp.int32, (1, GLA_KP), 1)
    lane_v = lax.broadcasted_iota(jnp.int32, (1, GLA_W), 1)
    head_k = [((lane_k >= h * GLA_DK) & (lane_k < (h + 1) * GLA_DK)).astype(F32) for h in range(GLA_HEADS)]
    head_v = [((lane_v >= h * GLA_DV) & (lane_v < (h + 1) * GLA_DV)).astype(F32) for h in range(GLA_HEADS)]
    row_t = lax.broadcasted_iota(jnp.int32, (GLA_HEADS * lc, lc), 0)
    col_s = lax.broadcasted_iota(jnp.int32, (GLA_HEADS * lc, lc), 1)
    causal = (row_t & (lc - 1)) >= col_s
    mid = lc // 2 - 1
    span = n_sub * lc
    st = None if indep else s_ref[...]
    for part in range(n_part):
        rows = slice(part * span, (part + 1) * span)
        z = p_ref[0, rows, GLA_Z0:GLA_Z0 + LANES]
        gin = _dot(z.astype(BF16), wg_ref[...]) + bg_ref[...]
        glog = (jnp.minimum(gin, 0.0) - jnp.log(1.0 + jnp.exp(-jnp.abs(gin)))) / GLA_GATE_NORM
        g_all = _dot_f32_rhs(tril_ref[...], glog)
        outs = []
        for c in range(n_sub):
            crow = slice(part * span + c * lc, part * span + (c + 1) * lc)
            q = p_ref[0, crow, GLA_Q0:GLA_Q0 + GLA_KP] * (GLA_DK ** -0.5)
            k = p_ref[0, crow, GLA_K0:GLA_K0 + GLA_KP]
            vb = p_ref[0, crow, GLA_V0:GLA_V0 + GLA_W].astype(BF16)
            if indep:
                st = spread(s0_ref[0, c])
            gc = g_all[c * lc:(c + 1) * lc, :]
            gl = gc[lc - 1:lc, :]
            gm = gc[mid:mid + 1, :]
            qc = (q * jnp.exp(gc - gm))
            ke = (k * jnp.exp(gm - gc)).astype(BF16)
            kd = (k * jnp.exp(gl - gc)).astype(BF16)
            qs = jnp.concatenate([qc * m for m in head_k], axis=0).astype(BF16)
            attn = jnp.where(causal, _dot_nt(qs, ke), 0.0)
            o2 = _dot(attn.astype(BF16), vb)
            o = _dot_nt((q * jnp.exp(gc)).astype(BF16), st.astype(BF16))
            for h in range(GLA_HEADS):
                o = o + head_v[h] * o2[h * lc:(h + 1) * lc, :]
            st = st * jnp.exp(gl) + _dot_tn(vb, kd) * same_head
            if indep:
                sfin_ref[0, c] = gather(st)
            outs.append(o)
        o = jnp.concatenate(outs, axis=0) if n_sub > 1 else outs[0]
        og = p_ref[0, rows, GLA_OG0:GLA_OG0 + GLA_W]
        ms = _dot((o * o).astype(BF16), mavg_ref[...])
        on = o * lax.rsqrt(ms + RMS_EPS) * gn_ref[...]
        o_ref[0, rows, :] = on * (og * _sigmoid(og))
    if not indep:
        s_ref[...] = st

        @pl.when(pl.program_id(1) == pl.num_programs(1) - 1)
        def _():
            sfin_ref[0, 0] = gather(st)


def _gla(p, s0, gw, lblk, span, lc, indep):
    b, t, _ = p.shape
    n_state = span // lc if indep else 1
    assert not indep or (t == lblk == span)
    consts = [gw["wg"], gw["bg"], gw["tril"], gw["mavg"], gw["gn"], gw["spread"]]
    st_spec = pl.BlockSpec((1, n_state, GLA_W, GLA_DK), lambda i, j: (i, 0, 0, 0))
    return pl.pallas_call(
        functools.partial(_gla_body, n_part=lblk // span, n_sub=span // lc, lc=lc, indep=indep),
        grid=(b, t // lblk),
        in_specs=[pl.BlockSpec((1, lblk, GLA_PW), lambda i, j: (i, j, 0)), st_spec]
        + [_const_spec(c.shape) for c in consts],
        out_specs=[pl.BlockSpec((1, lblk, GLA_W), lambda i, j: (i, j, 0)), st_spec],
        out_shape=[jax.ShapeDtypeStruct((b, t, GLA_W), F32),
                   jax.ShapeDtypeStruct((b, n_state, GLA_W, GLA_DK), F32)],
        scratch_shapes=[pltpu.VMEM((GLA_W, GLA_KP), F32)],
        compiler_params=_cparams("parallel", "arbitrary"),
        name="gla",
    )(p, s0, *consts)


def _t5_bias(t5_table, q_pos, k_pos):
    nq, nk = q_pos.shape[0], k_pos.shape[0]
    rel = (k_pos[0] - q_pos[0]) + jnp.arange(-(nq - 1), nk, dtype=jnp.int32)
    half = N_BUCKETS // 2
    max_exact = half // 2
    n = jnp.abs(rel)
    far = max_exact + (jnp.log(jnp.maximum(n, 1).astype(jnp.float32) / max_exact)
                       / math.log(T5_MAX_DIST / max_exact) * (half - max_exact)).astype(jnp.int32)
    bucket = jnp.where(rel > 0, half, 0) + jnp.where(n < max_exact, n, jnp.minimum(far, half - 1))
    by_dist = t5_table.astype(F32)[bucket].T
    return jnp.stack([by_dist[:, nq - 1 - i:nq - 1 - i + nk] for i in range(nq)], axis=1)


def _swa_body(sink_ref, q_ref, kp_ref, vp_ref, kc_ref, vc_ref, bias_ref, o_ref, *, bb, qb, cq, win, pos0):
    blk = pl.program_id(1)
    col = lax.broadcasted_iota(jnp.int32, (1, win), 1)
    row = lax.broadcasted_iota(jnp.int32, (SWA_REP * cq, 1), 0)
    for b in range(bb):
        kwin = jnp.concatenate([kp_ref[b], kc_ref[b]], axis=0)
        vwin = jnp.concatenate([vp_ref[b], vc_ref[b]], axis=0)
        for j in range(qb // cq):
            valid = (pos0 + blk * qb + j * cq + col) >= 0
            for g in range(SWA_KV_HEADS):
                heads = [SWA_REP * g + r for r in range(SWA_REP)]
                qs = jnp.concatenate(
                    [q_ref[b, j * cq:(j + 1) * cq, h * SWA_HEAD_DIM:(h + 1) * SWA_HEAD_DIM] for h in heads], axis=0)
                kk = kwin[j * cq:j * cq + win, g * SWA_HEAD_DIM:(g + 1) * SWA_HEAD_DIM]
                vv = vwin[j * cq:j * cq + win, g * SWA_HEAD_DIM:(g + 1) * SWA_HEAD_DIM]
                s = _dot_nt(qs.astype(BF16), kk.astype(BF16)) + bias_ref[g]
                if pos0 + j * cq < 0:
                    s = jnp.where(valid, s, -1e30)
                sink = jnp.where(row < cq, sink_ref[heads[0]],
                                 jnp.where(row < 2 * cq, sink_ref[heads[1]], sink_ref[heads[2]]))
                m = jnp.maximum(jnp.max(s, axis=-1, keepdims=True), sink)
                e = jnp.exp(s - m)
                den = jnp.sum(e, axis=-1, keepdims=True) + jnp.exp(sink - m)
                o = _dot(e.astype(BF16), vv.astype(BF16)) * (1.0 / den)
                for r, h in enumerate(heads):
                    o_ref[b, j * cq:(j + 1) * cq, h * SWA_HEAD_DIM:(h + 1) * SWA_HEAD_DIM] = o[r * cq:(r + 1) * cq, :]


def _swa(sinks, p_swa, k_prev, v_prev, prev_k_map, prev_v_map, bias, bb, qb, cq, win, pos0):
    b, t, _ = p_swa.shape
    kcol, vcol = SWA_W // SWA_KV_W, SWA_W // SWA_KV_W + 1
    return pl.pallas_call(
        functools.partial(_swa_body, bb=bb, qb=qb, cq=cq, win=win, pos0=pos0),
        grid=(b // bb, t // qb),
        in_specs=[pl.BlockSpec(memory_space=pltpu.SMEM),
                  pl.BlockSpec((bb, qb, SWA_W), lambda i, j: (i, j, 0)),
                  pl.BlockSpec((bb, SWA_WINDOW, SWA_KV_W), prev_k_map),
                  pl.BlockSpec((bb, SWA_WINDOW, SWA_KV_W), prev_v_map),
                  pl.BlockSpec((bb, qb, SWA_KV_W), lambda i, j: (i, j, kcol)),
                  pl.BlockSpec((bb, qb, SWA_KV_W), lambda i, j: (i, j, vcol)),
                  _const_spec(bias.shape)],
        out_specs=pl.BlockSpec((bb, qb, SWA_W), lambda i, j: (i, j, 0)),
        out_shape=jax.ShapeDtypeStruct((b, t, SWA_W), F32),
        compiler_params=_cparams("parallel", "parallel"),
        name="swa",
    )(sinks, p_swa, k_prev, v_prev, p_swa, p_swa, bias)


def _out_proj_body(x_ref, ys_ref, og_ref, os_ref, wo_ref, gf_ref, wr_ref, br_ref, triu_ref, before_ref, x1_ref, xs_ref,
                   aux_ref, cnt_ref):
    x1 = (x_ref[...] + _dot(ys_ref[...].astype(BF16), wo_ref[0:SSM_W, :])
          + _dot(og_ref[...].astype(BF16), wo_ref[SSM_W:SSM_W + GLA_W, :])
          + _dot(os_ref[...].astype(BF16), wo_ref[SSM_W + GLA_W:MIX_W, :]))
    x1_ref[...] = x1
    xn = x1 * lax.rsqrt(jnp.mean(x1 * x1, axis=-1, keepdims=True) + RMS_EPS) * gf_ref[...]
    xb = xn.astype(BF16)
    logit = _dot(xb, wr_ref[...]) + br_ref[...]
    lt = jnp.transpose(logit)[:2 * N_EXPERTS, :]
    row_i = lax.broadcasted_iota(jnp.int32, lt.shape, 0)
    row = row_i.astype(F32)
    neg = -jnp.inf
    big = float(LANES)
    g_row = row_i - ROUTER_GROUP_LANE0
    lg = jnp.where((g_row >= 0) & (g_row < N_EXPERT_GROUPS), lt, neg)
    gmax = jnp.max(lg, axis=0, keepdims=True)
    p_sel = 1.0 / jnp.sum(jnp.exp(lg - gmax), axis=0, keepdims=True)
    g_idx = jnp.min(jnp.where(lg == gmax, g_row.astype(F32), big), axis=0, keepdims=True)
    row_group = (row_i >> _log2(EXPERTS_PER_GROUP)).astype(F32)
    in_group = (row_i < N_EXPERTS) & (row_group == g_idx)
    le = jnp.where(in_group, lt, neg)
    m1 = jnp.max(le, axis=0, keepdims=True)
    i1 = jnp.min(jnp.where(le == m1, row, big), axis=0, keepdims=True)
    le2 = jnp.where(row == i1, neg, le)
    m2 = jnp.max(le2, axis=0, keepdims=True)
    i2 = jnp.min(jnp.where(le2 == m2, row, big), axis=0, keepdims=True)
    r = jnp.exp(m2 - m1)
    w1 = 1.0 / (1.0 + r)
    exp_row = lax.broadcasted_iota(jnp.int32, (N_EXPERTS, MOE_TB), 0).astype(F32)
    s1 = jnp.where(exp_row == i1, 1.0, 0.0)
    s2 = jnp.where(exp_row == i2, 1.0, 0.0)
    both = s1 + s2
    cnt = jnp.sum(both, axis=1, keepdims=True) + jnp.zeros_like(both)
    seg = jnp.floor((cnt + (MOE_PIECE - 1)) * (1.0 / MOE_PIECE)) * MOE_PIECE
    seg_k = jnp.concatenate([seg, jnp.zeros((LANES - N_EXPERTS, MOE_TB), F32)], axis=0).astype(BF16)
    lo = _dot(before_ref[...], seg_k)
    rank = _dot(both.astype(BF16), triu_ref[...])
    pos = lo + rank
    d1 = jnp.sum(s1 * pos, axis=0, keepdims=True)
    d2 = jnp.sum(s2 * pos, axis=0, keepdims=True)
    r_iota = lax.broadcasted_iota(jnp.int32, (MOE_RLOC, MOE_TB), 0).astype(F32)
    perm = jnp.where((r_iota == d1) | (r_iota == d2), 1.0, 0.0).astype(BF16)
    xs_ref[...] = _dot(perm, xb).astype(BF16)
    arow = lax.broadcasted_iota(jnp.int32, (LANES, MOE_TB), 0)
    at = jnp.where(arow == AUX_W1, p_sel * w1, jnp.where(arow == AUX_W2, p_sel * (r * w1),
                   jnp.where(arow == AUX_D1, d1, jnp.where(arow == AUX_D2, d2, 0.0))))
    aux_ref[...] = jnp.transpose(at)
    cnt_ref[0] = cnt[:, :LANES]


def _out_proj(x, ys, og, osw, wo, gf, wr, br, triu, before):
    n = x.shape[0]
    nblk = n // MOE_TB
    row = lambda w: pl.BlockSpec((MOE_TB, w), lambda i: (i, 0))
    return pl.pallas_call(
        _out_proj_body,
        grid=(nblk,),
        in_specs=[row(D_MODEL), row(SSM_W), row(GLA_W), row(SWA_W), _const_spec(wo.shape), _const_spec(gf.shape),
                  _const_spec(wr.shape), _const_spec(br.shape), _const_spec(triu.shape), _const_spec(before.shape)],
        out_specs=[row(D_MODEL), pl.BlockSpec((MOE_RLOC, D_MODEL), lambda i: (i, 0)), row(LANES),
                   pl.BlockSpec((1, N_EXPERTS, LANES), lambda i: (i, 0, 0))],
        out_shape=[jax.ShapeDtypeStruct((n, D_MODEL), F32), jax.ShapeDtypeStruct((nblk * MOE_RLOC, D_MODEL), BF16),
                   jax.ShapeDtypeStruct((n, LANES), F32), jax.ShapeDtypeStruct((nblk, N_EXPERTS, LANES), F32)],
        compiler_params=_cparams("parallel"),
        name="out_proj",
    )(x, ys, og, osw, wo, gf, wr, br, triu, before)


def _piece_copy(hbm_ref, piece, buf_ref, slot, p, sem, to_hbm):
    start = lambda i: i * MOE_PIECE if isinstance(i, int) else pl.multiple_of(i * MOE_PIECE, MOE_PIECE)
    rows = pl.ds(start(piece), MOE_PIECE)
    vm = buf_ref.at[slot, pl.ds(start(p), MOE_PIECE)]
    if to_hbm:
        return pltpu.make_async_copy(vm, hbm_ref.at[rows], sem)
    return pltpu.make_async_copy(hbm_ref.at[rows], vm, sem)


def _moe_expert_body(texp_ref, piece_ref, nv_ref, xs_hbm, wg_ref, wu_ref, wd_ref, ys_hbm,
                     xbuf, ybuf, wgb, wub, wdb, sem_in, sem_out, *, pt, n_steps):
    i = pl.program_id(0)
    slot = i % 2

    def for_pieces(tile, s, hbm_ref, buf_ref, sem, to_hbm, wait):
        def one(p):
            piece = 0 if wait else piece_ref[tile * pt + p]
            c = _piece_copy(hbm_ref, piece, buf_ref, s, p, sem.at[s], to_hbm)
            c.wait() if wait else c.start()

        nv = nv_ref[tile]

        @pl.when(nv == pt)
        def _():
            for p in range(pt):
                one(p)

        @pl.when(nv < pt)
        def _():
            def body(p, carry):
                one(p)
                return carry
            lax.fori_loop(0, nv, body, 0)

    gather = lambda tile, s, wait: for_pieces(tile, s, xs_hbm, xbuf, sem_in, False, wait)
    scatter = lambda tile, s, wait: for_pieces(tile, s, ys_hbm, ybuf, sem_out, True, wait)

    @pl.when(i == 0)
    def _():
        xbuf[...] = jnp.zeros_like(xbuf)
        gather(0, 0, False)

    @pl.when(i + 1 < n_steps)
    def _():
        gather(i + 1, 1 - slot, False)

    gather(i, slot, True)

    @pl.when((i == 0) | (texp_ref[i] != texp_ref[jnp.maximum(i - 1, 0)]))
    def _():
        wgb[...] = wg_ref[0].astype(BF16)
        wub[...] = wu_ref[0].astype(BF16)
        wdb[...] = wd_ref[0].astype(BF16)

    @pl.when(nv_ref[i] > 0)
    def _():
        xb = xbuf[slot]
        a = _dot(xb, wgb[...])
        h = (a * _sigmoid(a)) * _dot(xb, wub[...])
        ybuf[slot] = _dot(h.astype(BF16), wdb[...]).astype(BF16)

    scatter(i, slot, False)

    @pl.when(i >= 1)
    def _():
        scatter(i - 1, 1 - slot, True)

    @pl.when(i == n_steps - 1)
    def _():
        scatter(i, slot, True)


def _moe_experts(tile_expert, piece, nvalid, xs, wg, wu, wd, layer, pt):
    n_steps = tile_expert.shape[0]
    tm = pt * MOE_PIECE
    wspec = lambda shape: pl.BlockSpec(shape, lambda i, te, pc, nv: (te[i] + layer * N_EXPERTS, 0, 0))
    grid_spec = pltpu.PrefetchScalarGridSpec(
        num_scalar_prefetch=3,
        grid=(n_steps,),
        in_specs=[pl.BlockSpec(memory_space=pl.ANY), wspec((1, D_MODEL, EXPERT_FF)), wspec((1, D_MODEL, EXPERT_FF)),
                  wspec((1, EXPERT_FF, D_MODEL))],
        out_specs=pl.BlockSpec(memory_space=pl.ANY),
        scratch_shapes=[pltpu.VMEM((2, tm, D_MODEL), BF16), pltpu.VMEM((2, tm, D_MODEL), BF16),
                        pltpu.VMEM((D_MODEL, EXPERT_FF), BF16), pltpu.VMEM((D_MODEL, EXPERT_FF), BF16),
                        pltpu.VMEM((EXPERT_FF, D_MODEL), BF16),
                        pltpu.SemaphoreType.DMA((2,)), pltpu.SemaphoreType.DMA((2,))],
    )
    return pl.pallas_call(
        functools.partial(_moe_expert_body, pt=pt, n_steps=n_steps),
        grid_spec=grid_spec,
        out_shape=jax.ShapeDtypeStruct(xs.shape, BF16),
        input_output_aliases={3: 0},
        compiler_params=_cparams("arbitrary"),
        name="moe_experts",
    )(tile_expert, piece, nvalid, xs, wg, wu, wd)


def _combine_math(ys_ref, aux_ref, x1_ref):
    col = lax.broadcasted_iota(jnp.int32, (MOE_TB, MOE_RLOC), 1).astype(F32)
    mix = jnp.where(col == aux_ref[:, AUX_D1:AUX_D1 + 1], aux_ref[:, AUX_W1:AUX_W1 + 1],
                    jnp.where(col == aux_ref[:, AUX_D2:AUX_D2 + 1], aux_ref[:, AUX_W2:AUX_W2 + 1], 0.0))
    return x1_ref[...] + _dot(mix.astype(BF16), ys_ref[...])


def _moe_combine_body(ys_ref, aux_ref, x1_ref, o_ref):
    o_ref[...] = _combine_math(ys_ref, aux_ref, x1_ref)


def _combine_in_proj_body(ys_ref, aux_ref, x1_ref, g_ref, w_ref, mavg_ref, qkg_ref, o_ref, gla_ref, swa_ref, ssm_ref):
    x = _combine_math(ys_ref, aux_ref, x1_ref)
    o_ref[...] = x
    _in_proj_math(x, g_ref, w_ref, mavg_ref, qkg_ref, gla_ref, swa_ref, ssm_ref)


def _combine_in_proj(ys, aux, x1, g, w, mavg, qkg):
    n = x1.shape[0]
    row = lambda wdt: pl.BlockSpec((MOE_TB, wdt), lambda i: (i, 0))
    return pl.pallas_call(
        _combine_in_proj_body,
        grid=(n // MOE_TB,),
        in_specs=[pl.BlockSpec((MOE_RLOC, D_MODEL), lambda i: (i, 0)), row(LANES), row(D_MODEL), _const_spec(g.shape),
                  _const_spec(w.shape), _const_spec(mavg.shape), _const_spec(qkg.shape)],
        out_specs=[row(D_MODEL), row(GLA_PW), row(SWA_PW), row(SSM_W)],
        out_shape=[jax.ShapeDtypeStruct((n, D_MODEL), F32), jax.ShapeDtypeStruct((n, GLA_PW), F32),
                   jax.ShapeDtypeStruct((n, SWA_PW), F32), jax.ShapeDtypeStruct((n, SSM_W), F32)],
        compiler_params=_cparams("parallel"),
        name="combine_in_proj",
    )(ys, aux, x1, g, w, mavg, qkg)


def _moe_combine(ys, aux, x1):
    n = x1.shape[0]
    return pl.pallas_call(
        _moe_combine_body,
        grid=(n // MOE_TB,),
        in_specs=[pl.BlockSpec((MOE_RLOC, D_MODEL), lambda i: (i, 0)), pl.BlockSpec((MOE_TB, LANES), lambda i: (i, 0)),
                  pl.BlockSpec((MOE_TB, D_MODEL), lambda i: (i, 0))],
        out_specs=pl.BlockSpec((MOE_TB, D_MODEL), lambda i: (i, 0)),
        out_shape=jax.ShapeDtypeStruct((n, D_MODEL), F32),
        compiler_params=_cparams("parallel"),
        name="moe_combine",
    )(ys, aux, x1)


def _route_tables(counts, pt):
    nblk = counts.shape[0]
    n = nblk * MOE_TB
    cnt = counts[:, :, 0].T.astype(jnp.int32)
    pc = (cnt + MOE_PIECE - 1) // MOE_PIECE
    lo_p = jnp.cumsum(pc, axis=0) - pc
    pe = pc.sum(1)
    tiles_e = (pe + pt - 1) // pt
    tile_start = jnp.cumsum(tiles_e) - tiles_e
    seg_start = (tile_start[:, None] * pt + jnp.cumsum(pc, axis=1) - pc).reshape(-1)
    pcs = pc.reshape(-1)
    seg_src = (jnp.arange(nblk, dtype=jnp.int32)[None, :] * MOE_PPB + lo_p).reshape(-1)
    n_steps = -(-(2 * n // MOE_PIECE + nblk * N_EXPERTS + N_EXPERTS * (pt - 1)) // pt)
    slot = jnp.arange(n_steps * pt, dtype=jnp.int32)[:, None]
    in_seg = (slot >= seg_start[None, :]) & (slot < (seg_start + pcs)[None, :])
    piece = jnp.sum(jnp.where(in_seg, seg_src[None, :] + slot - seg_start[None, :], 0), axis=1).astype(jnp.int32)
    tile = jnp.arange(n_steps, dtype=jnp.int32)[:, None]
    in_exp = (tile >= tile_start[None, :]) & (tile < (tile_start + tiles_e)[None, :])
    tile_expert = jnp.sum(jnp.where(in_exp, jnp.arange(N_EXPERTS, dtype=jnp.int32)[None, :], 0), axis=1)
    tile_expert = jnp.where(tile[:, 0] < tiles_e.sum(), tile_expert, N_EXPERTS - 1).astype(jnp.int32)
    nvalid = jnp.sum(jnp.where(in_exp, jnp.clip(pe[None, :] - (tile - tile_start[None, :]) * pt, 0, pt), 0),
                     axis=1).astype(jnp.int32)
    return tile_expert, piece, nvalid


def _moe(xs, counts, wg, wu, wd, layer, pt):
    tile_expert, piece, nvalid = _route_tables(counts, pt)
    return _moe_experts(tile_expert, piece, nvalid, xs, wg, wu, wd, layer, pt)


def _seg_mean_matrix(width, seg):
    i = jnp.arange(width)
    return jnp.where((i[:, None] // seg) == (i[None, :] // seg), 1.0 / seg, 0.0).astype(BF16)


def _chunk_tril(span, lc):
    i = jnp.arange(span)
    return ((i[:, None] >= i[None, :]) & ((i[:, None] // lc) == (i[None, :] // lc))).astype(BF16)


def _prep_consts(ssm_shapes, gla_shapes):
    ti = jnp.arange(MOE_TB)
    kk = jnp.arange(GLA_KP)
    spread = ((kk[None, :] % GLA_DK) == jnp.arange(GLA_DK)[:, None]) & (kk[None, :] < GLA_K)
    return {"mavg_qk": _seg_mean_matrix(SWA_W + SWA_KV_W, SWA_HEAD_DIM), "mavg_gla": _seg_mean_matrix(GLA_W, GLA_DV),
            "spread": spread.astype(BF16),
            "ssm_tril": {s: _chunk_tril(*s) for s in ssm_shapes}, "gla_tril": {s: _chunk_tril(*s) for s in gla_shapes},
            "moe_triu": (ti[:, None] < ti[None, :]).astype(BF16),
            "moe_before": (jnp.arange(LANES)[None, :] < jnp.arange(N_EXPERTS)[:, None]).astype(BF16)}


def _prep_layer(lw, ssm_chunks):
    w_in = lw["w_in"].astype(F32)
    cols = {}
    off = 0
    for name, wdt in (("u", SSM_W), ("qg", GLA_K), ("kg", GLA_K), ("vg", GLA_W), ("z", GLA_GATE_RANK), ("og", GLA_W),
                      ("qs", SWA_W), ("ks", SWA_KV_W), ("vs", SWA_KV_W)):
        cols[name] = w_in[:, off:off + wdt]
        off += wdt
    zpad = lambda wdt: jnp.zeros((D_MODEL, wdt), F32)
    w_all = jnp.concatenate(
        [cols["qg"], zpad(GLA_KP - GLA_K), cols["kg"], zpad(GLA_KP - GLA_K), cols["vg"], cols["og"], cols["z"],
         zpad(LANES - GLA_GATE_RANK), cols["qs"], cols["ks"], cols["vs"], cols["u"]], axis=1).astype(BF16)
    out = {"norm_mix": lw["norm_mix"].astype(F32)[None, :], "w_all": w_all,
           "qk_gain": jnp.concatenate([jnp.tile(lw["swa_q_norm"].astype(F32) * SWA_HEAD_DIM ** -0.5, SWA_HEADS),
                                       jnp.tile(lw["swa_k_norm"].astype(F32), SWA_KV_HEADS)])[None, :]}
    a_re = lw["ssm_a_re"].astype(F32)
    a_im = lw["ssm_a_im"].astype(F32)
    dt = jnp.exp(lw["ssm_log_dt"].astype(F32))[:, None]
    mag = jnp.exp(a_re * dt)
    ab_re = mag * jnp.cos(a_im * dt)
    ab_im = mag * jnp.sin(a_im * dt)
    den = a_re * a_re + a_im * a_im
    nr = ab_re - 1.0
    f_re = (nr * a_re + ab_im * a_im) / den
    f_im = (ab_im * a_re - nr * a_im) / den
    b_re = lw["ssm_b_re"].astype(F32)
    b_im = lw["ssm_b_im"].astype(F32)
    bb_re = f_re[..., None] * b_re - f_im[..., None] * b_im
    bb_im = f_re[..., None] * b_im + f_im[..., None] * b_re
    eye_g = jnp.eye(SSM_GROUPS, dtype=F32)
    blockdiag_in = lambda bb: jnp.einsum("gpc,gh->gchp", bb, eye_g).reshape(SSM_W, SSM_FLAT)
    blockdiag_out = lambda cc: jnp.einsum("gcp,gh->gphc", cc, eye_g).reshape(SSM_FLAT, SSM_W)
    bd = jnp.concatenate([blockdiag_in(bb_re), blockdiag_in(bb_im)], axis=1).astype(BF16)
    cd = jnp.concatenate([blockdiag_out(lw["ssm_c_re"].astype(F32)),
                          -blockdiag_out(lw["ssm_c_im"].astype(F32))], axis=0).astype(BF16)
    ssm = {"ab": jnp.concatenate([ab_re.reshape(1, SSM_FLAT), ab_im.reshape(1, SSM_FLAT),
                                  jnp.zeros((6, SSM_FLAT), F32)], axis=0),
           "bd": bd, "cd": cd, "d": lw["ssm_d"].astype(F32)[None, :], "wglu": lw["ssm_w_glu"].astype(BF16)}
    def powers(lc, sign):
        tt = jnp.arange(lc, dtype=F32)[:, None, None]
        m = jnp.exp(sign * tt * (a_re * dt)[None])
        ang = sign * tt * (a_im * dt)[None]
        return jnp.stack([(m * jnp.cos(ang)).reshape(lc, SSM_FLAT), (m * jnp.sin(ang)).reshape(lc, SSM_FLAT)])

    out["ssm"] = ssm
    out["ssm_pw"] = {lc: {"pwp": powers(lc, 1.0), "pwn": powers(lc, -1.0)} for lc in ssm_chunks}
    wg = jnp.zeros((LANES, GLA_KP), F32).at[:GLA_GATE_RANK, :GLA_K].set(lw["gla_w_gate"].astype(F32)).astype(BF16)
    bg = jnp.zeros((1, GLA_KP), F32).at[0, :GLA_K].set(lw["gla_b_gate"].astype(F32))
    out["gla"] = {"wg": wg, "bg": bg, "gn": jnp.tile(lw["gla_norm"].astype(F32), GLA_HEADS)[None, :]}
    out["sinks"] = lw["swa_sinks"].astype(F32)
    out["w_out"] = lw["w_out"].astype(BF16)
    out["norm_ffn"] = lw["norm_ffn"].astype(F32)[None, :]
    wr = jnp.zeros((D_MODEL, LANES), F32)
    wr = wr.at[:, :N_EXPERTS].set(lw["moe_w_expert"].astype(F32))
    wr = wr.at[:, ROUTER_GROUP_LANE0:ROUTER_GROUP_LANE0 + N_EXPERT_GROUPS].set(lw["moe_w_group"].astype(F32))
    out["wr"] = wr.astype(BF16)
    br = jnp.zeros((1, LANES), F32)
    br = br.at[0, :N_EXPERTS].set(lw["moe_b_expert"].astype(F32))
    br = br.at[0, ROUTER_GROUP_LANE0:ROUTER_GROUP_LANE0 + N_EXPERT_GROUPS].set(lw["moe_b_group"].astype(F32))
    out["br"] = br
    return out


def _gla_state_in(h0):
    return jnp.transpose(h0.astype(F32), (0, 1, 3, 2)).reshape(h0.shape[0], GLA_W, GLA_DK)


def _gla_state_out(st):
    return jnp.transpose(st.reshape(st.shape[0], GLA_HEADS, GLA_DV, GLA_DK), (0, 1, 3, 2))


def _layer(x, pending, pw, bias, ssm_h0r, ssm_h0i, gla_h0, past_k, past_v, cfg, b, t):
    n = b * t
    pc = pw["consts"]
    in_w = (pw["norm_mix"], pw["w_all"], pc["mavg_qk"], pw["qk_gain"])
    ssm_w = dict(pw["ssm"], **pw["ssm_pw"][cfg["ssm"][1]], tril=pc["ssm_tril"][cfg["ssm"]])
    gla_w = dict(pw["gla"], tril=pc["gla_tril"][cfg["gla"]], mavg=pc["mavg_gla"], spread=pc["spread"])
    if pending is None:
        p_gla, p_swa, p_ssm = _in_proj(x, *in_w, cfg["tm"])
    else:
        x, p_gla, p_swa, p_ssm = _combine_in_proj(*pending, *in_w)
    p_gla = p_gla.reshape(b, t, GLA_PW)
    p_swa = p_swa.reshape(b, t, SWA_PW)
    stack = cfg["stack"]
    indep = stack > 1
    grp = b // stack
    y_ssm, h_re, h_im = _ssm(p_ssm.reshape(grp, stack * t, SSM_W), ssm_h0r.reshape(grp, stack, SSM_FLAT).astype(F32),
                             ssm_h0i.reshape(grp, stack, SSM_FLAT).astype(F32), ssm_w, cfg["lblk"],
                             *cfg["ssm"], indep)
    o_g, s_fin = _gla(p_gla.reshape(grp, stack * t, GLA_PW), _gla_state_in(gla_h0).reshape(grp, stack, GLA_W, GLA_DK),
                      gla_w, cfg["lblk"], *cfg["gla"], indep)
    s_fin = s_fin.reshape(b, GLA_W, GLA_DK)
    kcol, vcol = SWA_W // SWA_KV_W, SWA_W // SWA_KV_W + 1
    if past_k is None:
        per_blk = cfg["qb"] // SWA_WINDOW
        prev_k_map = lambda i, j: (i, jnp.maximum(j * per_blk - 1, 0), kcol)
        prev_v_map = lambda i, j: (i, jnp.maximum(j * per_blk - 1, 0), vcol)
        k_prev, v_prev = p_swa, p_swa
        keep = min(SWA_WINDOW, t)
        new_k = p_swa[:, t - keep:, SWA_W:SWA_W + SWA_KV_W]
        new_v = p_swa[:, t - keep:, SWA_W + SWA_KV_W:]
    else:
        prev_k_map = prev_v_map = lambda i, j: (i, 0, 0)
        k_prev = past_k.reshape(b, SWA_WINDOW, SWA_KV_W).astype(F32)
        v_prev = past_v.reshape(b, SWA_WINDOW, SWA_KV_W).astype(F32)
        new_k = p_swa[:, :, SWA_W:SWA_W + SWA_KV_W]
        new_v = p_swa[:, :, SWA_W + SWA_KV_W:]
    o_s = _swa(pw["sinks"], p_swa, k_prev, v_prev, prev_k_map, prev_v_map, bias, stack, cfg["qb"], cfg["cq"],
               cfg["win"], cfg["pos0"])
    x1, xs, aux, counts = _out_proj(x, y_ssm.reshape(n, SSM_W), o_g.reshape(n, GLA_W),
                                    o_s.reshape(n, SWA_W), pw["w_out"], pw["norm_ffn"], pw["wr"], pw["br"],
                                    pc["moe_triu"], pc["moe_before"])
    ys = _moe(xs, counts, pw["moe_wg"], pw["moe_wu"], pw["moe_wd"], pw["layer"], cfg["pt"])
    kv_shape = (b, new_k.shape[1], SWA_KV_HEADS, SWA_HEAD_DIM)
    return ((ys, aux, x1), new_k.reshape(kv_shape), new_v.reshape(kv_shape), _gla_state_out(s_fin),
            h_re.reshape(b, SSM_GROUPS, SSM_STATE), h_im.reshape(b, SSM_GROUPS, SSM_STATE))


def _group_cfg(t, past_len, n_past):
    if n_past is None:
        assert t % (16 * CHUNK) == 0
        cfg = dict(ssm=(4 * CHUNK, CHUNK), gla=(4 * CHUNK, 2 * CHUNK), lblk=16 * CHUNK, stack=1, qb=8 * CHUNK, cq=CHUNK,
                   win=SWA_WINDOW + CHUNK, pos0=-SWA_WINDOW, tm=1024, pt=64)
        q_pos = jnp.arange(CHUNK, dtype=jnp.int32) + SWA_WINDOW
        k_pos = jnp.arange(SWA_WINDOW + CHUNK, dtype=jnp.int32)
    else:
        first_key, last_q = past_len - n_past, past_len + t - 1
        assert n_past == SWA_WINDOW and t <= CHUNK and t % 8 == 0
        assert past_len // CHUNK == last_q // CHUNK and first_key // CHUNK >= past_len // CHUNK - SWA_WINDOW // CHUNK
        stack = DECODE_STACK
        cfg = dict(ssm=(stack * t, t), gla=(stack * t, t), lblk=stack * t, stack=stack, qb=t, cq=t, win=n_past + t,
                   pos0=first_key, tm=512, pt=16)
        q_pos = past_len + jnp.arange(t, dtype=jnp.int32)
        k_pos = first_key + jnp.arange(n_past + t, dtype=jnp.int32)
    return cfg, q_pos, k_pos


def _stacked_bias(t5_table, q_pos, k_pos):
    bias = _t5_bias(t5_table, q_pos, k_pos)
    nq, nk = bias.shape[1:]
    return bias.reshape(SWA_KV_HEADS, SWA_REP * nq, nk)


PAST_LEN = 1024


def kernel(x_prompt, x_sample, cache_swa_k, cache_swa_v, state_gla, state_ssm_re, state_ssm_im, norm_mix, w_in, ssm_a_re, ssm_a_im, ssm_log_dt, ssm_b_re, ssm_b_im, ssm_c_re, ssm_c_im, ssm_d, ssm_w_glu, gla_w_gate, gla_b_gate, gla_norm, swa_q_norm, swa_k_norm, swa_sinks, t5_table, w_out, norm_ffn, moe_w_group, moe_b_group, moe_w_expert, moe_b_expert, moe_w_gate, moe_w_up, moe_w_down):
    depth = w_in.shape[0]
    bp, tp, _ = x_prompt.shape
    bs, ts, _ = x_sample.shape
    cfg_p, qpos_p, kpos_p = _group_cfg(tp, 0, None)
    cfg_s, qpos_s, kpos_s = _group_cfg(ts, PAST_LEN, cache_swa_k.shape[2])
    bias_p = _stacked_bias(t5_table, qpos_p, kpos_p)
    bias_s = _stacked_bias(t5_table, qpos_s, kpos_s)
    xp = x_prompt.astype(F32).reshape(bp * tp, D_MODEL)
    xs = x_sample.astype(F32).reshape(bs * ts, D_MODEL)
    pend_p = pend_s = None
    moe_wg = moe_w_gate.astype(F32).reshape(depth * N_EXPERTS, D_MODEL, EXPERT_FF)
    moe_wu = moe_w_up.astype(F32).reshape(depth * N_EXPERTS, D_MODEL, EXPERT_FF)
    moe_wd = moe_w_down.astype(F32).reshape(depth * N_EXPERTS, EXPERT_FF, D_MODEL)
    outs = [[] for _ in range(10)]
    lw_all = {
        "norm_mix": norm_mix, "w_in": w_in, "ssm_a_re": ssm_a_re, "ssm_a_im": ssm_a_im, "ssm_log_dt": ssm_log_dt,
        "ssm_b_re": ssm_b_re, "ssm_b_im": ssm_b_im, "ssm_c_re": ssm_c_re, "ssm_c_im": ssm_c_im, "ssm_d": ssm_d,
        "ssm_w_glu": ssm_w_glu, "gla_w_gate": gla_w_gate, "gla_b_gate": gla_b_gate, "gla_norm": gla_norm,
        "swa_q_norm": swa_q_norm, "swa_k_norm": swa_k_norm, "swa_sinks": swa_sinks, "w_out": w_out,
        "norm_ffn": norm_ffn, "moe_w_group": moe_w_group, "moe_b_group": moe_b_group, "moe_w_expert": moe_w_expert,
        "moe_b_expert": moe_b_expert,
    }
    consts = _prep_consts({cfg_p["ssm"], cfg_s["ssm"]}, {cfg_p["gla"], cfg_s["gla"]})
    for l in range(depth):
        pw = _prep_layer({name: w[l] for name, w in lw_all.items()}, (cfg_p["ssm"][1], cfg_s["ssm"][1]))
        pw.update(consts=consts, layer=l, moe_wg=moe_wg, moe_wu=moe_wu, moe_wd=moe_wd)
        zs = jnp.zeros((bp, SSM_GROUPS, SSM_STATE), F32)
        zg = jnp.zeros((bp, GLA_HEADS, GLA_DK, GLA_DV), F32)
        pend_p, nk, nv, ng, nr, ni = _layer(xp, pend_p, pw, bias_p, zs, zs, zg, None, None, cfg_p, bp, tp)
        for slot, val in zip((0, 1, 4, 6, 7), (nk, nv, ng, nr, ni)):
            outs[slot].append(val)
        pend_s, nk, nv, ng, nr, ni = _layer(xs, pend_s, pw, bias_s, state_ssm_re[l], state_ssm_im[l], state_gla[l],
                                            cache_swa_k[l], cache_swa_v[l], cfg_s, bs, ts)
        for slot, val in zip((2, 3, 5, 8, 9), (nk, nv, ng, nr, ni)):
            outs[slot].append(val)
    hp = _moe_combine(*pend_p).reshape(bp, tp, D_MODEL)
    hs = _moe_combine(*pend_s).reshape(bs, ts, D_MODEL)
    return (hp, hs) + tuple(jnp.stack(o) for o in outs)
```
